```python
import functools
import jax, jax.numpy as jnp
from jax import lax
import numpy as np

D_MODEL = 1024
BATCH = 32
SEQ = 256
DEPTH = 4
DEC_BATCH = 8
DEC_SEQ = 4096
PAST_LEN = 512

GRID_W = 64
POS_BASE = 10000.0
D_LRU = D_MODEL // 2
N_LRU_HEADS = 8
LRU_HEAD_DIM = D_LRU // N_LRU_HEADS
CONV_W = 4
LRU_C = 8.0
D_POOL = D_MODEL - D_LRU
POOL_WINDOWS = (2, 4, 8, 16)
N_POOL_GROUPS = len(POOL_WINDOWS)
POOL_CH = D_POOL // N_POOL_GROUPS
D_IN = 2 * D_LRU + D_POOL
D_FF = 2816
N_EXPERTS = 8
TOP_K = 2
D_FF_EXPERT = D_FF // TOP_K
N_DENSE = (DEPTH + 1) // 2
N_MOE = DEPTH // 2
EPS = 1e-6

kernel_name = 'hybrid_rglru_pool_diffusion_step'


def rmsnorm(x, g):
    xf = x.astype(jnp.float32)
    y = xf * lax.rsqrt(jnp.mean(xf * xf, axis=-1, keepdims=True) + EPS)
    return (y * g.astype(jnp.float32)).astype(x.dtype)


def pos_embed_2d(rows, dtype):
    quarter = D_MODEL // 4
    omega = 1.0 / (POS_BASE ** (jnp.arange(quarter, dtype=jnp.float32) / quarter))
    er = jnp.arange(rows, dtype=jnp.float32)[:, None] * omega
    ec = jnp.arange(GRID_W, dtype=jnp.float32)[:, None] * omega
    row_emb = jnp.concatenate([jnp.sin(er), jnp.cos(er)], axis=-1)
    col_emb = jnp.concatenate([jnp.sin(ec), jnp.cos(ec)], axis=-1)
    pe = jnp.concatenate([
        jnp.broadcast_to(row_emb[:, None, :], (rows, GRID_W, D_MODEL // 2)),
        jnp.broadcast_to(col_emb[None, :, :], (rows, GRID_W, D_MODEL // 2))], axis=-1)
    return pe.reshape(rows * GRID_W, D_MODEL).astype(dtype)


def centred_depthwise_conv(x, w, b):
    t = x.shape[1]
    pad_l = CONV_W // 2
    xp = jnp.pad(x, ((0, 0), (pad_l, CONV_W - 1 - pad_l), (0, 0)))
    y = b
    for k in range(CONV_W):
        y = y + xp[:, k:k + t, :] * w[k]
    return y


def _lru_combine(left, right):
    a_l, b_l = left
    a_r, b_r = right
    return a_l * a_r, a_r * b_l + b_r


def rg_lru_scan(xc, wa, ba, wx, bx, lam, h0, reverse):
    bsz, t, _ = xc.shape
    xh = xc.reshape(bsz, t, N_LRU_HEADS, LRU_HEAD_DIM)
    r = jax.nn.sigmoid(jnp.einsum('bthi,hij->bthj', xh, wa).reshape(bsz, t, D_LRU) + ba)
    i = jax.nn.sigmoid(jnp.einsum('bthi,hij->bthj', xh, wx).reshape(bsz, t, D_LRU) + bx)
    log_a = -LRU_C * r.astype(jnp.float32) * jax.nn.softplus(-lam.astype(jnp.float32))
    a = jnp.exp(log_a)
    u = jnp.sqrt(-jnp.expm1(2.0 * log_a)) * (i * xc).astype(jnp.float32)
    if h0 is not None:
        edge = -1 if reverse else 0
        u = u.at[:, edge].add(a[:, edge] * h0.astype(jnp.float32))
    _, h = lax.associative_scan(_lru_combine, (a, u), axis=1, reverse=reverse)
    return h


def multi_scale_pool(x):
    t = x.shape[-2]
    xf = x.astype(jnp.float32)
    csum = jnp.concatenate([jnp.zeros_like(xf[..., :1, :]), jnp.cumsum(xf, axis=-2)], axis=-2)
    pos = jnp.arange(t)
    outs = []
    for g, k in enumerate(POOL_WINDOWS):
        left = k // 2
        right = k - 1 - left
        lo = jnp.clip(pos - left, 0, t)
        hi = jnp.clip(pos + right + 1, 0, t)
        cg = csum[..., g * POOL_CH:(g + 1) * POOL_CH]
        s = jnp.take(cg, hi, axis=-2) - jnp.take(cg, lo, axis=-2)
        outs.append(s / (hi - lo).astype(jnp.float32)[:, None])
    return jnp.concatenate(outs, axis=-1).astype(x.dtype)


def pool_mixer(xb, pool_w, pool_b, pool_scale, grid):
    bsz, t, _ = xb.shape
    xg = xb.reshape(bsz, t // GRID_W, GRID_W, D_POOL) if grid else xb
    d = (multi_scale_pool(xg) - xg).reshape(bsz, t, N_POOL_GROUPS, POOL_CH)
    y = jnp.einsum('btgi,gij->btgj', d, pool_w).reshape(bsz, t, D_POOL) + pool_b
    return y * pool_scale


def swiglu(h, wg, wu, wd):
    return (jax.nn.silu(h @ wg) * (h @ wu)) @ wd


def moe_swiglu(h, router_w, router_b, wg, wu, wd):
    bsz, t, d = h.shape
    hf = h.reshape(bsz * t, d)
    logits = (hf @ router_w).astype(jnp.float32) + router_b.astype(jnp.float32)
    top_v, top_i = lax.top_k(logits, TOP_K)
    top_w = jax.nn.softmax(top_v, axis=-1)
    combine = jnp.sum(jax.nn.one_hot(top_i, N_EXPERTS, dtype=jnp.float32) * top_w[..., None], axis=1)
    y = jnp.zeros((bsz * t, d), jnp.float32)
    for e in range(N_EXPERTS):
        y = y + combine[:, e:e + 1] * swiglu(hf, wg[e], wu[e], wd[e]).astype(jnp.float32)
    return y.astype(h.dtype).reshape(bsz, t, d)


def trunk_layer(x, cond, h0, is_context, ffn, norm_mix_g, w_ada, b_ada, w_in, conv_w, conv_b,
                lru_wa, lru_ba, lru_wx, lru_bx, lru_lam, pool_w, pool_b, pool_scale, w_out, norm_ffn_g):
    mod = jax.nn.silu(cond) @ w_ada + b_ada
    sh_m, sc_m, g_m, sh_f, sc_f, g_f = jnp.split(mod[:, None, :], 6, axis=-1)
    h = rmsnorm(x, norm_mix_g) * (1.0 + sc_m) + sh_m
    proj = h @ w_in
    xa = proj[..., :D_LRU]
    ga = proj[..., D_LRU:2 * D_LRU]
    xb = proj[..., 2 * D_LRU:]
    xc = centred_depthwise_conv(xa, conv_w, conv_b)
    h0_f = None if h0 is None else h0[:, 0]
    h0_b = None if h0 is None else h0[:, 1]
    hf = rg_lru_scan(xc, lru_wa[0], lru_ba[0], lru_wx[0], lru_bx[0], lru_lam[0], h0_f, False)
    hb = rg_lru_scan(xc, lru_wa[1], lru_ba[1], lru_wx[1], lru_bx[1], lru_lam[1], h0_b, True)
    y_a = (hf + hb).astype(x.dtype) * jax.nn.gelu(ga, approximate=True)
    y_b = pool_mixer(xb, pool_w, pool_b, pool_scale, grid=not is_context)
    x = x + g_m * (jnp.concatenate([y_a, y_b], axis=-1) @ w_out)
    h2 = rmsnorm(x, norm_ffn_g) * (1.0 + sc_f) + sh_f
    x = x + g_f * ffn(h2)
    if is_context:
        state = jnp.stack([hf[:, -1], hb[:, 0]], axis=1).astype(x.dtype)
        return x, state
    return x, None


def setup_inputs(seed: int = 0) -> dict:
    key = jax.random.key(seed)
    ks = iter(jax.random.split(key, 40))
    D = D_MODEL

    def nrm(shape, s):
        return s * jax.random.normal(next(ks), shape, jnp.float32)

    u = jax.random.uniform(next(ks), (DEPTH, 2, D_LRU), jnp.float32, 0.9, 0.999)
    return {
        'x_prompt': nrm((BATCH, SEQ, D), 1.0),
        'x_sample': nrm((DEC_BATCH, DEC_SEQ, D), 1.0),
        'state_lru': nrm((DEC_BATCH, DEPTH, 2, D_LRU), 0.5),
        'c': nrm((DEC_BATCH, D), 1.0),
        'c_ctx': nrm((D,), 1.0),
        'norm_mix_g': 1.0 + nrm((DEPTH, D), 0.1),
        'w_ada': nrm((DEPTH, D, 6 * D), 0.5 * D ** -0.5),
        'b_ada': nrm((DEPTH, 6 * D), 0.02),
        'w_in': nrm((DEPTH, D, D_IN), D ** -0.5),
        'conv_w': nrm((DEPTH, CONV_W, D_LRU), CONV_W ** -0.5),
        'conv_b': nrm((DEPTH, D_LRU), 0.02),
        'lru_wa': nrm((DEPTH, 2, N_LRU_HEADS, LRU_HEAD_DIM, LRU_HEAD_DIM), LRU_HEAD_DIM ** -0.5),
        'lru_ba': nrm((DEPTH, 2, D_LRU), 0.02),
        'lru_wx': nrm((DEPTH, 2, N_LRU_HEADS, LRU_HEAD_DIM, LRU_HEAD_DIM), LRU_HEAD_DIM ** -0.5),
        'lru_bx': nrm((DEPTH, 2, D_LRU), 0.02),
        'lru_lam': jnp.log(u) - jnp.log1p(-u),
        'pool_w': nrm((DEPTH, N_POOL_GROUPS, POOL_CH, POOL_CH), POOL_CH ** -0.5),
        'pool_b': nrm((DEPTH, D_POOL), 0.02),
        'pool_scale': 1.0 + nrm((DEPTH, D_POOL), 0.1),
        'w_out': nrm((DEPTH, D, D), D ** -0.5),
        'norm_ffn_g': 1.0 + nrm((DEPTH, D), 0.1),
        'ffn_wg': nrm((N_DENSE, D, D_FF), D ** -0.5),
        'ffn_wu': nrm((N_DENSE, D, D_FF), D ** -0.5),
        'ffn_wd': nrm((N_DENSE, D_FF, D), D_FF ** -0.5),
        'moe_router_w': nrm((N_MOE, D, N_EXPERTS), D ** -0.5),
        'moe_router_b': nrm((N_MOE, N_EXPERTS), 0.01),
        'moe_wg': nrm((N_MOE, N_EXPERTS, D, D_FF_EXPERT), D ** -0.5),
        'moe_wu': nrm((N_MOE, N_EXPERTS, D, D_FF_EXPERT), D ** -0.5),
        'moe_wd': nrm((N_MOE, N_EXPERTS, D_FF_EXPERT, D), D_FF_EXPERT ** -0.5),
        'norm_final_g': 1.0 + nrm((D,), 0.1),
    }


def reference(x_prompt, x_sample, state_lru, c, c_ctx, norm_mix_g, w_ada, b_ada, w_in, conv_w, conv_b,
              lru_wa, lru_ba, lru_wx, lru_bx, lru_lam, pool_w, pool_b, pool_scale, w_out, norm_ffn_g,
              ffn_wg, ffn_wu, ffn_wd, moe_router_w, moe_router_b, moe_wg, moe_wu, moe_wd, norm_final_g):
    rows = x_sample.shape[1] // GRID_W
    xs = x_sample + pos_embed_2d(rows, x_sample.dtype)[None]
    xp = x_prompt
    cond_ctx = c_ctx[None, :]
    states = []
    for l in range(DEPTH):
        j = l // 2
        if l % 2 == 0:
            ffn = functools.partial(swiglu, wg=ffn_wg[j], wu=ffn_wu[j], wd=ffn_wd[j])
        else:
            ffn = functools.partial(moe_swiglu, router_w=moe_router_w[j], router_b=moe_router_b[j],
                                    wg=moe_wg[j], wu=moe_wu[j], wd=moe_wd[j])
        mix = (norm_mix_g[l], w_ada[l], b_ada[l], w_in[l], conv_w[l], conv_b[l],
               lru_wa[l], lru_ba[l], lru_wx[l], lru_bx[l], lru_lam[l],
               pool_w[l], pool_b[l], pool_scale[l], w_out[l], norm_ffn_g[l])
        xp, h_ctx = trunk_layer(xp, cond_ctx, None, True, ffn, *mix)
        states.append(h_ctx)
        xs, _ = trunk_layer(xs, c, state_lru[:, l], False, ffn, *mix)
    y_prompt = rmsnorm(xp, norm_final_g)
    y_sample = rmsnorm(xs, norm_final_g)
    new_state_lru = jnp.stack(states, axis=1)
    return (y_prompt, y_sample, new_state_lru)
```

```python
import functools

import jax
import jax.numpy as jnp
from jax import lax
from jax.experimental import pallas as pl
from jax.experimental.pallas import tpu as pltpu

F32 = jnp.float32
BF16 = jnp.bfloat16

SUB = 8
LANES = 128
GRID_W = 64
TM = GRID_W * SUB
POS_BASE = 10000.0
N_LRU_HEADS = 8
CONV_W = 4
CONV_LEFT = CONV_W // 2
CONV_RIGHT = CONV_W - 1 - CONV_LEFT
LRU_C = 8.0
POOL_WINDOWS = (2, 4, 8, 16)
POOL_HALO = 8 * SUB
N_EXPERTS = 8
EPS = 1e-6
FF_CHUNK = 256
VMEM_LIMIT = 56 * 1024 * 1024


def _cparams(sem):
    return pltpu.CompilerParams(dimension_semantics=sem, vmem_limit_bytes=VMEM_LIMIT)


def _const_spec(shape):
    nd = len(shape)
    return pl.BlockSpec(shape, lambda *_: (0,) * nd, pipeline_mode=pl.Buffered(1))


def _rms(x, g):
    ms = jnp.mean(x * x, axis=-1, keepdims=True)
    return x * lax.rsqrt(ms + EPS) * g


def _per_seq(x, v, op):
    r, c = x.shape
    x3 = x.reshape(r // SUB, SUB, c)
    return op(x3, v[None]).reshape(r, c)


def _modulate(h, scale, shift):
    r, c = h.shape
    h3 = h.reshape(r // SUB, SUB, c)
    return (h3 * (1.0 + scale)[None] + shift[None]).reshape(r, c)


def _gated_add(x, gate, y):
    return x + _per_seq(y, gate, lambda a, b: a * b)


def _dot(a, b):
    return jnp.dot(a, b, preferred_element_type=F32)


def _ada_kernel(c_ref, w_ref, b_ref, o_ref):
    c = c_ref[...]
    s = (c * jax.nn.sigmoid(c)).astype(BF16)
    o_ref[0] = _dot(s, w_ref[0].astype(BF16)) + b_ref[0]


def _ada_call(cond, w_ada, b_ada):
    depth, d, d6 = w_ada.shape
    nr = cond.shape[0]
    bn = d6 // 4
    return pl.pallas_call(
        _ada_kernel,
        grid=(depth, d6 // bn),
        in_specs=[
            pl.BlockSpec((nr, d), lambda l, n: (0, 0)),
            pl.BlockSpec((1, d, bn), lambda l, n: (l, 0, n)),
            pl.BlockSpec((1, 1, bn), lambda l, n: (l, 0, n)),
        ],
        out_specs=pl.BlockSpec((1, nr, bn), lambda l, n: (l, 0, n)),
        out_shape=jax.ShapeDtypeStruct((depth, nr, d6), F32),
        compiler_params=_cparams(("parallel", "parallel")),
        name="ada_mod",
    )(cond, w_ada, b_ada.reshape(depth, 1, d6))


def _inproj_kernel(with_pos, *refs):
    if with_pos:
        x_ref, row_ref, col_ref, mod_ref, g_ref, w_ref, o_ref, x0_ref = refs
        x = x_ref[...]
        half = x.shape[1] // 2
        x = jnp.concatenate([x[:, :half] + row_ref[0], x[:, half:] + col_ref[0]], axis=1)
        x0_ref[...] = x
    else:
        x_ref, mod_ref, g_ref, w_ref, o_ref = refs
        x = x_ref[...]
    h = _modulate(_rms(x, g_ref[...]), mod_ref[0, 1], mod_ref[0, 0])
    o_ref[...] = _dot(h.astype(BF16), w_ref[...])


def _mod_spec(cfg):
    n_ctx_tiles = cfg[0]
    d = cfg[3]
    return pl.BlockSpec((1, 6, SUB, d), lambda j: (jnp.where(j >= n_ctx_tiles, 1, 0), 0, 0, 0))


def _inproj_call(cfg, x, mod_l, g, w_in, pos=None):
    n_ctx_tiles, _, n_tiles, d = cfg
    d_in = w_in.shape[1]
    row_spec = pl.BlockSpec((TM, d), lambda j: (j, 0))
    in_specs = [row_spec]
    args = [x]
    out_specs = [pl.BlockSpec((TM, d_in), lambda j: (j, 0))]
    out_shape = [jax.ShapeDtypeStruct((n_tiles * TM, d_in), F32)]
    if pos is not None:
        row_tab, col_tab = pos
        in_specs += [
            pl.BlockSpec((1, 1, d // 2), lambda j: (j, 0, 0)),
            pl.BlockSpec((1, TM, d // 2), lambda j: (jnp.where(j >= n_ctx_tiles, 1, 0), 0, 0)),
        ]
        args += [row_tab, col_tab]
        out_specs.append(row_spec)
        out_shape.append(jax.ShapeDtypeStruct((n_tiles * TM, d), F32))
    in_specs += [_mod_spec(cfg), _const_spec((1, d)), _const_spec((d, d_in))]
    args += [mod_l, g.reshape(1, d), w_in]
    return pl.pallas_call(
        functools.partial(_inproj_kernel, pos is not None),
        grid=(n_tiles,),
        in_specs=in_specs,
        out_specs=out_specs,
        out_shape=out_shape,
        compiler_params=_cparams(("parallel",)),
        name="in_proj",
    )(*args)


def _seq_flags(cfg, j):
    n_ctx_tiles, tiles_per_ctx, n_tiles, _ = cfg
    is_ctx = j < n_ctx_tiles
    pos = lax.rem(j, tiles_per_ctx)
    first = jnp.where(is_ctx, pos == 0, j == n_ctx_tiles)
    last = jnp.where(is_ctx, pos == tiles_per_ctx - 1, j == n_tiles - 1)
    return is_ctx, pos, first, last


def _conv(pad_ref, xa, prev, nxt, first, last, cw, cb):
    lo = CONV_LEFT * SUB
    pad_ref[0:lo, :] = jnp.where(first, 0.0, prev)
    pad_ref[lo:lo + TM, :] = xa
    pad_ref[lo + TM:lo + TM + CONV_RIGHT * SUB, :] = jnp.where(last, 0.0, nxt)
    y = cb
    for k in range(CONV_W):
        y = y + pad_ref[k * SUB:k * SUB + TM, :] * cw[k:k + 1, :]
    return y


def _gates(xc, wg_ref, ba, bx, lam):
    half = xc.shape[1] // 2
    xb = xc.astype(BF16)
    g0 = _dot(xb[:, :half], wg_ref[0])
    g1 = _dot(xb[:, half:], wg_ref[1])
    r = jax.nn.sigmoid(jnp.concatenate([g0[:, :half], g1[:, :half]], axis=1) + ba)
    i = jax.nn.sigmoid(jnp.concatenate([g0[:, half:], g1[:, half:]], axis=1) + bx)
    z = -lam
    softplus = jnp.maximum(z, 0.0) + jnp.log1p(jnp.exp(-jnp.abs(z)))
    log_a = (-LRU_C) * r * softplus
    a = jnp.exp(log_a)
    u = jnp.sqrt(jnp.tanh(-log_a) * (a * a + 1.0)) * (i * xc)
    return a, u


def _scan(a_ref, u_ref, h_ref, h, reverse):
    steps = TM // SUB

    def body(k, h):
        t = steps - 1 - k if reverse else k
        r0 = pl.multiple_of(t * SUB, SUB)
        h = a_ref[pl.ds(r0, SUB), :] * h + u_ref[pl.ds(r0, SUB), :]
        h_ref[pl.ds(r0, SUB), :] = h
        return h

    return lax.fori_loop(0, steps, body, h, unroll=8)


def _scan_fwd_kernel(cfg, xa_ref, xp_ref, xn_ref, cw_ref, cb_ref, wg_ref, ba_ref, bx_ref,
                     lam_ref, h0_ref, hf_ref, st_ref, pad_ref, a_ref, u_ref, carry_ref):
    j = pl.program_id(0)
    _, _, first, last = _seq_flags(cfg, j)
    xc = _conv(pad_ref, xa_ref[...], xp_ref[...], xn_ref[...], first, last, cw_ref[...], cb_ref[...])
    a, u = _gates(xc, wg_ref, ba_ref[...], bx_ref[...], lam_ref[...])
    a_ref[...] = a
    u_ref[...] = u

    @pl.when(first)
    def _():
        carry_ref[...] = h0_ref[0]

    h = _scan(a_ref, u_ref, hf_ref, carry_ref[...], reverse=False)
    carry_ref[...] = h
    st_ref[0] = h


def _halo_specs(cfg, col, rows_prev, rows_next, tile_of):
    n_tiles = cfg[2]
    c = 512
    nb_prev = TM // rows_prev
    nb_next = TM // rows_next
    last_next = n_tiles * nb_next - 1
    prev = pl.BlockSpec((rows_prev, c), lambda j: (jnp.maximum(tile_of(j) * nb_prev - 1, 0), col))
    nxt = pl.BlockSpec((rows_next, c), lambda j: (jnp.minimum((tile_of(j) + 1) * nb_next, last_next), col))
    return prev, nxt


def _state_spec(cfg, tile_of):
    n_ctx_tiles, tiles_per_ctx, _, _ = cfg
    n_groups = n_ctx_tiles // tiles_per_ctx
    c = 512
    return pl.BlockSpec((1, SUB, c), lambda j: (jnp.minimum(tile_of(j) // tiles_per_ctx, n_groups), 0, 0))


def _h0_spec(cfg, tile_of):
    n_ctx_tiles = cfg[0]
    return pl.BlockSpec((1, SUB, 512), lambda j: (jnp.where(tile_of(j) >= n_ctx_tiles, 1, 0), 0, 0))


def _scan_fwd_call(cfg, proj, conv_w, conv_b, wg, ba, bx, lam, h0):
    n_ctx_tiles, tiles_per_ctx, n_tiles, _ = cfg
    n_groups = n_ctx_tiles // tiles_per_ctx
    c = conv_w.shape[1]
    ident = lambda j: j
    xp_spec, xn_spec = _halo_specs(cfg, 0, CONV_LEFT * SUB, CONV_RIGHT * SUB, ident)
    return pl.pallas_call(
        functools.partial(_scan_fwd_kernel, cfg),
        grid=(n_tiles,),
        in_specs=[
            pl.BlockSpec((TM, c), lambda j: (j, 0)), xp_spec, xn_spec,
            _const_spec((CONV_W, c)), _const_spec((1, c)), _const_spec((2, c // 2, c)),
            _const_spec((1, c)), _const_spec((1, c)), _const_spec((1, c)),
            _h0_spec(cfg, ident),
        ],
        out_specs=[pl.BlockSpec((TM, c), lambda j: (j, 0)), _state_spec(cfg, ident)],
        out_shape=[jax.ShapeDtypeStruct((n_tiles * TM, c), F32),
                   jax.ShapeDtypeStruct((n_groups + 1, SUB, c), F32)],
        scratch_shapes=[
            pltpu.VMEM((TM + (CONV_W - 1) * SUB, c), F32),
            pltpu.VMEM((TM, c), F32), pltpu.VMEM((TM, c), F32), pltpu.VMEM((SUB, c), F32),
        ],
        compiler_params=_cparams(("arbitrary",)),
        name="lru_fwd",
    )(proj, proj, proj, conv_w, conv_b.reshape(1, c), wg, ba.reshape(1, c), bx.reshape(1, c),
      lam.reshape(1, c), h0)


def _pool_mix(pad_ref, t0, t_len, pw_ref, pb, ps):
    c = pad_ref.shape[1]
    gw = c // len(POOL_WINDOWS)
    t_pos = lax.shift_right_logical(lax.broadcasted_iota(jnp.int32, (TM, gw), 0), 3) + t0
    outs = []
    for g, k in enumerate(POOL_WINDOWS):
        left = k // 2
        right = k - 1 - left
        s = None
        for o in range(-left, right + 1):
            v = pad_ref[POOL_HALO + o * SUB:POOL_HALO + o * SUB + TM, g * gw:(g + 1) * gw]
            s = v if s is None else s + v
        cnt = (jnp.minimum(t_pos + right + 1, t_len) - jnp.maximum(t_pos - left, 0)).astype(F32)
        outs.append(s / cnt - pad_ref[POOL_HALO:POOL_HALO + TM, g * gw:(g + 1) * gw])
    d = jnp.concatenate(outs, axis=1).astype(BF16)
    half = c // 2
    y = jnp.concatenate([_dot(d[:, :half], pw_ref[0]), _dot(d[:, half:], pw_ref[1])], axis=1)
    return (y + pb) * ps


def _scan_bwd_kernel(cfg, ctx_len, proj_ref, xp_ref, xn_ref, bp_ref, bn_ref, hf_ref, cw_ref, cb_ref,
                     wg_ref, ba_ref, bx_ref, lam_ref, h0_ref, pw_ref, pb_ref, ps_ref,
                     y_ref, st_ref, pad_ref, ppad_ref, a_ref, u_ref, hb_ref, carry_ref):
    n_tiles = cfg[2]
    j = n_tiles - 1 - pl.program_id(0)
    is_ctx, pos, first, last = _seq_flags(cfg, j)
    c = hf_ref.shape[1]
    xc = _conv(pad_ref, proj_ref[:, 0:c], xp_ref[...], xn_ref[...], first, last, cw_ref[...], cb_ref[...])
    a, u = _gates(xc, wg_ref, ba_ref[...], bx_ref[...], lam_ref[...])
    a_ref[...] = a
    u_ref[...] = u

    @pl.when(last)
    def _():
        carry_ref[...] = h0_ref[0]

    h = _scan(a_ref, u_ref, hb_ref, carry_ref[...], reverse=True)
    carry_ref[...] = h
    st_ref[0] = h

    ga = proj_ref[:, c:2 * c]
    y_a = (hf_ref[...] + hb_ref[...]) * jax.nn.gelu(ga, approximate=True)

    use_prev = jnp.logical_and(is_ctx, jnp.logical_not(first))
    use_next = jnp.logical_and(is_ctx, jnp.logical_not(last))
    ppad_ref[0:POOL_HALO, :] = jnp.where(use_prev, bp_ref[...], 0.0)
    ppad_ref[POOL_HALO:POOL_HALO + TM, :] = proj_ref[:, 2 * c:3 * c]
    ppad_ref[POOL_HALO + TM:, :] = jnp.where(use_next, bn_ref[...], 0.0)
    t0 = jnp.where(is_ctx, pos * GRID_W, 0)
    t_len = jnp.where(is_ctx, ctx_len, GRID_W)
    y_b = _pool_mix(ppad_ref, t0, t_len, pw_ref, pb_ref[...], ps_ref[...])
    y_ref[...] = jnp.concatenate([y_a, y_b], axis=1).astype(BF16)


def _scan_bwd_call(cfg, ctx_len, proj, hf, conv_w, conv_b, wg, ba, bx, lam, h0, pw, pb, ps):
    n_ctx_tiles, tiles_per_ctx, n_tiles, d = cfg
    n_groups = n_ctx_tiles // tiles_per_ctx
    c = conv_w.shape[1]
    rev = lambda j: n_tiles - 1 - j
    xp_spec, xn_spec = _halo_specs(cfg, 0, CONV_LEFT * SUB, CONV_RIGHT * SUB, rev)
    bp_spec, bn_spec = _halo_specs(cfg, 2, POOL_HALO, POOL_HALO, rev)
    return pl.pallas_call(
        functools.partial(_scan_bwd_kernel, cfg, ctx_len),
        grid=(n_tiles,),
        in_specs=[
            pl.BlockSpec((TM, 3 * c), lambda j: (rev(j), 0)), xp_spec, xn_spec, bp_spec, bn_spec,
            pl.BlockSpec((TM, c), lambda j: (rev(j), 0)),
            _const_spec((CONV_W, c)), _const_spec((1, c)), _const_spec((2, c // 2, c)),
            _const_spec((1, c)), _const_spec((1, c)), _const_spec((1, c)),
            _h0_spec(cfg, rev),
            _const_spec((2, c // 2, c // 2)), _const_spec((1, c)), _const_spec((1, c)),
        ],
        out_specs=[pl.BlockSpec((TM, d), lambda j: (rev(j), 0)), _state_spec(cfg, rev)],
        out_shape=[jax.ShapeDtypeStruct((n_tiles * TM, d), BF16),
                   jax.ShapeDtypeStruct((n_groups + 1, SUB, c), F32)],
        scratch_shapes=[
            pltpu.VMEM((TM + (CONV_W - 1) * SUB, c), F32),
            pltpu.VMEM((TM + 2 * POOL_HALO, c), F32),
            pltpu.VMEM((TM, c), F32), pltpu.VMEM((TM, c), F32), pltpu.VMEM((TM, c), F32),
            pltpu.VMEM((SUB, c), F32),
        ],
        compiler_params=_cparams(("arbitrary",)),
        name="lru_bwd_mix",
    )(proj, proj, proj, proj, proj, hf, conv_w, conv_b.reshape(1, c), wg, ba.reshape(1, c),
      bx.reshape(1, c), lam.reshape(1, c), h0, pw, pb.reshape(1, c), ps.reshape(1, c))


def _out_proj(x_ref, ym_ref, mod_ref, wo_ref, g2_ref):
    xm = _gated_add(x_ref[...], mod_ref[0, 2], _dot(ym_ref[...], wo_ref[...]))
    h2 = _modulate(_rms(xm, g2_ref[...]), mod_ref[0, 4], mod_ref[0, 3])
    return xm, h2


def _swiglu_act(h2b, wg_ref, wu_ref, act_ref):
    n_chunks = act_ref.shape[1] // FF_CHUNK

    def body(f, carry):
        c0 = pl.multiple_of(f * FF_CHUNK, FF_CHUNK)
        g = _dot(h2b, wg_ref[:, pl.ds(c0, FF_CHUNK)])
        u = _dot(h2b, wu_ref[:, pl.ds(c0, FF_CHUNK)])
        act_ref[:, pl.ds(c0, FF_CHUNK)] = (g * jax.nn.sigmoid(g) * u).astype(BF16)
        return carry

    lax.fori_loop(0, n_chunks, body, 0)


def _dense_ffn_kernel(x_ref, ym_ref, mod_ref, wo_ref, g2_ref, wg_ref, wu_ref, wd_ref, o_ref, act_ref):
    xm, h2 = _out_proj(x_ref, ym_ref, mod_ref, wo_ref, g2_ref)
    _swiglu_act(h2.astype(BF16), wg_ref, wu_ref, act_ref)
    o_ref[...] = _gated_add(xm, mod_ref[0, 5], _dot(act_ref[...], wd_ref[...]))


def _dense_ffn_call(cfg, x, ymix, mod_l, w_out, g2, wg, wu, wd):
    _, _, n_tiles, d = cfg
    d_ff = wg.shape[1]
    row_spec = pl.BlockSpec((TM, d), lambda j: (j, 0))
    return pl.pallas_call(
        _dense_ffn_kernel,
        grid=(n_tiles,),
        in_specs=[row_spec, row_spec, _mod_spec(cfg), _const_spec((d, d)), _const_spec((1, d)),
                  _const_spec((d, d_ff)), _const_spec((d, d_ff)), _const_spec((d_ff, d))],
        out_specs=row_spec,
        out_shape=jax.ShapeDtypeStruct((n_tiles * TM, d), F32),
        scratch_shapes=[pltpu.VMEM((TM, d_ff), BF16)],
        compiler_params=_cparams(("parallel",)),
        name="out_proj_dense_ffn",
    )(x, ymix, mod_l, w_out, g2.reshape(1, d), wg, wu, wd)


def _route_kernel(x_ref, ym_ref, mod_ref, wo_ref, g2_ref, rw_ref, rb_ref, xm_ref, h2_ref, cb_ref):
    xm, h2 = _out_proj(x_ref, ym_ref, mod_ref, wo_ref, g2_ref)
    xm_ref[...] = xm
    h2_ref[...] = h2.astype(BF16)
    logits = jnp.dot(h2, rw_ref[...], preferred_element_type=F32, precision=lax.Precision.HIGHEST)
    lane = lax.broadcasted_iota(jnp.int32, logits.shape, 1)
    neg = jnp.float32(-jnp.inf)
    lg = jnp.where(lane < N_EXPERTS, logits + rb_ref[...], neg)
    m1 = jnp.max(lg, axis=1, keepdims=True)
    i1 = jnp.min(jnp.where(lg == m1, lane, LANES), axis=1, keepdims=True)
    lg2 = jnp.where(lane == i1, neg, lg)
    m2 = jnp.max(lg2, axis=1, keepdims=True)
    i2 = jnp.min(jnp.where(lg2 == m2, lane, LANES), axis=1, keepdims=True)
    e2 = jnp.exp(m2 - m1)
    den = 1.0 + e2
    cb_ref[...] = jnp.where(lane == i1, 1.0 / den, 0.0) + jnp.where(lane == i2, e2 / den, 0.0)


def _route_call(cfg, x, ymix, mod_l, w_out, g2, router_w, router_b):
    _, _, n_tiles, d = cfg
    n = n_tiles * TM
    row_spec = pl.BlockSpec((TM, d), lambda j: (j, 0))
    rw = jnp.zeros((d, LANES), F32).at[:, :N_EXPERTS].set(router_w)
    rb = jnp.zeros((1, LANES), F32).at[0, :N_EXPERTS].set(router_b)
    return pl.pallas_call(
        _route_kernel,
        grid=(n_tiles,),
        in_specs=[row_spec, row_spec, _mod_spec(cfg), _const_spec((d, d)), _const_spec((1, d)),
                  _const_spec((d, LANES)), _const_spec((1, LANES))],
        out_specs=[row_spec, row_spec, pl.BlockSpec((TM, LANES), lambda j: (j, 0))],
        out_shape=[jax.ShapeDtypeStruct((n, d), F32), jax.ShapeDtypeStruct((n, d), BF16),
                   jax.ShapeDtypeStruct((n, LANES), F32)],
        compiler_params=_cparams(("parallel",)),
        name="out_proj_route",
    )(x, ymix, mod_l, w_out, g2.reshape(1, d), rw, rb)


def _moe_kernel(xm_ref, h2_ref, cb_ref, mod_ref, wg_ref, wu_ref, wd_ref, o_ref, acc_ref):
    e = pl.program_id(1)

    @pl.when(e == 0)
    def _():
        acc_ref[...] = jnp.zeros_like(acc_ref)

    h2 = h2_ref[...]
    g = _dot(h2, wg_ref[0])
    u = _dot(h2, wu_ref[0])
    act = (g * jax.nn.sigmoid(g) * u).astype(BF16)
    y = _dot(act, wd_ref[0])
    cb = cb_ref[...]
    lane = lax.broadcasted_iota(jnp.int32, cb.shape, 1)
    w = jnp.sum(jnp.where(lane == e, cb, 0.0), axis=1, keepdims=True)
    acc_ref[...] += w * y

    @pl.when(e == N_EXPERTS - 1)
    def _():
        o_ref[...] = _gated_add(xm_ref[...], mod_ref[0, 5], acc_ref[...])


def _moe_call(cfg, xm, h2, comb, mod_l, wg, wu, wd):
    _, _, n_tiles, d = cfg
    n_ctx_tiles = cfg[0]
    dfe = wg.shape[2]
    row_spec = pl.BlockSpec((TM, d), lambda j, e: (j, 0))
    return pl.pallas_call(
        _moe_kernel,
        grid=(n_tiles, N_EXPERTS),
        in_specs=[row_spec, row_spec, pl.BlockSpec((TM, LANES), lambda j, e: (j, 0)),
                  pl.BlockSpec((1, 6, SUB, d), lambda j, e: (jnp.where(j >= n_ctx_tiles, 1, 0), 0, 0, 0)),
                  pl.BlockSpec((1, d, dfe), lambda j, e: (e, 0, 0)),
                  pl.BlockSpec((1, d, dfe), lambda j, e: (e, 0, 0)),
                  pl.BlockSpec((1, dfe, d), lambda j, e: (e, 0, 0))],
        out_specs=row_spec,
        out_shape=jax.ShapeDtypeStruct((n_tiles * TM, d), F32),
        scratch_shapes=[pltpu.VMEM((TM, d), F32)],
        compiler_params=_cparams(("parallel", "arbitrary")),
        name="moe_experts",
    )(xm, h2, comb, mod_l, wg, wu, wd)


def _final_norm_kernel(x_ref, g_ref, o_ref):
    o_ref[...] = _rms(x_ref[...], g_ref[...])


def _final_norm_call(cfg, x, g):
    _, _, n_tiles, d = cfg
    row_spec = pl.BlockSpec((TM, d), lambda j: (j, 0))
    return pl.pallas_call(
        _final_norm_kernel,
        grid=(n_tiles,),
        in_specs=[row_spec, _const_spec((1, d))],
        out_specs=row_spec,
        out_shape=jax.ShapeDtypeStruct((n_tiles * TM, d), F32),
        compiler_params=_cparams(("parallel",)),
        name="final_norm",
    )(x, g.reshape(1, d))


def _block_diag(w, per_block):
    *lead, n, k, _ = w.shape
    nb = n // per_block
    w = w.reshape(*lead, nb, per_block, k, k)
    eye = jnp.eye(per_block, dtype=w.dtype)
    out = w[..., :, :, None, :] * eye[:, None, :, None]
    return out.reshape(*lead, nb, per_block * k, per_block * k)


def _pos_tables(n_ctx_tiles, n_lat_tiles, d):
    quarter = d // 4
    omega = 1.0 / (POS_BASE ** (jnp.arange(quarter, dtype=F32) / quarter))
    er = jnp.arange(n_lat_tiles, dtype=F32)[:, None] * omega
    ec = jnp.arange(GRID_W, dtype=F32)[:, None] * omega
    row_emb = jnp.concatenate([jnp.sin(er), jnp.cos(er)], axis=-1)
    col_emb = jnp.concatenate([jnp.sin(ec), jnp.cos(ec)], axis=-1)
    row_tab = jnp.concatenate([jnp.zeros((n_ctx_tiles, d // 2), F32), row_emb], axis=0)[:, None, :]
    col_rep = jnp.repeat(col_emb, SUB, axis=0)
    col_tab = jnp.stack([jnp.zeros_like(col_rep), col_rep], axis=0)
    return row_tab, col_tab


def kernel(x_prompt, x_sample, state_lru, c, c_ctx, norm_mix_g, w_ada, b_ada, w_in, conv_w, conv_b, lru_wa, lru_ba, lru_wx, lru_bx, lru_lam, pool_w, pool_b, pool_scale, w_out, norm_ffn_g, ffn_wg, ffn_wu, ffn_wd, moe_router_w, moe_router_b, moe_wg, moe_wu, moe_wd, norm_final_g):
    bc, tc, d = x_prompt.shape
    bl, tl, _ = x_sample.shape
    depth = w_ada.shape[0]
    c_lru = conv_w.shape[-1]
    assert bl == SUB and bc % SUB == 0 and tc % GRID_W == 0 and tl % GRID_W == 0
    assert c_lru == 4 * LANES and d == 2 * c_lru
    n_groups = bc // SUB
    tiles_per_ctx = tc // GRID_W
    n_ctx_tiles = n_groups * tiles_per_ctx
    n_lat_tiles = tl // GRID_W
    n_tiles = n_ctx_tiles + n_lat_tiles
    n_ctx = n_ctx_tiles * TM
    cfg = (n_ctx_tiles, tiles_per_ctx, n_tiles, d)

    xp = x_prompt.reshape(n_groups, SUB, tc, d).transpose(0, 2, 1, 3).reshape(n_ctx, d)
    xs = x_sample.transpose(1, 0, 2).reshape(tl * SUB, d)
    x = jnp.concatenate([xp, xs], axis=0)

    cond = jnp.concatenate([c, c_ctx[None], jnp.zeros((SUB - 1, d), F32)], axis=0)
    mod = _ada_call(cond, w_ada, b_ada)
    mod_lat = mod[:, :SUB].reshape(depth, SUB, 6, d).transpose(0, 2, 1, 3)
    mod_ctx = jnp.broadcast_to(mod[:, SUB].reshape(depth, 6, 1, d), (depth, 6, SUB, d))
    mod = jnp.stack([mod_ctx, mod_lat], axis=1)

    pos = _pos_tables(n_ctx_tiles, n_lat_tiles, d)
    wg_lru = jnp.concatenate([_block_diag(lru_wa, 4), _block_diag(lru_wx, 4)], axis=-1).astype(BF16)
    pw = _block_diag(pool_w, 2).astype(BF16)
    h0 = jnp.concatenate([jnp.zeros_like(state_lru[None]), state_lru[None]], axis=0)

    states = []
    for l in range(depth):
        jdx = l // 2
        if l == 0:
            proj, x = _inproj_call(cfg, x, mod[l], norm_mix_g[l], w_in[l].astype(BF16), pos)
        else:
            (proj,) = _inproj_call(cfg, x, mod[l], norm_mix_g[l], w_in[l].astype(BF16))
        hf, st_f = _scan_fwd_call(cfg, proj, conv_w[l], conv_b[l], wg_lru[l, 0], lru_ba[l, 0],
                                  lru_bx[l, 0], lru_lam[l, 0], h0[:, :, l, 0])
        ymix, st_b = _scan_bwd_call(cfg, tc, proj, hf, conv_w[l], conv_b[l], wg_lru[l, 1], lru_ba[l, 1],
                                    lru_bx[l, 1], lru_lam[l, 1], h0[:, :, l, 1], pw[l], pool_b[l],
                                    pool_scale[l])
        states.append(jnp.stack([st_f[:n_groups].reshape(bc, c_lru), st_b[:n_groups].reshape(bc, c_lru)], axis=1))
        if l % 2 == 0:
            x = _dense_ffn_call(cfg, x, ymix, mod[l], w_out[l].astype(BF16), norm_ffn_g[l],
                                ffn_wg[jdx].astype(BF16), ffn_wu[jdx].astype(BF16), ffn_wd[jdx].astype(BF16))
        else:
            xm, h2, comb = _route_call(cfg, x, ymix, mod[l], w_out[l].astype(BF16), norm_ffn_g[l],
                                       moe_router_w[jdx], moe_router_b[jdx])
            x = _moe_call(cfg, xm, h2, comb, mod[l], moe_wg[jdx].astype(BF16), moe_wu[jdx].astype(BF16),
                          moe_wd[jdx].astype(BF16))

    y = _final_norm_call(cfg, x, norm_final_g)
    y_prompt = y[:n_ctx].reshape(n_groups, tc, SUB, d).transpose(0, 2, 1, 3).reshape(bc, tc, d)
    y_sample = y[n_ctx:].reshape(tl, SUB, d).transpose(1, 0, 2)
    new_state = jnp.stack(states, axis=1)
    return (y_prompt, y_sample, new_state)
```

```python
import functools

import jax
import jax.numpy as jnp
from jax import lax
from jax.experimental import pallas as pl
from jax.experimental.pallas import tpu as pltpu

F32 = jnp.float32
BF16 = jnp.bfloat16

SUB = 8
LANES = 128
GRID_W = 64
TM = GRID_W * SUB
POS_BASE = 10000.0
N_LRU_HEADS = 8
CONV_W = 4
CONV_LEFT = CONV_W // 2
CONV_RIGHT = CONV_W - 1 - CONV_LEFT
LRU_C = 8.0
POOL_WINDOWS = (2, 4, 8, 16)
POOL_HALO = 8 * SUB
N_EXPERTS = 8
EPS = 1e-6
FF_CHUNK = 256
VMEM_LIMIT = 56 * 1024 * 1024


def _cparams(sem):
    return pltpu.CompilerParams(dimension_semantics=sem, vmem_limit_bytes=VMEM_LIMIT)


def _const_spec(shape):
    nd = len(shape)
    return pl.BlockSpec(shape, lambda *_: (0,) * nd, pipeline_mode=pl.Buffered(1))


def _rms(x, g):
    ms = jnp.mean(x * x, axis=-1, keepdims=True)
    return x * lax.rsqrt(ms + EPS) * g


def _per_seq(x, v, op):
    r, c = x.shape
    x3 = x.reshape(r // SUB, SUB, c)
    return op(x3, v[None]).reshape(r, c)


def _modulate(h, scale, shift):
    r, c = h.shape
    h3 = h.reshape(r // SUB, SUB, c)
    return (h3 * (1.0 + scale)[None] + shift[None]).reshape(r, c)


def _gated_add(x, gate, y):
    return x + _per_seq(y, gate, lambda a, b: a * b)


def _dot(a, b):
    return jnp.dot(a, b, preferred_element_type=F32)


def _ada_kernel(c_ref, w_ref, b_ref, o_ref):
    c = c_ref[...]
    s = (c * jax.nn.sigmoid(c)).astype(BF16)
    o_ref[0] = _dot(s, w_ref[0].astype(BF16)) + b_ref[0]


def _ada_call(cond, w_ada, b_ada):
    depth, d, d6 = w_ada.shape
    nr = cond.shape[0]
    bn = d6 // 4
    return pl.pallas_call(
        _ada_kernel,
        grid=(depth, d6 // bn),
        in_specs=[
            pl.BlockSpec((nr, d), lambda l, n: (0, 0)),
            pl.BlockSpec((1, d, bn), lambda l, n: (l, 0, n)),
            pl.BlockSpec((1, 1, bn), lambda l, n: (l, 0, n)),
        ],
        out_specs=pl.BlockSpec((1, nr, bn), lambda l, n: (l, 0, n)),
        out_shape=jax.ShapeDtypeStruct((depth, nr, d6), F32),
        compiler_params=_cparams(("parallel", "parallel")),
        name="ada_mod",
    )(cond, w_ada, b_ada.reshape(depth, 1, d6))


def _inproj_kernel(with_pos, *refs):
    if with_pos:
        x_ref, row_ref, col_ref, mod_ref, g_ref, w_ref, o_ref, x0_ref = refs
        x = x_ref[...]
        half = x.shape[1] // 2
        x = jnp.concatenate([x[:, :half] + row_ref[0], x[:, half:] + col_ref[0]], axis=1)
        x0_ref[...] = x
    else:
        x_ref, mod_ref, g_ref, w_ref, o_ref = refs
        x = x_ref[...]
    h = _modulate(_rms(x, g_ref[...]), mod_ref[0, 1], mod_ref[0, 0])
    o_ref[...] = _dot(h.astype(BF16), w_ref[...])


def _mod_spec(cfg):
    n_ctx_tiles = cfg[0]
    d = cfg[3]
    return pl.BlockSpec((1, 6, SUB, d), lambda j: (jnp.where(j >= n_ctx_tiles, 1, 0), 0, 0, 0))


def _inproj_call(cfg, x, mod_l, g, w_in, pos=None):
    n_ctx_tiles, _, n_tiles, d = cfg
    d_in = w_in.shape[1]
    row_spec = pl.BlockSpec((TM, d), lambda j: (j, 0))
    in_specs = [row_spec]
    args = [x]
    out_specs = [pl.BlockSpec((TM, d_in), lambda j: (j, 0))]
    out_shape = [jax.ShapeDtypeStruct((n_tiles * TM, d_in), F32)]
    if pos is not None:
        row_tab, col_tab = pos
        in_specs += [
            pl.BlockSpec((1, 1, d // 2), lambda j: (j, 0, 0)),
            pl.BlockSpec((1, TM, d // 2), lambda j: (jnp.where(j >= n_ctx_tiles, 1, 0), 0, 0)),
        ]
        args += [row_tab, col_tab]
        out_specs.append(row_spec)
        out_shape.append(jax.ShapeDtypeStruct((n_tiles * TM, d), F32))
    in_specs += [_mod_spec(cfg), _const_spec((1, d)), _const_spec((d, d_in))]
    args += [mod_l, g.reshape(1, d), w_in]
    return pl.pallas_call(
        functools.partial(_inproj_kernel, pos is not None),
        grid=(n_tiles,),
        in_specs=in_specs,
        out_specs=out_specs,
        out_shape=out_shape,
        compiler_params=_cparams(("parallel",)),
        name="in_proj",
    )(*args)


def _seq_flags(cfg, j):
    n_ctx_tiles, tiles_per_ctx, n_tiles, _ = cfg
    is_ctx = j < n_ctx_tiles
    pos = lax.rem(j, tiles_per_ctx)
    first = jnp.where(is_ctx, pos == 0, j == n_ctx_tiles)
    last = jnp.where(is_ctx, pos == tiles_per_ctx - 1, j == n_tiles - 1)
    return is_ctx, pos, first, last


def _conv(pad_ref, xa, prev, nxt, first, last, cw, cb):
    lo = CONV_LEFT * SUB
    pad_ref[0:lo, :] = jnp.where(first, 0.0, prev)
    pad_ref[lo:lo + TM, :] = xa
    pad_ref[lo + TM:lo + TM + CONV_RIGHT * SUB, :] = jnp.where(last, 0.0, nxt)
    y = cb
    for k in range(CONV_W):
        y = y + pad_ref[k * SUB:k * SUB + TM, :] * cw[k:k + 1, :]
    return y


def _gates(xc, wg_ref, ba, bx, lam):
    half = xc.shape[1] // 2
    xb = xc.astype(BF16)
    g0 = _dot(xb[:, :half], wg_ref[0])
    g1 = _dot(xb[:, half:], wg_ref[1])
    r = jax.nn.sigmoid(jnp.concatenate([g0[:, :half], g1[:, :half]], axis=1) + ba)
    i = jax.nn.sigmoid(jnp.concatenate([g0[:, half:], g1[:, half:]], axis=1) + bx)
    z = -lam
    softplus = jnp.maximum(z, 0.0) + jnp.log1p(jnp.exp(-jnp.abs(z)))
    log_a = (-LRU_C) * r * softplus
    a = jnp.exp(log_a)
    u = jnp.sqrt(jnp.tanh(-log_a) * (a * a + 1.0)) * (i * xc)
    return a, u


def _scan(a_ref, u_ref, h_ref, h, reverse):
    steps = TM // SUB

    def body(k, h):
        t = steps - 1 - k if reverse else k
        r0 = pl.multiple_of(t * SUB, SUB)
        h = a_ref[pl.ds(r0, SUB), :] * h + u_ref[pl.ds(r0, SUB), :]
        h_ref[pl.ds(r0, SUB), :] = h
        return h

    return lax.fori_loop(0, steps, body, h, unroll=8)


def _scan_fwd_kernel(cfg, xa_ref, xp_ref, xn_ref, cw_ref, cb_ref, wg_ref, ba_ref, bx_ref,
                     lam_ref, h0_ref, hf_ref, st_ref, pad_ref, a_ref, u_ref, carry_ref):
    j = pl.program_id(0)
    _, _, first, last = _seq_flags(cfg, j)
    xc = _conv(pad_ref, xa_ref[...], xp_ref[...], xn_ref[...], first, last, cw_ref[...], cb_ref[...])
    a, u = _gates(xc, wg_ref, ba_ref[...], bx_ref[...], lam_ref[...])
    a_ref[...] = a
    u_ref[...] = u

    @pl.when(first)
    def _():
        carry_ref[...] = h0_ref[0]

    h = _scan(a_ref, u_ref, hf_ref, carry_ref[...], reverse=False)
    carry_ref[...] = h
    st_ref[0] = h


def _halo_specs(cfg, col, rows_prev, rows_next, tile_of):
    n_tiles = cfg[2]
    c = 512
    nb_prev = TM // rows_prev
    nb_next = TM // rows_next
    last_next = n_tiles * nb_next - 1
    prev = pl.BlockSpec((rows_prev, c), lambda j: (jnp.maximum(tile_of(j) * nb_prev - 1, 0), col))
    nxt = pl.BlockSpec((rows_next, c), lambda j: (jnp.minimum((tile_of(j) + 1) * nb_next, last_next), col))
    return prev, nxt


def _state_spec(cfg, tile_of):
    n_ctx_tiles, tiles_per_ctx, _, _ = cfg
    n_groups = n_ctx_tiles // tiles_per_ctx
    c = 512
    return pl.BlockSpec((1, SUB, c), lambda j: (jnp.minimum(tile_of(j) // tiles_per_ctx, n_groups), 0, 0))


def _h0_spec(cfg, tile_of):
    n_ctx_tiles = cfg[0]
    return pl.BlockSpec((1, SUB, 512), lambda j: (jnp.where(tile_of(j) >= n_ctx_tiles, 1, 0), 0, 0))


def _scan_fwd_call(cfg, proj, conv_w, conv_b, wg, ba, bx, lam, h0):
    n_ctx_tiles, tiles_per_ctx, n_tiles, _ = cfg
    n_groups = n_ctx_tiles // tiles_per_ctx
    c = conv_w.shape[1]
    ident = lambda j: j
    xp_spec, xn_spec = _halo_specs(cfg, 0, CONV_LEFT * SUB, CONV_RIGHT * SUB, ident)
    return pl.pallas_call(
        functools.partial(_scan_fwd_kernel, cfg),
        grid=(n_tiles,),
        in_specs=[
            pl.BlockSpec((TM, c), lambda j: (j, 0)), xp_spec, xn_spec,
            _const_spec((CONV_W, c)), _const_spec((1, c)), _const_spec((2, c // 2, c)),
            _const_spec((1, c)), _const_spec((1, c)), _const_spec((1, c)),
            _h0_spec(cfg, ident),
        ],
        out_specs=[pl.BlockSpec((TM, c), lambda j: (j, 0)), _state_spec(cfg, ident)],
        out_shape=[jax.ShapeDtypeStruct((n_tiles * TM, c), F32),
                   jax.ShapeDtypeStruct((n_groups + 1, SUB, c), F32)],
        scratch_shapes=[
            pltpu.VMEM((TM + (CONV_W - 1) * SUB, c), F32),
            pltpu.VMEM((TM, c), F32), pltpu.VMEM((TM, c), F32), pltpu.VMEM((SUB, c), F32),
        ],
        compiler_params=_cparams(("arbitrary",)),
        name="lru_fwd",
    )(proj, proj, proj, conv_w, conv_b.reshape(1, c), wg, ba.reshape(1, c), bx.reshape(1, c),
      lam.reshape(1, c), h0)


def _pool_mix(pad_ref, t0, t_len, pw_ref, pb, ps):
    c = pad_ref.shape[1]
    gw = c // len(POOL_WINDOWS)
    t_pos = lax.shift_right_logical(lax.broadcasted_iota(jnp.int32, (TM, gw), 0), 3) + t0
    outs = []
    for g, k in enumerate(POOL_WINDOWS):
        left = k // 2
        right = k - 1 - left
        s = None
        for o in range(-left, right + 1):
            v = pad_ref[POOL_HALO + o * SUB:POOL_HALO + o * SUB + TM, g * gw:(g + 1) * gw]
            s = v if s is None else s + v
        cnt = (jnp.minimum(t_pos + right + 1, t_len) - jnp.maximum(t_pos - left, 0)).astype(F32)
        outs.append(s / cnt - pad_ref[POOL_HALO:POOL_HALO + TM, g * gw:(g + 1) * gw])
    d = jnp.concatenate(outs, axis=1).astype(BF16)
    half = c // 2
    y = jnp.concatenate([_dot(d[:, :half], pw_ref[0]), _dot(d[:, half:], pw_ref[1])], axis=1)
    return (y + pb) * ps


def _scan_bwd_kernel(cfg, ctx_len, proj_ref, xp_ref, xn_ref, bp_ref, bn_ref, hf_ref, cw_ref, cb_ref,
                     wg_ref, ba_ref, bx_ref, lam_ref, h0_ref, pw_ref, pb_ref, ps_ref,
                     y_ref, st_ref, pad_ref, ppad_ref, a_ref, u_ref, hb_ref, carry_ref):
    n_tiles = cfg[2]
    j = n_tiles - 1 - pl.program_id(0)
    is_ctx, pos, first, last = _seq_flags(cfg, j)
    c = hf_ref.shape[1]
    xc = _conv(pad_ref, proj_ref[:, 0:c], xp_ref[...], xn_ref[...], first, last, cw_ref[...], cb_ref[...])
    a, u = _gates(xc, wg_ref, ba_ref[...], bx_ref[...], lam_ref[...])
    a_ref[...] = a
    u_ref[...] = u

    @pl.when(last)
    def _():
        carry_ref[...] = h0_ref[0]

    h = _scan(a_ref, u_ref, hb_ref, carry_ref[...], reverse=True)
    carry_ref[...] = h
    st_ref[0] = h

    ga = proj_ref[:, c:2 * c]
    y_a = (hf_ref[...] + hb_ref[...]) * jax.nn.gelu(ga, approximate=True)

    use_prev = jnp.logical_and(is_ctx, jnp.logical_not(first))
    use_next = jnp.logical_and(is_ctx, jnp.logical_not(last))
    ppad_ref[0:POOL_HALO, :] = jnp.where(use_prev, bp_ref[...], 0.0)
    ppad_ref[POOL_HALO:POOL_HALO + TM, :] = proj_ref[:, 2 * c:3 * c]
    ppad_ref[POOL_HALO + TM:, :] = jnp.where(use_next, bn_ref[...], 0.0)
    t0 = jnp.where(is_ctx, pos * GRID_W, 0)
    t_len = jnp.where(is_ctx, ctx_len, GRID_W)
    y_b = _pool_mix(ppad_ref, t0, t_len, pw_ref, pb_ref[...], ps_ref[...])
    y_ref[...] = jnp.concatenate([y_a, y_b], axis=1).astype(BF16)


def _scan_bwd_call(cfg, ctx_len, proj, hf, conv_w, conv_b, wg, ba, bx, lam, h0, pw, pb, ps):
    n_ctx_tiles, tiles_per_ctx, n_tiles, d = cfg
    n_groups = n_ctx_tiles // tiles_per_ctx
    c = conv_w.shape[1]
    rev = lambda j: n_tiles - 1 - j
    xp_spec, xn_spec = _halo_specs(cfg, 0, CONV_LEFT * SUB, CONV_RIGHT * SUB, rev)
    bp_spec, bn_spec = _halo_specs(cfg, 2, POOL_HALO, POOL_HALO, rev)
    return pl.pallas_call(
        functools.partial(_scan_bwd_kernel, cfg, ctx_len),
        grid=(n_tiles,),
        in_specs=[
            pl.BlockSpec((TM, 3 * c), lambda j: (rev(j), 0)), xp_spec, xn_spec, bp_spec, bn_spec,
            pl.BlockSpec((TM, c), lambda j: (rev(j), 0)),
            _const_spec((CONV_W, c)), _const_spec((1, c)), _const_spec((2, c // 2, c)),
            _const_spec((1, c)), _const_spec((1, c)), _const_spec((1, c)),
            _h0_spec(cfg, rev),
            _const_spec((2, c // 2, c // 2)), _const_spec((1, c)), _const_spec((1, c)),
        ],
        out_specs=[pl.BlockSpec((TM, d), lambda j: (rev(j), 0)), _state_spec(cfg, rev)],
        out_shape=[jax.ShapeDtypeStruct((n_tiles * TM, d), BF16),
                   jax.ShapeDtypeStruct((n_groups + 1, SUB, c), F32)],
        scratch_shapes=[
            pltpu.VMEM((TM + (CONV_W - 1) * SUB, c), F32),
            pltpu.VMEM((TM + 2 * POOL_HALO, c), F32),
            pltpu.VMEM((TM, c), F32), pltpu.VMEM((TM, c), F32), pltpu.VMEM((TM, c), F32),
            pltpu.VMEM((SUB, c), F32),
        ],
        compiler_params=_cparams(("arbitrary",)),
        name="lru_bwd_mix",
    )(proj, proj, proj, proj, proj, hf, conv_w, conv_b.reshape(1, c), wg, ba.reshape(1, c),
      bx.reshape(1, c), lam.reshape(1, c), h0, pw, pb.reshape(1, c), ps.reshape(1, c))


def _out_proj(x_ref, ym_ref, mod_ref, wo_ref, g2_ref):
    xm = _gated_add(x_ref[...], mod_ref[0, 2], _dot(ym_ref[...], wo_ref[...]))
    h2 = _modulate(_rms(xm, g2_ref[...]), mod_ref[0, 4], mod_ref[0, 3])
    return xm, h2


def _swiglu_act(h2b, wg_ref, wu_ref, act_ref):
    n_chunks = act_ref.shape[1] // FF_CHUNK

    def body(f, carry):
        c0 = pl.multiple_of(f * FF_CHUNK, FF_CHUNK)
        g = _dot(h2b, wg_ref[:, pl.ds(c0, FF_CHUNK)])
        u = _dot(h2b, wu_ref[:, pl.ds(c0, FF_CHUNK)])
        act_ref[:, pl.ds(c0, FF_CHUNK)] = (g * jax.nn.sigmoid(g) * u).astype(BF16)
        return carry

    lax.fori_loop(0, n_chunks, body, 0)


def _dense_ffn_kernel(x_ref, ym_ref, mod_ref, wo_ref, g2_ref, wg_ref, wu_ref, wd_ref, o_ref, act_ref):
    xm, h2 = _out_proj(x_ref, ym_ref, mod_ref, wo_ref, g2_ref)
    _swiglu_act(h2.astype(BF16), wg_ref, wu_ref, act_ref)
    o_ref[...] = _gated_add(xm, mod_ref[0, 5], _dot(act_ref[...], wd_ref[...]))


def _dense_ffn_call(cfg, x, ymix, mod_l, w_out, g2, wg, wu, wd):
    _, _, n_tiles, d = cfg
    d_ff = wg.shape[1]
    row_spec = pl.BlockSpec((TM, d), lambda j: (j, 0))
    return pl.pallas_call(
        _dense_ffn_kernel,
        grid=(n_tiles,),
        in_specs=[row_spec, row_spec, _mod_spec(cfg), _const_spec((d, d)), _const_spec((1, d)),
                  _const_spec((d, d_ff)), _const_spec((d, d_ff)), _const_spec((d_ff, d))],
        out_specs=row_spec,
        out_shape=jax.ShapeDtypeStruct((n_tiles * TM, d), F32),
        scratch_shapes=[pltpu.VMEM((TM, d_ff), BF16)],
        compiler_params=_cparams(("parallel",)),
        name="out_proj_dense_ffn",
    )(x, ymix, mod_l, w_out, g2.reshape(1, d), wg, wu, wd)


M_E1, M_E2, M_W1, M_W2, M_R1, M_R2 = range(6)


def _route_kernel(x_ref, ym_ref, mod_ref, wo_ref, g2_ref, rw_ref, rb_ref, tri_ref,
                  xm_ref, hp_ref, meta_ref, cnt_ref, run_ref):
    xm, h2 = _out_proj(x_ref, ym_ref, mod_ref, wo_ref, g2_ref)
    xm_ref[...] = xm
    bits = lax.bitcast_convert_type(h2.astype(BF16).astype(F32), jnp.uint32)
    half = bits.shape[1] // 2
    hp_ref[...] = lax.shift_right_logical(bits[:, :half], jnp.uint32(16)) | (bits[:, half:] & jnp.uint32(0xFFFF0000))

    logits = jnp.dot(h2, rw_ref[...], preferred_element_type=F32, precision=lax.Precision.HIGHEST)
    lane = lax.broadcasted_iota(jnp.int32, logits.shape, 1)
    neg = jnp.float32(-jnp.inf)
    lg = jnp.where(lane < N_EXPERTS, logits + rb_ref[...], neg)
    m1 = jnp.max(lg, axis=1, keepdims=True)
    i1 = jnp.min(jnp.where(lg == m1, lane, LANES), axis=1, keepdims=True)
    lg2 = jnp.where(lane == i1, neg, lg)
    m2 = jnp.max(lg2, axis=1, keepdims=True)
    i2 = jnp.min(jnp.where(lg2 == m2, lane, LANES), axis=1, keepdims=True)
    e2 = jnp.exp(m2 - m1)
    den = 1.0 + e2

    @pl.when(pl.program_id(0) == 0)
    def _():
        run_ref[...] = jnp.zeros_like(run_ref)

    sel1 = lane == i1
    sel2 = lane == i2
    onehot = jnp.where(jnp.logical_or(sel1, sel2), 1.0, 0.0)
    rank = _dot(tri_ref[...], onehot.astype(BF16)) + run_ref[...]
    r1 = jnp.sum(jnp.where(sel1, rank, 0.0), axis=1, keepdims=True)
    r2 = jnp.sum(jnp.where(sel2, rank, 0.0), axis=1, keepdims=True)
    run_ref[...] = run_ref[...] + jnp.sum(onehot, axis=0, keepdims=True)
    cnt_ref[...] = run_ref[...]

    meta = jnp.zeros(logits.shape, F32)
    for k, v in ((M_E1, i1.astype(F32)), (M_E2, i2.astype(F32)), (M_W1, 1.0 / den), (M_W2, e2 / den),
                 (M_R1, r1), (M_R2, r2)):
        meta = jnp.where(lane == k, v, meta)
    meta_ref[...] = meta


def _route_call(cfg, x, ymix, mod_l, w_out, g2, router_w, router_b):
    _, _, n_tiles, d = cfg
    n = n_tiles * TM
    row_spec = pl.BlockSpec((TM, d), lambda j: (j, 0))
    rw = jnp.zeros((d, LANES), F32).at[:, :N_EXPERTS].set(router_w)
    rb = jnp.zeros((1, LANES), F32).at[0, :N_EXPERTS].set(router_b)
    tri = jnp.tril(jnp.ones((TM, TM), BF16), -1)
    return pl.pallas_call(
        _route_kernel,
        grid=(n_tiles,),
        in_specs=[row_spec, row_spec, _mod_spec(cfg), _const_spec((d, d)), _const_spec((1, d)),
                  _const_spec((d, LANES)), _const_spec((1, LANES)), _const_spec((TM, TM))],
        out_specs=[row_spec, pl.BlockSpec((TM, d // 2), lambda j: (j, 0)),
                   pl.BlockSpec((TM, LANES), lambda j: (j, 0)), pl.BlockSpec((1, LANES), lambda j: (0, 0))],
        out_shape=[jax.ShapeDtypeStruct((n, d), F32), jax.ShapeDtypeStruct((n, d // 2), jnp.uint32),
                   jax.ShapeDtypeStruct((n, LANES), F32), jax.ShapeDtypeStruct((1, LANES), F32)],
        scratch_shapes=[pltpu.VMEM((1, LANES), F32)],
        compiler_params=_cparams(("arbitrary",)),
        name="out_proj_route",
    )(x, ymix, mod_l, w_out, g2.reshape(1, d), rw, rb, tri)


def _routing_tables(cfg, meta, cnt):
    n_tiles = cfg[2]
    counts = cnt[0, :N_EXPERTS].astype(jnp.int32)
    padded = ((counts + TM - 1) // TM) * TM
    ends = jnp.cumsum(padded)
    offs = ends - padded
    e1 = meta[:, M_E1].astype(jnp.int32)
    e2 = meta[:, M_E2].astype(jnp.int32)
    pos1 = offs[e1] + meta[:, M_R1].astype(jnp.int32)
    pos2 = offs[e2] + meta[:, M_R2].astype(jnp.int32)
    pos = jnp.concatenate([pos1.reshape(n_tiles, 1, TM), pos2.reshape(n_tiles, 1, TM)], axis=2)
    n_sorted_tiles = 2 * n_tiles + N_EXPERTS
    starts = jnp.arange(n_sorted_tiles, dtype=jnp.int32) * TM
    tile_e = jnp.minimum(jnp.sum((starts[:, None] >= ends[None, :]).astype(jnp.int32), axis=1), N_EXPERTS - 1)
    n_active = (ends[-1] // TM).reshape(1)
    return pos, tile_e, n_active


def _load_positions(pos_ref, idx_ref, sem):
    cp = pltpu.make_async_copy(pos_ref.at[0, 0], idx_ref, sem)
    cp.start()
    cp.wait()


def _dispatch_kernel(pos_ref, hp_ref, buf_in_ref, buf_ref, idx_ref, sem_i, sem_a, sem_b):
    del buf_in_ref
    _load_positions(pos_ref, idx_ref, sem_i)

    def body(r, carry):
        src = hp_ref.at[pl.ds(r, 1)]
        pltpu.make_async_copy(src, buf_ref.at[pl.ds(idx_ref[r], 1)], sem_a).start()
        pltpu.make_async_copy(src, buf_ref.at[pl.ds(idx_ref[TM + r], 1)], sem_b).start()
        return carry

    lax.fori_loop(0, TM, body, 0, unroll=8)
    pltpu.make_async_copy(hp_ref, buf_ref.at[pl.ds(0, TM)], sem_a).wait()
    pltpu.make_async_copy(hp_ref, buf_ref.at[pl.ds(0, TM)], sem_b).wait()


def _dispatch_call(cfg, pos, hp):
    _, _, n_tiles, d = cfg
    n_sorted = (2 * n_tiles + N_EXPERTS) * TM
    buf = jnp.zeros((n_sorted, d // 2), jnp.uint32)
    return pl.pallas_call(
        _dispatch_kernel,
        grid=(n_tiles,),
        in_specs=[pl.BlockSpec((1, 1, 2 * TM), lambda j: (j, 0, 0)),
                  pl.BlockSpec((TM, d // 2), lambda j: (j, 0)),
                  pl.BlockSpec(memory_space=pl.ANY)],
        out_specs=pl.BlockSpec(memory_space=pl.ANY),
        out_shape=jax.ShapeDtypeStruct((n_sorted, d // 2), jnp.uint32),
        input_output_aliases={2: 0},
        scratch_shapes=[pltpu.SMEM((2 * TM,), jnp.int32), pltpu.SemaphoreType.DMA,
                        pltpu.SemaphoreType.DMA, pltpu.SemaphoreType.DMA],
        compiler_params=_cparams(("arbitrary",)),
        name="moe_dispatch",
    )(pos, hp, buf)


def _expert_kernel(te_ref, na_ref, s_ref, wg_ref, wu_ref, wd_ref, o_ref):
    del te_ref

    @pl.when(pl.program_id(0) < na_ref[0])
    def _():
        w = s_ref[...]
        lo = lax.bitcast_convert_type(lax.shift_left(w, jnp.uint32(16)), F32)
        hi = lax.bitcast_convert_type(w & jnp.uint32(0xFFFF0000), F32)
        h = jnp.concatenate([lo, hi], axis=1).astype(BF16)
        g = _dot(h, wg_ref[0])
        u = _dot(h, wu_ref[0])
        act = (g * jax.nn.sigmoid(g) * u).astype(BF16)
        o_ref[...] = _dot(act, wd_ref[0])

    @pl.when(pl.program_id(0) >= na_ref[0])
    def _():
        o_ref[...] = jnp.zeros_like(o_ref)


def _expert_call(cfg, tile_e, n_active, buf, wg, wu, wd):
    _, _, n_tiles, d = cfg
    dfe = wg.shape[2]
    n_sorted_tiles = 2 * n_tiles + N_EXPERTS
    tile = lambda t, te, na: (jnp.minimum(t, na[0] - 1), 0)
    w_spec = lambda shape: pl.BlockSpec(shape, lambda t, te, na: (te[jnp.minimum(t, na[0] - 1)], 0, 0))
    return pl.pallas_call(
        _expert_kernel,
        grid_spec=pltpu.PrefetchScalarGridSpec(
            num_scalar_prefetch=2,
            grid=(n_sorted_tiles,),
            in_specs=[pl.BlockSpec((TM, d // 2), tile), w_spec((1, d, dfe)), w_spec((1, d, dfe)),
                      w_spec((1, dfe, d))],
            out_specs=pl.BlockSpec((TM, d), lambda t, te, na: (t, 0)),
        ),
        out_shape=jax.ShapeDtypeStruct((n_sorted_tiles * TM, d), F32),
        compiler_params=_cparams(("arbitrary",)),
        name="moe_experts",
    )(tile_e, n_active, buf, wg, wu, wd)


def _combine_kernel(pos_ref, xm_ref, meta_ref, mod_ref, ys_ref, o_ref, idx_ref, y1_ref, y2_ref,
                    sem_i, sem_a, sem_b):
    _load_positions(pos_ref, idx_ref, sem_i)

    def body(r, carry):
        pltpu.make_async_copy(ys_ref.at[pl.ds(idx_ref[r], 1)], y1_ref.at[pl.ds(r, 1)], sem_a).start()
        pltpu.make_async_copy(ys_ref.at[pl.ds(idx_ref[TM + r], 1)], y2_ref.at[pl.ds(r, 1)], sem_b).start()
        return carry

    lax.fori_loop(0, TM, body, 0, unroll=8)
    pltpu.make_async_copy(ys_ref.at[pl.ds(0, TM)], y1_ref, sem_a).wait()
    pltpu.make_async_copy(ys_ref.at[pl.ds(0, TM)], y2_ref, sem_b).wait()
    meta = meta_ref[...]
    y = meta[:, M_W1:M_W1 + 1] * y1_ref[...] + meta[:, M_W2:M_W2 + 1] * y2_ref[...]
    o_ref[...] = _gated_add(xm_ref[...], mod_ref[0, 5], y)


def _combine_call(cfg, pos, xm, meta, mod_l, ys):
    _, _, n_tiles, d = cfg
    row_spec = pl.BlockSpec((TM, d), lambda j: (j, 0))
    return pl.pallas_call(
        _combine_kernel,
        grid=(n_tiles,),
        in_specs=[pl.BlockSpec((1, 1, 2 * TM), lambda j: (j, 0, 0)), row_spec,
                  pl.BlockSpec((TM, LANES), lambda j: (j, 0)), _mod_spec(cfg),
                  pl.BlockSpec(memory_space=pl.ANY)],
        out_specs=row_spec,
        out_shape=jax.ShapeDtypeStruct((n_tiles * TM, d), F32),
        scratch_shapes=[pltpu.SMEM((2 * TM,), jnp.int32), pltpu.VMEM((TM, d), F32), pltpu.VMEM((TM, d), F32),
                        pltpu.SemaphoreType.DMA, pltpu.SemaphoreType.DMA, pltpu.SemaphoreType.DMA],
        compiler_params=_cparams(("arbitrary",)),
        name="moe_combine",
    )(pos, xm, meta, mod_l, ys)


def _final_norm_kernel(x_ref, g_ref, o_ref):
    o_ref[...] = _rms(x_ref[...], g_ref[...])


def _final_norm_call(cfg, x, g):
    _, _, n_tiles, d = cfg
    row_spec = pl.BlockSpec((TM, d), lambda j: (j, 0))
    return pl.pallas_call(
        _final_norm_kernel,
        grid=(n_tiles,),
        in_specs=[row_spec, _const_spec((1, d))],
        out_specs=row_spec,
        out_shape=jax.ShapeDtypeStruct((n_tiles * TM, d), F32),
        compiler_params=_cparams(("parallel",)),
        name="final_norm",
    )(x, g.reshape(1, d))


def _block_diag(w, per_block):
    *lead, n, k, _ = w.shape
    nb = n // per_block
    w = w.reshape(*lead, nb, per_block, k, k)
    eye = jnp.eye(per_block, dtype=w.dtype)
    out = w[..., :, :, None, :] * eye[:, None, :, None]
    return out.reshape(*lead, nb, per_block * k, per_block * k)


def _pos_tables(n_ctx_tiles, n_lat_tiles, d):
    quarter = d // 4
    omega = 1.0 / (POS_BASE ** (jnp.arange(quarter, dtype=F32) / quarter))
    er = jnp.arange(n_lat_tiles, dtype=F32)[:, None] * omega
    ec = jnp.arange(GRID_W, dtype=F32)[:, None] * omega
    row_emb = jnp.concatenate([jnp.sin(er), jnp.cos(er)], axis=-1)
    col_emb = jnp.concatenate([jnp.sin(ec), jnp.cos(ec)], axis=-1)
    row_tab = jnp.concatenate([jnp.zeros((n_ctx_tiles, d // 2), F32), row_emb], axis=0)[:, None, :]
    col_rep = jnp.repeat(col_emb, SUB, axis=0)
    col_tab = jnp.stack([jnp.zeros_like(col_rep), col_rep], axis=0)
    return row_tab, col_tab


def kernel(x_prompt, x_sample, state_lru, c, c_ctx, norm_mix_g, w_ada, b_ada, w_in, conv_w, conv_b, lru_wa, lru_ba, lru_wx, lru_bx, lru_lam, pool_w, pool_b, pool_scale, w_out, norm_ffn_g, ffn_wg, ffn_wu, ffn_wd, moe_router_w, moe_router_b, moe_wg, moe_wu, moe_wd, norm_final_g):
    bc, tc, d = x_prompt.shape
    bl, tl, _ = x_sample.shape
    depth = w_ada.shape[0]
    c_lru = conv_w.shape[-1]
    assert bl == SUB and bc % SUB == 0 and tc % GRID_W == 0 and tl % GRID_W == 0
    assert c_lru == 4 * LANES and d == 2 * c_lru
    n_groups = bc // SUB
    tiles_per_ctx = tc // GRID_W
    n_ctx_tiles = n_groups * tiles_per_ctx
    n_lat_tiles = tl // GRID_W
    n_tiles = n_ctx_tiles + n_lat_tiles
    n_ctx = n_ctx_tiles * TM
    cfg = (n_ctx_tiles, tiles_per_ctx, n_tiles, d)

    xp = x_prompt.reshape(n_groups, SUB, tc, d).transpose(0, 2, 1, 3).reshape(n_ctx, d)
    xs = x_sample.transpose(1, 0, 2).reshape(tl * SUB, d)
    x = jnp.concatenate([xp, xs], axis=0)

    cond = jnp.concatenate([c, c_ctx[None], jnp.zeros((SUB - 1, d), F32)], axis=0)
    mod = _ada_call(cond, w_ada, b_ada)
    mod_lat = mod[:, :SUB].reshape(depth, SUB, 6, d).transpose(0, 2, 1, 3)
    mod_ctx = jnp.broadcast_to(mod[:, SUB].reshape(depth, 6, 1, d), (depth, 6, SUB, d))
    mod = jnp.stack([mod_ctx, mod_lat], axis=1)

    pos = _pos_tables(n_ctx_tiles, n_lat_tiles, d)
    wg_lru = jnp.concatenate([_block_diag(lru_wa, 4), _block_diag(lru_wx, 4)], axis=-1).astype(BF16)
    pw = _block_diag(pool_w, 2).astype(BF16)
    h0 = jnp.concatenate([jnp.zeros_like(state_lru[None]), state_lru[None]], axis=0)

    states = []
    for l in range(depth):
        jdx = l // 2
        if l == 0:
            proj, x = _inproj_call(cfg, x, mod[l], norm_mix_g[l], w_in[l].astype(BF16), pos)
        else:
            (proj,) = _inproj_call(cfg, x, mod[l], norm_mix_g[l], w_in[l].astype(BF16))
        hf, st_f = _scan_fwd_call(cfg, proj, conv_w[l], conv_b[l], wg_lru[l, 0], lru_ba[l, 0],
                                  lru_bx[l, 0], lru_lam[l, 0], h0[:, :, l, 0])
        ymix, st_b = _scan_bwd_call(cfg, tc, proj, hf, conv_w[l], conv_b[l], wg_lru[l, 1], lru_ba[l, 1],
                                    lru_bx[l, 1], lru_lam[l, 1], h0[:, :, l, 1], pw[l], pool_b[l],
                                    pool_scale[l])
        states.append(jnp.stack([st_f[:n_groups].reshape(bc, c_lru), st_b[:n_groups].reshape(bc, c_lru)], axis=1))
        if l % 2 == 0:
            x = _dense_ffn_call(cfg, x, ymix, mod[l], w_out[l].astype(BF16), norm_ffn_g[l],
                                ffn_wg[jdx].astype(BF16), ffn_wu[jdx].astype(BF16), ffn_wd[jdx].astype(BF16))
        else:
            xm, hp, meta, cnt = _route_call(cfg, x, ymix, mod[l], w_out[l].astype(BF16), norm_ffn_g[l],
                                            moe_router_w[jdx], moe_router_b[jdx])
            pos, tile_e, n_active = _routing_tables(cfg, meta, cnt)
            buf = _dispatch_call(cfg, pos, hp)
            ys = _expert_call(cfg, tile_e, n_active, buf, moe_wg[jdx].astype(BF16), moe_wu[jdx].astype(BF16),
                              moe_wd[jdx].astype(BF16))
            x = _combine_call(cfg, pos, xm, meta, mod[l], ys)

    y = _final_norm_call(cfg, x, norm_final_g)
    y_prompt = y[:n_ctx].reshape(n_groups, tc, SUB, d).transpose(0, 2, 1, 3).reshape(bc, tc, d)
    y_sample = y[n_ctx:].reshape(tl, SUB, d).transpose(1, 0, 2)
    new_state = jnp.stack(states, axis=1)
    return (y_prompt, y_sample, new_state)
```

```python
import functools

import jax
import jax.numpy as jnp
from jax import lax
from jax.experimental import pallas as pl
from jax.experimental.pallas import tpu as pltpu

F32 = jnp.float32
BF16 = jnp.bfloat16

SUB = 8
LANES = 128
GRID_W = 64
TM = GRID_W * SUB
POS_BASE = 10000.0
N_LRU_HEADS = 8
CONV_W = 4
CONV_LEFT = CONV_W // 2
CONV_RIGHT = CONV_W - 1 - CONV_LEFT
LRU_C = 8.0
POOL_WINDOWS = (2, 4, 8, 16)
POOL_HALO = 8 * SUB
N_EXPERTS = 8
EPS = 1e-6
FF_CHUNK = 256
VMEM_LIMIT = 56 * 1024 * 1024


def _cparams(sem):
    return pltpu.CompilerParams(dimension_semantics=sem, vmem_limit_bytes=VMEM_LIMIT)


def _const_spec(shape):
    nd = len(shape)
    return pl.BlockSpec(shape, lambda *_: (0,) * nd, pipeline_mode=pl.Buffered(1))


def _rms(x, g):
    ms = jnp.mean(x * x, axis=-1, keepdims=True)
    return x * lax.rsqrt(ms + EPS) * g


def _per_seq(x, v, op):
    r, c = x.shape
    x3 = x.reshape(r // SUB, SUB, c)
    return op(x3, v[None]).reshape(r, c)


def _modulate(h, scale, shift):
    r, c = h.shape
    h3 = h.reshape(r // SUB, SUB, c)
    return (h3 * (1.0 + scale)[None] + shift[None]).reshape(r, c)


def _gated_add(x, gate, y):
    return x + _per_seq(y, gate, lambda a, b: a * b)


def _dot(a, b):
    return jnp.dot(a, b, preferred_element_type=F32)


def _ada_kernel(c_ref, w_ref, b_ref, o_ref):
    c = c_ref[...]
    s = (c * jax.nn.sigmoid(c)).astype(BF16)
    o_ref[0] = _dot(s, w_ref[0].astype(BF16)) + b_ref[0]


def _ada_call(cond, w_ada, b_ada):
    depth, d, d6 = w_ada.shape
    nr = cond.shape[0]
    bn = d6 // 4
    return pl.pallas_call(
        _ada_kernel,
        grid=(depth, d6 // bn),
        in_specs=[
            pl.BlockSpec((nr, d), lambda l, n: (0, 0)),
            pl.BlockSpec((1, d, bn), lambda l, n: (l, 0, n)),
            pl.BlockSpec((1, 1, bn), lambda l, n: (l, 0, n)),
        ],
        out_specs=pl.BlockSpec((1, nr, bn), lambda l, n: (l, 0, n)),
        out_shape=jax.ShapeDtypeStruct((depth, nr, d6), F32),
        compiler_params=_cparams(("parallel", "parallel")),
        name="ada_mod",
    )(cond, w_ada, b_ada.reshape(depth, 1, d6))


def _inproj_kernel(with_pos, *refs):
    if with_pos:
        x_ref, row_ref, col_ref, mod_ref, g_ref, w_ref, o_ref, x0_ref = refs
        x = x_ref[...]
        half = x.shape[1] // 2
        x = jnp.concatenate([x[:, :half] + row_ref[0], x[:, half:] + col_ref[0]], axis=1)
        x0_ref[...] = x
    else:
        x_ref, mod_ref, g_ref, w_ref, o_ref = refs
        x = x_ref[...]
    h = _modulate(_rms(x, g_ref[...]), mod_ref[0, 1], mod_ref[0, 0])
    o_ref[...] = _dot(h.astype(BF16), w_ref[...])


def _wide(cfg):
    n_ctx_tiles, _, n_tiles, _ = cfg
    return 2 if n_ctx_tiles % 2 == 0 and n_tiles % 2 == 0 else 1


def _mod_spec(cfg, f=1):
    n_ctx_steps = cfg[0] // f
    d = cfg[3]
    return pl.BlockSpec((1, 6, SUB, d), lambda j: (jnp.where(j >= n_ctx_steps, 1, 0), 0, 0, 0))


def _inproj_call(cfg, x, mod_l, g, w_in, pos=None):
    n_ctx_tiles, _, n_tiles, d = cfg
    d_in = w_in.shape[1]
    f = 1 if pos is not None else _wide(cfg)
    tm = TM * f
    row_spec = pl.BlockSpec((tm, d), lambda j: (j, 0))
    in_specs = [row_spec]
    args = [x]
    out_specs = [pl.BlockSpec((tm, d_in), lambda j: (j, 0))]
    out_shape = [jax.ShapeDtypeStruct((n_tiles * TM, d_in), F32)]
    if pos is not None:
        row_tab, col_tab = pos
        in_specs += [
            pl.BlockSpec((1, 1, d // 2), lambda j: (j, 0, 0)),
            pl.BlockSpec((1, TM, d // 2), lambda j: (jnp.where(j >= n_ctx_tiles, 1, 0), 0, 0)),
        ]
        args += [row_tab, col_tab]
        out_specs.append(row_spec)
        out_shape.append(jax.ShapeDtypeStruct((n_tiles * TM, d), F32))
    in_specs += [_mod_spec(cfg, f), _const_spec((1, d)), _const_spec((d, d_in))]
    args += [mod_l, g.reshape(1, d), w_in]
    return pl.pallas_call(
        functools.partial(_inproj_kernel, pos is not None),
        grid=(n_tiles // f,),
        in_specs=in_specs,
        out_specs=out_specs,
        out_shape=out_shape,
        compiler_params=_cparams(("parallel",)),
        name="in_proj",
    )(*args)


def _seq_flags(cfg, j):
    n_ctx_tiles, tiles_per_ctx, n_tiles, _ = cfg
    is_ctx = j < n_ctx_tiles
    pos = lax.rem(j, tiles_per_ctx)
    first = jnp.where(is_ctx, pos == 0, j == n_ctx_tiles)
    last = jnp.where(is_ctx, pos == tiles_per_ctx - 1, j == n_tiles - 1)
    return is_ctx, pos, first, last


def _conv(pad_ref, xa, prev, nxt, first, last, cw, cb):
    lo = CONV_LEFT * SUB
    pad_ref[0:lo, :] = jnp.where(first, 0.0, prev)
    pad_ref[lo:lo + TM, :] = xa
    pad_ref[lo + TM:lo + TM + CONV_RIGHT * SUB, :] = jnp.where(last, 0.0, nxt)
    y = cb
    for k in range(CONV_W):
        y = y + pad_ref[k * SUB:k * SUB + TM, :] * cw[k:k + 1, :]
    return y


def _gates(xc, wg_ref, ba, bx, lam):
    half = xc.shape[1] // 2
    xb = xc.astype(BF16)
    g0 = _dot(xb[:, :half], wg_ref[0])
    g1 = _dot(xb[:, half:], wg_ref[1])
    r = jax.nn.sigmoid(jnp.concatenate([g0[:, :half], g1[:, :half]], axis=1) + ba)
    i = jax.nn.sigmoid(jnp.concatenate([g0[:, half:], g1[:, half:]], axis=1) + bx)
    z = -lam
    softplus = jnp.maximum(z, 0.0) + jnp.log1p(jnp.exp(-jnp.abs(z)))
    log_a = (-LRU_C) * r * softplus
    a = jnp.exp(log_a)
    z = jnp.tanh(-log_a) * (a * a + 1.0)
    root = jnp.where(z > 0.0, z * lax.rsqrt(z), 0.0)
    u = root * (i * xc)
    return a, u


def _scan(a_ref, u_ref, h_ref, h, reverse):
    steps = TM // SUB

    def body(k, h):
        t = steps - 1 - k if reverse else k
        r0 = pl.multiple_of(t * SUB, SUB)
        h = a_ref[pl.ds(r0, SUB), :] * h + u_ref[pl.ds(r0, SUB), :]
        h_ref[pl.ds(r0, SUB), :] = h
        return h

    return lax.fori_loop(0, steps, body, h, unroll=8)


def _scan_fwd_kernel(cfg, xa_ref, xp_ref, xn_ref, cw_ref, cb_ref, wg_ref, ba_ref, bx_ref,
                     lam_ref, h0_ref, hf_ref, st_ref, pad_ref, a_ref, u_ref, carry_ref):
    j = pl.program_id(0)
    _, _, first, last = _seq_flags(cfg, j)
    xc = _conv(pad_ref, xa_ref[...], xp_ref[...], xn_ref[...], first, last, cw_ref[...], cb_ref[...])
    a, u = _gates(xc, wg_ref, ba_ref[...], bx_ref[...], lam_ref[...])
    a_ref[...] = a
    u_ref[...] = u

    @pl.when(first)
    def _():
        carry_ref[...] = h0_ref[0]

    h = _scan(a_ref, u_ref, hf_ref, carry_ref[...], reverse=False)
    carry_ref[...] = h
    st_ref[0] = h


def _halo_specs(cfg, col, rows_prev, rows_next, tile_of):
    n_tiles = cfg[2]
    c = 512
    nb_prev = TM // rows_prev
    nb_next = TM // rows_next
    last_next = n_tiles * nb_next - 1
    prev = pl.BlockSpec((rows_prev, c), lambda j: (jnp.maximum(tile_of(j) * nb_prev - 1, 0), col))
    nxt = pl.BlockSpec((rows_next, c), lambda j: (jnp.minimum((tile_of(j) + 1) * nb_next, last_next), col))
    return prev, nxt


def _state_spec(cfg, tile_of):
    n_ctx_tiles, tiles_per_ctx, _, _ = cfg
    n_groups = n_ctx_tiles // tiles_per_ctx
    c = 512
    return pl.BlockSpec((1, SUB, c), lambda j: (jnp.minimum(tile_of(j) // tiles_per_ctx, n_groups), 0, 0))


def _h0_spec(cfg, tile_of):
    n_ctx_tiles = cfg[0]
    return pl.BlockSpec((1, SUB, 512), lambda j: (jnp.where(tile_of(j) >= n_ctx_tiles, 1, 0), 0, 0))


def _scan_fwd_call(cfg, proj, conv_w, conv_b, wg, ba, bx, lam, h0):
    n_ctx_tiles, tiles_per_ctx, n_tiles, _ = cfg
    n_groups = n_ctx_tiles // tiles_per_ctx
    c = conv_w.shape[1]
    ident = lambda j: j
    xp_spec, xn_spec = _halo_specs(cfg, 0, CONV_LEFT * SUB, CONV_RIGHT * SUB, ident)
    return pl.pallas_call(
        functools.partial(_scan_fwd_kernel, cfg),
        grid=(n_tiles,),
        in_specs=[
            pl.BlockSpec((TM, c), lambda j: (j, 0)), xp_spec, xn_spec,
            _const_spec((CONV_W, c)), _const_spec((1, c)), _const_spec((2, c // 2, c)),
            _const_spec((1, c)), _const_spec((1, c)), _const_spec((1, c)),
            _h0_spec(cfg, ident),
        ],
        out_specs=[pl.BlockSpec((TM, c), lambda j: (j, 0)), _state_spec(cfg, ident)],
        out_shape=[jax.ShapeDtypeStruct((n_tiles * TM, c), F32),
                   jax.ShapeDtypeStruct((n_groups + 1, SUB, c), F32)],
        scratch_shapes=[
            pltpu.VMEM((TM + (CONV_W - 1) * SUB, c), F32),
            pltpu.VMEM((TM, c), F32), pltpu.VMEM((TM, c), F32), pltpu.VMEM((SUB, c), F32),
        ],
        compiler_params=_cparams(("arbitrary",)),
        name="lru_fwd",
    )(proj, proj, proj, conv_w, conv_b.reshape(1, c), wg, ba.reshape(1, c), bx.reshape(1, c),
      lam.reshape(1, c), h0)


def _pool_mix(pad_ref, t0, t_len, pw_ref, pb, ps):
    c = pad_ref.shape[1]
    gw = c // len(POOL_WINDOWS)
    t_pos = lax.shift_right_logical(lax.broadcasted_iota(jnp.int32, (TM, gw), 0), 3) + t0
    outs = []
    for g, k in enumerate(POOL_WINDOWS):
        left = k // 2
        right = k - 1 - left
        s = None
        for o in range(-left, right + 1):
            v = pad_ref[POOL_HALO + o * SUB:POOL_HALO + o * SUB + TM, g * gw:(g + 1) * gw]
            s = v if s is None else s + v
        cnt = (jnp.minimum(t_pos + right + 1, t_len) - jnp.maximum(t_pos - left, 0)).astype(F32)
        outs.append(s / cnt - pad_ref[POOL_HALO:POOL_HALO + TM, g * gw:(g + 1) * gw])
    d = jnp.concatenate(outs, axis=1).astype(BF16)
    half = c // 2
    y = jnp.concatenate([_dot(d[:, :half], pw_ref[0]), _dot(d[:, half:], pw_ref[1])], axis=1)
    return (y + pb) * ps


def _scan_bwd_kernel(cfg, ctx_len, proj_ref, xp_ref, xn_ref, bp_ref, bn_ref, hf_ref, cw_ref, cb_ref,
                     wg_ref, ba_ref, bx_ref, lam_ref, h0_ref, pw_ref, pb_ref, ps_ref,
                     y_ref, st_ref, pad_ref, ppad_ref, a_ref, u_ref, hb_ref, carry_ref):
    n_tiles = cfg[2]
    j = n_tiles - 1 - pl.program_id(0)
    is_ctx, pos, first, last = _seq_flags(cfg, j)
    c = hf_ref.shape[1]
    xc = _conv(pad_ref, proj_ref[:, 0:c], xp_ref[...], xn_ref[...], first, last, cw_ref[...], cb_ref[...])
    a, u = _gates(xc, wg_ref, ba_ref[...], bx_ref[...], lam_ref[...])
    a_ref[...] = a
    u_ref[...] = u

    @pl.when(last)
    def _():
        carry_ref[...] = h0_ref[0]

    h = _scan(a_ref, u_ref, hb_ref, carry_ref[...], reverse=True)
    carry_ref[...] = h
    st_ref[0] = h

    ga = proj_ref[:, c:2 * c]
    y_a = (hf_ref[...] + hb_ref[...]) * jax.nn.gelu(ga, approximate=True)

    use_prev = jnp.logical_and(is_ctx, jnp.logical_not(first))
    use_next = jnp.logical_and(is_ctx, jnp.logical_not(last))
    ppad_ref[0:POOL_HALO, :] = jnp.where(use_prev, bp_ref[...], 0.0)
    ppad_ref[POOL_HALO:POOL_HALO + TM, :] = proj_ref[:, 2 * c:3 * c]
    ppad_ref[POOL_HALO + TM:, :] = jnp.where(use_next, bn_ref[...], 0.0)
    t0 = jnp.where(is_ctx, pos * GRID_W, 0)
    t_len = jnp.where(is_ctx, ctx_len, GRID_W)
    y_b = _pool_mix(ppad_ref, t0, t_len, pw_ref, pb_ref[...], ps_ref[...])
    y_ref[...] = jnp.concatenate([y_a, y_b], axis=1).astype(BF16)


def _scan_bwd_call(cfg, ctx_len, proj, hf, conv_w, conv_b, wg, ba, bx, lam, h0, pw, pb, ps):
    n_ctx_tiles, tiles_per_ctx, n_tiles, d = cfg
    n_groups = n_ctx_tiles // tiles_per_ctx
    c = conv_w.shape[1]
    rev = lambda j: n_tiles - 1 - j
    xp_spec, xn_spec = _halo_specs(cfg, 0, CONV_LEFT * SUB, CONV_RIGHT * SUB, rev)
    bp_spec, bn_spec = _halo_specs(cfg, 2, POOL_HALO, POOL_HALO, rev)
    return pl.pallas_call(
        functools.partial(_scan_bwd_kernel, cfg, ctx_len),
        grid=(n_tiles,),
        in_specs=[
            pl.BlockSpec((TM, 3 * c), lambda j: (rev(j), 0)), xp_spec, xn_spec, bp_spec, bn_spec,
            pl.BlockSpec((TM, c), lambda j: (rev(j), 0)),
            _const_spec((CONV_W, c)), _const_spec((1, c)), _const_spec((2, c // 2, c)),
            _const_spec((1, c)), _const_spec((1, c)), _const_spec((1, c)),
            _h0_spec(cfg, rev),
            _const_spec((2, c // 2, c // 2)), _const_spec((1, c)), _const_spec((1, c)),
        ],
        out_specs=[pl.BlockSpec((TM, d), lambda j: (rev(j), 0)), _state_spec(cfg, rev)],
        out_shape=[jax.ShapeDtypeStruct((n_tiles * TM, d), BF16),
                   jax.ShapeDtypeStruct((n_groups + 1, SUB, c), F32)],
        scratch_shapes=[
            pltpu.VMEM((TM + (CONV_W - 1) * SUB, c), F32),
            pltpu.VMEM((TM + 2 * POOL_HALO, c), F32),
            pltpu.VMEM((TM, c), F32), pltpu.VMEM((TM, c), F32), pltpu.VMEM((TM, c), F32),
            pltpu.VMEM((SUB, c), F32),
        ],
        compiler_params=_cparams(("arbitrary",)),
        name="lru_bwd_mix",
    )(proj, proj, proj, proj, proj, hf, conv_w, conv_b.reshape(1, c), wg, ba.reshape(1, c),
      bx.reshape(1, c), lam.reshape(1, c), h0, pw, pb.reshape(1, c), ps.reshape(1, c))


def _out_proj(x_ref, ym_ref, mod_ref, wo_ref, g2_ref):
    xm = _gated_add(x_ref[...], mod_ref[0, 2], _dot(ym_ref[...], wo_ref[...]))
    h2 = _modulate(_rms(xm, g2_ref[...]), mod_ref[0, 4], mod_ref[0, 3])
    return xm, h2


def _swiglu_act(h2b, wg_ref, wu_ref, act_ref):
    n_chunks = act_ref.shape[1] // FF_CHUNK

    def body(f, carry):
        c0 = pl.multiple_of(f * FF_CHUNK, FF_CHUNK)
        g = _dot(h2b, wg_ref[:, pl.ds(c0, FF_CHUNK)])
        u = _dot(h2b, wu_ref[:, pl.ds(c0, FF_CHUNK)])
        act_ref[:, pl.ds(c0, FF_CHUNK)] = (g * jax.nn.sigmoid(g) * u).astype(BF16)
        return carry

    lax.fori_loop(0, n_chunks, body, 0)


def _dense_ffn_kernel(x_ref, ym_ref, mod_ref, wo_ref, g2_ref, wg_ref, wu_ref, wd_ref, o_ref, act_ref):
    xm, h2 = _out_proj(x_ref, ym_ref, mod_ref, wo_ref, g2_ref)
    _swiglu_act(h2.astype(BF16), wg_ref, wu_ref, act_ref)
    o_ref[...] = _gated_add(xm, mod_ref[0, 5], _dot(act_ref[...], wd_ref[...]))


def _dense_ffn_call(cfg, x, ymix, mod_l, w_out, g2, wg, wu, wd):
    _, _, n_tiles, d = cfg
    d_ff = wg.shape[1]
    f = _wide(cfg)
    row_spec = pl.BlockSpec((TM * f, d), lambda j: (j, 0))
    return pl.pallas_call(
        _dense_ffn_kernel,
        grid=(n_tiles // f,),
        in_specs=[row_spec, row_spec, _mod_spec(cfg, f), _const_spec((d, d)), _const_spec((1, d)),
                  _const_spec((d, d_ff)), _const_spec((d, d_ff)), _const_spec((d_ff, d))],
        out_specs=row_spec,
        out_shape=jax.ShapeDtypeStruct((n_tiles * TM, d), F32),
        scratch_shapes=[pltpu.VMEM((TM * f, d_ff), BF16)],
        compiler_params=_cparams(("parallel",)),
        name="out_proj_dense_ffn",
    )(x, ymix, mod_l, w_out, g2.reshape(1, d), wg, wu, wd)


M_E1, M_E2, M_W1, M_W2, M_R1, M_R2 = range(6)


def _route_kernel(x_ref, ym_ref, mod_ref, wo_ref, g2_ref, rw_ref, rb_ref, tri_ref,
                  xm_ref, hp_ref, meta_ref, cnt_ref, run_ref):
    xm, h2 = _out_proj(x_ref, ym_ref, mod_ref, wo_ref, g2_ref)
    xm_ref[...] = xm
    h_hi = h2.astype(BF16)
    h_hi32 = h_hi.astype(F32)
    bits = lax.bitcast_convert_type(h_hi32, jnp.uint32)
    half = bits.shape[1] // 2
    hp_ref[...] = lax.shift_right_logical(bits[:, :half], jnp.uint32(16)) | (bits[:, half:] & jnp.uint32(0xFFFF0000))

    h_lo = (h2 - h_hi32).astype(BF16)
    p = _dot(h_hi, rw_ref[...])
    logits = p[:, :LANES] + p[:, LANES:] + _dot(h_lo, rw_ref[:, :LANES])
    lane = lax.broadcasted_iota(jnp.int32, logits.shape, 1)
    neg = jnp.float32(-jnp.inf)
    lg = jnp.where(lane < N_EXPERTS, logits + rb_ref[...], neg)
    m1 = jnp.max(lg, axis=1, keepdims=True)
    i1 = jnp.min(jnp.where(lg == m1, lane, LANES), axis=1, keepdims=True)
    lg2 = jnp.where(lane == i1, neg, lg)
    m2 = jnp.max(lg2, axis=1, keepdims=True)
    i2 = jnp.min(jnp.where(lg2 == m2, lane, LANES), axis=1, keepdims=True)
    e2 = jnp.exp(m2 - m1)
    den = 1.0 + e2

    @pl.when(pl.program_id(0) == 0)
    def _():
        run_ref[...] = jnp.zeros_like(run_ref)

    sel1 = lane == i1
    sel2 = lane == i2
    onehot = jnp.where(jnp.logical_or(sel1, sel2), 1.0, 0.0)
    rank = _dot(tri_ref[...], onehot.astype(BF16)) + run_ref[...]
    r1 = jnp.sum(jnp.where(sel1, rank, 0.0), axis=1, keepdims=True)
    r2 = jnp.sum(jnp.where(sel2, rank, 0.0), axis=1, keepdims=True)
    run_ref[...] = run_ref[...] + jnp.sum(onehot, axis=0, keepdims=True)
    cnt_ref[...] = run_ref[...]

    meta = jnp.zeros(logits.shape, F32)
    for k, v in ((M_E1, i1.astype(F32)), (M_E2, i2.astype(F32)), (M_W1, 1.0 / den), (M_W2, e2 / den),
                 (M_R1, r1), (M_R2, r2)):
        meta = jnp.where(lane == k, v, meta)
    meta_ref[...] = meta


def _route_call(cfg, x, ymix, mod_l, w_out, g2, router_w, router_b):
    _, _, n_tiles, d = cfg
    n = n_tiles * TM
    f = _wide(cfg)
    tm = TM * f
    row_spec = pl.BlockSpec((tm, d), lambda j: (j, 0))
    rw = jnp.zeros((d, LANES), F32).at[:, :N_EXPERTS].set(router_w)
    rw_hi = rw.astype(BF16)
    rw = jnp.concatenate([rw_hi, (rw - rw_hi.astype(F32)).astype(BF16)], axis=1)
    rb = jnp.zeros((1, LANES), F32).at[0, :N_EXPERTS].set(router_b)
    tri = jnp.tril(jnp.ones((tm, tm), BF16), -1)
    return pl.pallas_call(
        _route_kernel,
        grid=(n_tiles // f,),
        in_specs=[row_spec, row_spec, _mod_spec(cfg, f), _const_spec((d, d)), _const_spec((1, d)),
                  _const_spec((d, 2 * LANES)), _const_spec((1, LANES)), _const_spec((tm, tm))],
        out_specs=[row_spec, pl.BlockSpec((tm, d // 2), lambda j: (j, 0)),
                   pl.BlockSpec((tm, LANES), lambda j: (j, 0)), pl.BlockSpec((1, LANES), lambda j: (0, 0))],
        out_shape=[jax.ShapeDtypeStruct((n, d), F32), jax.ShapeDtypeStruct((n, d // 2), jnp.uint32),
                   jax.ShapeDtypeStruct((n, LANES), F32), jax.ShapeDtypeStruct((1, LANES), F32)],
        scratch_shapes=[pltpu.VMEM((1, LANES), F32)],
        compiler_params=_cparams(("arbitrary",)),
        name="out_proj_route",
    )(x, ymix, mod_l, w_out, g2.reshape(1, d), rw, rb, tri)


def _routing_tables(cfg, meta, cnt):
    n_tiles = cfg[2]
    counts = cnt[0, :N_EXPERTS].astype(jnp.int32)
    padded = ((counts + TM - 1) // TM) * TM
    ends = jnp.cumsum(padded)
    offs = ends - padded
    e1 = meta[:, M_E1].astype(jnp.int32)
    e2 = meta[:, M_E2].astype(jnp.int32)
    pos1 = offs[e1] + meta[:, M_R1].astype(jnp.int32)
    pos2 = offs[e2] + meta[:, M_R2].astype(jnp.int32)
    pos = jnp.concatenate([pos1.reshape(n_tiles, 1, TM), pos2.reshape(n_tiles, 1, TM)], axis=2)
    n_sorted_tiles = 2 * n_tiles + N_EXPERTS
    starts = jnp.arange(n_sorted_tiles, dtype=jnp.int32) * TM
    tile_e = jnp.minimum(jnp.sum((starts[:, None] >= ends[None, :]).astype(jnp.int32), axis=1), N_EXPERTS - 1)
    n_active = (ends[-1] // TM).reshape(1)
    return pos, tile_e, n_active


def _load_positions(pos_ref, idx_ref, sem):
    cp = pltpu.make_async_copy(pos_ref.at[0, 0], idx_ref, sem)
    cp.start()
    cp.wait()


def _dispatch_kernel(pos_ref, hp_ref, buf_in_ref, buf_ref, idx_ref, sem_i, sem_a, sem_b):
    del buf_in_ref
    _load_positions(pos_ref, idx_ref, sem_i)

    def body(r, carry):
        src = hp_ref.at[pl.ds(r, 1)]
        pltpu.make_async_copy(src, buf_ref.at[pl.ds(idx_ref[r], 1)], sem_a).start()
        pltpu.make_async_copy(src, buf_ref.at[pl.ds(idx_ref[TM + r], 1)], sem_b).start()
        return carry

    lax.fori_loop(0, TM, body, 0, unroll=8)
    pltpu.make_async_copy(hp_ref, buf_ref.at[pl.ds(0, TM)], sem_a).wait()
    pltpu.make_async_copy(hp_ref, buf_ref.at[pl.ds(0, TM)], sem_b).wait()


def _dispatch_call(cfg, pos, hp):
    _, _, n_tiles, d = cfg
    n_sorted = (2 * n_tiles + N_EXPERTS) * TM
    buf = jnp.zeros((n_sorted, d // 2), jnp.uint32)
    return pl.pallas_call(
        _dispatch_kernel,
        grid=(n_tiles,),
        in_specs=[pl.BlockSpec((1, 1, 2 * TM), lambda j: (j, 0, 0)),
                  pl.BlockSpec((TM, d // 2), lambda j: (j, 0)),
                  pl.BlockSpec(memory_space=pl.ANY)],
        out_specs=pl.BlockSpec(memory_space=pl.ANY),
        out_shape=jax.ShapeDtypeStruct((n_sorted, d // 2), jnp.uint32),
        input_output_aliases={2: 0},
        scratch_shapes=[pltpu.SMEM((2 * TM,), jnp.int32), pltpu.SemaphoreType.DMA,
                        pltpu.SemaphoreType.DMA, pltpu.SemaphoreType.DMA],
        compiler_params=_cparams(("arbitrary",)),
        name="moe_dispatch",
    )(pos, hp, buf)


def _expert_kernel(te_ref, na_ref, s_ref, wg_ref, wu_ref, wd_ref, o_ref):
    del te_ref

    @pl.when(pl.program_id(0) < na_ref[0])
    def _():
        w = s_ref[...]
        lo = lax.bitcast_convert_type(lax.shift_left(w, jnp.uint32(16)), F32)
        hi = lax.bitcast_convert_type(w & jnp.uint32(0xFFFF0000), F32)
        h = jnp.concatenate([lo, hi], axis=1).astype(BF16)
        g = _dot(h, wg_ref[0])
        u = _dot(h, wu_ref[0])
        act = (g * jax.nn.sigmoid(g) * u).astype(BF16)
        o_ref[...] = _dot(act, wd_ref[0])

    @pl.when(pl.program_id(0) >= na_ref[0])
    def _():
        o_ref[...] = jnp.zeros_like(o_ref)


def _expert_call(cfg, tile_e, n_active, buf, wg, wu, wd):
    _, _, n_tiles, d = cfg
    dfe = wg.shape[2]
    n_sorted_tiles = 2 * n_tiles + N_EXPERTS
    last = lambda t, na: jnp.maximum(jnp.minimum(t, na[0] - 1), 0)
    tile = lambda t, te, na: (last(t, na), 0)
    w_spec = lambda shape: pl.BlockSpec(shape, lambda t, te, na: (te[last(t, na)], 0, 0))
    return pl.pallas_call(
        _expert_kernel,
        grid_spec=pltpu.PrefetchScalarGridSpec(
            num_scalar_prefetch=2,
            grid=(n_sorted_tiles,),
            in_specs=[pl.BlockSpec((TM, d // 2), tile), w_spec((1, d, dfe)), w_spec((1, d, dfe)),
                      w_spec((1, dfe, d))],
            out_specs=pl.BlockSpec((TM, d), lambda t, te, na: (t, 0)),
        ),
        out_shape=jax.ShapeDtypeStruct((n_sorted_tiles * TM, d), F32),
        compiler_params=_cparams(("arbitrary",)),
        name="moe_experts",
    )(tile_e, n_active, buf, wg, wu, wd)


def _combine_kernel(final, pos_ref, xm_ref, meta_ref, mod_ref, gfin_ref, ys_ref, o_ref, idx_ref,
                    y1_ref, y2_ref, sem_i, sem_a, sem_b):
    _load_positions(pos_ref, idx_ref, sem_i)

    def body(r, carry):
        pltpu.make_async_copy(ys_ref.at[pl.ds(idx_ref[r], 1)], y1_ref.at[pl.ds(r, 1)], sem_a).start()
        pltpu.make_async_copy(ys_ref.at[pl.ds(idx_ref[TM + r], 1)], y2_ref.at[pl.ds(r, 1)], sem_b).start()
        return carry

    lax.fori_loop(0, TM, body, 0, unroll=8)
    pltpu.make_async_copy(ys_ref.at[pl.ds(0, TM)], y1_ref, sem_a).wait()
    pltpu.make_async_copy(ys_ref.at[pl.ds(0, TM)], y2_ref, sem_b).wait()
    meta = meta_ref[...]
    y = meta[:, M_W1:M_W1 + 1] * y1_ref[...] + meta[:, M_W2:M_W2 + 1] * y2_ref[...]
    x = _gated_add(xm_ref[...], mod_ref[0, 5], y)
    o_ref[...] = _rms(x, gfin_ref[...]) if final else x


def _combine_call(cfg, pos, xm, meta, mod_l, ys, g_final, final):
    _, _, n_tiles, d = cfg
    row_spec = pl.BlockSpec((TM, d), lambda j: (j, 0))
    return pl.pallas_call(
        functools.partial(_combine_kernel, final),
        grid=(n_tiles,),
        in_specs=[pl.BlockSpec((1, 1, 2 * TM), lambda j: (j, 0, 0)), row_spec,
                  pl.BlockSpec((TM, LANES), lambda j: (j, 0)), _mod_spec(cfg), _const_spec((1, d)),
                  pl.BlockSpec(memory_space=pl.ANY)],
        out_specs=row_spec,
        out_shape=jax.ShapeDtypeStruct((n_tiles * TM, d), F32),
        scratch_shapes=[pltpu.SMEM((2 * TM,), jnp.int32), pltpu.VMEM((TM, d), F32), pltpu.VMEM((TM, d), F32),
                        pltpu.SemaphoreType.DMA, pltpu.SemaphoreType.DMA, pltpu.SemaphoreType.DMA],
        compiler_params=_cparams(("arbitrary",)),
        name="moe_combine",
    )(pos, xm, meta, mod_l, g_final.reshape(1, d), ys)


def _final_norm_kernel(x_ref, g_ref, o_ref):
    o_ref[...] = _rms(x_ref[...], g_ref[...])


def _final_norm_call(cfg, x, g):
    _, _, n_tiles, d = cfg
    row_spec = pl.BlockSpec((TM, d), lambda j: (j, 0))
    return pl.pallas_call(
        _final_norm_kernel,
        grid=(n_tiles,),
        in_specs=[row_spec, _const_spec((1, d))],
        out_specs=row_spec,
        out_shape=jax.ShapeDtypeStruct((n_tiles * TM, d), F32),
        compiler_params=_cparams(("parallel",)),
        name="final_norm",
    )(x, g.reshape(1, d))


def _block_diag(w, per_block):
    *lead, n, k, _ = w.shape
    nb = n // per_block
    w = w.reshape(*lead, nb, per_block, k, k)
    eye = jnp.eye(per_block, dtype=w.dtype)
    out = w[..., :, :, None, :] * eye[:, None, :, None]
    return out.reshape(*lead, nb, per_block * k, per_block * k)


def _pos_tables(n_ctx_tiles, n_lat_tiles, d):
    quarter = d // 4
    omega = 1.0 / (POS_BASE ** (jnp.arange(quarter, dtype=F32) / quarter))
    er = jnp.arange(n_lat_tiles, dtype=F32)[:, None] * omega
    ec = jnp.arange(GRID_W, dtype=F32)[:, None] * omega
    row_emb = jnp.concatenate([jnp.sin(er), jnp.cos(er)], axis=-1)
    col_emb = jnp.concatenate([jnp.sin(ec), jnp.cos(ec)], axis=-1)
    row_tab = jnp.concatenate([jnp.zeros((n_ctx_tiles, d // 2), F32), row_emb], axis=0)[:, None, :]
    col_rep = jnp.repeat(col_emb, SUB, axis=0)
    col_tab = jnp.stack([jnp.zeros_like(col_rep), col_rep], axis=0)
    return row_tab, col_tab


def kernel(x_prompt, x_sample, state_lru, c, c_ctx, norm_mix_g, w_ada, b_ada, w_in, conv_w, conv_b, lru_wa, lru_ba, lru_wx, lru_bx, lru_lam, pool_w, pool_b, pool_scale, w_out, norm_ffn_g, ffn_wg, ffn_wu, ffn_wd, moe_router_w, moe_router_b, moe_wg, moe_wu, moe_wd, norm_final_g):
    bc, tc, d = x_prompt.shape
    bl, tl, _ = x_sample.shape
    depth = w_ada.shape[0]
    c_lru = conv_w.shape[-1]
    assert bl == SUB and bc % SUB == 0 and tc % GRID_W == 0 and tl % GRID_W == 0
    assert c_lru == 4 * LANES and d == 2 * c_lru
    n_groups = bc // SUB
    tiles_per_ctx = tc // GRID_W
    n_ctx_tiles = n_groups * tiles_per_ctx
    n_lat_tiles = tl // GRID_W
    n_tiles = n_ctx_tiles + n_lat_tiles
    n_ctx = n_ctx_tiles * TM
    cfg = (n_ctx_tiles, tiles_per_ctx, n_tiles, d)

    xp = x_prompt.reshape(n_groups, SUB, tc, d).transpose(0, 2, 1, 3).reshape(n_ctx, d)
    xs = x_sample.transpose(1, 0, 2).reshape(tl * SUB, d)
    x = jnp.concatenate([xp, xs], axis=0)

    cond = jnp.concatenate([c, c_ctx[None], jnp.zeros((SUB - 1, d), F32)], axis=0)
    mod = _ada_call(cond, w_ada, b_ada)
    mod_lat = mod[:, :SUB].reshape(depth, SUB, 6, d).transpose(0, 2, 1, 3)
    mod_ctx = jnp.broadcast_to(mod[:, SUB].reshape(depth, 6, 1, d), (depth, 6, SUB, d))
    mod = jnp.stack([mod_ctx, mod_lat], axis=1)

    pos = _pos_tables(n_ctx_tiles, n_lat_tiles, d)
    wg_lru = jnp.concatenate([_block_diag(lru_wa, 4), _block_diag(lru_wx, 4)], axis=-1).astype(BF16)
    pw = _block_diag(pool_w, 2).astype(BF16)
    h0 = jnp.concatenate([jnp.zeros_like(state_lru[None]), state_lru[None]], axis=0)

    states = []
    for l in range(depth):
        jdx = l // 2
        if l == 0:
            proj, x = _inproj_call(cfg, x, mod[l], norm_mix_g[l], w_in[l].astype(BF16), pos)
        else:
            (proj,) = _inproj_call(cfg, x, mod[l], norm_mix_g[l], w_in[l].astype(BF16))
        hf, st_f = _scan_fwd_call(cfg, proj, conv_w[l], conv_b[l], wg_lru[l, 0], lru_ba[l, 0],
                                  lru_bx[l, 0], lru_lam[l, 0], h0[:, :, l, 0])
        ymix, st_b = _scan_bwd_call(cfg, tc, proj, hf, conv_w[l], conv_b[l], wg_lru[l, 1], lru_ba[l, 1],
                                    lru_bx[l, 1], lru_lam[l, 1], h0[:, :, l, 1], pw[l], pool_b[l],
                                    pool_scale[l])
        states.append(jnp.stack([st_f[:n_groups].reshape(bc, c_lru), st_b[:n_groups].reshape(bc, c_lru)], axis=1))
        if l % 2 == 0:
            x = _dense_ffn_call(cfg, x, ymix, mod[l], w_out[l].astype(BF16), norm_ffn_g[l],
                                ffn_wg[jdx].astype(BF16), ffn_wu[jdx].astype(BF16), ffn_wd[jdx].astype(BF16))
        else:
            xm, hp, meta, cnt = _route_call(cfg, x, ymix, mod[l], w_out[l].astype(BF16), norm_ffn_g[l],
                                            moe_router_w[jdx], moe_router_b[jdx])
            pos, tile_e, n_active = _routing_tables(cfg, meta, cnt)
            buf = _dispatch_call(cfg, pos, hp)
            ys = _expert_call(cfg, tile_e, n_active, buf, moe_wg[jdx].astype(BF16), moe_wu[jdx].astype(BF16),
                              moe_wd[jdx].astype(BF16))
            x = _combine_call(cfg, pos, xm, meta, mod[l], ys, norm_final_g, final=(l == depth - 1))

    y = x if depth % 2 == 0 else _final_norm_call(cfg, x, norm_final_g)
    y_prompt = y[:n_ctx].reshape(n_groups, tc, SUB, d).transpose(0, 2, 1, 3).reshape(bc, tc, d)
    y_sample = y[n_ctx:].reshape(tl, SUB, d).transpose(1, 0, 2)
    new_state = jnp.stack(states, axis=1)
    return (y_prompt, y_sample, new_state)
```

```python
import functools

import jax
import jax.numpy as jnp
from jax import lax
from jax.experimental import pallas as pl
from jax.experimental.pallas import tpu as pltpu

F32 = jnp.float32
BF16 = jnp.bfloat16

SUB = 8
LANES = 128
GRID_W = 64
TM = GRID_W * SUB
POS_BASE = 10000.0
N_LRU_HEADS = 8
CONV_W = 4
CONV_LEFT = CONV_W // 2
CONV_RIGHT = CONV_W - 1 - CONV_LEFT
LRU_C = 8.0
POOL_WINDOWS = (2, 4, 8, 16)
POOL_HALO = 8 * SUB
N_EXPERTS = 8
EPS = 1e-6
FF_CHUNK = 256
VMEM_LIMIT = 56 * 1024 * 1024


def _cparams(sem):
    return pltpu.CompilerParams(dimension_semantics=sem, vmem_limit_bytes=VMEM_LIMIT)


def _const_spec(shape):
    nd = len(shape)
    return pl.BlockSpec(shape, lambda *_: (0,) * nd, pipeline_mode=pl.Buffered(1))


def _rms(x, g):
    ms = jnp.mean(x * x, axis=-1, keepdims=True)
    return x * lax.rsqrt(ms + EPS) * g


def _per_seq(x, v, op):
    r, c = x.shape
    x3 = x.reshape(r // SUB, SUB, c)
    return op(x3, v[None]).reshape(r, c)


def _modulate(h, scale, shift):
    r, c = h.shape
    h3 = h.reshape(r // SUB, SUB, c)
    return (h3 * (1.0 + scale)[None] + shift[None]).reshape(r, c)


def _gated_add(x, gate, y):
    return x + _per_seq(y, gate, lambda a, b: a * b)


def _dot(a, b):
    return jnp.dot(a, b, preferred_element_type=F32)


def _sigmoid(x):
    return 0.5 * jnp.tanh(0.5 * x) + 0.5


def _ada_kernel(c_ref, w_ref, b_ref, o_ref):
    c = c_ref[...]
    s = (c * jax.nn.sigmoid(c)).astype(BF16)
    o_ref[0] = _dot(s, w_ref[0].astype(BF16)) + b_ref[0]


def _ada_call(cond, w_ada, b_ada):
    depth, d, d6 = w_ada.shape
    nr = cond.shape[0]
    bn = d6 // 4
    return pl.pallas_call(
        _ada_kernel,
        grid=(depth, d6 // bn),
        in_specs=[
            pl.BlockSpec((nr, d), lambda l, n: (0, 0)),
            pl.BlockSpec((1, d, bn), lambda l, n: (l, 0, n)),
            pl.BlockSpec((1, 1, bn), lambda l, n: (l, 0, n)),
        ],
        out_specs=pl.BlockSpec((1, nr, bn), lambda l, n: (l, 0, n)),
        out_shape=jax.ShapeDtypeStruct((depth, nr, d6), F32),
        compiler_params=_cparams(("parallel", "parallel")),
        name="ada_mod",
    )(cond, w_ada, b_ada.reshape(depth, 1, d6))


def _to_time_major(x):
    s, t, d = x.shape
    return jnp.swapaxes(x, 0, 1).reshape(s * t, d)


def _from_time_major(x):
    r, d = x.shape
    return jnp.swapaxes(x.reshape(r // SUB, SUB, d), 0, 1)


def _inproj_kernel(n_ctx_tiles, *refs):
    if n_ctx_tiles is not None:
        xp_ref, xs_ref, row_ref, col_ref, mod_ref, g_ref, w_ref, o_ref, x0_ref = refs
        is_ctx = pl.program_id(0) < n_ctx_tiles

        @pl.when(is_ctx)
        def _():
            x0_ref[...] = _to_time_major(xp_ref[0])

        @pl.when(jnp.logical_not(is_ctx))
        def _():
            x = _to_time_major(xs_ref[...])
            half = x.shape[1] // 2
            x0_ref[...] = jnp.concatenate([x[:, :half] + row_ref[0], x[:, half:] + col_ref[...]], axis=1)

        x = x0_ref[...]
    else:
        x_ref, mod_ref, g_ref, w_ref, o_ref = refs
        x = x_ref[...]
    h = _modulate(_rms(x, g_ref[...]), mod_ref[0, 1], mod_ref[0, 0])
    o_ref[...] = _dot(h.astype(BF16), w_ref[...])


def _natural_specs(cfg):
    n_ctx_tiles, tiles_per_ctx, _, d = cfg
    n_groups = n_ctx_tiles // tiles_per_ctx

    def ctx_idx(j):
        jc = jnp.minimum(j, n_ctx_tiles - 1)
        return (jc // tiles_per_ctx, 0, lax.rem(jc, tiles_per_ctx), 0)

    del n_groups
    return (pl.BlockSpec((1, SUB, GRID_W, d), ctx_idx),
            pl.BlockSpec((SUB, GRID_W, d), lambda j: (0, jnp.maximum(j - n_ctx_tiles, 0), 0)))


def _wide(cfg):
    n_ctx_tiles, _, n_tiles, _ = cfg
    return 2 if n_ctx_tiles % 2 == 0 and n_tiles % 2 == 0 else 1


def _mod_spec(cfg, f=1):
    n_ctx_steps = cfg[0] // f
    d = cfg[3]
    return pl.BlockSpec((1, 6, SUB, d), lambda j: (jnp.where(j >= n_ctx_steps, 1, 0), 0, 0, 0))


def _inproj_call(cfg, x, mod_l, g, w_in, first=None):
    n_ctx_tiles, _, n_tiles, d = cfg
    d_in = w_in.shape[1]
    f = 1 if first is not None else _wide(cfg)
    tm = TM * f
    row_spec = pl.BlockSpec((tm, d), lambda j: (j, 0))
    out_specs = [pl.BlockSpec((tm, d_in), lambda j: (j, 0))]
    out_shape = [jax.ShapeDtypeStruct((n_tiles * TM, d_in), F32)]
    if first is not None:
        xp_spec, xs_spec = _natural_specs(cfg)
        in_specs = [xp_spec, xs_spec,
                    pl.BlockSpec((1, 1, d // 2), lambda j: (jnp.maximum(j - n_ctx_tiles, 0), 0, 0)),
                    _const_spec((TM, d // 2))]
        args = list(first)
        out_specs.append(row_spec)
        out_shape.append(jax.ShapeDtypeStruct((n_tiles * TM, d), F32))
    else:
        in_specs = [row_spec]
        args = [x]
    in_specs += [_mod_spec(cfg, f), _const_spec((1, d)), _const_spec((d, d_in))]
    args += [mod_l, g.reshape(1, d), w_in]
    return pl.pallas_call(
        functools.partial(_inproj_kernel, n_ctx_tiles if first is not None else None),
        grid=(n_tiles // f,),
        in_specs=in_specs,
        out_specs=out_specs,
        out_shape=out_shape,
        compiler_params=_cparams(("parallel",)),
        name="in_proj",
    )(*args)


def _seq_flags(cfg, j):
    n_ctx_tiles, tiles_per_ctx, n_tiles, _ = cfg
    is_ctx = j < n_ctx_tiles
    pos = lax.rem(j, tiles_per_ctx)
    first = jnp.where(is_ctx, pos == 0, j == n_ctx_tiles)
    last = jnp.where(is_ctx, pos == tiles_per_ctx - 1, j == n_tiles - 1)
    return is_ctx, pos, first, last


def _conv(pad_ref, xa, prev, nxt, first, last, cw, cb):
    lo = CONV_LEFT * SUB
    pad_ref[0:lo, :] = jnp.where(first, 0.0, prev)
    pad_ref[lo:lo + TM, :] = xa
    pad_ref[lo + TM:lo + TM + CONV_RIGHT * SUB, :] = jnp.where(last, 0.0, nxt)
    y = cb
    for k in range(CONV_W):
        y = y + pad_ref[k * SUB:k * SUB + TM, :] * cw[k:k + 1, :]
    return y


def _gates(xc, wg_ref, ba, bx, lam):
    half = xc.shape[1] // 2
    xb = xc.astype(BF16)
    g0 = _dot(xb[:, :half], wg_ref[0])
    g1 = _dot(xb[:, half:], wg_ref[1])
    r = _sigmoid(jnp.concatenate([g0[:, :half], g1[:, :half]], axis=1) + ba)
    i = _sigmoid(jnp.concatenate([g0[:, half:], g1[:, half:]], axis=1) + bx)
    z = -lam
    decay = LRU_C * (jnp.maximum(z, 0.0) + jnp.log1p(jnp.exp(-jnp.abs(z))))
    neg_log_a = r * decay
    a = jnp.exp(-neg_log_a)
    z = jnp.tanh(neg_log_a) * (a * a + 1.0)
    root = jnp.where(z > 0.0, z * lax.rsqrt(z), 0.0)
    u = root * (i * xc)
    return a, u


def _scan(a_ref, u_ref, h_ref, h, reverse):
    steps = TM // SUB

    def body(k, h):
        t = steps - 1 - k if reverse else k
        r0 = pl.multiple_of(t * SUB, SUB)
        h = a_ref[pl.ds(r0, SUB), :] * h + u_ref[pl.ds(r0, SUB), :]
        h_ref[pl.ds(r0, SUB), :] = h
        return h

    return lax.fori_loop(0, steps, body, h, unroll=8)


def _scan_fwd_kernel(cfg, xa_ref, xp_ref, xn_ref, cw_ref, cb_ref, wg_ref, ba_ref, bx_ref,
                     lam_ref, h0_ref, hf_ref, st_ref, pad_ref, a_ref, u_ref, carry_ref):
    j = pl.program_id(0)
    _, _, first, last = _seq_flags(cfg, j)
    xc = _conv(pad_ref, xa_ref[...], xp_ref[...], xn_ref[...], first, last, cw_ref[...], cb_ref[...])
    a, u = _gates(xc, wg_ref, ba_ref[...], bx_ref[...], lam_ref[...])
    a_ref[...] = a
    u_ref[...] = u

    @pl.when(first)
    def _():
        carry_ref[...] = h0_ref[0]

    h = _scan(a_ref, u_ref, hf_ref, carry_ref[...], reverse=False)
    carry_ref[...] = h
    st_ref[0] = h


def _halo_specs(cfg, col, rows_prev, rows_next, tile_of):
    n_tiles = cfg[2]
    c = 512
    nb_prev = TM // rows_prev
    nb_next = TM // rows_next
    last_next = n_tiles * nb_next - 1
    prev = pl.BlockSpec((rows_prev, c), lambda j: (jnp.maximum(tile_of(j) * nb_prev - 1, 0), col))
    nxt = pl.BlockSpec((rows_next, c), lambda j: (jnp.minimum((tile_of(j) + 1) * nb_next, last_next), col))
    return prev, nxt


def _state_spec(cfg, tile_of):
    n_ctx_tiles, tiles_per_ctx, _, _ = cfg
    n_groups = n_ctx_tiles // tiles_per_ctx
    c = 512
    return pl.BlockSpec((1, SUB, c), lambda j: (jnp.minimum(tile_of(j) // tiles_per_ctx, n_groups), 0, 0))


def _h0_spec(cfg, tile_of):
    n_ctx_tiles = cfg[0]
    return pl.BlockSpec((1, SUB, 512), lambda j: (jnp.where(tile_of(j) >= n_ctx_tiles, 1, 0), 0, 0))


def _scan_fwd_call(cfg, proj, conv_w, conv_b, wg, ba, bx, lam, h0):
    n_ctx_tiles, tiles_per_ctx, n_tiles, _ = cfg
    n_groups = n_ctx_tiles // tiles_per_ctx
    c = conv_w.shape[1]
    ident = lambda j: j
    xp_spec, xn_spec = _halo_specs(cfg, 0, CONV_LEFT * SUB, CONV_RIGHT * SUB, ident)
    return pl.pallas_call(
        functools.partial(_scan_fwd_kernel, cfg),
        grid=(n_tiles,),
        in_specs=[
            pl.BlockSpec((TM, c), lambda j: (j, 0)), xp_spec, xn_spec,
            _const_spec((CONV_W, c)), _const_spec((1, c)), _const_spec((2, c // 2, c)),
            _const_spec((1, c)), _const_spec((1, c)), _const_spec((1, c)),
            _h0_spec(cfg, ident),
        ],
        out_specs=[pl.BlockSpec((TM, c), lambda j: (j, 0)), _state_spec(cfg, ident)],
        out_shape=[jax.ShapeDtypeStruct((n_tiles * TM, c), F32),
                   jax.ShapeDtypeStruct((n_groups + 1, SUB, c), F32)],
        scratch_shapes=[
            pltpu.VMEM((TM + (CONV_W - 1) * SUB, c), F32),
            pltpu.VMEM((TM, c), F32), pltpu.VMEM((TM, c), F32), pltpu.VMEM((SUB, c), F32),
        ],
        compiler_params=_cparams(("arbitrary",)),
        name="lru_fwd",
    )(proj, proj, proj, conv_w, conv_b.reshape(1, c), wg, ba.reshape(1, c), bx.reshape(1, c),
      lam.reshape(1, c), h0)


def _pool_mix(pad_ref, t0, t_len, pw_ref, pb, ps):
    c = pad_ref.shape[1]
    gw = c // len(POOL_WINDOWS)
    t_pos = lax.shift_right_logical(lax.broadcasted_iota(jnp.int32, (TM, gw), 0), 3) + t0
    outs = []
    for g, k in enumerate(POOL_WINDOWS):
        left = k // 2
        right = k - 1 - left
        s = None
        for o in range(-left, right + 1):
            v = pad_ref[POOL_HALO + o * SUB:POOL_HALO + o * SUB + TM, g * gw:(g + 1) * gw]
            s = v if s is None else s + v
        cnt = (jnp.minimum(t_pos + right + 1, t_len) - jnp.maximum(t_pos - left, 0)).astype(F32)
        outs.append(s / cnt - pad_ref[POOL_HALO:POOL_HALO + TM, g * gw:(g + 1) * gw])
    d = jnp.concatenate(outs, axis=1).astype(BF16)
    half = c // 2
    y = jnp.concatenate([_dot(d[:, :half], pw_ref[0]), _dot(d[:, half:], pw_ref[1])], axis=1)
    return (y + pb) * ps


def _scan_bwd_kernel(cfg, ctx_len, proj_ref, xp_ref, xn_ref, bp_ref, bn_ref, hf_ref, cw_ref, cb_ref,
                     wg_ref, ba_ref, bx_ref, lam_ref, h0_ref, pw_ref, pb_ref, ps_ref,
                     y_ref, st_ref, pad_ref, ppad_ref, a_ref, u_ref, hb_ref, carry_ref):
    n_tiles = cfg[2]
    j = n_tiles - 1 - pl.program_id(0)
    is_ctx, pos, first, last = _seq_flags(cfg, j)
    c = hf_ref.shape[1]
    xc = _conv(pad_ref, proj_ref[:, 0:c], xp_ref[...], xn_ref[...], first, last, cw_ref[...], cb_ref[...])
    a, u = _gates(xc, wg_ref, ba_ref[...], bx_ref[...], lam_ref[...])
    a_ref[...] = a
    u_ref[...] = u

    @pl.when(last)
    def _():
        carry_ref[...] = h0_ref[0]

    h = _scan(a_ref, u_ref, hb_ref, carry_ref[...], reverse=True)
    carry_ref[...] = h
    st_ref[0] = h

    ga = proj_ref[:, c:2 * c]
    y_a = (hf_ref[...] + hb_ref[...]) * jax.nn.gelu(ga, approximate=True)

    use_prev = jnp.logical_and(is_ctx, jnp.logical_not(first))
    use_next = jnp.logical_and(is_ctx, jnp.logical_not(last))
    ppad_ref[0:POOL_HALO, :] = jnp.where(use_prev, bp_ref[...], 0.0)
    ppad_ref[POOL_HALO:POOL_HALO + TM, :] = proj_ref[:, 2 * c:3 * c]
    ppad_ref[POOL_HALO + TM:, :] = jnp.where(use_next, bn_ref[...], 0.0)
    t0 = jnp.where(is_ctx, pos * GRID_W, 0)
    t_len = jnp.where(is_ctx, ctx_len, GRID_W)
    y_b = _pool_mix(ppad_ref, t0, t_len, pw_ref, pb_ref[...], ps_ref[...])
    y_ref[...] = jnp.concatenate([y_a, y_b], axis=1).astype(BF16)


def _scan_bwd_call(cfg, ctx_len, proj, hf, conv_w, conv_b, wg, ba, bx, lam, h0, pw, pb, ps):
    n_ctx_tiles, tiles_per_ctx, n_tiles, d = cfg
    n_groups = n_ctx_tiles // tiles_per_ctx
    c = conv_w.shape[1]
    rev = lambda j: n_tiles - 1 - j
    xp_spec, xn_spec = _halo_specs(cfg, 0, CONV_LEFT * SUB, CONV_RIGHT * SUB, rev)
    bp_spec, bn_spec = _halo_specs(cfg, 2, POOL_HALO, POOL_HALO, rev)
    return pl.pallas_call(
        functools.partial(_scan_bwd_kernel, cfg, ctx_len),
        grid=(n_tiles,),
        in_specs=[
            pl.BlockSpec((TM, 3 * c), lambda j: (rev(j), 0)), xp_spec, xn_spec, bp_spec, bn_spec,
            pl.BlockSpec((TM, c), lambda j: (rev(j), 0)),
            _const_spec((CONV_W, c)), _const_spec((1, c)), _const_spec((2, c // 2, c)),
            _const_spec((1, c)), _const_spec((1, c)), _const_spec((1, c)),
            _h0_spec(cfg, rev),
            _const_spec((2, c // 2, c // 2)), _const_spec((1, c)), _const_spec((1, c)),
        ],
        out_specs=[pl.BlockSpec((TM, d), lambda j: (rev(j), 0)), _state_spec(cfg, rev)],
        out_shape=[jax.ShapeDtypeStruct((n_tiles * TM, d), BF16),
                   jax.ShapeDtypeStruct((n_groups + 1, SUB, c), F32)],
        scratch_shapes=[
            pltpu.VMEM((TM + (CONV_W - 1) * SUB, c), F32),
            pltpu.VMEM((TM + 2 * POOL_HALO, c), F32),
            pltpu.VMEM((TM, c), F32), pltpu.VMEM((TM, c), F32), pltpu.VMEM((TM, c), F32),
            pltpu.VMEM((SUB, c), F32),
        ],
        compiler_params=_cparams(("arbitrary",)),
        name="lru_bwd_mix",
    )(proj, proj, proj, proj, proj, hf, conv_w, conv_b.reshape(1, c), wg, ba.reshape(1, c),
      bx.reshape(1, c), lam.reshape(1, c), h0, pw, pb.reshape(1, c), ps.reshape(1, c))


def _out_proj(x_ref, ym_ref, mod_ref, wo_ref, g2_ref):
    xm = _gated_add(x_ref[...], mod_ref[0, 2], _dot(ym_ref[...], wo_ref[...]))
    h2 = _modulate(_rms(xm, g2_ref[...]), mod_ref[0, 4], mod_ref[0, 3])
    return xm, h2


def _swiglu_act(h2b, wg_ref, wu_ref, act_ref):
    n_chunks = act_ref.shape[1] // FF_CHUNK

    def body(f, carry):
        c0 = pl.multiple_of(f * FF_CHUNK, FF_CHUNK)
        g = _dot(h2b, wg_ref[:, pl.ds(c0, FF_CHUNK)])
        u = _dot(h2b, wu_ref[:, pl.ds(c0, FF_CHUNK)])
        act_ref[:, pl.ds(c0, FF_CHUNK)] = (g * _sigmoid(g) * u).astype(BF16)
        return carry

    lax.fori_loop(0, n_chunks, body, 0)


def _dense_ffn_kernel(x_ref, ym_ref, mod_ref, wo_ref, g2_ref, wg_ref, wu_ref, wd_ref, o_ref, act_ref):
    xm, h2 = _out_proj(x_ref, ym_ref, mod_ref, wo_ref, g2_ref)
    _swiglu_act(h2.astype(BF16), wg_ref, wu_ref, act_ref)
    o_ref[...] = _gated_add(xm, mod_ref[0, 5], _dot(act_ref[...], wd_ref[...]))


def _dense_ffn_call(cfg, x, ymix, mod_l, w_out, g2, wg, wu, wd):
    _, _, n_tiles, d = cfg
    d_ff = wg.shape[1]
    f = _wide(cfg)
    row_spec = pl.BlockSpec((TM * f, d), lambda j: (j, 0))
    return pl.pallas_call(
        _dense_ffn_kernel,
        grid=(n_tiles // f,),
        in_specs=[row_spec, row_spec, _mod_spec(cfg, f), _const_spec((d, d)), _const_spec((1, d)),
                  _const_spec((d, d_ff)), _const_spec((d, d_ff)), _const_spec((d_ff, d))],
        out_specs=row_spec,
        out_shape=jax.ShapeDtypeStruct((n_tiles * TM, d), F32),
        scratch_shapes=[pltpu.VMEM((TM * f, d_ff), BF16)],
        compiler_params=_cparams(("parallel",)),
        name="out_proj_dense_ffn",
    )(x, ymix, mod_l, w_out, g2.reshape(1, d), wg, wu, wd)


M_E1, M_E2, M_W1, M_W2, M_R1, M_R2 = range(6)


def _route_kernel(x_ref, ym_ref, mod_ref, wo_ref, g2_ref, rw_ref, rb_ref, tri_ref,
                  xm_ref, hp_ref, meta_ref, cnt_ref, run_ref):
    xm, h2 = _out_proj(x_ref, ym_ref, mod_ref, wo_ref, g2_ref)
    xm_ref[...] = xm
    h_hi = h2.astype(BF16)
    h_hi32 = h_hi.astype(F32)
    bits = lax.bitcast_convert_type(h_hi32, jnp.uint32)
    half = bits.shape[1] // 2
    hp_ref[...] = lax.shift_right_logical(bits[:, :half], jnp.uint32(16)) | (bits[:, half:] & jnp.uint32(0xFFFF0000))

    h_lo = (h2 - h_hi32).astype(BF16)
    p = _dot(h_hi, rw_ref[...])
    logits = p[:, :LANES] + p[:, LANES:] + _dot(h_lo, rw_ref[:, :LANES])
    lane = lax.broadcasted_iota(jnp.int32, logits.shape, 1)
    neg = jnp.float32(-jnp.inf)
    lg = jnp.where(lane < N_EXPERTS, logits + rb_ref[...], neg)
    m1 = jnp.max(lg, axis=1, keepdims=True)
    i1 = jnp.min(jnp.where(lg == m1, lane, LANES), axis=1, keepdims=True)
    lg2 = jnp.where(lane == i1, neg, lg)
    m2 = jnp.max(lg2, axis=1, keepdims=True)
    i2 = jnp.min(jnp.where(lg2 == m2, lane, LANES), axis=1, keepdims=True)
    e2 = jnp.exp(m2 - m1)
    den = 1.0 + e2

    @pl.when(pl.program_id(0) == 0)
    def _():
        run_ref[...] = jnp.zeros_like(run_ref)

    sel1 = lane == i1
    sel2 = lane == i2
    onehot = jnp.where(jnp.logical_or(sel1, sel2), 1.0, 0.0)
    rank = _dot(tri_ref[...], onehot.astype(BF16)) + run_ref[...]
    r1 = jnp.sum(jnp.where(sel1, rank, 0.0), axis=1, keepdims=True)
    r2 = jnp.sum(jnp.where(sel2, rank, 0.0), axis=1, keepdims=True)
    run_ref[...] = run_ref[...] + jnp.sum(onehot, axis=0, keepdims=True)
    cnt_ref[...] = run_ref[...]

    meta = jnp.zeros(logits.shape, F32)
    for k, v in ((M_E1, i1.astype(F32)), (M_E2, i2.astype(F32)), (M_W1, 1.0 / den), (M_W2, e2 / den),
                 (M_R1, r1), (M_R2, r2)):
        meta = jnp.where(lane == k, v, meta)
    meta_ref[...] = meta


def _route_call(cfg, x, ymix, mod_l, w_out, g2, router_w, router_b):
    _, _, n_tiles, d = cfg
    n = n_tiles * TM
    f = _wide(cfg)
    tm = TM * f
    row_spec = pl.BlockSpec((tm, d), lambda j: (j, 0))
    rw = jnp.zeros((d, LANES), F32).at[:, :N_EXPERTS].set(router_w)
    rw_hi = rw.astype(BF16)
    rw = jnp.concatenate([rw_hi, (rw - rw_hi.astype(F32)).astype(BF16)], axis=1)
    rb = jnp.zeros((1, LANES), F32).at[0, :N_EXPERTS].set(router_b)
    tri = jnp.tril(jnp.ones((tm, tm), BF16), -1)
    return pl.pallas_call(
        _route_kernel,
        grid=(n_tiles // f,),
        in_specs=[row_spec, row_spec, _mod_spec(cfg, f), _const_spec((d, d)), _const_spec((1, d)),
                  _const_spec((d, 2 * LANES)), _const_spec((1, LANES)), _const_spec((tm, tm))],
        out_specs=[row_spec, pl.BlockSpec((tm, d // 2), lambda j: (j, 0)),
                   pl.BlockSpec((tm, LANES), lambda j: (j, 0)), pl.BlockSpec((1, LANES), lambda j: (0, 0))],
        out_shape=[jax.ShapeDtypeStruct((n, d), F32), jax.ShapeDtypeStruct((n, d // 2), jnp.uint32),
                   jax.ShapeDtypeStruct((n, LANES), F32), jax.ShapeDtypeStruct((1, LANES), F32)],
        scratch_shapes=[pltpu.VMEM((1, LANES), F32)],
        compiler_params=_cparams(("arbitrary",)),
        name="out_proj_route",
    )(x, ymix, mod_l, w_out, g2.reshape(1, d), rw, rb, tri)


def _routing_tables(cfg, meta, cnt):
    n_tiles = cfg[2]
    counts = cnt[0, :N_EXPERTS].astype(jnp.int32)
    padded = ((counts + TM - 1) // TM) * TM
    ends = jnp.cumsum(padded)
    offs = ends - padded
    e1 = meta[:, M_E1].astype(jnp.int32)
    e2 = meta[:, M_E2].astype(jnp.int32)
    pos1 = offs[e1] + meta[:, M_R1].astype(jnp.int32)
    pos2 = offs[e2] + meta[:, M_R2].astype(jnp.int32)
    pos = jnp.concatenate([pos1.reshape(n_tiles, 1, TM), pos2.reshape(n_tiles, 1, TM)], axis=2)
    n_sorted_tiles = 2 * n_tiles + N_EXPERTS
    starts = jnp.arange(n_sorted_tiles, dtype=jnp.int32) * TM
    tile_e = jnp.minimum(jnp.sum((starts[:, None] >= ends[None, :]).astype(jnp.int32), axis=1), N_EXPERTS - 1)
    n_active = (ends[-1] // TM).reshape(1)
    return pos, tile_e, n_active


def _load_positions(pos_ref, idx_ref, sem):
    cp = pltpu.make_async_copy(pos_ref.at[0, 0], idx_ref, sem)
    cp.start()
    cp.wait()


def _dispatch_kernel(pos_ref, hp_ref, buf_in_ref, buf_ref, idx_ref, sem_i, sem_a, sem_b):
    del buf_in_ref
    _load_positions(pos_ref, idx_ref, sem_i)

    def body(r, carry):
        src = hp_ref.at[pl.ds(r, 1)]
        pltpu.make_async_copy(src, buf_ref.at[pl.ds(idx_ref[r], 1)], sem_a).start()
        pltpu.make_async_copy(src, buf_ref.at[pl.ds(idx_ref[TM + r], 1)], sem_b).start()
        return carry

    lax.fori_loop(0, TM, body, 0, unroll=8)
    pltpu.make_async_copy(hp_ref, buf_ref.at[pl.ds(0, TM)], sem_a).wait()
    pltpu.make_async_copy(hp_ref, buf_ref.at[pl.ds(0, TM)], sem_b).wait()


def _dispatch_call(cfg, pos, hp):
    _, _, n_tiles, d = cfg
    n_sorted = (2 * n_tiles + N_EXPERTS) * TM
    buf = jnp.zeros((n_sorted, d // 2), jnp.uint32)
    return pl.pallas_call(
        _dispatch_kernel,
        grid=(n_tiles,),
        in_specs=[pl.BlockSpec((1, 1, 2 * TM), lambda j: (j, 0, 0)),
                  pl.BlockSpec((TM, d // 2), lambda j: (j, 0)),
                  pl.BlockSpec(memory_space=pl.ANY)],
        out_specs=pl.BlockSpec(memory_space=pl.ANY),
        out_shape=jax.ShapeDtypeStruct((n_sorted, d // 2), jnp.uint32),
        input_output_aliases={2: 0},
        scratch_shapes=[pltpu.SMEM((2 * TM,), jnp.int32), pltpu.SemaphoreType.DMA,
                        pltpu.SemaphoreType.DMA, pltpu.SemaphoreType.DMA],
        compiler_params=_cparams(("arbitrary",)),
        name="moe_dispatch",
    )(pos, hp, buf)


def _expert_kernel(te_ref, na_ref, s_ref, wg_ref, wu_ref, wd_ref, o_ref):
    del te_ref

    @pl.when(pl.program_id(0) < na_ref[0])
    def _():
        w = s_ref[...]
        lo = lax.bitcast_convert_type(lax.shift_left(w, jnp.uint32(16)), F32)
        hi = lax.bitcast_convert_type(w & jnp.uint32(0xFFFF0000), F32)
        h = jnp.concatenate([lo, hi], axis=1).astype(BF16)
        g = _dot(h, wg_ref[0])
        u = _dot(h, wu_ref[0])
        act = (g * _sigmoid(g) * u).astype(BF16)
        o_ref[...] = _dot(act, wd_ref[0])

    @pl.when(pl.program_id(0) >= na_ref[0])
    def _():
        o_ref[...] = jnp.zeros_like(o_ref)


def _expert_call(cfg, tile_e, n_active, buf, wg, wu, wd):
    _, _, n_tiles, d = cfg
    dfe = wg.shape[2]
    n_sorted_tiles = 2 * n_tiles + N_EXPERTS
    last = lambda t, na: jnp.maximum(jnp.minimum(t, na[0] - 1), 0)
    tile = lambda t, te, na: (last(t, na), 0)
    w_spec = lambda shape: pl.BlockSpec(shape, lambda t, te, na: (te[last(t, na)], 0, 0))
    return pl.pallas_call(
        _expert_kernel,
        grid_spec=pltpu.PrefetchScalarGridSpec(
            num_scalar_prefetch=2,
            grid=(n_sorted_tiles,),
            in_specs=[pl.BlockSpec((TM, d // 2), tile), w_spec((1, d, dfe)), w_spec((1, d, dfe)),
                      w_spec((1, dfe, d))],
            out_specs=pl.BlockSpec((TM, d), lambda t, te, na: (t, 0)),
        ),
        out_shape=jax.ShapeDtypeStruct((n_sorted_tiles * TM, d), F32),
        compiler_params=_cparams(("arbitrary",)),
        name="moe_experts",
    )(tile_e, n_active, buf, wg, wu, wd)


def _store_natural(n_ctx_tiles, y, yp_ref, ys_ref):
    is_ctx = pl.program_id(0) < n_ctx_tiles

    @pl.when(is_ctx)
    def _():
        yp_ref[0] = _from_time_major(y)

    @pl.when(jnp.logical_not(is_ctx))
    def _():
        ys_ref[...] = _from_time_major(y)


def _natural_out(cfg):
    n_ctx_tiles, tiles_per_ctx, n_tiles, d = cfg
    n_groups = n_ctx_tiles // tiles_per_ctx
    return (list(_natural_specs(cfg)),
            [jax.ShapeDtypeStruct((n_groups, SUB, tiles_per_ctx * GRID_W, d), F32),
             jax.ShapeDtypeStruct((SUB, (n_tiles - n_ctx_tiles) * GRID_W, d), F32)])


def _combine_kernel(final_ctx_tiles, pos_ref, xm_ref, meta_ref, mod_ref, gfin_ref, ys_ref, *refs):
    o_refs, (idx_ref, y1_ref, y2_ref, sem_i, sem_a, sem_b) = refs[:-6], refs[-6:]
    _load_positions(pos_ref, idx_ref, sem_i)

    def body(r, carry):
        pltpu.make_async_copy(ys_ref.at[pl.ds(idx_ref[r], 1)], y1_ref.at[pl.ds(r, 1)], sem_a).start()
        pltpu.make_async_copy(ys_ref.at[pl.ds(idx_ref[TM + r], 1)], y2_ref.at[pl.ds(r, 1)], sem_b).start()
        return carry

    lax.fori_loop(0, TM, body, 0, unroll=8)
    pltpu.make_async_copy(ys_ref.at[pl.ds(0, TM)], y1_ref, sem_a).wait()
    pltpu.make_async_copy(ys_ref.at[pl.ds(0, TM)], y2_ref, sem_b).wait()
    meta = meta_ref[...]
    y = meta[:, M_W1:M_W1 + 1] * y1_ref[...] + meta[:, M_W2:M_W2 + 1] * y2_ref[...]
    x = _gated_add(xm_ref[...], mod_ref[0, 5], y)
    if final_ctx_tiles is None:
        o_refs[0][...] = x
    else:
        _store_natural(final_ctx_tiles, _rms(x, gfin_ref[...]), *o_refs)


def _combine_call(cfg, pos, xm, meta, mod_l, ys, g_final, final):
    n_ctx_tiles, _, n_tiles, d = cfg
    row_spec = pl.BlockSpec((TM, d), lambda j: (j, 0))
    if final:
        out_specs, out_shape = _natural_out(cfg)
    else:
        out_specs, out_shape = [row_spec], [jax.ShapeDtypeStruct((n_tiles * TM, d), F32)]
    outs = pl.pallas_call(
        functools.partial(_combine_kernel, n_ctx_tiles if final else None),
        grid=(n_tiles,),
        in_specs=[pl.BlockSpec((1, 1, 2 * TM), lambda j: (j, 0, 0)), row_spec,
                  pl.BlockSpec((TM, LANES), lambda j: (j, 0)), _mod_spec(cfg), _const_spec((1, d)),
                  pl.BlockSpec(memory_space=pl.ANY)],
        out_specs=out_specs,
        out_shape=out_shape,
        scratch_shapes=[pltpu.SMEM((2 * TM,), jnp.int32), pltpu.VMEM((TM, d), F32), pltpu.VMEM((TM, d), F32),
                        pltpu.SemaphoreType.DMA, pltpu.SemaphoreType.DMA, pltpu.SemaphoreType.DMA],
        compiler_params=_cparams(("arbitrary",)),
        name="moe_combine",
    )(pos, xm, meta, mod_l, g_final.reshape(1, d), ys)
    return outs if final else outs[0]


def _final_norm_kernel(n_ctx_tiles, x_ref, g_ref, yp_ref, ys_ref):
    _store_natural(n_ctx_tiles, _rms(x_ref[...], g_ref[...]), yp_ref, ys_ref)


def _final_norm_call(cfg, x, g):
    n_ctx_tiles, _, n_tiles, d = cfg
    out_specs, out_shape = _natural_out(cfg)
    return pl.pallas_call(
        functools.partial(_final_norm_kernel, n_ctx_tiles),
        grid=(n_tiles,),
        in_specs=[pl.BlockSpec((TM, d), lambda j: (j, 0)), _const_spec((1, d))],
        out_specs=out_specs,
        out_shape=out_shape,
        compiler_params=_cparams(("arbitrary",)),
        name="final_norm",
    )(x, g.reshape(1, d))


def _block_diag(w, per_block):
    *lead, n, k, _ = w.shape
    nb = n // per_block
    w = w.reshape(*lead, nb, per_block, k, k)
    eye = jnp.eye(per_block, dtype=w.dtype)
    out = w[..., :, :, None, :] * eye[:, None, :, None]
    return out.reshape(*lead, nb, per_block * k, per_block * k)


def _pos_tables(n_ctx_tiles, n_lat_tiles, d):
    quarter = d // 4
    omega = 1.0 / (POS_BASE ** (jnp.arange(quarter, dtype=F32) / quarter))
    er = jnp.arange(n_lat_tiles, dtype=F32)[:, None] * omega
    ec = jnp.arange(GRID_W, dtype=F32)[:, None] * omega
    row_emb = jnp.concatenate([jnp.sin(er), jnp.cos(er)], axis=-1)
    col_emb = jnp.concatenate([jnp.sin(ec), jnp.cos(ec)], axis=-1)
    del n_ctx_tiles
    return row_emb[:, None, :], jnp.repeat(col_emb, SUB, axis=0)


def kernel(x_prompt, x_sample, state_lru, c, c_ctx, norm_mix_g, w_ada, b_ada, w_in, conv_w, conv_b, lru_wa, lru_ba, lru_wx, lru_bx, lru_lam, pool_w, pool_b, pool_scale, w_out, norm_ffn_g, ffn_wg, ffn_wu, ffn_wd, moe_router_w, moe_router_b, moe_wg, moe_wu, moe_wd, norm_final_g):
    bc, tc, d = x_prompt.shape
    bl, tl, _ = x_sample.shape
    depth = w_ada.shape[0]
    c_lru = conv_w.shape[-1]
    assert bl == SUB and bc % SUB == 0 and tc % GRID_W == 0 and tl % GRID_W == 0
    assert c_lru == 4 * LANES and d == 2 * c_lru
    n_groups = bc // SUB
    tiles_per_ctx = tc // GRID_W
    n_ctx_tiles = n_groups * tiles_per_ctx
    n_lat_tiles = tl // GRID_W
    n_tiles = n_ctx_tiles + n_lat_tiles
    n_ctx = n_ctx_tiles * TM
    cfg = (n_ctx_tiles, tiles_per_ctx, n_tiles, d)

    cond = jnp.concatenate([c, c_ctx[None], jnp.zeros((SUB - 1, d), F32)], axis=0)
    mod = _ada_call(cond, w_ada, b_ada)
    mod_lat = mod[:, :SUB].reshape(depth, SUB, 6, d).transpose(0, 2, 1, 3)
    mod_ctx = jnp.broadcast_to(mod[:, SUB].reshape(depth, 6, 1, d), (depth, 6, SUB, d))
    mod = jnp.stack([mod_ctx, mod_lat], axis=1)

    row_tab, col_rep = _pos_tables(n_ctx_tiles, n_lat_tiles, d)
    first = (x_prompt.reshape(n_groups, SUB, tc, d), x_sample, row_tab, col_rep)
    x = None
    wg_lru = jnp.concatenate([_block_diag(lru_wa, 4), _block_diag(lru_wx, 4)], axis=-1).astype(BF16)
    pw = _block_diag(pool_w, 2).astype(BF16)
    h0 = jnp.concatenate([jnp.zeros_like(state_lru[None]), state_lru[None]], axis=0)

    states = []
    for l in range(depth):
        jdx = l // 2
        if l == 0:
            proj, x = _inproj_call(cfg, x, mod[l], norm_mix_g[l], w_in[l].astype(BF16), first)
        else:
            (proj,) = _inproj_call(cfg, x, mod[l], norm_mix_g[l], w_in[l].astype(BF16))
        hf, st_f = _scan_fwd_call(cfg, proj, conv_w[l], conv_b[l], wg_lru[l, 0], lru_ba[l, 0],
                                  lru_bx[l, 0], lru_lam[l, 0], h0[:, :, l, 0])
        ymix, st_b = _scan_bwd_call(cfg, tc, proj, hf, conv_w[l], conv_b[l], wg_lru[l, 1], lru_ba[l, 1],
                                    lru_bx[l, 1], lru_lam[l, 1], h0[:, :, l, 1], pw[l], pool_b[l],
                                    pool_scale[l])
        states.append(jnp.stack([st_f[:n_groups].reshape(bc, c_lru), st_b[:n_groups].reshape(bc, c_lru)], axis=1))
        if l % 2 == 0:
            x = _dense_ffn_call(cfg, x, ymix, mod[l], w_out[l].astype(BF16), norm_ffn_g[l],
                                ffn_wg[jdx].astype(BF16), ffn_wu[jdx].astype(BF16), ffn_wd[jdx].astype(BF16))
        else:
            xm, hp, meta, cnt = _route_call(cfg, x, ymix, mod[l], w_out[l].astype(BF16), norm_ffn_g[l],
                                            moe_router_w[jdx], moe_router_b[jdx])
            pos, tile_e, n_active = _routing_tables(cfg, meta, cnt)
            buf = _dispatch_call(cfg, pos, hp)
            ys = _expert_call(cfg, tile_e, n_active, buf, moe_wg[jdx].astype(BF16), moe_wu[jdx].astype(BF16),
                              moe_wd[jdx].astype(BF16))
            x = _combine_call(cfg, pos, xm, meta, mod[l], ys, norm_final_g, final=(l == depth - 1))

    y_prompt, y_sample = x if depth % 2 == 0 else _final_norm_call(cfg, x, norm_final_g)
    new_state = jnp.stack(states, axis=1)
    return (y_prompt.reshape(bc, tc, d), y_sample, new_state)
```

```python
import functools

import jax
import jax.numpy as jnp
from jax import lax
from jax.experimental import pallas as pl
from jax.experimental.pallas import tpu as pltpu
from jax.experimental.pallas import tpu_sc as plsc

F32 = jnp.float32
BF16 = jnp.bfloat16

SUB = 8
LANES = 128
GRID_W = 64
TM = GRID_W * SUB
POS_BASE = 10000.0
N_LRU_HEADS = 8
CONV_W = 4
CONV_LEFT = CONV_W // 2
CONV_RIGHT = CONV_W - 1 - CONV_LEFT
LRU_C = 8.0
POOL_WINDOWS = (2, 4, 8, 16)
POOL_HALO = 8 * SUB
N_EXPERTS = 8
EPS = 1e-6
FF_CHUNK = 256
VMEM_LIMIT = 56 * 1024 * 1024


def _cparams(sem):
    return pltpu.CompilerParams(dimension_semantics=sem, vmem_limit_bytes=VMEM_LIMIT)


def _const_spec(shape):
    nd = len(shape)
    return pl.BlockSpec(shape, lambda *_: (0,) * nd, pipeline_mode=pl.Buffered(1))


def _rms(x, g):
    ms = jnp.mean(x * x, axis=-1, keepdims=True)
    return x * lax.rsqrt(ms + EPS) * g


def _per_seq(x, v, op):
    r, c = x.shape
    x3 = x.reshape(r // SUB, SUB, c)
    return op(x3, v[None]).reshape(r, c)


def _modulate(h, scale, shift):
    r, c = h.shape
    h3 = h.reshape(r // SUB, SUB, c)
    return (h3 * (1.0 + scale)[None] + shift[None]).reshape(r, c)


def _gated_add(x, gate, y):
    return x + _per_seq(y, gate, lambda a, b: a * b)


def _dot(a, b):
    return jnp.dot(a, b, preferred_element_type=F32)


def _pack_pairs(x):
    bits = lax.bitcast_convert_type(x, jnp.uint32)
    half = bits.shape[1] // 2
    w = lax.shift_right_logical(bits[:, :half], jnp.uint32(16)) | (bits[:, half:] & jnp.uint32(0xFFFF0000))
    return lax.bitcast_convert_type(w, jnp.int32)


def _unpack_pairs(w):
    w = lax.bitcast_convert_type(w, jnp.uint32)
    lo = lax.bitcast_convert_type(lax.shift_left(w, jnp.uint32(16)), F32)
    hi = lax.bitcast_convert_type(w & jnp.uint32(0xFFFF0000), F32)
    return jnp.concatenate([lo, hi], axis=1)


def _sigmoid(x):
    return 0.5 * jnp.tanh(0.5 * x) + 0.5


def _ada_kernel(c_ref, w_ref, b_ref, o_ref):
    c = c_ref[...]
    s = (c * jax.nn.sigmoid(c)).astype(BF16)
    o_ref[0] = _dot(s, w_ref[0].astype(BF16)) + b_ref[0]


def _ada_call(cond, w_ada, b_ada):
    depth, d, d6 = w_ada.shape
    nr = cond.shape[0]
    bn = d6 // 4
    return pl.pallas_call(
        _ada_kernel,
        grid=(depth, d6 // bn),
        in_specs=[
            pl.BlockSpec((nr, d), lambda l, n: (0, 0)),
            pl.BlockSpec((1, d, bn), lambda l, n: (l, 0, n)),
            pl.BlockSpec((1, 1, bn), lambda l, n: (l, 0, n)),
        ],
        out_specs=pl.BlockSpec((1, nr, bn), lambda l, n: (l, 0, n)),
        out_shape=jax.ShapeDtypeStruct((depth, nr, d6), F32),
        compiler_params=_cparams(("parallel", "parallel")),
        name="ada_mod",
    )(cond, w_ada, b_ada.reshape(depth, 1, d6))


def _to_time_major(x):
    s, t, d = x.shape
    return jnp.swapaxes(x, 0, 1).reshape(s * t, d)


def _from_time_major(x):
    r, d = x.shape
    return jnp.swapaxes(x.reshape(r // SUB, SUB, d), 0, 1)


def _inproj_kernel(n_ctx_tiles, *refs):
    if n_ctx_tiles is not None:
        xp_ref, xs_ref, row_ref, col_ref, mod_ref, g_ref, w_ref, o_ref, x0_ref = refs
        is_ctx = pl.program_id(0) < n_ctx_tiles

        @pl.when(is_ctx)
        def _():
            x0_ref[...] = _to_time_major(xp_ref[0])

        @pl.when(jnp.logical_not(is_ctx))
        def _():
            x = _to_time_major(xs_ref[...])
            half = x.shape[1] // 2
            x0_ref[...] = jnp.concatenate([x[:, :half] + row_ref[0], x[:, half:] + col_ref[...]], axis=1)

        x = x0_ref[...]
    else:
        x_ref, mod_ref, g_ref, w_ref, o_ref = refs
        x = x_ref[...]
    h = _modulate(_rms(x, g_ref[...]), mod_ref[0, 1], mod_ref[0, 0])
    o_ref[...] = _dot(h.astype(BF16), w_ref[...])


def _natural_specs(cfg):
    n_ctx_tiles, tiles_per_ctx, _, d = cfg
    n_groups = n_ctx_tiles // tiles_per_ctx

    def ctx_idx(j):
        jc = jnp.minimum(j, n_ctx_tiles - 1)
        return (jc // tiles_per_ctx, 0, lax.rem(jc, tiles_per_ctx), 0)

    del n_groups
    return (pl.BlockSpec((1, SUB, GRID_W, d), ctx_idx),
            pl.BlockSpec((SUB, GRID_W, d), lambda j: (0, jnp.maximum(j - n_ctx_tiles, 0), 0)))


def _wide(cfg):
    n_ctx_tiles, _, n_tiles, _ = cfg
    return 2 if n_ctx_tiles % 2 == 0 and n_tiles % 2 == 0 else 1


def _mod_spec(cfg, f=1):
    n_ctx_steps = cfg[0] // f
    d = cfg[3]
    return pl.BlockSpec((1, 6, SUB, d), lambda j: (jnp.where(j >= n_ctx_steps, 1, 0), 0, 0, 0))


def _inproj_call(cfg, x, mod_l, g, w_in, first=None):
    n_ctx_tiles, _, n_tiles, d = cfg
    d_in = w_in.shape[1]
    f = 1 if first is not None else _wide(cfg)
    tm = TM * f
    row_spec = pl.BlockSpec((tm, d), lambda j: (j, 0))
    out_specs = [pl.BlockSpec((tm, d_in), lambda j: (j, 0))]
    out_shape = [jax.ShapeDtypeStruct((n_tiles * TM, d_in), F32)]
    if first is not None:
        xp_spec, xs_spec = _natural_specs(cfg)
        in_specs = [xp_spec, xs_spec,
                    pl.BlockSpec((1, 1, d // 2), lambda j: (jnp.maximum(j - n_ctx_tiles, 0), 0, 0)),
                    _const_spec((TM, d // 2))]
        args = list(first)
        out_specs.append(row_spec)
        out_shape.append(jax.ShapeDtypeStruct((n_tiles * TM, d), F32))
    else:
        in_specs = [row_spec]
        args = [x]
    in_specs += [_mod_spec(cfg, f), _const_spec((1, d)), _const_spec((d, d_in))]
    args += [mod_l, g.reshape(1, d), w_in]
    return pl.pallas_call(
        functools.partial(_inproj_kernel, n_ctx_tiles if first is not None else None),
        grid=(n_tiles // f,),
        in_specs=in_specs,
        out_specs=out_specs,
        out_shape=out_shape,
        compiler_params=_cparams(("parallel",)),
        name="in_proj",
    )(*args)


def _seq_flags(cfg, j):
    n_ctx_tiles, tiles_per_ctx, n_tiles, _ = cfg
    is_ctx = j < n_ctx_tiles
    pos = lax.rem(j, tiles_per_ctx)
    first = jnp.where(is_ctx, pos == 0, j == n_ctx_tiles)
    last = jnp.where(is_ctx, pos == tiles_per_ctx - 1, j == n_tiles - 1)
    return is_ctx, pos, first, last


def _conv(pad_ref, xa, prev, nxt, first, last, cw, cb):
    lo = CONV_LEFT * SUB
    pad_ref[0:lo, :] = jnp.where(first, 0.0, prev)
    pad_ref[lo:lo + TM, :] = xa
    pad_ref[lo + TM:lo + TM + CONV_RIGHT * SUB, :] = jnp.where(last, 0.0, nxt)
    y = cb
    for k in range(CONV_W):
        y = y + pad_ref[k * SUB:k * SUB + TM, :] * cw[k:k + 1, :]
    return y


def _gates(xc, wg_ref, ba, bx, lam):
    half = xc.shape[1] // 2
    xb = xc.astype(BF16)
    g0 = _dot(xb[:, :half], wg_ref[0])
    g1 = _dot(xb[:, half:], wg_ref[1])
    r = _sigmoid(jnp.concatenate([g0[:, :half], g1[:, :half]], axis=1) + ba)
    i = _sigmoid(jnp.concatenate([g0[:, half:], g1[:, half:]], axis=1) + bx)
    z = -lam
    decay = LRU_C * (jnp.maximum(z, 0.0) + jnp.log1p(jnp.exp(-jnp.abs(z))))
    neg_log_a = r * decay
    a = jnp.exp(-neg_log_a)
    z = jnp.tanh(neg_log_a) * (a * a + 1.0)
    root = jnp.where(z > 0.0, z * lax.rsqrt(z), 0.0)
    u = root * (i * xc)
    return a, u


def _scan(a_ref, u_ref, h_ref, h, reverse):
    steps = TM // SUB

    def body(k, h):
        t = steps - 1 - k if reverse else k
        r0 = pl.multiple_of(t * SUB, SUB)
        h = a_ref[pl.ds(r0, SUB), :] * h + u_ref[pl.ds(r0, SUB), :]
        h_ref[pl.ds(r0, SUB), :] = h
        return h

    return lax.fori_loop(0, steps, body, h, unroll=8)


def _scan_fwd_kernel(cfg, xa_ref, xp_ref, xn_ref, cw_ref, cb_ref, wg_ref, ba_ref, bx_ref,
                     lam_ref, h0_ref, hf_ref, st_ref, pad_ref, a_ref, u_ref, carry_ref):
    j = pl.program_id(0)
    _, _, first, last = _seq_flags(cfg, j)
    xc = _conv(pad_ref, xa_ref[...], xp_ref[...], xn_ref[...], first, last, cw_ref[...], cb_ref[...])
    a, u = _gates(xc, wg_ref, ba_ref[...], bx_ref[...], lam_ref[...])
    a_ref[...] = a
    u_ref[...] = u

    @pl.when(first)
    def _():
        carry_ref[...] = h0_ref[0]

    h = _scan(a_ref, u_ref, hf_ref, carry_ref[...], reverse=False)
    carry_ref[...] = h
    st_ref[0] = h


def _halo_specs(cfg, col, rows_prev, rows_next, tile_of):
    n_tiles = cfg[2]
    c = 512
    nb_prev = TM // rows_prev
    nb_next = TM // rows_next
    last_next = n_tiles * nb_next - 1
    prev = pl.BlockSpec((rows_prev, c), lambda j: (jnp.maximum(tile_of(j) * nb_prev - 1, 0), col))
    nxt = pl.BlockSpec((rows_next, c), lambda j: (jnp.minimum((tile_of(j) + 1) * nb_next, last_next), col))
    return prev, nxt


def _state_spec(cfg, tile_of):
    n_ctx_tiles, tiles_per_ctx, _, _ = cfg
    n_groups = n_ctx_tiles // tiles_per_ctx
    c = 512
    return pl.BlockSpec((1, SUB, c), lambda j: (jnp.minimum(tile_of(j) // tiles_per_ctx, n_groups), 0, 0))


def _h0_spec(cfg, tile_of):
    n_ctx_tiles = cfg[0]
    return pl.BlockSpec((1, SUB, 512), lambda j: (jnp.where(tile_of(j) >= n_ctx_tiles, 1, 0), 0, 0))


def _scan_fwd_call(cfg, proj, conv_w, conv_b, wg, ba, bx, lam, h0):
    n_ctx_tiles, tiles_per_ctx, n_tiles, _ = cfg
    n_groups = n_ctx_tiles // tiles_per_ctx
    c = conv_w.shape[1]
    ident = lambda j: j
    xp_spec, xn_spec = _halo_specs(cfg, 0, CONV_LEFT * SUB, CONV_RIGHT * SUB, ident)
    return pl.pallas_call(
        functools.partial(_scan_fwd_kernel, cfg),
        grid=(n_tiles,),
        in_specs=[
            pl.BlockSpec((TM, c), lambda j: (j, 0)), xp_spec, xn_spec,
            _const_spec((CONV_W, c)), _const_spec((1, c)), _const_spec((2, c // 2, c)),
            _const_spec((1, c)), _const_spec((1, c)), _const_spec((1, c)),
            _h0_spec(cfg, ident),
        ],
        out_specs=[pl.BlockSpec((TM, c), lambda j: (j, 0)), _state_spec(cfg, ident)],
        out_shape=[jax.ShapeDtypeStruct((n_tiles * TM, c), F32),
                   jax.ShapeDtypeStruct((n_groups + 1, SUB, c), F32)],
        scratch_shapes=[
            pltpu.VMEM((TM + (CONV_W - 1) * SUB, c), F32),
            pltpu.VMEM((TM, c), F32), pltpu.VMEM((TM, c), F32), pltpu.VMEM((SUB, c), F32),
        ],
        compiler_params=_cparams(("arbitrary",)),
        name="lru_fwd",
    )(proj, proj, proj, conv_w, conv_b.reshape(1, c), wg, ba.reshape(1, c), bx.reshape(1, c),
      lam.reshape(1, c), h0)


def _pool_mix(pad_ref, t0, t_len, pw_ref, pb, ps):
    c = pad_ref.shape[1]
    gw = c // len(POOL_WINDOWS)
    t_pos = lax.shift_right_logical(lax.broadcasted_iota(jnp.int32, (TM, gw), 0), 3) + t0
    outs = []
    for g, k in enumerate(POOL_WINDOWS):
        left = k // 2
        right = k - 1 - left
        s = None
        for o in range(-left, right + 1):
            v = pad_ref[POOL_HALO + o * SUB:POOL_HALO + o * SUB + TM, g * gw:(g + 1) * gw]
            s = v if s is None else s + v
        cnt = (jnp.minimum(t_pos + right + 1, t_len) - jnp.maximum(t_pos - left, 0)).astype(F32)
        outs.append(s / cnt - pad_ref[POOL_HALO:POOL_HALO + TM, g * gw:(g + 1) * gw])
    d = jnp.concatenate(outs, axis=1).astype(BF16)
    half = c // 2
    y = jnp.concatenate([_dot(d[:, :half], pw_ref[0]), _dot(d[:, half:], pw_ref[1])], axis=1)
    return (y + pb) * ps


def _scan_bwd_kernel(cfg, ctx_len, proj_ref, xp_ref, xn_ref, bp_ref, bn_ref, hf_ref, cw_ref, cb_ref,
                     wg_ref, ba_ref, bx_ref, lam_ref, h0_ref, pw_ref, pb_ref, ps_ref,
                     y_ref, st_ref, pad_ref, ppad_ref, a_ref, u_ref, hb_ref, carry_ref):
    n_tiles = cfg[2]
    j = n_tiles - 1 - pl.program_id(0)
    is_ctx, pos, first, last = _seq_flags(cfg, j)
    c = hf_ref.shape[1]
    xc = _conv(pad_ref, proj_ref[:, 0:c], xp_ref[...], xn_ref[...], first, last, cw_ref[...], cb_ref[...])
    a, u = _gates(xc, wg_ref, ba_ref[...], bx_ref[...], lam_ref[...])
    a_ref[...] = a
    u_ref[...] = u

    @pl.when(last)
    def _():
        carry_ref[...] = h0_ref[0]

    h = _scan(a_ref, u_ref, hb_ref, carry_ref[...], reverse=True)
    carry_ref[...] = h
    st_ref[0] = h

    ga = proj_ref[:, c:2 * c]
    y_a = (hf_ref[...] + hb_ref[...]) * jax.nn.gelu(ga, approximate=True)

    use_prev = jnp.logical_and(is_ctx, jnp.logical_not(first))
    use_next = jnp.logical_and(is_ctx, jnp.logical_not(last))
    ppad_ref[0:POOL_HALO, :] = jnp.where(use_prev, bp_ref[...], 0.0)
    ppad_ref[POOL_HALO:POOL_HALO + TM, :] = proj_ref[:, 2 * c:3 * c]
    ppad_ref[POOL_HALO + TM:, :] = jnp.where(use_next, bn_ref[...], 0.0)
    t0 = jnp.where(is_ctx, pos * GRID_W, 0)
    t_len = jnp.where(is_ctx, ctx_len, GRID_W)
    y_b = _pool_mix(ppad_ref, t0, t_len, pw_ref, pb_ref[...], ps_ref[...])
    y_ref[...] = jnp.concatenate([y_a, y_b], axis=1).astype(BF16)


def _scan_bwd_call(cfg, ctx_len, proj, hf, conv_w, conv_b, wg, ba, bx, lam, h0, pw, pb, ps):
    n_ctx_tiles, tiles_per_ctx, n_tiles, d = cfg
    n_groups = n_ctx_tiles // tiles_per_ctx
    c = conv_w.shape[1]
    rev = lambda j: n_tiles - 1 - j
    xp_spec, xn_spec = _halo_specs(cfg, 0, CONV_LEFT * SUB, CONV_RIGHT * SUB, rev)
    bp_spec, bn_spec = _halo_specs(cfg, 2, POOL_HALO, POOL_HALO, rev)
    return pl.pallas_call(
        functools.partial(_scan_bwd_kernel, cfg, ctx_len),
        grid=(n_tiles,),
        in_specs=[
            pl.BlockSpec((TM, 3 * c), lambda j: (rev(j), 0)), xp_spec, xn_spec, bp_spec, bn_spec,
            pl.BlockSpec((TM, c), lambda j: (rev(j), 0)),
            _const_spec((CONV_W, c)), _const_spec((1, c)), _const_spec((2, c // 2, c)),
            _const_spec((1, c)), _const_spec((1, c)), _const_spec((1, c)),
            _h0_spec(cfg, rev),
            _const_spec((2, c // 2, c // 2)), _const_spec((1, c)), _const_spec((1, c)),
        ],
        out_specs=[pl.BlockSpec((TM, d), lambda j: (rev(j), 0)), _state_spec(cfg, rev)],
        out_shape=[jax.ShapeDtypeStruct((n_tiles * TM, d), BF16),
                   jax.ShapeDtypeStruct((n_groups + 1, SUB, c), F32)],
        scratch_shapes=[
            pltpu.VMEM((TM + (CONV_W - 1) * SUB, c), F32),
            pltpu.VMEM((TM + 2 * POOL_HALO, c), F32),
            pltpu.VMEM((TM, c), F32), pltpu.VMEM((TM, c), F32), pltpu.VMEM((TM, c), F32),
            pltpu.VMEM((SUB, c), F32),
        ],
        compiler_params=_cparams(("arbitrary",)),
        name="lru_bwd_mix",
    )(proj, proj, proj, proj, proj, hf, conv_w, conv_b.reshape(1, c), wg, ba.reshape(1, c),
      bx.reshape(1, c), lam.reshape(1, c), h0, pw, pb.reshape(1, c), ps.reshape(1, c))


def _out_proj(x_ref, ym_ref, mod_ref, wo_ref, g2_ref):
    xm = _gated_add(x_ref[...], mod_ref[0, 2], _dot(ym_ref[...], wo_ref[...]))
    h2 = _modulate(_rms(xm, g2_ref[...]), mod_ref[0, 4], mod_ref[0, 3])
    return xm, h2


def _swiglu_act(h2b, wg_ref, wu_ref, act_ref):
    n_chunks = act_ref.shape[1] // FF_CHUNK

    def body(f, carry):
        c0 = pl.multiple_of(f * FF_CHUNK, FF_CHUNK)
        g = _dot(h2b, wg_ref[:, pl.ds(c0, FF_CHUNK)])
        u = _dot(h2b, wu_ref[:, pl.ds(c0, FF_CHUNK)])
        act_ref[:, pl.ds(c0, FF_CHUNK)] = (g * _sigmoid(g) * u).astype(BF16)
        return carry

    lax.fori_loop(0, n_chunks, body, 0)


def _dense_ffn_kernel(x_ref, ym_ref, mod_ref, wo_ref, g2_ref, wg_ref, wu_ref, wd_ref, o_ref, act_ref):
    xm, h2 = _out_proj(x_ref, ym_ref, mod_ref, wo_ref, g2_ref)
    _swiglu_act(h2.astype(BF16), wg_ref, wu_ref, act_ref)
    o_ref[...] = _gated_add(xm, mod_ref[0, 5], _dot(act_ref[...], wd_ref[...]))


def _dense_ffn_call(cfg, x, ymix, mod_l, w_out, g2, wg, wu, wd):
    _, _, n_tiles, d = cfg
    d_ff = wg.shape[1]
    f = _wide(cfg)
    row_spec = pl.BlockSpec((TM * f, d), lambda j: (j, 0))
    return pl.pallas_call(
        _dense_ffn_kernel,
        grid=(n_tiles // f,),
        in_specs=[row_spec, row_spec, _mod_spec(cfg, f), _const_spec((d, d)), _const_spec((1, d)),
                  _const_spec((d, d_ff)), _const_spec((d, d_ff)), _const_spec((d_ff, d))],
        out_specs=row_spec,
        out_shape=jax.ShapeDtypeStruct((n_tiles * TM, d), F32),
        scratch_shapes=[pltpu.VMEM((TM * f, d_ff), BF16)],
        compiler_params=_cparams(("parallel",)),
        name="out_proj_dense_ffn",
    )(x, ymix, mod_l, w_out, g2.reshape(1, d), wg, wu, wd)


M_E1, M_E2, M_W1, M_W2, M_R1, M_R2 = range(6)


def _route_kernel(x_ref, ym_ref, mod_ref, wo_ref, g2_ref, rw_ref, rb_ref, tri_ref,
                  xm_ref, hp_ref, meta_ref, cnt_ref, run_ref):
    xm, h2 = _out_proj(x_ref, ym_ref, mod_ref, wo_ref, g2_ref)
    xm_ref[...] = xm
    h_hi = h2.astype(BF16)
    h_hi32 = h_hi.astype(F32)
    hp_ref[...] = _pack_pairs(h_hi32)

    h_lo = (h2 - h_hi32).astype(BF16)
    p = _dot(h_hi, rw_ref[...])
    logits = p[:, :LANES] + p[:, LANES:] + _dot(h_lo, rw_ref[:, :LANES])
    lane = lax.broadcasted_iota(jnp.int32, logits.shape, 1)
    neg = jnp.float32(-jnp.inf)
    lg = jnp.where(lane < N_EXPERTS, logits + rb_ref[...], neg)
    m1 = jnp.max(lg, axis=1, keepdims=True)
    i1 = jnp.min(jnp.where(lg == m1, lane, LANES), axis=1, keepdims=True)
    lg2 = jnp.where(lane == i1, neg, lg)
    m2 = jnp.max(lg2, axis=1, keepdims=True)
    i2 = jnp.min(jnp.where(lg2 == m2, lane, LANES), axis=1, keepdims=True)
    e2 = jnp.exp(m2 - m1)
    den = 1.0 + e2

    @pl.when(pl.program_id(0) == 0)
    def _():
        run_ref[...] = jnp.zeros_like(run_ref)

    sel1 = lane == i1
    sel2 = lane == i2
    onehot = jnp.where(jnp.logical_or(sel1, sel2), 1.0, 0.0)
    rank = _dot(tri_ref[...], onehot.astype(BF16)) + run_ref[...]
    r1 = jnp.sum(jnp.where(sel1, rank, 0.0), axis=1, keepdims=True)
    r2 = jnp.sum(jnp.where(sel2, rank, 0.0), axis=1, keepdims=True)
    run_ref[...] = run_ref[...] + jnp.sum(onehot, axis=0, keepdims=True)
    cnt_ref[...] = run_ref[...]

    meta = jnp.zeros(logits.shape, F32)
    for k, v in ((M_E1, i1.astype(F32)), (M_E2, i2.astype(F32)), (M_W1, 1.0 / den), (M_W2, e2 / den),
                 (M_R1, r1), (M_R2, r2)):
        meta = jnp.where(lane == k, v, meta)
    meta_ref[...] = meta


def _route_call(cfg, x, ymix, mod_l, w_out, g2, router_w, router_b):
    _, _, n_tiles, d = cfg
    n = n_tiles * TM
    f = _wide(cfg)
    tm = TM * f
    row_spec = pl.BlockSpec((tm, d), lambda j: (j, 0))
    rw = jnp.zeros((d, LANES), F32).at[:, :N_EXPERTS].set(router_w)
    rw_hi = rw.astype(BF16)
    rw = jnp.concatenate([rw_hi, (rw - rw_hi.astype(F32)).astype(BF16)], axis=1)
    rb = jnp.zeros((1, LANES), F32).at[0, :N_EXPERTS].set(router_b)
    tri = jnp.tril(jnp.ones((tm, tm), BF16), -1)
    return pl.pallas_call(
        _route_kernel,
        grid=(n_tiles // f,),
        in_specs=[row_spec, row_spec, _mod_spec(cfg, f), _const_spec((d, d)), _const_spec((1, d)),
                  _const_spec((d, 2 * LANES)), _const_spec((1, LANES)), _const_spec((tm, tm))],
        out_specs=[row_spec, pl.BlockSpec((tm, d // 2), lambda j: (j, 0)),
                   pl.BlockSpec((tm, LANES), lambda j: (j, 0)), pl.BlockSpec((1, LANES), lambda j: (0, 0))],
        out_shape=[jax.ShapeDtypeStruct((n, d), F32), jax.ShapeDtypeStruct((n, d // 2), jnp.int32),
                   jax.ShapeDtypeStruct((n, LANES), F32), jax.ShapeDtypeStruct((1, LANES), F32)],
        scratch_shapes=[pltpu.VMEM((1, LANES), F32)],
        compiler_params=_cparams(("arbitrary",)),
        name="out_proj_route",
    )(x, ymix, mod_l, w_out, g2.reshape(1, d), rw, rb, tri)


def _routing_tables(cfg, meta, cnt):
    n_tiles = cfg[2]
    counts = cnt[0, :N_EXPERTS].astype(jnp.int32)
    padded = ((counts + TM - 1) // TM) * TM
    ends = jnp.cumsum(padded)
    offs = ends - padded
    e1 = meta[:, M_E1].astype(jnp.int32)
    e2 = meta[:, M_E2].astype(jnp.int32)
    pos1 = offs[e1] + meta[:, M_R1].astype(jnp.int32)
    pos2 = offs[e2] + meta[:, M_R2].astype(jnp.int32)
    pos = jnp.concatenate([pos1.reshape(n_tiles, 1, TM), pos2.reshape(n_tiles, 1, TM)], axis=2)
    n_sorted_tiles = 2 * n_tiles + N_EXPERTS
    starts = jnp.arange(n_sorted_tiles, dtype=jnp.int32) * TM
    tile_e = jnp.minimum(jnp.sum((starts[:, None] >= ends[None, :]).astype(jnp.int32), axis=1), N_EXPERTS - 1)
    n_active = (ends[-1] // TM).reshape(1)
    return pos, jnp.concatenate([pos1, pos2]), tile_e, n_active


def _load_positions(pos_ref, idx_ref, sem):
    cp = pltpu.make_async_copy(pos_ref.at[0, 0], idx_ref, sem)
    cp.start()
    cp.wait()


def _dispatch_kernel(pos_ref, hp_ref, buf_in_ref, buf_ref, idx_ref, sem_i, sem_a, sem_b):
    del buf_in_ref
    _load_positions(pos_ref, idx_ref, sem_i)

    def body(r, carry):
        src = hp_ref.at[pl.ds(r, 1)]
        pltpu.make_async_copy(src, buf_ref.at[pl.ds(idx_ref[r], 1)], sem_a).start()
        pltpu.make_async_copy(src, buf_ref.at[pl.ds(idx_ref[TM + r], 1)], sem_b).start()
        return carry

    lax.fori_loop(0, TM, body, 0, unroll=8)
    pltpu.make_async_copy(hp_ref, buf_ref.at[pl.ds(0, TM)], sem_a).wait()
    pltpu.make_async_copy(hp_ref, buf_ref.at[pl.ds(0, TM)], sem_b).wait()


def _dispatch_call(cfg, pos, hp):
    _, _, n_tiles, d = cfg
    n_sorted = (2 * n_tiles + N_EXPERTS) * TM
    buf = jnp.zeros((n_sorted, d // 2), jnp.int32)
    return pl.pallas_call(
        _dispatch_kernel,
        grid=(n_tiles,),
        in_specs=[pl.BlockSpec((1, 1, 2 * TM), lambda j: (j, 0, 0)),
                  pl.BlockSpec((TM, d // 2), lambda j: (j, 0)),
                  pl.BlockSpec(memory_space=pl.ANY)],
        out_specs=pl.BlockSpec(memory_space=pl.ANY),
        out_shape=jax.ShapeDtypeStruct((n_sorted, d // 2), jnp.int32),
        input_output_aliases={2: 0},
        scratch_shapes=[pltpu.SMEM((2 * TM,), jnp.int32), pltpu.SemaphoreType.DMA,
                        pltpu.SemaphoreType.DMA, pltpu.SemaphoreType.DMA],
        compiler_params=_cparams(("arbitrary",)),
        name="moe_dispatch",
    )(pos, hp, buf)


def _expert_kernel(te_ref, na_ref, s_ref, wg_ref, wu_ref, wd_ref, o_ref):
    del te_ref

    @pl.when(pl.program_id(0) < na_ref[0])
    def _():
        h = _unpack_pairs(s_ref[...]).astype(BF16)
        g = _dot(h, wg_ref[0])
        u = _dot(h, wu_ref[0])
        act = (g * _sigmoid(g) * u).astype(BF16)
        o_ref[...] = _pack_pairs(_dot(act, wd_ref[0]).astype(BF16).astype(F32))

    @pl.when(pl.program_id(0) >= na_ref[0])
    def _():
        o_ref[...] = jnp.zeros_like(o_ref)


def _expert_call(cfg, tile_e, n_active, buf, wg, wu, wd):
    _, _, n_tiles, d = cfg
    dfe = wg.shape[2]
    n_sorted_tiles = 2 * n_tiles + N_EXPERTS
    last = lambda t, na: jnp.maximum(jnp.minimum(t, na[0] - 1), 0)
    tile = lambda t, te, na: (last(t, na), 0)
    w_spec = lambda shape: pl.BlockSpec(shape, lambda t, te, na: (te[last(t, na)], 0, 0))
    return pl.pallas_call(
        _expert_kernel,
        grid_spec=pltpu.PrefetchScalarGridSpec(
            num_scalar_prefetch=2,
            grid=(n_sorted_tiles,),
            in_specs=[pl.BlockSpec((TM, d // 2), tile), w_spec((1, d, dfe)), w_spec((1, d, dfe)),
                      w_spec((1, dfe, d))],
            out_specs=pl.BlockSpec((TM, d // 2), lambda t, te, na: (t, 0)),
        ),
        out_shape=jax.ShapeDtypeStruct((n_sorted_tiles * TM, d // 2), jnp.int32),
        compiler_params=_cparams(("arbitrary",)),
        name="moe_experts",
    )(tile_e, n_active, buf, wg, wu, wd)


def _store_natural(n_ctx_tiles, y, yp_ref, ys_ref):
    is_ctx = pl.program_id(0) < n_ctx_tiles

    @pl.when(is_ctx)
    def _():
        yp_ref[0] = _from_time_major(y)

    @pl.when(jnp.logical_not(is_ctx))
    def _():
        ys_ref[...] = _from_time_major(y)


def _natural_out(cfg):
    n_ctx_tiles, tiles_per_ctx, n_tiles, d = cfg
    n_groups = n_ctx_tiles // tiles_per_ctx
    return (list(_natural_specs(cfg)),
            [jax.ShapeDtypeStruct((n_groups, SUB, tiles_per_ctx * GRID_W, d), F32),
             jax.ShapeDtypeStruct((SUB, (n_tiles - n_ctx_tiles) * GRID_W, d), F32)])


SC_CORES = 2
SC_SUBCORES = 16
SC_CHUNK = 64


def _sc_gather_rows(table, idx):
    n_rows = idx.shape[0]
    width = table.shape[1]
    n_workers = SC_CORES * SC_SUBCORES
    assert n_rows % (n_workers * SC_CHUNK) == 0
    per_w = n_rows // n_workers
    mesh = plsc.VectorSubcoreMesh(core_axis_name="c", subcore_axis_name="s",
                                  num_cores=SC_CORES, num_subcores=SC_SUBCORES)

    def body(table_hbm, idx_hbm, out_hbm, idx_v, rows_v, sem):
        wid = lax.axis_index("s") * SC_CORES + lax.axis_index("c")
        base = wid * per_w
        pltpu.sync_copy(idx_hbm.at[pl.ds(base, per_w)], idx_v)

        @pl.loop(0, per_w // SC_CHUNK)
        def _(i):
            off = pl.multiple_of(i * SC_CHUNK, SC_CHUNK)
            pltpu.async_copy(table_hbm.at[idx_v.at[pl.ds(off, SC_CHUNK)]], rows_v, sem).wait()
            pltpu.sync_copy(rows_v, out_hbm.at[pl.ds(base + off, SC_CHUNK)])

    return pl.kernel(
        body,
        out_type=jax.ShapeDtypeStruct((n_rows, width), table.dtype),
        mesh=mesh,
        scratch_types=[pltpu.VMEM((per_w,), jnp.int32), pltpu.VMEM((SC_CHUNK, width), table.dtype),
                       pltpu.SemaphoreType.DMA],
        name="sc_gather_rows",
    )(table, idx)


def _combine_kernel(final_ctx_tiles, xm_ref, y1_ref, y2_ref, meta_ref, mod_ref, gfin_ref, *o_refs):
    meta = meta_ref[...]
    y = (meta[:, M_W1:M_W1 + 1] * _unpack_pairs(y1_ref[...])
         + meta[:, M_W2:M_W2 + 1] * _unpack_pairs(y2_ref[...]))
    x = _gated_add(xm_ref[...], mod_ref[0, 5], y)
    if final_ctx_tiles is None:
        o_refs[0][...] = x
    else:
        _store_natural(final_ctx_tiles, _rms(x, gfin_ref[...]), *o_refs)


def _combine_call(cfg, xm, yg, meta, mod_l, g_final, final):
    n_ctx_tiles, _, n_tiles, d = cfg
    row_spec = pl.BlockSpec((TM, d), lambda j: (j, 0))
    if final:
        out_specs, out_shape = _natural_out(cfg)
    else:
        out_specs, out_shape = [row_spec], [jax.ShapeDtypeStruct((n_tiles * TM, d), F32)]
    outs = pl.pallas_call(
        functools.partial(_combine_kernel, n_ctx_tiles if final else None),
        grid=(n_tiles,),
        in_specs=[row_spec, pl.BlockSpec((TM, d // 2), lambda j: (j, 0)),
                  pl.BlockSpec((TM, d // 2), lambda j: (n_tiles + j, 0)),
                  pl.BlockSpec((TM, LANES), lambda j: (j, 0)), _mod_spec(cfg), _const_spec((1, d))],
        out_specs=out_specs,
        out_shape=out_shape,
        compiler_params=_cparams(("arbitrary",)),
        name="moe_combine",
    )(xm, yg, yg, meta, mod_l, g_final.reshape(1, d))
    return outs if final else outs[0]


def _final_norm_kernel(n_ctx_tiles, x_ref, g_ref, yp_ref, ys_ref):
    _store_natural(n_ctx_tiles, _rms(x_ref[...], g_ref[...]), yp_ref, ys_ref)


def _final_norm_call(cfg, x, g):
    n_ctx_tiles, _, n_tiles, d = cfg
    out_specs, out_shape = _natural_out(cfg)
    return pl.pallas_call(
        functools.partial(_final_norm_kernel, n_ctx_tiles),
        grid=(n_tiles,),
        in_specs=[pl.BlockSpec((TM, d), lambda j: (j, 0)), _const_spec((1, d))],
        out_specs=out_specs,
        out_shape=out_shape,
        compiler_params=_cparams(("arbitrary",)),
        name="final_norm",
    )(x, g.reshape(1, d))


def _block_diag(w, per_block):
    *lead, n, k, _ = w.shape
    nb = n // per_block
    w = w.reshape(*lead, nb, per_block, k, k)
    eye = jnp.eye(per_block, dtype=w.dtype)
    out = w[..., :, :, None, :] * eye[:, None, :, None]
    return out.reshape(*lead, nb, per_block * k, per_block * k)


def _pos_tables(n_ctx_tiles, n_lat_tiles, d):
    quarter = d // 4
    omega = 1.0 / (POS_BASE ** (jnp.arange(quarter, dtype=F32) / quarter))
    er = jnp.arange(n_lat_tiles, dtype=F32)[:, None] * omega
    ec = jnp.arange(GRID_W, dtype=F32)[:, None] * omega
    row_emb = jnp.concatenate([jnp.sin(er), jnp.cos(er)], axis=-1)
    col_emb = jnp.concatenate([jnp.sin(ec), jnp.cos(ec)], axis=-1)
    del n_ctx_tiles
    return row_emb[:, None, :], jnp.repeat(col_emb, SUB, axis=0)


def kernel(x_prompt, x_sample, state_lru, c, c_ctx, norm_mix_g, w_ada, b_ada, w_in, conv_w, conv_b, lru_wa, lru_ba, lru_wx, lru_bx, lru_lam, pool_w, pool_b, pool_scale, w_out, norm_ffn_g, ffn_wg, ffn_wu, ffn_wd, moe_router_w, moe_router_b, moe_wg, moe_wu, moe_wd, norm_final_g):
    bc, tc, d = x_prompt.shape
    bl, tl, _ = x_sample.shape
    depth = w_ada.shape[0]
    c_lru = conv_w.shape[-1]
    assert bl == SUB and bc % SUB == 0 and tc % GRID_W == 0 and tl % GRID_W == 0
    assert c_lru == 4 * LANES and d == 2 * c_lru
    n_groups = bc // SUB
    tiles_per_ctx = tc // GRID_W
    n_ctx_tiles = n_groups * tiles_per_ctx
    n_lat_tiles = tl // GRID_W
    n_tiles = n_ctx_tiles + n_lat_tiles
    n_ctx = n_ctx_tiles * TM
    cfg = (n_ctx_tiles, tiles_per_ctx, n_tiles, d)

    cond = jnp.concatenate([c, c_ctx[None], jnp.zeros((SUB - 1, d), F32)], axis=0)
    mod = _ada_call(cond, w_ada, b_ada)
    mod_lat = mod[:, :SUB].reshape(depth, SUB, 6, d).transpose(0, 2, 1, 3)
    mod_ctx = jnp.broadcast_to(mod[:, SUB].reshape(depth, 6, 1, d), (depth, 6, SUB, d))
    mod = jnp.stack([mod_ctx, mod_lat], axis=1)

    row_tab, col_rep = _pos_tables(n_ctx_tiles, n_lat_tiles, d)
    first = (x_prompt.reshape(n_groups, SUB, tc, d), x_sample, row_tab, col_rep)
    x = None
    wg_lru = jnp.concatenate([_block_diag(lru_wa, 4), _block_diag(lru_wx, 4)], axis=-1).astype(BF16)
    pw = _block_diag(pool_w, 2).astype(BF16)
    h0 = jnp.concatenate([jnp.zeros_like(state_lru[None]), state_lru[None]], axis=0)

    states = []
    for l in range(depth):
        jdx = l // 2
        if l == 0:
            proj, x = _inproj_call(cfg, x, mod[l], norm_mix_g[l], w_in[l].astype(BF16), first)
        else:
            (proj,) = _inproj_call(cfg, x, mod[l], norm_mix_g[l], w_in[l].astype(BF16))
        hf, st_f = _scan_fwd_call(cfg, proj, conv_w[l], conv_b[l], wg_lru[l, 0], lru_ba[l, 0],
                                  lru_bx[l, 0], lru_lam[l, 0], h0[:, :, l, 0])
        ymix, st_b = _scan_bwd_call(cfg, tc, proj, hf, conv_w[l], conv_b[l], wg_lru[l, 1], lru_ba[l, 1],
                                    lru_bx[l, 1], lru_lam[l, 1], h0[:, :, l, 1], pw[l], pool_b[l],
                                    pool_scale[l])
        states.append(jnp.stack([st_f[:n_groups].reshape(bc, c_lru), st_b[:n_groups].reshape(bc, c_lru)], axis=1))
        if l % 2 == 0:
            x = _dense_ffn_call(cfg, x, ymix, mod[l], w_out[l].astype(BF16), norm_ffn_g[l],
                                ffn_wg[jdx].astype(BF16), ffn_wu[jdx].astype(BF16), ffn_wd[jdx].astype(BF16))
        else:
            xm, hp, meta, cnt = _route_call(cfg, x, ymix, mod[l], w_out[l].astype(BF16), norm_ffn_g[l],
                                            moe_router_w[jdx], moe_router_b[jdx])
            pos, pos_flat, tile_e, n_active = _routing_tables(cfg, meta, cnt)
            buf = _dispatch_call(cfg, pos, hp)
            ys = _expert_call(cfg, tile_e, n_active, buf, moe_wg[jdx].astype(BF16), moe_wu[jdx].astype(BF16),
                              moe_wd[jdx].astype(BF16))
            yg = _sc_gather_rows(ys, pos_flat)
            x = _combine_call(cfg, xm, yg, meta, mod[l], norm_final_g, final=(l == depth - 1))

    y_prompt, y_sample = x if depth % 2 == 0 else _final_norm_call(cfg, x, norm_final_g)
    new_state = jnp.stack(states, axis=1)
    return (y_prompt.reshape(bc, tc, d), y_sample, new_state)
```

```python
import functools

import jax
import jax.numpy as jnp
from jax import lax
from jax.experimental import pallas as pl
from jax.experimental.pallas import tpu as pltpu
from jax.experimental.pallas import tpu_sc as plsc

F32 = jnp.float32
BF16 = jnp.bfloat16

SUB = 8
LANES = 128
GRID_W = 64
TM = GRID_W * SUB
POS_BASE = 10000.0
N_LRU_HEADS = 8
CONV_W = 4
CONV_LEFT = CONV_W // 2
CONV_RIGHT = CONV_W - 1 - CONV_LEFT
LRU_C = 8.0
POOL_WINDOWS = (2, 4, 8, 16)
POOL_HALO = 8 * SUB
N_EXPERTS = 8
EPS = 1e-6
FF_CHUNK = 256
VMEM_LIMIT = 56 * 1024 * 1024


def _cparams(sem):
    return pltpu.CompilerParams(dimension_semantics=sem, vmem_limit_bytes=VMEM_LIMIT)


def _const_spec(shape):
    nd = len(shape)
    return pl.BlockSpec(shape, lambda *_: (0,) * nd, pipeline_mode=pl.Buffered(1))


def _rms(x, g):
    ms = jnp.mean(x * x, axis=-1, keepdims=True)
    return x * lax.rsqrt(ms + EPS) * g


def _per_seq(x, v, op):
    r, c = x.shape
    x3 = x.reshape(r // SUB, SUB, c)
    return op(x3, v[None]).reshape(r, c)


def _modulate(h, scale, shift):
    r, c = h.shape
    h3 = h.reshape(r // SUB, SUB, c)
    return (h3 * (1.0 + scale)[None] + shift[None]).reshape(r, c)


def _gated_add(x, gate, y):
    return x + _per_seq(y, gate, lambda a, b: a * b)


def _dot(a, b):
    return jnp.dot(a, b, preferred_element_type=F32)


def _pack_pairs(x):
    bits = lax.bitcast_convert_type(x, jnp.uint32)
    half = bits.shape[1] // 2
    w = lax.shift_right_logical(bits[:, :half], jnp.uint32(16)) | (bits[:, half:] & jnp.uint32(0xFFFF0000))
    return lax.bitcast_convert_type(w, jnp.int32)


def _unpack_pairs(w):
    w = lax.bitcast_convert_type(w, jnp.uint32)
    lo = lax.bitcast_convert_type(lax.shift_left(w, jnp.uint32(16)), F32)
    hi = lax.bitcast_convert_type(w & jnp.uint32(0xFFFF0000), F32)
    return jnp.concatenate([lo, hi], axis=1)


def _sigmoid(x):
    return 0.5 * jnp.tanh(0.5 * x) + 0.5


def _ada_kernel(c_ref, w_ref, b_ref, o_ref):
    c = c_ref[...]
    s = (c * jax.nn.sigmoid(c)).astype(BF16)
    o_ref[0] = _dot(s, w_ref[0].astype(BF16)) + b_ref[0]


def _ada_call(cond, w_ada, b_ada):
    depth, d, d6 = w_ada.shape
    nr = cond.shape[0]
    bn = d6 // 4
    return pl.pallas_call(
        _ada_kernel,
        grid=(depth, d6 // bn),
        in_specs=[
            pl.BlockSpec((nr, d), lambda l, n: (0, 0)),
            pl.BlockSpec((1, d, bn), lambda l, n: (l, 0, n)),
            pl.BlockSpec((1, 1, bn), lambda l, n: (l, 0, n)),
        ],
        out_specs=pl.BlockSpec((1, nr, bn), lambda l, n: (l, 0, n)),
        out_shape=jax.ShapeDtypeStruct((depth, nr, d6), F32),
        compiler_params=_cparams(("parallel", "parallel")),
        name="ada_mod",
    )(cond, w_ada, b_ada.reshape(depth, 1, d6))


def _to_time_major(x):
    s, t, d = x.shape
    return jnp.swapaxes(x, 0, 1).reshape(s * t, d)


def _from_time_major(x):
    r, d = x.shape
    return jnp.swapaxes(x.reshape(r // SUB, SUB, d), 0, 1)


def _inproj_kernel(n_ctx_tiles, *refs):
    if n_ctx_tiles is not None:
        xp_ref, xs_ref, row_ref, col_ref, mod_ref, g_ref, w_ref, o_ref, x0_ref = refs
        is_ctx = pl.program_id(0) < n_ctx_tiles

        @pl.when(is_ctx)
        def _():
            x0_ref[...] = _to_time_major(xp_ref[0])

        @pl.when(jnp.logical_not(is_ctx))
        def _():
            x = _to_time_major(xs_ref[...])
            half = x.shape[1] // 2
            x0_ref[...] = jnp.concatenate([x[:, :half] + row_ref[0], x[:, half:] + col_ref[...]], axis=1)

        x = x0_ref[...]
    else:
        x_ref, mod_ref, g_ref, w_ref, o_ref = refs
        x = x_ref[...]
    h = _modulate(_rms(x, g_ref[...]), mod_ref[0, 1], mod_ref[0, 0])
    o_ref[...] = _dot(h.astype(BF16), w_ref[...])


def _natural_specs(cfg):
    n_ctx_tiles, tiles_per_ctx, _, d = cfg
    n_groups = n_ctx_tiles // tiles_per_ctx

    def ctx_idx(j):
        jc = jnp.minimum(j, n_ctx_tiles - 1)
        return (jc // tiles_per_ctx, 0, lax.rem(jc, tiles_per_ctx), 0)

    del n_groups
    return (pl.BlockSpec((1, SUB, GRID_W, d), ctx_idx),
            pl.BlockSpec((SUB, GRID_W, d), lambda j: (0, jnp.maximum(j - n_ctx_tiles, 0), 0)))


def _wide(cfg):
    n_ctx_tiles, _, n_tiles, _ = cfg
    return 2 if n_ctx_tiles % 2 == 0 and n_tiles % 2 == 0 else 1


def _mod_spec(cfg, f=1):
    n_ctx_steps = cfg[0] // f
    d = cfg[3]
    return pl.BlockSpec((1, 6, SUB, d), lambda j: (jnp.where(j >= n_ctx_steps, 1, 0), 0, 0, 0))


def _inproj_call(cfg, x, mod_l, g, w_in, first=None):
    n_ctx_tiles, _, n_tiles, d = cfg
    d_in = w_in.shape[1]
    f = 1 if first is not None else _wide(cfg)
    tm = TM * f
    row_spec = pl.BlockSpec((tm, d), lambda j: (j, 0))
    out_specs = [pl.BlockSpec((tm, d_in), lambda j: (j, 0))]
    out_shape = [jax.ShapeDtypeStruct((n_tiles * TM, d_in), F32)]
    if first is not None:
        xp_spec, xs_spec = _natural_specs(cfg)
        in_specs = [xp_spec, xs_spec,
                    pl.BlockSpec((1, 1, d // 2), lambda j: (jnp.maximum(j - n_ctx_tiles, 0), 0, 0)),
                    _const_spec((TM, d // 2))]
        args = list(first)
        out_specs.append(row_spec)
        out_shape.append(jax.ShapeDtypeStruct((n_tiles * TM, d), F32))
    else:
        in_specs = [row_spec]
        args = [x]
    in_specs += [_mod_spec(cfg, f), _const_spec((1, d)), _const_spec((d, d_in))]
    args += [mod_l, g.reshape(1, d), w_in]
    return pl.pallas_call(
        functools.partial(_inproj_kernel, n_ctx_tiles if first is not None else None),
        grid=(n_tiles // f,),
        in_specs=in_specs,
        out_specs=out_specs,
        out_shape=out_shape,
        compiler_params=_cparams(("parallel",)),
        name="in_proj",
    )(*args)


def _seq_flags(cfg, j):
    n_ctx_tiles, tiles_per_ctx, n_tiles, _ = cfg
    is_ctx = j < n_ctx_tiles
    pos = lax.rem(j, tiles_per_ctx)
    first = jnp.where(is_ctx, pos == 0, j == n_ctx_tiles)
    last = jnp.where(is_ctx, pos == tiles_per_ctx - 1, j == n_tiles - 1)
    return is_ctx, pos, first, last


def _conv(pad_ref, xa, prev, nxt, first, last, cw, cb):
    lo = CONV_LEFT * SUB
    pad_ref[0:lo, :] = jnp.where(first, 0.0, prev)
    pad_ref[lo:lo + TM, :] = xa
    pad_ref[lo + TM:lo + TM + CONV_RIGHT * SUB, :] = jnp.where(last, 0.0, nxt)
    y = cb
    for k in range(CONV_W):
        y = y + pad_ref[k * SUB:k * SUB + TM, :] * cw[k:k + 1, :]
    return y


def _gates(xc, wg_ref, ba, bx, lam):
    half = xc.shape[1] // 2
    xb = xc.astype(BF16)
    g0 = _dot(xb[:, :half], wg_ref[0])
    g1 = _dot(xb[:, half:], wg_ref[1])
    r = _sigmoid(jnp.concatenate([g0[:, :half], g1[:, :half]], axis=1) + ba)
    i = _sigmoid(jnp.concatenate([g0[:, half:], g1[:, half:]], axis=1) + bx)
    z = -lam
    decay = LRU_C * (jnp.maximum(z, 0.0) + jnp.log1p(jnp.exp(-jnp.abs(z))))
    neg_log_a = r * decay
    a = jnp.exp(-neg_log_a)
    z = jnp.tanh(neg_log_a) * (a * a + 1.0)
    root = jnp.where(z > 0.0, z * lax.rsqrt(z), 0.0)
    u = root * (i * xc)
    return a, u


def _scan(a_ref, u_ref, h_ref, h, reverse):
    steps = TM // SUB

    def body(k, h):
        t = steps - 1 - k if reverse else k
        r0 = pl.multiple_of(t * SUB, SUB)
        h = a_ref[pl.ds(r0, SUB), :] * h + u_ref[pl.ds(r0, SUB), :]
        h_ref[pl.ds(r0, SUB), :] = h
        return h

    return lax.fori_loop(0, steps, body, h, unroll=8)


def _scan_fwd_kernel(cfg, xa_ref, xp_ref, xn_ref, cw_ref, cb_ref, wg_ref, ba_ref, bx_ref,
                     lam_ref, h0_ref, hf_ref, st_ref, pad_ref, a_ref, u_ref, carry_ref):
    j = pl.program_id(0)
    _, _, first, last = _seq_flags(cfg, j)
    xc = _conv(pad_ref, xa_ref[...], xp_ref[...], xn_ref[...], first, last, cw_ref[...], cb_ref[...])
    a, u = _gates(xc, wg_ref, ba_ref[...], bx_ref[...], lam_ref[...])
    a_ref[...] = a
    u_ref[...] = u

    @pl.when(first)
    def _():
        carry_ref[...] = h0_ref[0]

    h = _scan(a_ref, u_ref, hf_ref, carry_ref[...], reverse=False)
    carry_ref[...] = h
    st_ref[0] = h


def _halo_specs(cfg, col, rows_prev, rows_next, tile_of):
    n_tiles = cfg[2]
    c = 512
    nb_prev = TM // rows_prev
    nb_next = TM // rows_next
    last_next = n_tiles * nb_next - 1
    prev = pl.BlockSpec((rows_prev, c), lambda j: (jnp.maximum(tile_of(j) * nb_prev - 1, 0), col))
    nxt = pl.BlockSpec((rows_next, c), lambda j: (jnp.minimum((tile_of(j) + 1) * nb_next, last_next), col))
    return prev, nxt


def _state_spec(cfg, tile_of):
    n_ctx_tiles, tiles_per_ctx, _, _ = cfg
    n_groups = n_ctx_tiles // tiles_per_ctx
    c = 512
    return pl.BlockSpec((1, SUB, c), lambda j: (jnp.minimum(tile_of(j) // tiles_per_ctx, n_groups), 0, 0))


def _h0_spec(cfg, tile_of):
    n_ctx_tiles = cfg[0]
    return pl.BlockSpec((1, SUB, 512), lambda j: (jnp.where(tile_of(j) >= n_ctx_tiles, 1, 0), 0, 0))


def _scan_fwd_call(cfg, proj, conv_w, conv_b, wg, ba, bx, lam, h0):
    n_ctx_tiles, tiles_per_ctx, n_tiles, _ = cfg
    n_groups = n_ctx_tiles // tiles_per_ctx
    c = conv_w.shape[1]
    ident = lambda j: j
    xp_spec, xn_spec = _halo_specs(cfg, 0, CONV_LEFT * SUB, CONV_RIGHT * SUB, ident)
    return pl.pallas_call(
        functools.partial(_scan_fwd_kernel, cfg),
        grid=(n_tiles,),
        in_specs=[
            pl.BlockSpec((TM, c), lambda j: (j, 0)), xp_spec, xn_spec,
            _const_spec((CONV_W, c)), _const_spec((1, c)), _const_spec((2, c // 2, c)),
            _const_spec((1, c)), _const_spec((1, c)), _const_spec((1, c)),
            _h0_spec(cfg, ident),
        ],
        out_specs=[pl.BlockSpec((TM, c), lambda j: (j, 0)), _state_spec(cfg, ident)],
        out_shape=[jax.ShapeDtypeStruct((n_tiles * TM, c), F32),
                   jax.ShapeDtypeStruct((n_groups + 1, SUB, c), F32)],
        scratch_shapes=[
            pltpu.VMEM((TM + (CONV_W - 1) * SUB, c), F32),
            pltpu.VMEM((TM, c), F32), pltpu.VMEM((TM, c), F32), pltpu.VMEM((SUB, c), F32),
        ],
        compiler_params=_cparams(("arbitrary",)),
        name="lru_fwd",
    )(proj, proj, proj, conv_w, conv_b.reshape(1, c), wg, ba.reshape(1, c), bx.reshape(1, c),
      lam.reshape(1, c), h0)


def _pool_mix(pad_ref, t0, t_len, pw_ref, pb, ps):
    c = pad_ref.shape[1]
    gw = c // len(POOL_WINDOWS)
    t_pos = lax.shift_right_logical(lax.broadcasted_iota(jnp.int32, (TM, gw), 0), 3) + t0
    outs = []
    for g, k in enumerate(POOL_WINDOWS):
        left = k // 2
        right = k - 1 - left
        s = None
        for o in range(-left, right + 1):
            v = pad_ref[POOL_HALO + o * SUB:POOL_HALO + o * SUB + TM, g * gw:(g + 1) * gw]
            s = v if s is None else s + v
        cnt = (jnp.minimum(t_pos + right + 1, t_len) - jnp.maximum(t_pos - left, 0)).astype(F32)
        outs.append(s / cnt - pad_ref[POOL_HALO:POOL_HALO + TM, g * gw:(g + 1) * gw])
    d = jnp.concatenate(outs, axis=1).astype(BF16)
    half = c // 2
    y = jnp.concatenate([_dot(d[:, :half], pw_ref[0]), _dot(d[:, half:], pw_ref[1])], axis=1)
    return (y + pb) * ps


def _scan_bwd_kernel(cfg, ctx_len, proj_ref, xp_ref, xn_ref, bp_ref, bn_ref, hf_ref, cw_ref, cb_ref,
                     wg_ref, ba_ref, bx_ref, lam_ref, h0_ref, pw_ref, pb_ref, ps_ref,
                     y_ref, st_ref, pad_ref, ppad_ref, a_ref, u_ref, hb_ref, carry_ref):
    n_tiles = cfg[2]
    j = n_tiles - 1 - pl.program_id(0)
    is_ctx, pos, first, last = _seq_flags(cfg, j)
    c = hf_ref.shape[1]
    xc = _conv(pad_ref, proj_ref[:, 0:c], xp_ref[...], xn_ref[...], first, last, cw_ref[...], cb_ref[...])
    a, u = _gates(xc, wg_ref, ba_ref[...], bx_ref[...], lam_ref[...])
    a_ref[...] = a
    u_ref[...] = u

    @pl.when(last)
    def _():
        carry_ref[...] = h0_ref[0]

    h = _scan(a_ref, u_ref, hb_ref, carry_ref[...], reverse=True)
    carry_ref[...] = h
    st_ref[0] = h

    ga = proj_ref[:, c:2 * c]
    y_a = (hf_ref[...] + hb_ref[...]) * jax.nn.gelu(ga, approximate=True)

    use_prev = jnp.logical_and(is_ctx, jnp.logical_not(first))
    use_next = jnp.logical_and(is_ctx, jnp.logical_not(last))
    ppad_ref[0:POOL_HALO, :] = jnp.where(use_prev, bp_ref[...], 0.0)
    ppad_ref[POOL_HALO:POOL_HALO + TM, :] = proj_ref[:, 2 * c:3 * c]
    ppad_ref[POOL_HALO + TM:, :] = jnp.where(use_next, bn_ref[...], 0.0)
    t0 = jnp.where(is_ctx, pos * GRID_W, 0)
    t_len = jnp.where(is_ctx, ctx_len, GRID_W)
    y_b = _pool_mix(ppad_ref, t0, t_len, pw_ref, pb_ref[...], ps_ref[...])
    y_ref[...] = jnp.concatenate([y_a, y_b], axis=1).astype(BF16)


def _scan_bwd_call(cfg, ctx_len, proj, hf, conv_w, conv_b, wg, ba, bx, lam, h0, pw, pb, ps):
    n_ctx_tiles, tiles_per_ctx, n_tiles, d = cfg
    n_groups = n_ctx_tiles // tiles_per_ctx
    c = conv_w.shape[1]
    rev = lambda j: n_tiles - 1 - j
    xp_spec, xn_spec = _halo_specs(cfg, 0, CONV_LEFT * SUB, CONV_RIGHT * SUB, rev)
    bp_spec, bn_spec = _halo_specs(cfg, 2, POOL_HALO, POOL_HALO, rev)
    return pl.pallas_call(
        functools.partial(_scan_bwd_kernel, cfg, ctx_len),
        grid=(n_tiles,),
        in_specs=[
            pl.BlockSpec((TM, 3 * c), lambda j: (rev(j), 0)), xp_spec, xn_spec, bp_spec, bn_spec,
            pl.BlockSpec((TM, c), lambda j: (rev(j), 0)),
            _const_spec((CONV_W, c)), _const_spec((1, c)), _const_spec((2, c // 2, c)),
            _const_spec((1, c)), _const_spec((1, c)), _const_spec((1, c)),
            _h0_spec(cfg, rev),
            _const_spec((2, c // 2, c // 2)), _const_spec((1, c)), _const_spec((1, c)),
        ],
        out_specs=[pl.BlockSpec((TM, d), lambda j: (rev(j), 0)), _state_spec(cfg, rev)],
        out_shape=[jax.ShapeDtypeStruct((n_tiles * TM, d), BF16),
                   jax.ShapeDtypeStruct((n_groups + 1, SUB, c), F32)],
        scratch_shapes=[
            pltpu.VMEM((TM + (CONV_W - 1) * SUB, c), F32),
            pltpu.VMEM((TM + 2 * POOL_HALO, c), F32),
            pltpu.VMEM((TM, c), F32), pltpu.VMEM((TM, c), F32), pltpu.VMEM((TM, c), F32),
            pltpu.VMEM((SUB, c), F32),
        ],
        compiler_params=_cparams(("arbitrary",)),
        name="lru_bwd_mix",
    )(proj, proj, proj, proj, proj, hf, conv_w, conv_b.reshape(1, c), wg, ba.reshape(1, c),
      bx.reshape(1, c), lam.reshape(1, c), h0, pw, pb.reshape(1, c), ps.reshape(1, c))


def _out_proj(x_ref, ym_ref, mod_ref, wo_ref, g2_ref):
    xm = _gated_add(x_ref[...], mod_ref[0, 2], _dot(ym_ref[...], wo_ref[...]))
    h2 = _modulate(_rms(xm, g2_ref[...]), mod_ref[0, 4], mod_ref[0, 3])
    return xm, h2


def _swiglu_act(h2b, wg_ref, wu_ref, act_ref):
    n_chunks = act_ref.shape[1] // FF_CHUNK

    def body(f, carry):
        c0 = pl.multiple_of(f * FF_CHUNK, FF_CHUNK)
        g = _dot(h2b, wg_ref[:, pl.ds(c0, FF_CHUNK)])
        u = _dot(h2b, wu_ref[:, pl.ds(c0, FF_CHUNK)])
        act_ref[:, pl.ds(c0, FF_CHUNK)] = (g * _sigmoid(g) * u).astype(BF16)
        return carry

    lax.fori_loop(0, n_chunks, body, 0)


def _dense_ffn_kernel(x_ref, ym_ref, mod_ref, wo_ref, g2_ref, wg_ref, wu_ref, wd_ref, o_ref, act_ref):
    xm, h2 = _out_proj(x_ref, ym_ref, mod_ref, wo_ref, g2_ref)
    _swiglu_act(h2.astype(BF16), wg_ref, wu_ref, act_ref)
    o_ref[...] = _gated_add(xm, mod_ref[0, 5], _dot(act_ref[...], wd_ref[...]))


def _dense_ffn_call(cfg, x, ymix, mod_l, w_out, g2, wg, wu, wd):
    _, _, n_tiles, d = cfg
    d_ff = wg.shape[1]
    f = _wide(cfg)
    row_spec = pl.BlockSpec((TM * f, d), lambda j: (j, 0))
    return pl.pallas_call(
        _dense_ffn_kernel,
        grid=(n_tiles // f,),
        in_specs=[row_spec, row_spec, _mod_spec(cfg, f), _const_spec((d, d)), _const_spec((1, d)),
                  _const_spec((d, d_ff)), _const_spec((d, d_ff)), _const_spec((d_ff, d))],
        out_specs=row_spec,
        out_shape=jax.ShapeDtypeStruct((n_tiles * TM, d), F32),
        scratch_shapes=[pltpu.VMEM((TM * f, d_ff), BF16)],
        compiler_params=_cparams(("parallel",)),
        name="out_proj_dense_ffn",
    )(x, ymix, mod_l, w_out, g2.reshape(1, d), wg, wu, wd)


M_E1, M_E2, M_W1, M_W2, M_R1, M_R2 = range(6)


def _route_kernel(x_ref, ym_ref, mod_ref, wo_ref, g2_ref, rw_ref, rb_ref, tri_ref,
                  xm_ref, hp_ref, meta_ref, cnt_ref, run_ref):
    xm, h2 = _out_proj(x_ref, ym_ref, mod_ref, wo_ref, g2_ref)
    xm_ref[...] = xm
    h_hi = h2.astype(BF16)
    h_hi32 = h_hi.astype(F32)
    hp_ref[...] = _pack_pairs(h_hi32)

    h_lo = (h2 - h_hi32).astype(BF16)
    p = _dot(h_hi, rw_ref[...])
    logits = p[:, :LANES] + p[:, LANES:] + _dot(h_lo, rw_ref[:, :LANES])
    lane = lax.broadcasted_iota(jnp.int32, logits.shape, 1)
    neg = jnp.float32(-jnp.inf)
    lg = jnp.where(lane < N_EXPERTS, logits + rb_ref[...], neg)
    m1 = jnp.max(lg, axis=1, keepdims=True)
    i1 = jnp.min(jnp.where(lg == m1, lane, LANES), axis=1, keepdims=True)
    lg2 = jnp.where(lane == i1, neg, lg)
    m2 = jnp.max(lg2, axis=1, keepdims=True)
    i2 = jnp.min(jnp.where(lg2 == m2, lane, LANES), axis=1, keepdims=True)
    e2 = jnp.exp(m2 - m1)
    den = 1.0 + e2

    @pl.when(pl.program_id(0) == 0)
    def _():
        run_ref[...] = jnp.zeros_like(run_ref)

    sel1 = lane == i1
    sel2 = lane == i2
    onehot = jnp.where(jnp.logical_or(sel1, sel2), 1.0, 0.0)
    rank = _dot(tri_ref[...], onehot.astype(BF16)) + run_ref[...]
    r1 = jnp.sum(jnp.where(sel1, rank, 0.0), axis=1, keepdims=True)
    r2 = jnp.sum(jnp.where(sel2, rank, 0.0), axis=1, keepdims=True)
    run_ref[...] = run_ref[...] + jnp.sum(onehot, axis=0, keepdims=True)
    cnt_ref[...] = run_ref[...]

    meta = jnp.zeros(logits.shape, F32)
    for k, v in ((M_E1, i1.astype(F32)), (M_E2, i2.astype(F32)), (M_W1, 1.0 / den), (M_W2, e2 / den),
                 (M_R1, r1), (M_R2, r2)):
        meta = jnp.where(lane == k, v, meta)
    meta_ref[...] = meta


def _route_call(cfg, x, ymix, mod_l, w_out, g2, router_w, router_b):
    _, _, n_tiles, d = cfg
    n = n_tiles * TM
    f = _wide(cfg)
    tm = TM * f
    row_spec = pl.BlockSpec((tm, d), lambda j: (j, 0))
    rw = jnp.zeros((d, LANES), F32).at[:, :N_EXPERTS].set(router_w)
    rw_hi = rw.astype(BF16)
    rw = jnp.concatenate([rw_hi, (rw - rw_hi.astype(F32)).astype(BF16)], axis=1)
    rb = jnp.zeros((1, LANES), F32).at[0, :N_EXPERTS].set(router_b)
    tri = jnp.tril(jnp.ones((tm, tm), BF16), -1)
    return pl.pallas_call(
        _route_kernel,
        grid=(n_tiles // f,),
        in_specs=[row_spec, row_spec, _mod_spec(cfg, f), _const_spec((d, d)), _const_spec((1, d)),
                  _const_spec((d, 2 * LANES)), _const_spec((1, LANES)), _const_spec((tm, tm))],
        out_specs=[row_spec, pl.BlockSpec((tm, d // 2), lambda j: (j, 0)),
                   pl.BlockSpec((tm, LANES), lambda j: (j, 0)), pl.BlockSpec((1, LANES), lambda j: (0, 0))],
        out_shape=[jax.ShapeDtypeStruct((n, d), F32), jax.ShapeDtypeStruct((n, d // 2), jnp.int32),
                   jax.ShapeDtypeStruct((n, LANES), F32), jax.ShapeDtypeStruct((1, LANES), F32)],
        scratch_shapes=[pltpu.VMEM((1, LANES), F32)],
        compiler_params=_cparams(("arbitrary",)),
        name="out_proj_route",
    )(x, ymix, mod_l, w_out, g2.reshape(1, d), rw, rb, tri)


def _routing_tables(cfg, meta, cnt):
    n_tiles = cfg[2]
    counts = cnt[0, :N_EXPERTS].astype(jnp.int32)
    padded = ((counts + TM - 1) // TM) * TM
    ends = jnp.cumsum(padded)
    offs = ends - padded
    e1 = meta[:, M_E1].astype(jnp.int32)
    e2 = meta[:, M_E2].astype(jnp.int32)
    pos1 = offs[e1] + meta[:, M_R1].astype(jnp.int32)
    pos2 = offs[e2] + meta[:, M_R2].astype(jnp.int32)
    n_sorted_tiles = 2 * n_tiles + N_EXPERTS
    starts = jnp.arange(n_sorted_tiles, dtype=jnp.int32) * TM
    tile_e = jnp.minimum(jnp.sum((starts[:, None] >= ends[None, :]).astype(jnp.int32), axis=1), N_EXPERTS - 1)
    n_active = (ends[-1] // TM).reshape(1)
    return jnp.concatenate([pos1, pos2]), tile_e, n_active


SC_CORES = 2
SC_SUBCORES = 16
SC_CHUNK = 64


def _sc_mesh():
    return plsc.VectorSubcoreMesh(core_axis_name="c", subcore_axis_name="s",
                                  num_cores=SC_CORES, num_subcores=SC_SUBCORES)


def _sc_scatter_rows(rows, idx, n_out):
    n_src, width = rows.shape
    n_idx = idx.shape[0]
    n_workers = SC_CORES * SC_SUBCORES
    per_w = n_idx // n_workers
    chunks = per_w // SC_CHUNK
    assert n_idx % (n_workers * SC_CHUNK) == 0 and n_src % per_w == 0
    idx3 = idx.reshape(n_workers, chunks, SC_CHUNK)

    def body(rows_hbm, idx_hbm, out_hbm, idx_v, rows_v, sem):
        wid = lax.axis_index("s") * SC_CORES + lax.axis_index("c")
        src_base = lax.rem(wid * per_w, n_src)
        pltpu.sync_copy(idx_hbm.at[wid], idx_v)

        @pl.loop(0, chunks)
        def _(i):
            off = pl.multiple_of(i * SC_CHUNK, SC_CHUNK)
            pltpu.sync_copy(rows_hbm.at[pl.ds(src_base + off, SC_CHUNK)], rows_v)
            pltpu.async_copy(rows_v, out_hbm.at[idx_v.at[i]], sem).wait()

    return pl.kernel(
        body,
        out_type=jax.ShapeDtypeStruct((n_out, width), rows.dtype),
        mesh=_sc_mesh(),
        scratch_types=[pltpu.VMEM((chunks, SC_CHUNK), jnp.int32), pltpu.VMEM((SC_CHUNK, width), rows.dtype),
                       pltpu.SemaphoreType.DMA],
        name="sc_scatter_rows",
    )(rows, idx3)


def _expert_kernel(te_ref, na_ref, s_ref, wg_ref, wu_ref, wd_ref, o_ref):
    del te_ref

    @pl.when(pl.program_id(0) < na_ref[0])
    def _():
        h = _unpack_pairs(s_ref[...]).astype(BF16)
        g = _dot(h, wg_ref[0])
        u = _dot(h, wu_ref[0])
        act = (g * _sigmoid(g) * u).astype(BF16)
        o_ref[...] = _pack_pairs(_dot(act, wd_ref[0]).astype(BF16).astype(F32))

    @pl.when(pl.program_id(0) >= na_ref[0])
    def _():
        o_ref[...] = jnp.zeros_like(o_ref)


def _expert_call(cfg, tile_e, n_active, buf, wg, wu, wd):
    _, _, n_tiles, d = cfg
    dfe = wg.shape[2]
    n_sorted_tiles = 2 * n_tiles + N_EXPERTS
    last = lambda t, na: jnp.maximum(jnp.minimum(t, na[0] - 1), 0)
    tile = lambda t, te, na: (last(t, na), 0)
    w_spec = lambda shape: pl.BlockSpec(shape, lambda t, te, na: (te[last(t, na)], 0, 0))
    return pl.pallas_call(
        _expert_kernel,
        grid_spec=pltpu.PrefetchScalarGridSpec(
            num_scalar_prefetch=2,
            grid=(n_sorted_tiles,),
            in_specs=[pl.BlockSpec((TM, d // 2), tile), w_spec((1, d, dfe)), w_spec((1, d, dfe)),
                      w_spec((1, dfe, d))],
            out_specs=pl.BlockSpec((TM, d // 2), lambda t, te, na: (t, 0)),
        ),
        out_shape=jax.ShapeDtypeStruct((n_sorted_tiles * TM, d // 2), jnp.int32),
        compiler_params=_cparams(("arbitrary",)),
        name="moe_experts",
    )(tile_e, n_active, buf, wg, wu, wd)


def _store_natural(n_ctx_tiles, y, yp_ref, ys_ref):
    is_ctx = pl.program_id(0) < n_ctx_tiles

    @pl.when(is_ctx)
    def _():
        yp_ref[0] = _from_time_major(y)

    @pl.when(jnp.logical_not(is_ctx))
    def _():
        ys_ref[...] = _from_time_major(y)


def _natural_out(cfg):
    n_ctx_tiles, tiles_per_ctx, n_tiles, d = cfg
    n_groups = n_ctx_tiles // tiles_per_ctx
    return (list(_natural_specs(cfg)),
            [jax.ShapeDtypeStruct((n_groups, SUB, tiles_per_ctx * GRID_W, d), F32),
             jax.ShapeDtypeStruct((SUB, (n_tiles - n_ctx_tiles) * GRID_W, d), F32)])


def _sc_gather_rows(table, idx):
    n_rows = idx.shape[0]
    width = table.shape[1]
    n_workers = SC_CORES * SC_SUBCORES
    assert n_rows % (n_workers * SC_CHUNK) == 0
    per_w = n_rows // n_workers
    mesh = _sc_mesh()

    def body(table_hbm, idx_hbm, out_hbm, idx_v, rows_v, sem):
        wid = lax.axis_index("s") * SC_CORES + lax.axis_index("c")
        base = wid * per_w
        pltpu.sync_copy(idx_hbm.at[pl.ds(base, per_w)], idx_v)

        @pl.loop(0, per_w // SC_CHUNK)
        def _(i):
            off = pl.multiple_of(i * SC_CHUNK, SC_CHUNK)
            pltpu.async_copy(table_hbm.at[idx_v.at[pl.ds(off, SC_CHUNK)]], rows_v, sem).wait()
            pltpu.sync_copy(rows_v, out_hbm.at[pl.ds(base + off, SC_CHUNK)])

    return pl.kernel(
        body,
        out_type=jax.ShapeDtypeStruct((n_rows, width), table.dtype),
        mesh=mesh,
        scratch_types=[pltpu.VMEM((per_w,), jnp.int32), pltpu.VMEM((SC_CHUNK, width), table.dtype),
                       pltpu.SemaphoreType.DMA],
        name="sc_gather_rows",
    )(table, idx)


def _combine_kernel(final_ctx_tiles, xm_ref, y1_ref, y2_ref, meta_ref, mod_ref, gfin_ref, *o_refs):
    meta = meta_ref[...]
    y = (meta[:, M_W1:M_W1 + 1] * _unpack_pairs(y1_ref[...])
         + meta[:, M_W2:M_W2 + 1] * _unpack_pairs(y2_ref[...]))
    x = _gated_add(xm_ref[...], mod_ref[0, 5], y)
    if final_ctx_tiles is None:
        o_refs[0][...] = x
    else:
        _store_natural(final_ctx_tiles, _rms(x, gfin_ref[...]), *o_refs)


def _combine_call(cfg, xm, yg, meta, mod_l, g_final, final):
    n_ctx_tiles, _, n_tiles, d = cfg
    row_spec = pl.BlockSpec((TM, d), lambda j: (j, 0))
    if final:
        out_specs, out_shape = _natural_out(cfg)
    else:
        out_specs, out_shape = [row_spec], [jax.ShapeDtypeStruct((n_tiles * TM, d), F32)]
    outs = pl.pallas_call(
        functools.partial(_combine_kernel, n_ctx_tiles if final else None),
        grid=(n_tiles,),
        in_specs=[row_spec, pl.BlockSpec((TM, d // 2), lambda j: (j, 0)),
                  pl.BlockSpec((TM, d // 2), lambda j: (n_tiles + j, 0)),
                  pl.BlockSpec((TM, LANES), lambda j: (j, 0)), _mod_spec(cfg), _const_spec((1, d))],
        out_specs=out_specs,
        out_shape=out_shape,
        compiler_params=_cparams(("arbitrary",)),
        name="moe_combine",
    )(xm, yg, yg, meta, mod_l, g_final.reshape(1, d))
    return outs if final else outs[0]


def _final_norm_kernel(n_ctx_tiles, x_ref, g_ref, yp_ref, ys_ref):
    _store_natural(n_ctx_tiles, _rms(x_ref[...], g_ref[...]), yp_ref, ys_ref)


def _final_norm_call(cfg, x, g):
    n_ctx_tiles, _, n_tiles, d = cfg
    out_specs, out_shape = _natural_out(cfg)
    return pl.pallas_call(
        functools.partial(_final_norm_kernel, n_ctx_tiles),
        grid=(n_tiles,),
        in_specs=[pl.BlockSpec((TM, d), lambda j: (j, 0)), _const_spec((1, d))],
        out_specs=out_specs,
        out_shape=out_shape,
        compiler_params=_cparams(("arbitrary",)),
        name="final_norm",
    )(x, g.reshape(1, d))


def _block_diag(w, per_block):
    *lead, n, k, _ = w.shape
    nb = n // per_block
    w = w.reshape(*lead, nb, per_block, k, k)
    eye = jnp.eye(per_block, dtype=w.dtype)
    out = w[..., :, :, None, :] * eye[:, None, :, None]
    return out.reshape(*lead, nb, per_block * k, per_block * k)


def _pos_tables(n_ctx_tiles, n_lat_tiles, d):
    quarter = d // 4
    omega = 1.0 / (POS_BASE ** (jnp.arange(quarter, dtype=F32) / quarter))
    er = jnp.arange(n_lat_tiles, dtype=F32)[:, None] * omega
    ec = jnp.arange(GRID_W, dtype=F32)[:, None] * omega
    row_emb = jnp.concatenate([jnp.sin(er), jnp.cos(er)], axis=-1)
    col_emb = jnp.concatenate([jnp.sin(ec), jnp.cos(ec)], axis=-1)
    del n_ctx_tiles
    return row_emb[:, None, :], jnp.repeat(col_emb, SUB, axis=0)


def kernel(x_prompt, x_sample, state_lru, c, c_ctx, norm_mix_g, w_ada, b_ada, w_in, conv_w, conv_b, lru_wa, lru_ba, lru_wx, lru_bx, lru_lam, pool_w, pool_b, pool_scale, w_out, norm_ffn_g, ffn_wg, ffn_wu, ffn_wd, moe_router_w, moe_router_b, moe_wg, moe_wu, moe_wd, norm_final_g):
    bc, tc, d = x_prompt.shape
    bl, tl, _ = x_sample.shape
    depth = w_ada.shape[0]
    c_lru = conv_w.shape[-1]
    assert bl == SUB and bc % SUB == 0 and tc % GRID_W == 0 and tl % GRID_W == 0
    assert c_lru == 4 * LANES and d == 2 * c_lru
    n_groups = bc // SUB
    tiles_per_ctx = tc // GRID_W
    n_ctx_tiles = n_groups * tiles_per_ctx
    n_lat_tiles = tl // GRID_W
    n_tiles = n_ctx_tiles + n_lat_tiles
    n_ctx = n_ctx_tiles * TM
    cfg = (n_ctx_tiles, tiles_per_ctx, n_tiles, d)

    cond = jnp.concatenate([c, c_ctx[None], jnp.zeros((SUB - 1, d), F32)], axis=0)
    mod = _ada_call(cond, w_ada, b_ada)
    mod_lat = mod[:, :SUB].reshape(depth, SUB, 6, d).transpose(0, 2, 1, 3)
    mod_ctx = jnp.broadcast_to(mod[:, SUB].reshape(depth, 6, 1, d), (depth, 6, SUB, d))
    mod = jnp.stack([mod_ctx, mod_lat], axis=1)

    row_tab, col_rep = _pos_tables(n_ctx_tiles, n_lat_tiles, d)
    first = (x_prompt.reshape(n_groups, SUB, tc, d), x_sample, row_tab, col_rep)
    x = None
    wg_lru = jnp.concatenate([_block_diag(lru_wa, 4), _block_diag(lru_wx, 4)], axis=-1).astype(BF16)
    pw = _block_diag(pool_w, 2).astype(BF16)
    h0 = jnp.concatenate([jnp.zeros_like(state_lru[None]), state_lru[None]], axis=0)

    states = []
    for l in range(depth):
        jdx = l // 2
        if l == 0:
            proj, x = _inproj_call(cfg, x, mod[l], norm_mix_g[l], w_in[l].astype(BF16), first)
        else:
            (proj,) = _inproj_call(cfg, x, mod[l], norm_mix_g[l], w_in[l].astype(BF16))
        hf, st_f = _scan_fwd_call(cfg, proj, conv_w[l], conv_b[l], wg_lru[l, 0], lru_ba[l, 0],
                                  lru_bx[l, 0], lru_lam[l, 0], h0[:, :, l, 0])
        ymix, st_b = _scan_bwd_call(cfg, tc, proj, hf, conv_w[l], conv_b[l], wg_lru[l, 1], lru_ba[l, 1],
                                    lru_bx[l, 1], lru_lam[l, 1], h0[:, :, l, 1], pw[l], pool_b[l],
                                    pool_scale[l])
        states.append(jnp.stack([st_f[:n_groups].reshape(bc, c_lru), st_b[:n_groups].reshape(bc, c_lru)], axis=1))
        if l % 2 == 0:
            x = _dense_ffn_call(cfg, x, ymix, mod[l], w_out[l].astype(BF16), norm_ffn_g[l],
                                ffn_wg[jdx].astype(BF16), ffn_wu[jdx].astype(BF16), ffn_wd[jdx].astype(BF16))
        else:
            xm, hp, meta, cnt = _route_call(cfg, x, ymix, mod[l], w_out[l].astype(BF16), norm_ffn_g[l],
                                            moe_router_w[jdx], moe_router_b[jdx])
            pos_flat, tile_e, n_active = _routing_tables(cfg, meta, cnt)
            buf = _sc_scatter_rows(hp, pos_flat, (2 * n_tiles + N_EXPERTS) * TM)
            ys = _expert_call(cfg, tile_e, n_active, buf, moe_wg[jdx].astype(BF16), moe_wu[jdx].astype(BF16),
                              moe_wd[jdx].astype(BF16))
            yg = _sc_gather_rows(ys, pos_flat)
            x = _combine_call(cfg, xm, yg, meta, mod[l], norm_final_g, final=(l == depth - 1))

    y_prompt, y_sample = x if depth % 2 == 0 else _final_norm_call(cfg, x, norm_final_g)
    new_state = jnp.stack(states, axis=1)
    return (y_prompt.reshape(bc, tc, d), y_sample, new_state)
```

```python
import functools

import jax
import jax.numpy as jnp
from jax import lax
from jax.experimental import pallas as pl
from jax.experimental.pallas import tpu as pltpu
from jax.experimental.pallas import tpu_sc as plsc

F32 = jnp.float32
BF16 = jnp.bfloat16

SUB = 8
LANES = 128
GRID_W = 64
TM = GRID_W * SUB
POS_BASE = 10000.0
N_LRU_HEADS = 8
CONV_W = 4
CONV_LEFT = CONV_W // 2
CONV_RIGHT = CONV_W - 1 - CONV_LEFT
LRU_C = 8.0
POOL_WINDOWS = (2, 4, 8, 16)
POOL_HALO = 8 * SUB
N_EXPERTS = 8
EPS = 1e-6
FF_CHUNK = 256
VMEM_LIMIT = 56 * 1024 * 1024


def _cparams(sem):
    return pltpu.CompilerParams(dimension_semantics=sem, vmem_limit_bytes=VMEM_LIMIT)


def _const_spec(shape):
    nd = len(shape)
    return pl.BlockSpec(shape, lambda *_: (0,) * nd, pipeline_mode=pl.Buffered(1))


def _rms(x, g):
    ms = jnp.mean(x * x, axis=-1, keepdims=True)
    return x * lax.rsqrt(ms + EPS) * g


def _per_seq(x, v, op):
    r, c = x.shape
    x3 = x.reshape(r // SUB, SUB, c)
    return op(x3, v[None]).reshape(r, c)


def _modulate(h, scale, shift):
    r, c = h.shape
    h3 = h.reshape(r // SUB, SUB, c)
    return (h3 * (1.0 + scale)[None] + shift[None]).reshape(r, c)


def _gated_add(x, gate, y):
    return x + _per_seq(y, gate, lambda a, b: a * b)


def _dot(a, b):
    return jnp.dot(a, b, preferred_element_type=F32)


def _pack_pairs(x):
    bits = lax.bitcast_convert_type(x, jnp.uint32)
    half = bits.shape[1] // 2
    w = lax.shift_right_logical(bits[:, :half], jnp.uint32(16)) | (bits[:, half:] & jnp.uint32(0xFFFF0000))
    return lax.bitcast_convert_type(w, jnp.int32)


def _unpack_pairs(w):
    w = lax.bitcast_convert_type(w, jnp.uint32)
    lo = lax.bitcast_convert_type(lax.shift_left(w, jnp.uint32(16)), F32)
    hi = lax.bitcast_convert_type(w & jnp.uint32(0xFFFF0000), F32)
    return jnp.concatenate([lo, hi], axis=1)


def _sigmoid(x):
    return 0.5 * jnp.tanh(0.5 * x) + 0.5


def _ada_kernel(c_ref, w_ref, b_ref, o_ref):
    c = c_ref[...]
    s = (c * jax.nn.sigmoid(c)).astype(BF16)
    o_ref[0] = _dot(s, w_ref[0].astype(BF16)) + b_ref[0]


def _ada_call(cond, w_ada, b_ada):
    depth, d, d6 = w_ada.shape
    nr = cond.shape[0]
    bn = d6 // 4
    return pl.pallas_call(
        _ada_kernel,
        grid=(depth, d6 // bn),
        in_specs=[
            pl.BlockSpec((nr, d), lambda l, n: (0, 0)),
            pl.BlockSpec((1, d, bn), lambda l, n: (l, 0, n)),
            pl.BlockSpec((1, 1, bn), lambda l, n: (l, 0, n)),
        ],
        out_specs=pl.BlockSpec((1, nr, bn), lambda l, n: (l, 0, n)),
        out_shape=jax.ShapeDtypeStruct((depth, nr, d6), F32),
        compiler_params=_cparams(("parallel", "parallel")),
        name="ada_mod",
    )(cond, w_ada, b_ada.reshape(depth, 1, d6))


def _to_time_major(x):
    s, t, d = x.shape
    return jnp.swapaxes(x, 0, 1).reshape(s * t, d)


def _from_time_major(x):
    r, d = x.shape
    return jnp.swapaxes(x.reshape(r // SUB, SUB, d), 0, 1)


def _inproj_kernel(n_ctx_tiles, *refs):
    if n_ctx_tiles is not None:
        xp_ref, xs_ref, row_ref, col_ref, mod_ref, g_ref, w_ref, o_ref, x0_ref = refs
        is_ctx = pl.program_id(0) < n_ctx_tiles

        @pl.when(is_ctx)
        def _():
            x0_ref[...] = _to_time_major(xp_ref[0])

        @pl.when(jnp.logical_not(is_ctx))
        def _():
            x = _to_time_major(xs_ref[...])
            half = x.shape[1] // 2
            x0_ref[...] = jnp.concatenate([x[:, :half] + row_ref[0], x[:, half:] + col_ref[...]], axis=1)

        x = x0_ref[...]
    else:
        x_ref, mod_ref, g_ref, w_ref, o_ref = refs
        x = x_ref[...]
    h = _modulate(_rms(x, g_ref[...]), mod_ref[0, 1], mod_ref[0, 0])
    o_ref[...] = _dot(h.astype(BF16), w_ref[...])


def _natural_specs(cfg):
    n_ctx_tiles, tiles_per_ctx, _, d = cfg
    n_groups = n_ctx_tiles // tiles_per_ctx

    def ctx_idx(j):
        jc = jnp.minimum(j, n_ctx_tiles - 1)
        return (jc // tiles_per_ctx, 0, lax.rem(jc, tiles_per_ctx), 0)

    del n_groups
    return (pl.BlockSpec((1, SUB, GRID_W, d), ctx_idx),
            pl.BlockSpec((SUB, GRID_W, d), lambda j: (0, jnp.maximum(j - n_ctx_tiles, 0), 0)))


def _wide(cfg):
    n_ctx_tiles, _, n_tiles, _ = cfg
    return 2 if n_ctx_tiles % 2 == 0 and n_tiles % 2 == 0 else 1


def _mod_spec(cfg, f=1):
    n_ctx_steps = cfg[0] // f
    d = cfg[3]
    return pl.BlockSpec((1, 6, SUB, d), lambda j: (jnp.where(j >= n_ctx_steps, 1, 0), 0, 0, 0))


def _inproj_call(cfg, x, mod_l, g, w_in, first=None):
    n_ctx_tiles, _, n_tiles, d = cfg
    d_in = w_in.shape[1]
    f = 1 if first is not None else _wide(cfg)
    tm = TM * f
    row_spec = pl.BlockSpec((tm, d), lambda j: (j, 0))
    out_specs = [pl.BlockSpec((tm, d_in), lambda j: (j, 0))]
    out_shape = [jax.ShapeDtypeStruct((n_tiles * TM, d_in), F32)]
    if first is not None:
        xp_spec, xs_spec = _natural_specs(cfg)
        in_specs = [xp_spec, xs_spec,
                    pl.BlockSpec((1, 1, d // 2), lambda j: (jnp.maximum(j - n_ctx_tiles, 0), 0, 0)),
                    _const_spec((TM, d // 2))]
        args = list(first)
        out_specs.append(row_spec)
        out_shape.append(jax.ShapeDtypeStruct((n_tiles * TM, d), F32))
    else:
        in_specs = [row_spec]
        args = [x]
    in_specs += [_mod_spec(cfg, f), _const_spec((1, d)), _const_spec((d, d_in))]
    args += [mod_l, g.reshape(1, d), w_in]
    return pl.pallas_call(
        functools.partial(_inproj_kernel, n_ctx_tiles if first is not None else None),
        grid=(n_tiles // f,),
        in_specs=in_specs,
        out_specs=out_specs,
        out_shape=out_shape,
        compiler_params=_cparams(("parallel",)),
        name="in_proj",
    )(*args)


def _seq_flags(cfg, j):
    n_ctx_tiles, tiles_per_ctx, n_tiles, _ = cfg
    is_ctx = j < n_ctx_tiles
    pos = lax.rem(j, tiles_per_ctx)
    first = jnp.where(is_ctx, pos == 0, j == n_ctx_tiles)
    last = jnp.where(is_ctx, pos == tiles_per_ctx - 1, j == n_tiles - 1)
    return is_ctx, pos, first, last


def _conv(pad_ref, xa, prev, nxt, first, last, cw, cb):
    lo = CONV_LEFT * SUB
    pad_ref[0:lo, :] = jnp.where(first, 0.0, prev)
    pad_ref[lo:lo + TM, :] = xa
    pad_ref[lo + TM:lo + TM + CONV_RIGHT * SUB, :] = jnp.where(last, 0.0, nxt)
    y = cb
    for k in range(CONV_W):
        y = y + pad_ref[k * SUB:k * SUB + TM, :] * cw[k:k + 1, :]
    return y


def _gates(xc, wg_ref, ba, bx, lam):
    half = xc.shape[1] // 2
    xb = xc.astype(BF16)
    g0 = _dot(xb[:, :half], wg_ref[0])
    g1 = _dot(xb[:, half:], wg_ref[1])
    r = _sigmoid(jnp.concatenate([g0[:, :half], g1[:, :half]], axis=1) + ba)
    i = _sigmoid(jnp.concatenate([g0[:, half:], g1[:, half:]], axis=1) + bx)
    z = -lam
    decay = LRU_C * (jnp.maximum(z, 0.0) + jnp.log1p(jnp.exp(-jnp.abs(z))))
    neg_log_a = r * decay
    a = jnp.exp(-neg_log_a)
    z = jnp.tanh(neg_log_a) * (a * a + 1.0)
    root = jnp.where(z > 0.0, z * lax.rsqrt(z), 0.0)
    u = root * (i * xc)
    return a, u


def _scan(a_ref, u_ref, h_ref, h, reverse):
    steps = TM // SUB

    def body(k, h):
        t = steps - 1 - k if reverse else k
        r0 = pl.multiple_of(t * SUB, SUB)
        h = a_ref[pl.ds(r0, SUB), :] * h + u_ref[pl.ds(r0, SUB), :]
        h_ref[pl.ds(r0, SUB), :] = h
        return h

    return lax.fori_loop(0, steps, body, h, unroll=8)


def _scan_fwd_kernel(cfg, xa_ref, xp_ref, xn_ref, cw_ref, cb_ref, wg_ref, ba_ref, bx_ref,
                     lam_ref, h0_ref, hf_ref, st_ref, pad_ref, a_ref, u_ref, carry_ref):
    j = pl.program_id(0)
    _, _, first, last = _seq_flags(cfg, j)
    xc = _conv(pad_ref, xa_ref[...], xp_ref[...], xn_ref[...], first, last, cw_ref[...], cb_ref[...])
    a, u = _gates(xc, wg_ref, ba_ref[...], bx_ref[...], lam_ref[...])
    a_ref[...] = a
    u_ref[...] = u

    @pl.when(first)
    def _():
        carry_ref[...] = h0_ref[0]

    h = _scan(a_ref, u_ref, hf_ref, carry_ref[...], reverse=False)
    carry_ref[...] = h
    st_ref[0] = h


def _halo_specs(cfg, col, rows_prev, rows_next, tile_of):
    n_tiles = cfg[2]
    c = 512
    nb_prev = TM // rows_prev
    nb_next = TM // rows_next
    last_next = n_tiles * nb_next - 1
    prev = pl.BlockSpec((rows_prev, c), lambda j: (jnp.maximum(tile_of(j) * nb_prev - 1, 0), col))
    nxt = pl.BlockSpec((rows_next, c), lambda j: (jnp.minimum((tile_of(j) + 1) * nb_next, last_next), col))
    return prev, nxt


def _state_spec(cfg, tile_of):
    n_ctx_tiles, tiles_per_ctx, _, _ = cfg
    n_groups = n_ctx_tiles // tiles_per_ctx
    c = 512
    return pl.BlockSpec((1, SUB, c), lambda j: (jnp.minimum(tile_of(j) // tiles_per_ctx, n_groups), 0, 0))


def _h0_spec(cfg, tile_of):
    n_ctx_tiles = cfg[0]
    return pl.BlockSpec((1, SUB, 512), lambda j: (jnp.where(tile_of(j) >= n_ctx_tiles, 1, 0), 0, 0))


def _scan_fwd_call(cfg, proj, conv_w, conv_b, wg, ba, bx, lam, h0):
    n_ctx_tiles, tiles_per_ctx, n_tiles, _ = cfg
    n_groups = n_ctx_tiles // tiles_per_ctx
    c = conv_w.shape[1]
    ident = lambda j: j
    xp_spec, xn_spec = _halo_specs(cfg, 0, CONV_LEFT * SUB, CONV_RIGHT * SUB, ident)
    return pl.pallas_call(
        functools.partial(_scan_fwd_kernel, cfg),
        grid=(n_tiles,),
        in_specs=[
            pl.BlockSpec((TM, c), lambda j: (j, 0)), xp_spec, xn_spec,
            _const_spec((CONV_W, c)), _const_spec((1, c)), _const_spec((2, c // 2, c)),
            _const_spec((1, c)), _const_spec((1, c)), _const_spec((1, c)),
            _h0_spec(cfg, ident),
        ],
        out_specs=[pl.BlockSpec((TM, c), lambda j: (j, 0)), _state_spec(cfg, ident)],
        out_shape=[jax.ShapeDtypeStruct((n_tiles * TM, c), F32),
                   jax.ShapeDtypeStruct((n_groups + 1, SUB, c), F32)],
        scratch_shapes=[
            pltpu.VMEM((TM + (CONV_W - 1) * SUB, c), F32),
            pltpu.VMEM((TM, c), F32), pltpu.VMEM((TM, c), F32), pltpu.VMEM((SUB, c), F32),
        ],
        compiler_params=_cparams(("arbitrary",)),
        name="lru_fwd",
    )(proj, proj, proj, conv_w, conv_b.reshape(1, c), wg, ba.reshape(1, c), bx.reshape(1, c),
      lam.reshape(1, c), h0)


def _pool_mix(pad_ref, t0, t_len, pw_ref, pb, ps):
    c = pad_ref.shape[1]
    gw = c // len(POOL_WINDOWS)
    t_pos = lax.shift_right_logical(lax.broadcasted_iota(jnp.int32, (TM, gw), 0), 3) + t0
    outs = []
    for g, k in enumerate(POOL_WINDOWS):
        left = k // 2
        right = k - 1 - left
        s = None
        for o in range(-left, right + 1):
            v = pad_ref[POOL_HALO + o * SUB:POOL_HALO + o * SUB + TM, g * gw:(g + 1) * gw]
            s = v if s is None else s + v
        cnt = (jnp.minimum(t_pos + right + 1, t_len) - jnp.maximum(t_pos - left, 0)).astype(F32)
        outs.append(s / cnt - pad_ref[POOL_HALO:POOL_HALO + TM, g * gw:(g + 1) * gw])
    d = jnp.concatenate(outs, axis=1).astype(BF16)
    half = c // 2
    y = jnp.concatenate([_dot(d[:, :half], pw_ref[0]), _dot(d[:, half:], pw_ref[1])], axis=1)
    return (y + pb) * ps


def _scan_bwd_kernel(cfg, ctx_len, proj_ref, xp_ref, xn_ref, bp_ref, bn_ref, hf_ref, cw_ref, cb_ref,
                     wg_ref, ba_ref, bx_ref, lam_ref, h0_ref, pw_ref, pb_ref, ps_ref,
                     y_ref, st_ref, pad_ref, ppad_ref, a_ref, u_ref, hb_ref, carry_ref):
    n_tiles = cfg[2]
    j = n_tiles - 1 - pl.program_id(0)
    is_ctx, pos, first, last = _seq_flags(cfg, j)
    c = hf_ref.shape[1]
    xc = _conv(pad_ref, proj_ref[:, 0:c], xp_ref[...], xn_ref[...], first, last, cw_ref[...], cb_ref[...])
    a, u = _gates(xc, wg_ref, ba_ref[...], bx_ref[...], lam_ref[...])
    a_ref[...] = a
    u_ref[...] = u

    @pl.when(last)
    def _():
        carry_ref[...] = h0_ref[0]

    h = _scan(a_ref, u_ref, hb_ref, carry_ref[...], reverse=True)
    carry_ref[...] = h
    st_ref[0] = h

    ga = proj_ref[:, c:2 * c]
    y_a = (hf_ref[...] + hb_ref[...]) * jax.nn.gelu(ga, approximate=True)

    use_prev = jnp.logical_and(is_ctx, jnp.logical_not(first))
    use_next = jnp.logical_and(is_ctx, jnp.logical_not(last))
    ppad_ref[0:POOL_HALO, :] = jnp.where(use_prev, bp_ref[...], 0.0)
    ppad_ref[POOL_HALO:POOL_HALO + TM, :] = proj_ref[:, 2 * c:3 * c]
    ppad_ref[POOL_HALO + TM:, :] = jnp.where(use_next, bn_ref[...], 0.0)
    t0 = jnp.where(is_ctx, pos * GRID_W, 0)
    t_len = jnp.where(is_ctx, ctx_len, GRID_W)
    y_b = _pool_mix(ppad_ref, t0, t_len, pw_ref, pb_ref[...], ps_ref[...])
    y_ref[...] = jnp.concatenate([y_a, y_b], axis=1).astype(BF16)


def _scan_bwd_call(cfg, ctx_len, proj, hf, conv_w, conv_b, wg, ba, bx, lam, h0, pw, pb, ps):
    n_ctx_tiles, tiles_per_ctx, n_tiles, d = cfg
    n_groups = n_ctx_tiles // tiles_per_ctx
    c = conv_w.shape[1]
    rev = lambda j: n_tiles - 1 - j
    xp_spec, xn_spec = _halo_specs(cfg, 0, CONV_LEFT * SUB, CONV_RIGHT * SUB, rev)
    bp_spec, bn_spec = _halo_specs(cfg, 2, POOL_HALO, POOL_HALO, rev)
    return pl.pallas_call(
        functools.partial(_scan_bwd_kernel, cfg, ctx_len),
        grid=(n_tiles,),
        in_specs=[
            pl.BlockSpec((TM, 3 * c), lambda j: (rev(j), 0)), xp_spec, xn_spec, bp_spec, bn_spec,
            pl.BlockSpec((TM, c), lambda j: (rev(j), 0)),
            _const_spec((CONV_W, c)), _const_spec((1, c)), _const_spec((2, c // 2, c)),
            _const_spec((1, c)), _const_spec((1, c)), _const_spec((1, c)),
            _h0_spec(cfg, rev),
            _const_spec((2, c // 2, c // 2)), _const_spec((1, c)), _const_spec((1, c)),
        ],
        out_specs=[pl.BlockSpec((TM, d), lambda j: (rev(j), 0)), _state_spec(cfg, rev)],
        out_shape=[jax.ShapeDtypeStruct((n_tiles * TM, d), BF16),
                   jax.ShapeDtypeStruct((n_groups + 1, SUB, c), F32)],
        scratch_shapes=[
            pltpu.VMEM((TM + (CONV_W - 1) * SUB, c), F32),
            pltpu.VMEM((TM + 2 * POOL_HALO, c), F32),
            pltpu.VMEM((TM, c), F32), pltpu.VMEM((TM, c), F32), pltpu.VMEM((TM, c), F32),
            pltpu.VMEM((SUB, c), F32),
        ],
        compiler_params=_cparams(("arbitrary",)),
        name="lru_bwd_mix",
    )(proj, proj, proj, proj, proj, hf, conv_w, conv_b.reshape(1, c), wg, ba.reshape(1, c),
      bx.reshape(1, c), lam.reshape(1, c), h0, pw, pb.reshape(1, c), ps.reshape(1, c))


def _out_proj(x_ref, ym_ref, mod_ref, wo_ref, g2_ref):
    xm = _gated_add(x_ref[...], mod_ref[0, 2], _dot(ym_ref[...], wo_ref[...]))
    h2 = _modulate(_rms(xm, g2_ref[...]), mod_ref[0, 4], mod_ref[0, 3])
    return xm, h2


def _swiglu_act(h2b, wg_ref, wu_ref, act_ref):
    n_chunks = act_ref.shape[1] // FF_CHUNK

    def body(f, carry):
        c0 = pl.multiple_of(f * FF_CHUNK, FF_CHUNK)
        g = _dot(h2b, wg_ref[:, pl.ds(c0, FF_CHUNK)])
        u = _dot(h2b, wu_ref[:, pl.ds(c0, FF_CHUNK)])
        act_ref[:, pl.ds(c0, FF_CHUNK)] = (g * _sigmoid(g) * u).astype(BF16)
        return carry

    lax.fori_loop(0, n_chunks, body, 0, unroll=True)


def _dense_ffn_kernel(x_ref, ym_ref, mod_ref, wo_ref, g2_ref, wg_ref, wu_ref, wd_ref, o_ref, act_ref):
    xm, h2 = _out_proj(x_ref, ym_ref, mod_ref, wo_ref, g2_ref)
    _swiglu_act(h2.astype(BF16), wg_ref, wu_ref, act_ref)
    o_ref[...] = _gated_add(xm, mod_ref[0, 5], _dot(act_ref[...], wd_ref[...]))


def _dense_ffn_call(cfg, x, ymix, mod_l, w_out, g2, wg, wu, wd):
    _, _, n_tiles, d = cfg
    d_ff = wg.shape[1]
    f = _wide(cfg)
    row_spec = pl.BlockSpec((TM * f, d), lambda j: (j, 0))
    return pl.pallas_call(
        _dense_ffn_kernel,
        grid=(n_tiles // f,),
        in_specs=[row_spec, row_spec, _mod_spec(cfg, f), _const_spec((d, d)), _const_spec((1, d)),
                  _const_spec((d, d_ff)), _const_spec((d, d_ff)), _const_spec((d_ff, d))],
        out_specs=row_spec,
        out_shape=jax.ShapeDtypeStruct((n_tiles * TM, d), F32),
        scratch_shapes=[pltpu.VMEM((TM * f, d_ff), BF16)],
        compiler_params=_cparams(("parallel",)),
        name="out_proj_dense_ffn",
    )(x, ymix, mod_l, w_out, g2.reshape(1, d), wg, wu, wd)


M_E1, M_E2, M_W1, M_W2, M_R1, M_R2 = range(6)


def _route_kernel(x_ref, ym_ref, mod_ref, wo_ref, g2_ref, rw_ref, rb_ref, tri_ref,
                  xm_ref, hp_ref, meta_ref, mt_ref, cnt_ref, run_ref):
    xm, h2 = _out_proj(x_ref, ym_ref, mod_ref, wo_ref, g2_ref)
    xm_ref[...] = xm
    h_hi = h2.astype(BF16)
    h_hi32 = h_hi.astype(F32)
    hp_ref[...] = _pack_pairs(h_hi32)

    h_lo = (h2 - h_hi32).astype(BF16)
    p = _dot(h_hi, rw_ref[...])
    logits = p[:, :LANES] + p[:, LANES:] + _dot(h_lo, rw_ref[:, :LANES])
    lane = lax.broadcasted_iota(jnp.int32, logits.shape, 1)
    neg = jnp.float32(-jnp.inf)
    lg = jnp.where(lane < N_EXPERTS, logits + rb_ref[...], neg)
    m1 = jnp.max(lg, axis=1, keepdims=True)
    i1 = jnp.min(jnp.where(lg == m1, lane, LANES), axis=1, keepdims=True)
    lg2 = jnp.where(lane == i1, neg, lg)
    m2 = jnp.max(lg2, axis=1, keepdims=True)
    i2 = jnp.min(jnp.where(lg2 == m2, lane, LANES), axis=1, keepdims=True)
    e2 = jnp.exp(m2 - m1)
    den = 1.0 + e2

    @pl.when(pl.program_id(0) == 0)
    def _():
        run_ref[...] = jnp.zeros_like(run_ref)

    sel1 = lane == i1
    sel2 = lane == i2
    onehot = jnp.where(jnp.logical_or(sel1, sel2), 1.0, 0.0)
    rank = _dot(tri_ref[...], onehot.astype(BF16)) + run_ref[...]
    r1 = jnp.sum(jnp.where(sel1, rank, 0.0), axis=1, keepdims=True)
    r2 = jnp.sum(jnp.where(sel2, rank, 0.0), axis=1, keepdims=True)
    run_ref[...] = run_ref[...] + jnp.sum(onehot, axis=0, keepdims=True)
    cnt_ref[...] = run_ref[...]

    meta = jnp.zeros(logits.shape, F32)
    for k, v in ((M_E1, i1.astype(F32)), (M_E2, i2.astype(F32)), (M_W1, 1.0 / den), (M_W2, e2 / den),
                 (M_R1, r1), (M_R2, r2)):
        meta = jnp.where(lane == k, v, meta)
    meta_ref[...] = meta
    mt_ref[0] = jnp.transpose(meta)[:SUB]


def _route_call(cfg, x, ymix, mod_l, w_out, g2, router_w, router_b):
    _, _, n_tiles, d = cfg
    n = n_tiles * TM
    f = _wide(cfg)
    tm = TM * f
    row_spec = pl.BlockSpec((tm, d), lambda j: (j, 0))
    rw = jnp.zeros((d, LANES), F32).at[:, :N_EXPERTS].set(router_w)
    rw_hi = rw.astype(BF16)
    rw = jnp.concatenate([rw_hi, (rw - rw_hi.astype(F32)).astype(BF16)], axis=1)
    rb = jnp.zeros((1, LANES), F32).at[0, :N_EXPERTS].set(router_b)
    tri = jnp.tril(jnp.ones((tm, tm), BF16), -1)
    return pl.pallas_call(
        _route_kernel,
        grid=(n_tiles // f,),
        in_specs=[row_spec, row_spec, _mod_spec(cfg, f), _const_spec((d, d)), _const_spec((1, d)),
                  _const_spec((d, 2 * LANES)), _const_spec((1, LANES)), _const_spec((tm, tm))],
        out_specs=[row_spec, pl.BlockSpec((tm, d // 2), lambda j: (j, 0)),
                   pl.BlockSpec((tm, LANES), lambda j: (j, 0)), pl.BlockSpec((1, SUB, tm), lambda j: (j, 0, 0)),
                   pl.BlockSpec((1, LANES), lambda j: (0, 0))],
        out_shape=[jax.ShapeDtypeStruct((n, d), F32), jax.ShapeDtypeStruct((n, d // 2), jnp.int32),
                   jax.ShapeDtypeStruct((n, LANES), F32), jax.ShapeDtypeStruct((n // tm, SUB, tm), F32),
                   jax.ShapeDtypeStruct((1, LANES), F32)],
        scratch_shapes=[pltpu.VMEM((1, LANES), F32)],
        compiler_params=_cparams(("arbitrary",)),
        name="out_proj_route",
    )(x, ymix, mod_l, w_out, g2.reshape(1, d), rw, rb, tri)


def _routing_tables(cfg, mt, cnt):
    n_tiles = cfg[2]
    counts = cnt[0, :N_EXPERTS].astype(jnp.int32)
    padded = ((counts + TM - 1) // TM) * TM
    ends = jnp.cumsum(padded)
    offs = ends - padded
    row = lambda k: mt[:, k, :].reshape(-1).astype(jnp.int32)
    pos1 = offs[row(M_E1)] + row(M_R1)
    pos2 = offs[row(M_E2)] + row(M_R2)
    n_sorted_tiles = 2 * n_tiles + N_EXPERTS
    starts = jnp.arange(n_sorted_tiles, dtype=jnp.int32) * TM
    tile_e = jnp.minimum(jnp.sum((starts[:, None] >= ends[None, :]).astype(jnp.int32), axis=1), N_EXPERTS - 1)
    n_active = (ends[-1] // TM).reshape(1)
    return jnp.concatenate([pos1, pos2]), tile_e, n_active


SC_CORES = 2
SC_SUBCORES = 16
SC_CHUNK = 64


def _sc_mesh():
    return plsc.VectorSubcoreMesh(core_axis_name="c", subcore_axis_name="s",
                                  num_cores=SC_CORES, num_subcores=SC_SUBCORES)


def _sc_scatter_rows(rows, idx, n_out):
    n_src, width = rows.shape
    n_idx = idx.shape[0]
    n_workers = SC_CORES * SC_SUBCORES
    per_w = n_idx // n_workers
    chunks = per_w // SC_CHUNK
    assert n_idx % (n_workers * SC_CHUNK) == 0 and n_src % per_w == 0
    idx3 = idx.reshape(n_workers, chunks, SC_CHUNK)

    def body(rows_hbm, idx_hbm, out_hbm, idx_v, rows_v, sem):
        wid = lax.axis_index("s") * SC_CORES + lax.axis_index("c")
        src_base = lax.rem(wid * per_w, n_src)
        pltpu.sync_copy(idx_hbm.at[wid], idx_v)

        @pl.loop(0, chunks)
        def _(i):
            off = pl.multiple_of(i * SC_CHUNK, SC_CHUNK)
            pltpu.sync_copy(rows_hbm.at[pl.ds(src_base + off, SC_CHUNK)], rows_v)
            pltpu.async_copy(rows_v, out_hbm.at[idx_v.at[i]], sem).wait()

    return pl.kernel(
        body,
        out_type=jax.ShapeDtypeStruct((n_out, width), rows.dtype),
        mesh=_sc_mesh(),
        scratch_types=[pltpu.VMEM((chunks, SC_CHUNK), jnp.int32), pltpu.VMEM((SC_CHUNK, width), rows.dtype),
                       pltpu.SemaphoreType.DMA],
        name="sc_scatter_rows",
    )(rows, idx3)


def _expert_kernel(te_ref, na_ref, s_ref, wgu_ref, wd_ref, o_ref):
    del te_ref

    @pl.when(pl.program_id(0) < na_ref[0])
    def _():
        h = _unpack_pairs(s_ref[...]).astype(BF16)
        gu = _dot(h, wgu_ref[0])
        dfe = gu.shape[1] // 2
        g = gu[:, :dfe]
        act = (g * _sigmoid(g) * gu[:, dfe:]).astype(BF16)
        o_ref[...] = _pack_pairs(_dot(act, wd_ref[0]).astype(BF16).astype(F32))

    @pl.when(pl.program_id(0) >= na_ref[0])
    def _():
        o_ref[...] = jnp.zeros_like(o_ref)


def _expert_call(cfg, tile_e, n_active, buf, wgu, wd):
    _, _, n_tiles, d = cfg
    dfe = wd.shape[1]
    n_sorted_tiles = 2 * n_tiles + N_EXPERTS
    last = lambda t, na: jnp.maximum(jnp.minimum(t, na[0] - 1), 0)
    tile = lambda t, te, na: (last(t, na), 0)
    w_spec = lambda shape: pl.BlockSpec(shape, lambda t, te, na: (te[last(t, na)], 0, 0))
    return pl.pallas_call(
        _expert_kernel,
        grid_spec=pltpu.PrefetchScalarGridSpec(
            num_scalar_prefetch=2,
            grid=(n_sorted_tiles,),
            in_specs=[pl.BlockSpec((TM, d // 2), tile), w_spec((1, d, 2 * dfe)), w_spec((1, dfe, d))],
            out_specs=pl.BlockSpec((TM, d // 2), lambda t, te, na: (t, 0)),
        ),
        out_shape=jax.ShapeDtypeStruct((n_sorted_tiles * TM, d // 2), jnp.int32),
        compiler_params=_cparams(("arbitrary",)),
        name="moe_experts",
    )(tile_e, n_active, buf, wgu, wd)


def _store_natural(n_ctx_tiles, y, yp_ref, ys_ref):
    is_ctx = pl.program_id(0) < n_ctx_tiles

    @pl.when(is_ctx)
    def _():
        yp_ref[0] = _from_time_major(y)

    @pl.when(jnp.logical_not(is_ctx))
    def _():
        ys_ref[...] = _from_time_major(y)


def _natural_out(cfg):
    n_ctx_tiles, tiles_per_ctx, n_tiles, d = cfg
    n_groups = n_ctx_tiles // tiles_per_ctx
    return (list(_natural_specs(cfg)),
            [jax.ShapeDtypeStruct((n_groups, SUB, tiles_per_ctx * GRID_W, d), F32),
             jax.ShapeDtypeStruct((SUB, (n_tiles - n_ctx_tiles) * GRID_W, d), F32)])


def _sc_gather_rows(table, idx):
    n_rows = idx.shape[0]
    width = table.shape[1]
    n_workers = SC_CORES * SC_SUBCORES
    assert n_rows % (n_workers * SC_CHUNK) == 0
    per_w = n_rows // n_workers
    mesh = _sc_mesh()

    def body(table_hbm, idx_hbm, out_hbm, idx_v, rows_v, sem):
        wid = lax.axis_index("s") * SC_CORES + lax.axis_index("c")
        base = wid * per_w
        pltpu.sync_copy(idx_hbm.at[pl.ds(base, per_w)], idx_v)

        @pl.loop(0, per_w // SC_CHUNK)
        def _(i):
            off = pl.multiple_of(i * SC_CHUNK, SC_CHUNK)
            pltpu.async_copy(table_hbm.at[idx_v.at[pl.ds(off, SC_CHUNK)]], rows_v, sem).wait()
            pltpu.sync_copy(rows_v, out_hbm.at[pl.ds(base + off, SC_CHUNK)])

    return pl.kernel(
        body,
        out_type=jax.ShapeDtypeStruct((n_rows, width), table.dtype),
        mesh=mesh,
        scratch_types=[pltpu.VMEM((per_w,), jnp.int32), pltpu.VMEM((SC_CHUNK, width), table.dtype),
                       pltpu.SemaphoreType.DMA],
        name="sc_gather_rows",
    )(table, idx)


def _combine_kernel(final_ctx_tiles, xm_ref, y1_ref, y2_ref, meta_ref, mod_ref, gfin_ref, *o_refs):
    meta = meta_ref[...]
    y = (meta[:, M_W1:M_W1 + 1] * _unpack_pairs(y1_ref[...])
         + meta[:, M_W2:M_W2 + 1] * _unpack_pairs(y2_ref[...]))
    x = _gated_add(xm_ref[...], mod_ref[0, 5], y)
    if final_ctx_tiles is None:
        o_refs[0][...] = x
    else:
        _store_natural(final_ctx_tiles, _rms(x, gfin_ref[...]), *o_refs)


def _combine_call(cfg, xm, yg, meta, mod_l, g_final, final):
    n_ctx_tiles, _, n_tiles, d = cfg
    row_spec = pl.BlockSpec((TM, d), lambda j: (j, 0))
    if final:
        out_specs, out_shape = _natural_out(cfg)
    else:
        out_specs, out_shape = [row_spec], [jax.ShapeDtypeStruct((n_tiles * TM, d), F32)]
    outs = pl.pallas_call(
        functools.partial(_combine_kernel, n_ctx_tiles if final else None),
        grid=(n_tiles,),
        in_specs=[row_spec, pl.BlockSpec((TM, d // 2), lambda j: (j, 0)),
                  pl.BlockSpec((TM, d // 2), lambda j: (n_tiles + j, 0)),
                  pl.BlockSpec((TM, LANES), lambda j: (j, 0)), _mod_spec(cfg), _const_spec((1, d))],
        out_specs=out_specs,
        out_shape=out_shape,
        compiler_params=_cparams(("arbitrary",)),
        name="moe_combine",
    )(xm, yg, yg, meta, mod_l, g_final.reshape(1, d))
    return outs if final else outs[0]


def _final_norm_kernel(n_ctx_tiles, x_ref, g_ref, yp_ref, ys_ref):
    _store_natural(n_ctx_tiles, _rms(x_ref[...], g_ref[...]), yp_ref, ys_ref)


def _final_norm_call(cfg, x, g):
    n_ctx_tiles, _, n_tiles, d = cfg
    out_specs, out_shape = _natural_out(cfg)
    return pl.pallas_call(
        functools.partial(_final_norm_kernel, n_ctx_tiles),
        grid=(n_tiles,),
        in_specs=[pl.BlockSpec((TM, d), lambda j: (j, 0)), _const_spec((1, d))],
        out_specs=out_specs,
        out_shape=out_shape,
        compiler_params=_cparams(("arbitrary",)),
        name="final_norm",
    )(x, g.reshape(1, d))


def _block_diag(w, per_block):
    *lead, n, k, _ = w.shape
    nb = n // per_block
    w = w.reshape(*lead, nb, per_block, k, k)
    eye = jnp.eye(per_block, dtype=w.dtype)
    out = w[..., :, :, None, :] * eye[:, None, :, None]
    return out.reshape(*lead, nb, per_block * k, per_block * k)


def _pos_tables(n_ctx_tiles, n_lat_tiles, d):
    quarter = d // 4
    omega = 1.0 / (POS_BASE ** (jnp.arange(quarter, dtype=F32) / quarter))
    er = jnp.arange(n_lat_tiles, dtype=F32)[:, None] * omega
    ec = jnp.arange(GRID_W, dtype=F32)[:, None] * omega
    row_emb = jnp.concatenate([jnp.sin(er), jnp.cos(er)], axis=-1)
    col_emb = jnp.concatenate([jnp.sin(ec), jnp.cos(ec)], axis=-1)
    del n_ctx_tiles
    return row_emb[:, None, :], jnp.repeat(col_emb, SUB, axis=0)


def kernel(x_prompt, x_sample, state_lru, c, c_ctx, norm_mix_g, w_ada, b_ada, w_in, conv_w, conv_b, lru_wa, lru_ba, lru_wx, lru_bx, lru_lam, pool_w, pool_b, pool_scale, w_out, norm_ffn_g, ffn_wg, ffn_wu, ffn_wd, moe_router_w, moe_router_b, moe_wg, moe_wu, moe_wd, norm_final_g):
    bc, tc, d = x_prompt.shape
    bl, tl, _ = x_sample.shape
    depth = w_ada.shape[0]
    c_lru = conv_w.shape[-1]
    assert bl == SUB and bc % SUB == 0 and tc % GRID_W == 0 and tl % GRID_W == 0
    assert c_lru == 4 * LANES and d == 2 * c_lru
    n_groups = bc // SUB
    tiles_per_ctx = tc // GRID_W
    n_ctx_tiles = n_groups * tiles_per_ctx
    n_lat_tiles = tl // GRID_W
    n_tiles = n_ctx_tiles + n_lat_tiles
    n_ctx = n_ctx_tiles * TM
    cfg = (n_ctx_tiles, tiles_per_ctx, n_tiles, d)

    cond = jnp.concatenate([c, c_ctx[None], jnp.zeros((SUB - 1, d), F32)], axis=0)
    mod = _ada_call(cond, w_ada, b_ada)
    mod_lat = mod[:, :SUB].reshape(depth, SUB, 6, d).transpose(0, 2, 1, 3)
    mod_ctx = jnp.broadcast_to(mod[:, SUB].reshape(depth, 6, 1, d), (depth, 6, SUB, d))
    mod = jnp.stack([mod_ctx, mod_lat], axis=1)

    row_tab, col_rep = _pos_tables(n_ctx_tiles, n_lat_tiles, d)
    first = (x_prompt.reshape(n_groups, SUB, tc, d), x_sample, row_tab, col_rep)
    x = None
    wg_lru = jnp.concatenate([_block_diag(lru_wa, 4), _block_diag(lru_wx, 4)], axis=-1).astype(BF16)
    pw = _block_diag(pool_w, 2).astype(BF16)
    h0 = jnp.concatenate([jnp.zeros_like(state_lru[None]), state_lru[None]], axis=0)

    states = []
    for l in range(depth):
        jdx = l // 2
        if l == 0:
            proj, x = _inproj_call(cfg, x, mod[l], norm_mix_g[l], w_in[l].astype(BF16), first)
        else:
            (proj,) = _inproj_call(cfg, x, mod[l], norm_mix_g[l], w_in[l].astype(BF16))
        hf, st_f = _scan_fwd_call(cfg, proj, conv_w[l], conv_b[l], wg_lru[l, 0], lru_ba[l, 0],
                                  lru_bx[l, 0], lru_lam[l, 0], h0[:, :, l, 0])
        ymix, st_b = _scan_bwd_call(cfg, tc, proj, hf, conv_w[l], conv_b[l], wg_lru[l, 1], lru_ba[l, 1],
                                    lru_bx[l, 1], lru_lam[l, 1], h0[:, :, l, 1], pw[l], pool_b[l],
                                    pool_scale[l])
        states.append(jnp.stack([st_f[:n_groups].reshape(bc, c_lru), st_b[:n_groups].reshape(bc, c_lru)], axis=1))
        if l % 2 == 0:
            x = _dense_ffn_call(cfg, x, ymix, mod[l], w_out[l].astype(BF16), norm_ffn_g[l],
                                ffn_wg[jdx].astype(BF16), ffn_wu[jdx].astype(BF16), ffn_wd[jdx].astype(BF16))
        else:
            xm, hp, meta, mt, cnt = _route_call(cfg, x, ymix, mod[l], w_out[l].astype(BF16), norm_ffn_g[l],
                                                moe_router_w[jdx], moe_router_b[jdx])
            pos_flat, tile_e, n_active = _routing_tables(cfg, mt, cnt)
            buf = _sc_scatter_rows(hp, pos_flat, (2 * n_tiles + N_EXPERTS) * TM)
            wgu = jnp.concatenate([moe_wg[jdx].astype(BF16), moe_wu[jdx].astype(BF16)], axis=2)
            ys = _expert_call(cfg, tile_e, n_active, buf, wgu, moe_wd[jdx].astype(BF16))
            yg = _sc_gather_rows(ys, pos_flat)
            x = _combine_call(cfg, xm, yg, meta, mod[l], norm_final_g, final=(l == depth - 1))

    y_prompt, y_sample = x if depth % 2 == 0 else _final_norm_call(cfg, x, norm_final_g)
    new_state = jnp.stack(states, axis=1)
    return (y_prompt.reshape(bc, tc, d), y_sample, new_state)
```

```python
import functools

import jax
import jax.numpy as jnp
from jax import lax
from jax.experimental import pallas as pl
from jax.experimental.pallas import tpu as pltpu
from jax.experimental.pallas import tpu_sc as plsc

F32 = jnp.float32
BF16 = jnp.bfloat16

SUB = 8
LANES = 128
GRID_W = 64
TM = GRID_W * SUB
POS_BASE = 10000.0
N_LRU_HEADS = 8
CONV_W = 4
CONV_LEFT = CONV_W // 2
CONV_RIGHT = CONV_W - 1 - CONV_LEFT
LRU_C = 8.0
POOL_WINDOWS = (2, 4, 8, 16)
POOL_HALO = 8 * SUB
N_EXPERTS = 8
EPS = 1e-6
FF_CHUNK = 256
VMEM_LIMIT = 56 * 1024 * 1024


def _cparams(sem):
    return pltpu.CompilerParams(dimension_semantics=sem, vmem_limit_bytes=VMEM_LIMIT)


def _const_spec(shape):
    nd = len(shape)
    return pl.BlockSpec(shape, lambda *_: (0,) * nd, pipeline_mode=pl.Buffered(1))


def _rms(x, g):
    ms = jnp.mean(x * x, axis=-1, keepdims=True)
    return x * lax.rsqrt(ms + EPS) * g


def _per_seq(x, v, op):
    r, c = x.shape
    x3 = x.reshape(r // SUB, SUB, c)
    return op(x3, v[None]).reshape(r, c)


def _modulate(h, scale, shift):
    r, c = h.shape
    h3 = h.reshape(r // SUB, SUB, c)
    return (h3 * (1.0 + scale)[None] + shift[None]).reshape(r, c)


def _gated_add(x, gate, y):
    return x + _per_seq(y, gate, lambda a, b: a * b)


def _dot(a, b):
    return jnp.dot(a, b, preferred_element_type=F32)


def _pack_pairs(x):
    bits = lax.bitcast_convert_type(x, jnp.uint32)
    half = bits.shape[1] // 2
    w = lax.shift_right_logical(bits[:, :half], jnp.uint32(16)) | (bits[:, half:] & jnp.uint32(0xFFFF0000))
    return lax.bitcast_convert_type(w, jnp.int32)


def _unpack_pairs(w):
    w = lax.bitcast_convert_type(w, jnp.uint32)
    lo = lax.bitcast_convert_type(lax.shift_left(w, jnp.uint32(16)), F32)
    hi = lax.bitcast_convert_type(w & jnp.uint32(0xFFFF0000), F32)
    return jnp.concatenate([lo, hi], axis=1)


def _sigmoid(x):
    return 0.5 * jnp.tanh(0.5 * x) + 0.5


def _ada_kernel(c_ref, w_ref, b_ref, o_ref):
    c = c_ref[...]
    s = (c * jax.nn.sigmoid(c)).astype(BF16)
    o_ref[0] = _dot(s, w_ref[0].astype(BF16)) + b_ref[0]


def _ada_call(cond, w_ada, b_ada):
    depth, d, d6 = w_ada.shape
    nr = cond.shape[0]
    bn = d6 // 4
    return pl.pallas_call(
        _ada_kernel,
        grid=(depth, d6 // bn),
        in_specs=[
            pl.BlockSpec((nr, d), lambda l, n: (0, 0)),
            pl.BlockSpec((1, d, bn), lambda l, n: (l, 0, n)),
            pl.BlockSpec((1, 1, bn), lambda l, n: (l, 0, n)),
        ],
        out_specs=pl.BlockSpec((1, nr, bn), lambda l, n: (l, 0, n)),
        out_shape=jax.ShapeDtypeStruct((depth, nr, d6), F32),
        compiler_params=_cparams(("parallel", "parallel")),
        name="ada_mod",
    )(cond, w_ada, b_ada.reshape(depth, 1, d6))


def _to_time_major(x):
    s, t, d = x.shape
    return jnp.swapaxes(x, 0, 1).reshape(s * t, d)


def _from_time_major(x):
    r, d = x.shape
    return jnp.swapaxes(x.reshape(r // SUB, SUB, d), 0, 1)


def _inproj_kernel(n_ctx_tiles, *refs):
    if n_ctx_tiles is not None:
        xp_ref, xs_ref, row_ref, col_ref, mod_ref, g_ref, w_ref, o_ref, x0_ref = refs
        is_ctx = pl.program_id(0) < n_ctx_tiles

        @pl.when(is_ctx)
        def _():
            x0_ref[...] = _to_time_major(xp_ref[0])

        @pl.when(jnp.logical_not(is_ctx))
        def _():
            x = _to_time_major(xs_ref[...])
            half = x.shape[1] // 2
            x0_ref[...] = jnp.concatenate([x[:, :half] + row_ref[0], x[:, half:] + col_ref[...]], axis=1)

        x = x0_ref[...]
    else:
        x_ref, mod_ref, g_ref, w_ref, o_ref = refs
        x = x_ref[...]
    h = _modulate(_rms(x, g_ref[...]), mod_ref[0, 1], mod_ref[0, 0])
    o_ref[...] = _dot(h.astype(BF16), w_ref[...])


def _natural_specs(cfg):
    n_ctx_tiles, tiles_per_ctx, _, d = cfg
    n_groups = n_ctx_tiles // tiles_per_ctx

    def ctx_idx(j):
        jc = jnp.minimum(j, n_ctx_tiles - 1)
        return (jc // tiles_per_ctx, 0, lax.rem(jc, tiles_per_ctx), 0)

    del n_groups
    return (pl.BlockSpec((1, SUB, GRID_W, d), ctx_idx),
            pl.BlockSpec((SUB, GRID_W, d), lambda j: (0, jnp.maximum(j - n_ctx_tiles, 0), 0)))


def _wide(cfg):
    n_ctx_tiles, _, n_tiles, _ = cfg
    return 2 if n_ctx_tiles % 2 == 0 and n_tiles % 2 == 0 else 1


def _mod_spec(cfg, f=1):
    n_ctx_steps = cfg[0] // f
    d = cfg[3]
    return pl.BlockSpec((1, 6, SUB, d), lambda j: (jnp.where(j >= n_ctx_steps, 1, 0), 0, 0, 0))


def _inproj_call(cfg, x, mod_l, g, w_in, first=None):
    n_ctx_tiles, _, n_tiles, d = cfg
    d_in = w_in.shape[1]
    f = 1 if first is not None else _wide(cfg)
    tm = TM * f
    row_spec = pl.BlockSpec((tm, d), lambda j: (j, 0))
    out_specs = [pl.BlockSpec((tm, d_in), lambda j: (j, 0))]
    out_shape = [jax.ShapeDtypeStruct((n_tiles * TM, d_in), F32)]
    if first is not None:
        xp_spec, xs_spec = _natural_specs(cfg)
        in_specs = [xp_spec, xs_spec,
                    pl.BlockSpec((1, 1, d // 2), lambda j: (jnp.maximum(j - n_ctx_tiles, 0), 0, 0)),
                    _const_spec((TM, d // 2))]
        args = list(first)
        out_specs.append(row_spec)
        out_shape.append(jax.ShapeDtypeStruct((n_tiles * TM, d), F32))
    else:
        in_specs = [row_spec]
        args = [x]
    in_specs += [_mod_spec(cfg, f), _const_spec((1, d)), _const_spec((d, d_in))]
    args += [mod_l, g.reshape(1, d), w_in]
    return pl.pallas_call(
        functools.partial(_inproj_kernel, n_ctx_tiles if first is not None else None),
        grid=(n_tiles // f,),
        in_specs=in_specs,
        out_specs=out_specs,
        out_shape=out_shape,
        compiler_params=_cparams(("parallel",)),
        name="in_proj",
    )(*args)


def _seq_flags(cfg, j):
    n_ctx_tiles, tiles_per_ctx, n_tiles, _ = cfg
    is_ctx = j < n_ctx_tiles
    pos = lax.rem(j, tiles_per_ctx)
    first = jnp.where(is_ctx, pos == 0, j == n_ctx_tiles)
    last = jnp.where(is_ctx, pos == tiles_per_ctx - 1, j == n_tiles - 1)
    return is_ctx, pos, first, last


def _conv(pad_ref, xa, prev, nxt, first, last, cw, cb):
    lo = CONV_LEFT * SUB
    pad_ref[0:lo, :] = jnp.where(first, 0.0, prev)
    pad_ref[lo:lo + TM, :] = xa
    pad_ref[lo + TM:lo + TM + CONV_RIGHT * SUB, :] = jnp.where(last, 0.0, nxt)
    y = cb
    for k in range(CONV_W):
        y = y + pad_ref[k * SUB:k * SUB + TM, :] * cw[k:k + 1, :]
    return y


def _gates(xc, wg_ref, ba, bx, lam):
    half = xc.shape[1] // 2
    xb = xc.astype(BF16)
    g0 = _dot(xb[:, :half], wg_ref[0])
    g1 = _dot(xb[:, half:], wg_ref[1])
    r = _sigmoid(jnp.concatenate([g0[:, :half], g1[:, :half]], axis=1) + ba)
    i = _sigmoid(jnp.concatenate([g0[:, half:], g1[:, half:]], axis=1) + bx)
    z = -lam
    decay = LRU_C * (jnp.maximum(z, 0.0) + jnp.log1p(jnp.exp(-jnp.abs(z))))
    neg_log_a = r * decay
    a = jnp.exp(-neg_log_a)
    z = jnp.tanh(neg_log_a) * (a * a + 1.0)
    root = jnp.where(z > 0.0, z * lax.rsqrt(z), 0.0)
    u = root * (i * xc)
    return a, u


def _scan(a_ref, u_ref, h_ref, h, reverse):
    steps = TM // SUB

    def body(k, h):
        t = steps - 1 - k if reverse else k
        r0 = pl.multiple_of(t * SUB, SUB)
        h = a_ref[pl.ds(r0, SUB), :] * h + u_ref[pl.ds(r0, SUB), :]
        h_ref[pl.ds(r0, SUB), :] = h
        return h

    return lax.fori_loop(0, steps, body, h, unroll=8)


def _scan_fwd_kernel(cfg, xa_ref, xp_ref, xn_ref, cw_ref, cb_ref, wg_ref, ba_ref, bx_ref,
                     lam_ref, h0_ref, hf_ref, st_ref, pad_ref, a_ref, u_ref, carry_ref):
    j = pl.program_id(0)
    _, _, first, last = _seq_flags(cfg, j)
    xc = _conv(pad_ref, xa_ref[...], xp_ref[...], xn_ref[...], first, last, cw_ref[...], cb_ref[...])
    a, u = _gates(xc, wg_ref, ba_ref[...], bx_ref[...], lam_ref[...])
    a_ref[...] = a
    u_ref[...] = u

    @pl.when(first)
    def _():
        carry_ref[...] = h0_ref[0]

    h = _scan(a_ref, u_ref, hf_ref, carry_ref[...], reverse=False)
    carry_ref[...] = h
    st_ref[0] = h


def _halo_specs(cfg, col, rows_prev, rows_next, tile_of):
    n_tiles = cfg[2]
    c = 512
    nb_prev = TM // rows_prev
    nb_next = TM // rows_next
    last_next = n_tiles * nb_next - 1
    prev = pl.BlockSpec((rows_prev, c), lambda j: (jnp.maximum(tile_of(j) * nb_prev - 1, 0), col))
    nxt = pl.BlockSpec((rows_next, c), lambda j: (jnp.minimum((tile_of(j) + 1) * nb_next, last_next), col))
    return prev, nxt


def _state_spec(cfg, tile_of):
    n_ctx_tiles, tiles_per_ctx, _, _ = cfg
    n_groups = n_ctx_tiles // tiles_per_ctx
    c = 512
    return pl.BlockSpec((1, SUB, c), lambda j: (jnp.minimum(tile_of(j) // tiles_per_ctx, n_groups), 0, 0))


def _h0_spec(cfg, tile_of):
    n_ctx_tiles = cfg[0]
    return pl.BlockSpec((1, SUB, 512), lambda j: (jnp.where(tile_of(j) >= n_ctx_tiles, 1, 0), 0, 0))


def _scan_fwd_call(cfg, proj, conv_w, conv_b, wg, ba, bx, lam, h0):
    n_ctx_tiles, tiles_per_ctx, n_tiles, _ = cfg
    n_groups = n_ctx_tiles // tiles_per_ctx
    c = conv_w.shape[1]
    ident = lambda j: j
    xp_spec, xn_spec = _halo_specs(cfg, 0, CONV_LEFT * SUB, CONV_RIGHT * SUB, ident)
    return pl.pallas_call(
        functools.partial(_scan_fwd_kernel, cfg),
        grid=(n_tiles,),
        in_specs=[
            pl.BlockSpec((TM, c), lambda j: (j, 0)), xp_spec, xn_spec,
            _const_spec((CONV_W, c)), _const_spec((1, c)), _const_spec((2, c // 2, c)),
            _const_spec((1, c)), _const_spec((1, c)), _const_spec((1, c)),
            _h0_spec(cfg, ident),
        ],
        out_specs=[pl.BlockSpec((TM, c), lambda j: (j, 0)), _state_spec(cfg, ident)],
        out_shape=[jax.ShapeDtypeStruct((n_tiles * TM, c), F32),
                   jax.ShapeDtypeStruct((n_groups + 1, SUB, c), F32)],
        scratch_shapes=[
            pltpu.VMEM((TM + (CONV_W - 1) * SUB, c), F32),
            pltpu.VMEM((TM, c), F32), pltpu.VMEM((TM, c), F32), pltpu.VMEM((SUB, c), F32),
        ],
        compiler_params=_cparams(("arbitrary",)),
        name="lru_fwd",
    )(proj, proj, proj, conv_w, conv_b.reshape(1, c), wg, ba.reshape(1, c), bx.reshape(1, c),
      lam.reshape(1, c), h0)


def _pool_inv_counts(ctx_len, c):
    gw = c // len(POOL_WINDOWS)

    def table(t0, t_len):
        t = t0 + jnp.arange(GRID_W)
        cols = []
        for k in POOL_WINDOWS:
            left = k // 2
            right = k - 1 - left
            cnt = jnp.minimum(t + right + 1, t_len) - jnp.maximum(t - left, 0)
            cols.append(jnp.broadcast_to((1.0 / cnt.astype(F32))[:, None], (GRID_W, gw)))
        return jnp.repeat(jnp.concatenate(cols, axis=1), SUB, axis=0)

    return jnp.stack([table(p * GRID_W, ctx_len) for p in range(ctx_len // GRID_W)] + [table(0, GRID_W)])


def _gelu_tanh(x):
    k0 = 0.7978845608028654
    hx = 0.5 * x
    return hx + hx * jnp.tanh(x * (k0 + (k0 * 0.044715) * (x * x)))


def _pool_mix(pad_ref, inv_ref, pw_ref, pb, ps):
    c = pad_ref.shape[1]
    gw = c // len(POOL_WINDOWS)
    outs = []
    for g, k in enumerate(POOL_WINDOWS):
        left = k // 2
        right = k - 1 - left
        lanes = slice(g * gw, (g + 1) * gw)
        s = None
        for o in range(-left, right + 1):
            v = pad_ref[POOL_HALO + o * SUB:POOL_HALO + o * SUB + TM, lanes]
            s = v if s is None else s + v
        outs.append(s * inv_ref[0, :, lanes] - pad_ref[POOL_HALO:POOL_HALO + TM, lanes])
    d = jnp.concatenate(outs, axis=1).astype(BF16)
    half = c // 2
    y = jnp.concatenate([_dot(d[:, :half], pw_ref[0]), _dot(d[:, half:], pw_ref[1])], axis=1)
    return (y + pb) * ps


def _scan_bwd_kernel(cfg, proj_ref, xp_ref, xn_ref, bp_ref, bn_ref, hf_ref, inv_ref, cw_ref, cb_ref,
                     wg_ref, ba_ref, bx_ref, lam_ref, h0_ref, pw_ref, pb_ref, ps_ref,
                     y_ref, st_ref, pad_ref, ppad_ref, a_ref, u_ref, hb_ref, carry_ref):
    n_tiles = cfg[2]
    j = n_tiles - 1 - pl.program_id(0)
    is_ctx, _, first, last = _seq_flags(cfg, j)
    c = hf_ref.shape[1]
    xc = _conv(pad_ref, proj_ref[:, 0:c], xp_ref[...], xn_ref[...], first, last, cw_ref[...], cb_ref[...])
    a, u = _gates(xc, wg_ref, ba_ref[...], bx_ref[...], lam_ref[...])
    a_ref[...] = a
    u_ref[...] = u

    @pl.when(last)
    def _():
        carry_ref[...] = h0_ref[0]

    h = _scan(a_ref, u_ref, hb_ref, carry_ref[...], reverse=True)
    carry_ref[...] = h
    st_ref[0] = h

    ga = proj_ref[:, c:2 * c]
    y_a = (hf_ref[...] + hb_ref[...]) * _gelu_tanh(ga)

    use_prev = jnp.logical_and(is_ctx, jnp.logical_not(first))
    use_next = jnp.logical_and(is_ctx, jnp.logical_not(last))
    ppad_ref[0:POOL_HALO, :] = jnp.where(use_prev, bp_ref[...], 0.0)
    ppad_ref[POOL_HALO:POOL_HALO + TM, :] = proj_ref[:, 2 * c:3 * c]
    ppad_ref[POOL_HALO + TM:, :] = jnp.where(use_next, bn_ref[...], 0.0)
    y_b = _pool_mix(ppad_ref, inv_ref, pw_ref, pb_ref[...], ps_ref[...])
    y_ref[...] = jnp.concatenate([y_a, y_b], axis=1).astype(BF16)


def _scan_bwd_call(cfg, inv_cnt, proj, hf, conv_w, conv_b, wg, ba, bx, lam, h0, pw, pb, ps):
    n_ctx_tiles, tiles_per_ctx, n_tiles, d = cfg
    n_groups = n_ctx_tiles // tiles_per_ctx
    c = conv_w.shape[1]
    rev = lambda j: n_tiles - 1 - j
    xp_spec, xn_spec = _halo_specs(cfg, 0, CONV_LEFT * SUB, CONV_RIGHT * SUB, rev)
    bp_spec, bn_spec = _halo_specs(cfg, 2, POOL_HALO, POOL_HALO, rev)

    def inv_idx(j):
        jj = rev(j)
        return (jnp.where(jj < n_ctx_tiles, lax.rem(jj, tiles_per_ctx), tiles_per_ctx), 0, 0)

    return pl.pallas_call(
        functools.partial(_scan_bwd_kernel, cfg),
        grid=(n_tiles,),
        in_specs=[
            pl.BlockSpec((TM, 3 * c), lambda j: (rev(j), 0)), xp_spec, xn_spec, bp_spec, bn_spec,
            pl.BlockSpec((TM, c), lambda j: (rev(j), 0)),
            pl.BlockSpec((1, TM, c), inv_idx),
            _const_spec((CONV_W, c)), _const_spec((1, c)), _const_spec((2, c // 2, c)),
            _const_spec((1, c)), _const_spec((1, c)), _const_spec((1, c)),
            _h0_spec(cfg, rev),
            _const_spec((2, c // 2, c // 2)), _const_spec((1, c)), _const_spec((1, c)),
        ],
        out_specs=[pl.BlockSpec((TM, d), lambda j: (rev(j), 0)), _state_spec(cfg, rev)],
        out_shape=[jax.ShapeDtypeStruct((n_tiles * TM, d), BF16),
                   jax.ShapeDtypeStruct((n_groups + 1, SUB, c), F32)],
        scratch_shapes=[
            pltpu.VMEM((TM + (CONV_W - 1) * SUB, c), F32),
            pltpu.VMEM((TM + 2 * POOL_HALO, c), F32),
            pltpu.VMEM((TM, c), F32), pltpu.VMEM((TM, c), F32), pltpu.VMEM((TM, c), F32),
            pltpu.VMEM((SUB, c), F32),
        ],
        compiler_params=_cparams(("arbitrary",)),
        name="lru_bwd_mix",
    )(proj, proj, proj, proj, proj, hf, inv_cnt, conv_w, conv_b.reshape(1, c), wg, ba.reshape(1, c),
      bx.reshape(1, c), lam.reshape(1, c), h0, pw, pb.reshape(1, c), ps.reshape(1, c))


def _out_proj(x_ref, ym_ref, mod_ref, wo_ref, g2_ref):
    xm = _gated_add(x_ref[...], mod_ref[0, 2], _dot(ym_ref[...], wo_ref[...]))
    h2 = _modulate(_rms(xm, g2_ref[...]), mod_ref[0, 4], mod_ref[0, 3])
    return xm, h2


def _swiglu_act(h2b, wg_ref, wu_ref, act_ref):
    n_chunks = act_ref.shape[1] // FF_CHUNK

    def body(f, carry):
        c0 = pl.multiple_of(f * FF_CHUNK, FF_CHUNK)
        g = _dot(h2b, wg_ref[:, pl.ds(c0, FF_CHUNK)])
        u = _dot(h2b, wu_ref[:, pl.ds(c0, FF_CHUNK)])
        act_ref[:, pl.ds(c0, FF_CHUNK)] = (g * _sigmoid(g) * u).astype(BF16)
        return carry

    lax.fori_loop(0, n_chunks, body, 0, unroll=True)


def _dense_ffn_kernel(x_ref, ym_ref, mod_ref, wo_ref, g2_ref, wg_ref, wu_ref, wd_ref, o_ref, act_ref):
    xm, h2 = _out_proj(x_ref, ym_ref, mod_ref, wo_ref, g2_ref)
    _swiglu_act(h2.astype(BF16), wg_ref, wu_ref, act_ref)
    o_ref[...] = _gated_add(xm, mod_ref[0, 5], _dot(act_ref[...], wd_ref[...]))


def _dense_ffn_call(cfg, x, ymix, mod_l, w_out, g2, wg, wu, wd):
    _, _, n_tiles, d = cfg
    d_ff = wg.shape[1]
    f = _wide(cfg)
    row_spec = pl.BlockSpec((TM * f, d), lambda j: (j, 0))
    return pl.pallas_call(
        _dense_ffn_kernel,
        grid=(n_tiles // f,),
        in_specs=[row_spec, row_spec, _mod_spec(cfg, f), _const_spec((d, d)), _const_spec((1, d)),
                  _const_spec((d, d_ff)), _const_spec((d, d_ff)), _const_spec((d_ff, d))],
        out_specs=row_spec,
        out_shape=jax.ShapeDtypeStruct((n_tiles * TM, d), F32),
        scratch_shapes=[pltpu.VMEM((TM * f, d_ff), BF16)],
        compiler_params=_cparams(("parallel",)),
        name="out_proj_dense_ffn",
    )(x, ymix, mod_l, w_out, g2.reshape(1, d), wg, wu, wd)


M_E1, M_E2, M_W1, M_W2, M_R1, M_R2 = range(6)


def _route_kernel(x_ref, ym_ref, mod_ref, wo_ref, g2_ref, rw_ref, rb_ref, tri_ref,
                  xm_ref, hp_ref, meta_ref, mt_ref, cnt_ref, run_ref):
    xm, h2 = _out_proj(x_ref, ym_ref, mod_ref, wo_ref, g2_ref)
    xm_ref[...] = xm
    h_hi = h2.astype(BF16)
    h_hi32 = h_hi.astype(F32)
    hp_ref[...] = _pack_pairs(h_hi32)

    h_lo = (h2 - h_hi32).astype(BF16)
    p = _dot(h_hi, rw_ref[...])
    logits = p[:, :LANES] + p[:, LANES:] + _dot(h_lo, rw_ref[:, :LANES])
    lane = lax.broadcasted_iota(jnp.int32, logits.shape, 1)
    neg = jnp.float32(-jnp.inf)
    lg = jnp.where(lane < N_EXPERTS, logits + rb_ref[...], neg)
    m1 = jnp.max(lg, axis=1, keepdims=True)
    i1 = jnp.min(jnp.where(lg == m1, lane, LANES), axis=1, keepdims=True)
    lg2 = jnp.where(lane == i1, neg, lg)
    m2 = jnp.max(lg2, axis=1, keepdims=True)
    i2 = jnp.min(jnp.where(lg2 == m2, lane, LANES), axis=1, keepdims=True)
    e2 = jnp.exp(m2 - m1)
    den = 1.0 + e2

    @pl.when(pl.program_id(0) == 0)
    def _():
        run_ref[...] = jnp.zeros_like(run_ref)

    sel1 = lane == i1
    sel2 = lane == i2
    onehot = jnp.where(jnp.logical_or(sel1, sel2), 1.0, 0.0)
    rank = _dot(tri_ref[...], onehot.astype(BF16)) + run_ref[...]
    r1 = jnp.sum(jnp.where(sel1, rank, 0.0), axis=1, keepdims=True)
    r2 = jnp.sum(jnp.where(sel2, rank, 0.0), axis=1, keepdims=True)
    run_ref[...] = run_ref[...] + jnp.sum(onehot, axis=0, keepdims=True)
    cnt_ref[...] = run_ref[...]

    meta = jnp.zeros(logits.shape, F32)
    for k, v in ((M_E1, i1.astype(F32)), (M_E2, i2.astype(F32)), (M_W1, 1.0 / den), (M_W2, e2 / den),
                 (M_R1, r1), (M_R2, r2)):
        meta = jnp.where(lane == k, v, meta)
    meta_ref[...] = meta
    mt_ref[0] = jnp.transpose(meta)[:SUB]


def _route_call(cfg, x, ymix, mod_l, w_out, g2, router_w, router_b):
    _, _, n_tiles, d = cfg
    n = n_tiles * TM
    f = _wide(cfg)
    tm = TM * f
    row_spec = pl.BlockSpec((tm, d), lambda j: (j, 0))
    rw = jnp.zeros((d, LANES), F32).at[:, :N_EXPERTS].set(router_w)
    rw_hi = rw.astype(BF16)
    rw = jnp.concatenate([rw_hi, (rw - rw_hi.astype(F32)).astype(BF16)], axis=1)
    rb = jnp.zeros((1, LANES), F32).at[0, :N_EXPERTS].set(router_b)
    tri = jnp.tril(jnp.ones((tm, tm), BF16), -1)
    return pl.pallas_call(
        _route_kernel,
        grid=(n_tiles // f,),
        in_specs=[row_spec, row_spec, _mod_spec(cfg, f), _const_spec((d, d)), _const_spec((1, d)),
                  _const_spec((d, 2 * LANES)), _const_spec((1, LANES)), _const_spec((tm, tm))],
        out_specs=[row_spec, pl.BlockSpec((tm, d // 2), lambda j: (j, 0)),
                   pl.BlockSpec((tm, LANES), lambda j: (j, 0)), pl.BlockSpec((1, SUB, tm), lambda j: (j, 0, 0)),
                   pl.BlockSpec((1, LANES), lambda j: (0, 0))],
        out_shape=[jax.ShapeDtypeStruct((n, d), F32), jax.ShapeDtypeStruct((n, d // 2), jnp.int32),
                   jax.ShapeDtypeStruct((n, LANES), F32), jax.ShapeDtypeStruct((n // tm, SUB, tm), F32),
                   jax.ShapeDtypeStruct((1, LANES), F32)],
        scratch_shapes=[pltpu.VMEM((1, LANES), F32)],
        compiler_params=_cparams(("arbitrary",)),
        name="out_proj_route",
    )(x, ymix, mod_l, w_out, g2.reshape(1, d), rw, rb, tri)


def _routing_tables(cfg, mt, cnt):
    n_tiles = cfg[2]
    counts = cnt[0, :N_EXPERTS].astype(jnp.int32)
    padded = ((counts + TM - 1) // TM) * TM
    ends = jnp.cumsum(padded)
    offs = ends - padded
    row = lambda k: mt[:, k, :].reshape(-1).astype(jnp.int32)
    pos1 = offs[row(M_E1)] + row(M_R1)
    pos2 = offs[row(M_E2)] + row(M_R2)
    n_sorted_tiles = 2 * n_tiles + N_EXPERTS
    starts = jnp.arange(n_sorted_tiles, dtype=jnp.int32) * TM
    tile_e = jnp.minimum(jnp.sum((starts[:, None] >= ends[None, :]).astype(jnp.int32), axis=1), N_EXPERTS - 1)
    n_active = (ends[-1] // TM).reshape(1)
    return jnp.concatenate([pos1, pos2]), tile_e, n_active


SC_CORES = 2
SC_SUBCORES = 16
SC_CHUNK = 64


def _sc_mesh():
    return plsc.VectorSubcoreMesh(core_axis_name="c", subcore_axis_name="s",
                                  num_cores=SC_CORES, num_subcores=SC_SUBCORES)


def _sc_scatter_rows(rows, idx, n_out):
    n_src, width = rows.shape
    n_idx = idx.shape[0]
    n_workers = SC_CORES * SC_SUBCORES
    per_w = n_idx // n_workers
    chunks = per_w // SC_CHUNK
    assert n_idx % (n_workers * SC_CHUNK) == 0 and n_src % per_w == 0
    idx3 = idx.reshape(n_workers, chunks, SC_CHUNK)

    def body(rows_hbm, idx_hbm, out_hbm, idx_v, rows_v, sem):
        wid = lax.axis_index("s") * SC_CORES + lax.axis_index("c")
        src_base = lax.rem(wid * per_w, n_src)
        pltpu.sync_copy(idx_hbm.at[wid], idx_v)

        @pl.loop(0, chunks)
        def _(i):
            off = pl.multiple_of(i * SC_CHUNK, SC_CHUNK)
            pltpu.sync_copy(rows_hbm.at[pl.ds(src_base + off, SC_CHUNK)], rows_v)
            pltpu.async_copy(rows_v, out_hbm.at[idx_v.at[i]], sem).wait()

    return pl.kernel(
        body,
        out_type=jax.ShapeDtypeStruct((n_out, width), rows.dtype),
        mesh=_sc_mesh(),
        scratch_types=[pltpu.VMEM((chunks, SC_CHUNK), jnp.int32), pltpu.VMEM((SC_CHUNK, width), rows.dtype),
                       pltpu.SemaphoreType.DMA],
        name="sc_scatter_rows",
    )(rows, idx3)


def _expert_kernel(te_ref, na_ref, s_ref, wgu_ref, wd_ref, o_ref):
    del te_ref

    @pl.when(pl.program_id(0) < na_ref[0])
    def _():
        h = _unpack_pairs(s_ref[...]).astype(BF16)
        gu = _dot(h, wgu_ref[0, 0])
        dfe = gu.shape[1] // 2
        g = gu[:, :dfe]
        act = (g * _sigmoid(g) * gu[:, dfe:]).astype(BF16)
        o_ref[...] = _pack_pairs(_dot(act, wd_ref[0, 0]).astype(BF16).astype(F32))

    @pl.when(pl.program_id(0) >= na_ref[0])
    def _():
        o_ref[...] = jnp.zeros_like(o_ref)


def _expert_call(cfg, tile_e, n_active, buf, wgu, wd, layer):
    _, _, n_tiles, d = cfg
    dfe = wd.shape[2]
    n_sorted_tiles = 2 * n_tiles + N_EXPERTS
    last = lambda t, na: jnp.maximum(jnp.minimum(t, na[0] - 1), 0)
    tile = lambda t, te, na: (last(t, na), 0)
    w_spec = lambda shape: pl.BlockSpec((1,) + shape, lambda t, te, na: (layer, te[last(t, na)], 0, 0))
    return pl.pallas_call(
        _expert_kernel,
        grid_spec=pltpu.PrefetchScalarGridSpec(
            num_scalar_prefetch=2,
            grid=(n_sorted_tiles,),
            in_specs=[pl.BlockSpec((TM, d // 2), tile), w_spec((1, d, 2 * dfe)), w_spec((1, dfe, d))],
            out_specs=pl.BlockSpec((TM, d // 2), lambda t, te, na: (t, 0)),
        ),
        out_shape=jax.ShapeDtypeStruct((n_sorted_tiles * TM, d // 2), jnp.int32),
        compiler_params=_cparams(("arbitrary",)),
        name="moe_experts",
    )(tile_e, n_active, buf, wgu, wd)


def _store_natural(n_ctx_tiles, y, yp_ref, ys_ref):
    is_ctx = pl.program_id(0) < n_ctx_tiles

    @pl.when(is_ctx)
    def _():
        yp_ref[0] = _from_time_major(y)

    @pl.when(jnp.logical_not(is_ctx))
    def _():
        ys_ref[...] = _from_time_major(y)


def _natural_out(cfg):
    n_ctx_tiles, tiles_per_ctx, n_tiles, d = cfg
    n_groups = n_ctx_tiles // tiles_per_ctx
    return (list(_natural_specs(cfg)),
            [jax.ShapeDtypeStruct((n_groups, SUB, tiles_per_ctx * GRID_W, d), F32),
             jax.ShapeDtypeStruct((SUB, (n_tiles - n_ctx_tiles) * GRID_W, d), F32)])


def _sc_gather_rows(table, idx):
    n_rows = idx.shape[0]
    width = table.shape[1]
    n_workers = SC_CORES * SC_SUBCORES
    assert n_rows % (n_workers * SC_CHUNK) == 0
    per_w = n_rows // n_workers
    mesh = _sc_mesh()

    def body(table_hbm, idx_hbm, out_hbm, idx_v, rows_v, sem):
        wid = lax.axis_index("s") * SC_CORES + lax.axis_index("c")
        base = wid * per_w
        pltpu.sync_copy(idx_hbm.at[pl.ds(base, per_w)], idx_v)

        @pl.loop(0, per_w // SC_CHUNK)
        def _(i):
            off = pl.multiple_of(i * SC_CHUNK, SC_CHUNK)
            pltpu.async_copy(table_hbm.at[idx_v.at[pl.ds(off, SC_CHUNK)]], rows_v, sem).wait()
            pltpu.sync_copy(rows_v, out_hbm.at[pl.ds(base + off, SC_CHUNK)])

    return pl.kernel(
        body,
        out_type=jax.ShapeDtypeStruct((n_rows, width), table.dtype),
        mesh=mesh,
        scratch_types=[pltpu.VMEM((per_w,), jnp.int32), pltpu.VMEM((SC_CHUNK, width), table.dtype),
                       pltpu.SemaphoreType.DMA],
        name="sc_gather_rows",
    )(table, idx)


def _combine_kernel(final_ctx_tiles, xm_ref, y1_ref, y2_ref, meta_ref, mod_ref, gfin_ref, *o_refs):
    meta = meta_ref[...]
    y = (meta[:, M_W1:M_W1 + 1] * _unpack_pairs(y1_ref[...])
         + meta[:, M_W2:M_W2 + 1] * _unpack_pairs(y2_ref[...]))
    x = _gated_add(xm_ref[...], mod_ref[0, 5], y)
    if final_ctx_tiles is None:
        o_refs[0][...] = x
    else:
        _store_natural(final_ctx_tiles, _rms(x, gfin_ref[...]), *o_refs)


def _combine_call(cfg, xm, yg, meta, mod_l, g_final, final):
    n_ctx_tiles, _, n_tiles, d = cfg
    row_spec = pl.BlockSpec((TM, d), lambda j: (j, 0))
    if final:
        out_specs, out_shape = _natural_out(cfg)
    else:
        out_specs, out_shape = [row_spec], [jax.ShapeDtypeStruct((n_tiles * TM, d), F32)]
    outs = pl.pallas_call(
        functools.partial(_combine_kernel, n_ctx_tiles if final else None),
        grid=(n_tiles,),
        in_specs=[row_spec, pl.BlockSpec((TM, d // 2), lambda j: (j, 0)),
                  pl.BlockSpec((TM, d // 2), lambda j: (n_tiles + j, 0)),
                  pl.BlockSpec((TM, LANES), lambda j: (j, 0)), _mod_spec(cfg), _const_spec((1, d))],
        out_specs=out_specs,
        out_shape=out_shape,
        compiler_params=_cparams(("arbitrary",)),
        name="moe_combine",
    )(xm, yg, yg, meta, mod_l, g_final.reshape(1, d))
    return outs if final else outs[0]


def _final_norm_kernel(n_ctx_tiles, x_ref, g_ref, yp_ref, ys_ref):
    _store_natural(n_ctx_tiles, _rms(x_ref[...], g_ref[...]), yp_ref, ys_ref)


def _final_norm_call(cfg, x, g):
    n_ctx_tiles, _, n_tiles, d = cfg
    out_specs, out_shape = _natural_out(cfg)
    return pl.pallas_call(
        functools.partial(_final_norm_kernel, n_ctx_tiles),
        grid=(n_tiles,),
        in_specs=[pl.BlockSpec((TM, d), lambda j: (j, 0)), _const_spec((1, d))],
        out_specs=out_specs,
        out_shape=out_shape,
        compiler_params=_cparams(("arbitrary",)),
        name="final_norm",
    )(x, g.reshape(1, d))


def _block_diag(w, per_block):
    *lead, n, k, _ = w.shape
    nb = n // per_block
    w = w.reshape(*lead, nb, per_block, k, k)
    eye = jnp.eye(per_block, dtype=w.dtype)
    out = w[..., :, :, None, :] * eye[:, None, :, None]
    return out.reshape(*lead, nb, per_block * k, per_block * k)


def _pos_tables(n_ctx_tiles, n_lat_tiles, d):
    quarter = d // 4
    omega = 1.0 / (POS_BASE ** (jnp.arange(quarter, dtype=F32) / quarter))
    er = jnp.arange(n_lat_tiles, dtype=F32)[:, None] * omega
    ec = jnp.arange(GRID_W, dtype=F32)[:, None] * omega
    row_emb = jnp.concatenate([jnp.sin(er), jnp.cos(er)], axis=-1)
    col_emb = jnp.concatenate([jnp.sin(ec), jnp.cos(ec)], axis=-1)
    del n_ctx_tiles
    return row_emb[:, None, :], jnp.repeat(col_emb, SUB, axis=0)


def kernel(x_prompt, x_sample, state_lru, c, c_ctx, norm_mix_g, w_ada, b_ada, w_in, conv_w, conv_b, lru_wa, lru_ba, lru_wx, lru_bx, lru_lam, pool_w, pool_b, pool_scale, w_out, norm_ffn_g, ffn_wg, ffn_wu, ffn_wd, moe_router_w, moe_router_b, moe_wg, moe_wu, moe_wd, norm_final_g):
    bc, tc, d = x_prompt.shape
    bl, tl, _ = x_sample.shape
    depth = w_ada.shape[0]
    c_lru = conv_w.shape[-1]
    assert bl == SUB and bc % SUB == 0 and tc % GRID_W == 0 and tl % GRID_W == 0
    assert c_lru == 4 * LANES and d == 2 * c_lru
    n_groups = bc // SUB
    tiles_per_ctx = tc // GRID_W
    n_ctx_tiles = n_groups * tiles_per_ctx
    n_lat_tiles = tl // GRID_W
    n_tiles = n_ctx_tiles + n_lat_tiles
    n_ctx = n_ctx_tiles * TM
    cfg = (n_ctx_tiles, tiles_per_ctx, n_tiles, d)

    cond = jnp.concatenate([c, c_ctx[None], jnp.zeros((SUB - 1, d), F32)], axis=0)
    mod = _ada_call(cond, w_ada, b_ada)
    mod_lat = mod[:, :SUB].reshape(depth, SUB, 6, d).transpose(0, 2, 1, 3)
    mod_ctx = jnp.broadcast_to(mod[:, SUB].reshape(depth, 6, 1, d), (depth, 6, SUB, d))
    mod = jnp.stack([mod_ctx, mod_lat], axis=1)

    row_tab, col_rep = _pos_tables(n_ctx_tiles, n_lat_tiles, d)
    first = (x_prompt.reshape(n_groups, SUB, tc, d), x_sample, row_tab, col_rep)
    x = None
    wg_lru = jnp.concatenate([_block_diag(lru_wa, 4), _block_diag(lru_wx, 4)], axis=-1).astype(BF16)
    pw = _block_diag(pool_w, 2).astype(BF16)
    inv_cnt = _pool_inv_counts(tc, c_lru)
    moe_wgu = jnp.concatenate([moe_wg.astype(BF16), moe_wu.astype(BF16)], axis=3)
    moe_wd_b = moe_wd.astype(BF16)
    h0 = jnp.concatenate([jnp.zeros_like(state_lru[None]), state_lru[None]], axis=0)

    states = []
    for l in range(depth):
        jdx = l // 2
        if l == 0:
            proj, x = _inproj_call(cfg, x, mod[l], norm_mix_g[l], w_in[l].astype(BF16), first)
        else:
            (proj,) = _inproj_call(cfg, x, mod[l], norm_mix_g[l], w_in[l].astype(BF16))
        hf, st_f = _scan_fwd_call(cfg, proj, conv_w[l], conv_b[l], wg_lru[l, 0], lru_ba[l, 0],
                                  lru_bx[l, 0], lru_lam[l, 0], h0[:, :, l, 0])
        ymix, st_b = _scan_bwd_call(cfg, inv_cnt, proj, hf, conv_w[l], conv_b[l], wg_lru[l, 1], lru_ba[l, 1],
                                    lru_bx[l, 1], lru_lam[l, 1], h0[:, :, l, 1], pw[l], pool_b[l],
                                    pool_scale[l])
        states.append(jnp.stack([st_f[:n_groups].reshape(bc, c_lru), st_b[:n_groups].reshape(bc, c_lru)], axis=1))
        if l % 2 == 0:
            x = _dense_ffn_call(cfg, x, ymix, mod[l], w_out[l].astype(BF16), norm_ffn_g[l],
                                ffn_wg[jdx].astype(BF16), ffn_wu[jdx].astype(BF16), ffn_wd[jdx].astype(BF16))
        else:
            xm, hp, meta, mt, cnt = _route_call(cfg, x, ymix, mod[l], w_out[l].astype(BF16), norm_ffn_g[l],
                                                moe_router_w[jdx], moe_router_b[jdx])
            pos_flat, tile_e, n_active = _routing_tables(cfg, mt, cnt)
            buf = _sc_scatter_rows(hp, pos_flat, (2 * n_tiles + N_EXPERTS) * TM)
            ys = _expert_call(cfg, tile_e, n_active, buf, moe_wgu, moe_wd_b, jdx)
            yg = _sc_gather_rows(ys, pos_flat)
            x = _combine_call(cfg, xm, yg, meta, mod[l], norm_final_g, final=(l == depth - 1))

    y_prompt, y_sample = x if depth % 2 == 0 else _final_norm_call(cfg, x, norm_final_g)
    new_state = jnp.stack(states, axis=1)
    return (y_prompt.reshape(bc, tc, d), y_sample, new_state)
```

```python
import functools

import jax
import jax.numpy as jnp
from jax import lax
from jax.experimental import pallas as pl
from jax.experimental.pallas import tpu as pltpu
from jax.experimental.pallas import tpu_sc as plsc

F32 = jnp.float32
BF16 = jnp.bfloat16

SUB = 8
LANES = 128
GRID_W = 64
TM = GRID_W * SUB
POS_BASE = 10000.0
N_LRU_HEADS = 8
CONV_W = 4
CONV_LEFT = CONV_W // 2
CONV_RIGHT = CONV_W - 1 - CONV_LEFT
LRU_C = 8.0
POOL_WINDOWS = (2, 4, 8, 16)
POOL_HALO = 8 * SUB
N_EXPERTS = 8
EPS = 1e-6
FF_CHUNK = 256
VMEM_LIMIT = 56 * 1024 * 1024


def _cparams(sem):
    return pltpu.CompilerParams(dimension_semantics=sem, vmem_limit_bytes=VMEM_LIMIT)


def _const_spec(shape):
    nd = len(shape)
    return pl.BlockSpec(shape, lambda *_: (0,) * nd, pipeline_mode=pl.Buffered(1))


def _rms(x, g):
    ms = jnp.mean(x * x, axis=-1, keepdims=True)
    return x * lax.rsqrt(ms + EPS) * g


def _per_seq(x, v, op):
    r, c = x.shape
    x3 = x.reshape(r // SUB, SUB, c)
    return op(x3, v[None]).reshape(r, c)


def _modulate(h, scale, shift):
    r, c = h.shape
    h3 = h.reshape(r // SUB, SUB, c)
    return (h3 * (1.0 + scale)[None] + shift[None]).reshape(r, c)


def _gated_add(x, gate, y):
    return x + _per_seq(y, gate, lambda a, b: a * b)


def _dot(a, b):
    return jnp.dot(a, b, preferred_element_type=F32)


def _pack_pairs(x):
    bits = lax.bitcast_convert_type(x, jnp.uint32)
    half = bits.shape[1] // 2
    w = lax.shift_right_logical(bits[:, :half], jnp.uint32(16)) | (bits[:, half:] & jnp.uint32(0xFFFF0000))
    return lax.bitcast_convert_type(w, jnp.int32)


def _unpack_pairs(w):
    w = lax.bitcast_convert_type(w, jnp.uint32)
    lo = lax.bitcast_convert_type(lax.shift_left(w, jnp.uint32(16)), F32)
    hi = lax.bitcast_convert_type(w & jnp.uint32(0xFFFF0000), F32)
    return jnp.concatenate([lo, hi], axis=1)


def _sigmoid(x):
    return 0.5 * jnp.tanh(0.5 * x) + 0.5


def _ada_kernel(c_ref, w_ref, b_ref, o_ref):
    c = c_ref[...]
    s = (c * jax.nn.sigmoid(c)).astype(BF16)
    o_ref[0] = _dot(s, w_ref[0].astype(BF16)) + b_ref[0]


def _ada_call(cond, w_ada, b_ada):
    depth, d, d6 = w_ada.shape
    nr = cond.shape[0]
    bn = d6 // 4
    return pl.pallas_call(
        _ada_kernel,
        grid=(depth, d6 // bn),
        in_specs=[
            pl.BlockSpec((nr, d), lambda l, n: (0, 0)),
            pl.BlockSpec((1, d, bn), lambda l, n: (l, 0, n)),
            pl.BlockSpec((1, 1, bn), lambda l, n: (l, 0, n)),
        ],
        out_specs=pl.BlockSpec((1, nr, bn), lambda l, n: (l, 0, n)),
        out_shape=jax.ShapeDtypeStruct((depth, nr, d6), F32),
        compiler_params=_cparams(("parallel", "parallel")),
        name="ada_mod",
    )(cond, w_ada, b_ada.reshape(depth, 1, d6))


def _to_time_major(x):
    s, t, d = x.shape
    return jnp.swapaxes(x, 0, 1).reshape(s * t, d)


def _from_time_major(x):
    r, d = x.shape
    return jnp.swapaxes(x.reshape(r // SUB, SUB, d), 0, 1)


def _inproj_kernel(n_ctx_tiles, *refs):
    if n_ctx_tiles is not None:
        xp_ref, xs_ref, row_ref, col_ref, mod_ref, g_ref, w_ref, o_ref, x0_ref = refs
        is_ctx = pl.program_id(0) < n_ctx_tiles

        @pl.when(is_ctx)
        def _():
            x0_ref[...] = _to_time_major(xp_ref[0])

        @pl.when(jnp.logical_not(is_ctx))
        def _():
            x = _to_time_major(xs_ref[...])
            half = x.shape[1] // 2
            x0_ref[...] = jnp.concatenate([x[:, :half] + row_ref[0], x[:, half:] + col_ref[...]], axis=1)

        x = x0_ref[...]
    else:
        x_ref, mod_ref, g_ref, w_ref, o_ref = refs
        x = x_ref[...]
    h = _modulate(_rms(x, g_ref[...]), mod_ref[0, 1], mod_ref[0, 0])
    o_ref[...] = _dot(h.astype(BF16), w_ref[...])


def _natural_specs(cfg, t0=0):
    n_ctx_tiles, tiles_per_ctx, _, d = cfg

    def ctx_idx(j):
        jc = jnp.minimum(j + t0, n_ctx_tiles - 1)
        return (jc // tiles_per_ctx, 0, lax.rem(jc, tiles_per_ctx), 0)

    return (pl.BlockSpec((1, SUB, GRID_W, d), ctx_idx),
            pl.BlockSpec((SUB, GRID_W, d), lambda j: (0, jnp.maximum(j + t0 - n_ctx_tiles, 0), 0)))


def _wide(cfg):
    n_ctx_tiles, _, n_tiles, _ = cfg
    return 2 if n_ctx_tiles % 2 == 0 and n_tiles % 2 == 0 else 1


def _moe_parts(cfg):
    n_tiles = cfg[2]
    half = n_tiles // 2
    if _wide(cfg) == 2 and half % 2 == 0:
        return [(0, half), (half, n_tiles - half)]
    return [(0, n_tiles)]


def _mod_spec(cfg, f=1, t0=0):
    n_ctx_steps = cfg[0] // f
    d = cfg[3]
    s0 = t0 // f
    return pl.BlockSpec((1, 6, SUB, d), lambda j: (jnp.where(j + s0 >= n_ctx_steps, 1, 0), 0, 0, 0))


def _inproj_call(cfg, x, mod_l, g, w_in, first=None):
    n_ctx_tiles, _, n_tiles, d = cfg
    d_in = w_in.shape[1]
    f = 1 if first is not None else _wide(cfg)
    tm = TM * f
    row_spec = pl.BlockSpec((tm, d), lambda j: (j, 0))
    out_specs = [pl.BlockSpec((tm, d_in), lambda j: (j, 0))]
    out_shape = [jax.ShapeDtypeStruct((n_tiles * TM, d_in), F32)]
    if first is not None:
        xp_spec, xs_spec = _natural_specs(cfg)
        in_specs = [xp_spec, xs_spec,
                    pl.BlockSpec((1, 1, d // 2), lambda j: (jnp.maximum(j - n_ctx_tiles, 0), 0, 0)),
                    _const_spec((TM, d // 2))]
        args = list(first)
        out_specs.append(row_spec)
        out_shape.append(jax.ShapeDtypeStruct((n_tiles * TM, d), F32))
    else:
        in_specs = [row_spec]
        args = [x]
    in_specs += [_mod_spec(cfg, f), _const_spec((1, d)), _const_spec((d, d_in))]
    args += [mod_l, g.reshape(1, d), w_in]
    return pl.pallas_call(
        functools.partial(_inproj_kernel, n_ctx_tiles if first is not None else None),
        grid=(n_tiles // f,),
        in_specs=in_specs,
        out_specs=out_specs,
        out_shape=out_shape,
        compiler_params=_cparams(("parallel",)),
        name="in_proj",
    )(*args)


def _seq_flags(cfg, j):
    n_ctx_tiles, tiles_per_ctx, n_tiles, _ = cfg
    is_ctx = j < n_ctx_tiles
    pos = lax.rem(j, tiles_per_ctx)
    first = jnp.where(is_ctx, pos == 0, j == n_ctx_tiles)
    last = jnp.where(is_ctx, pos == tiles_per_ctx - 1, j == n_tiles - 1)
    return is_ctx, pos, first, last


def _conv(pad_ref, xa, prev, nxt, first, last, cw, cb):
    lo = CONV_LEFT * SUB
    pad_ref[0:lo, :] = jnp.where(first, 0.0, prev)
    pad_ref[lo:lo + TM, :] = xa
    pad_ref[lo + TM:lo + TM + CONV_RIGHT * SUB, :] = jnp.where(last, 0.0, nxt)
    y = cb
    for k in range(CONV_W):
        y = y + pad_ref[k * SUB:k * SUB + TM, :] * cw[k:k + 1, :]
    return y


def _gates(xc, wg_ref, ba, bx, lam):
    half = xc.shape[1] // 2
    xb = xc.astype(BF16)
    g0 = _dot(xb[:, :half], wg_ref[0])
    g1 = _dot(xb[:, half:], wg_ref[1])
    r = _sigmoid(jnp.concatenate([g0[:, :half], g1[:, :half]], axis=1) + ba)
    i = _sigmoid(jnp.concatenate([g0[:, half:], g1[:, half:]], axis=1) + bx)
    z = -lam
    decay = LRU_C * (jnp.maximum(z, 0.0) + jnp.log1p(jnp.exp(-jnp.abs(z))))
    neg_log_a = r * decay
    a = jnp.exp(-neg_log_a)
    z = jnp.tanh(neg_log_a) * (a * a + 1.0)
    root = jnp.where(z > 0.0, z * lax.rsqrt(z), 0.0)
    u = root * (i * xc)
    return a, u


def _scan(a_ref, u_ref, h_ref, h, reverse):
    steps = TM // SUB

    def body(k, h):
        t = steps - 1 - k if reverse else k
        r0 = pl.multiple_of(t * SUB, SUB)
        h = a_ref[pl.ds(r0, SUB), :] * h + u_ref[pl.ds(r0, SUB), :]
        h_ref[pl.ds(r0, SUB), :] = h
        return h

    return lax.fori_loop(0, steps, body, h, unroll=8)


def _scan_fwd_kernel(cfg, xa_ref, xp_ref, xn_ref, cw_ref, cb_ref, wg_ref, ba_ref, bx_ref,
                     lam_ref, h0_ref, hf_ref, st_ref, pad_ref, a_ref, u_ref, carry_ref):
    j = pl.program_id(0)
    _, _, first, last = _seq_flags(cfg, j)
    xc = _conv(pad_ref, xa_ref[...], xp_ref[...], xn_ref[...], first, last, cw_ref[...], cb_ref[...])
    a, u = _gates(xc, wg_ref, ba_ref[...], bx_ref[...], lam_ref[...])
    a_ref[...] = a
    u_ref[...] = u

    @pl.when(first)
    def _():
        carry_ref[...] = h0_ref[0]

    h = _scan(a_ref, u_ref, hf_ref, carry_ref[...], reverse=False)
    carry_ref[...] = h
    st_ref[0] = h


def _halo_specs(cfg, col, rows_prev, rows_next, tile_of):
    n_tiles = cfg[2]
    c = 512
    nb_prev = TM // rows_prev
    nb_next = TM // rows_next
    last_next = n_tiles * nb_next - 1
    prev = pl.BlockSpec((rows_prev, c), lambda j: (jnp.maximum(tile_of(j) * nb_prev - 1, 0), col))
    nxt = pl.BlockSpec((rows_next, c), lambda j: (jnp.minimum((tile_of(j) + 1) * nb_next, last_next), col))
    return prev, nxt


def _state_spec(cfg, tile_of):
    n_ctx_tiles, tiles_per_ctx, _, _ = cfg
    n_groups = n_ctx_tiles // tiles_per_ctx
    c = 512
    return pl.BlockSpec((1, SUB, c), lambda j: (jnp.minimum(tile_of(j) // tiles_per_ctx, n_groups), 0, 0))


def _h0_spec(cfg, tile_of):
    n_ctx_tiles = cfg[0]
    return pl.BlockSpec((1, SUB, 512), lambda j: (jnp.where(tile_of(j) >= n_ctx_tiles, 1, 0), 0, 0))


def _scan_fwd_call(cfg, proj, conv_w, conv_b, wg, ba, bx, lam, h0):
    n_ctx_tiles, tiles_per_ctx, n_tiles, _ = cfg
    n_groups = n_ctx_tiles // tiles_per_ctx
    c = conv_w.shape[1]
    ident = lambda j: j
    xp_spec, xn_spec = _halo_specs(cfg, 0, CONV_LEFT * SUB, CONV_RIGHT * SUB, ident)
    return pl.pallas_call(
        functools.partial(_scan_fwd_kernel, cfg),
        grid=(n_tiles,),
        in_specs=[
            pl.BlockSpec((TM, c), lambda j: (j, 0)), xp_spec, xn_spec,
            _const_spec((CONV_W, c)), _const_spec((1, c)), _const_spec((2, c // 2, c)),
            _const_spec((1, c)), _const_spec((1, c)), _const_spec((1, c)),
            _h0_spec(cfg, ident),
        ],
        out_specs=[pl.BlockSpec((TM, c), lambda j: (j, 0)), _state_spec(cfg, ident)],
        out_shape=[jax.ShapeDtypeStruct((n_tiles * TM, c), F32),
                   jax.ShapeDtypeStruct((n_groups + 1, SUB, c), F32)],
        scratch_shapes=[
            pltpu.VMEM((TM + (CONV_W - 1) * SUB, c), F32),
            pltpu.VMEM((TM, c), F32), pltpu.VMEM((TM, c), F32), pltpu.VMEM((SUB, c), F32),
        ],
        compiler_params=_cparams(("arbitrary",)),
        name="lru_fwd",
    )(proj, proj, proj, conv_w, conv_b.reshape(1, c), wg, ba.reshape(1, c), bx.reshape(1, c),
      lam.reshape(1, c), h0)


def _pool_inv_counts(ctx_len, c):
    gw = c // len(POOL_WINDOWS)

    def table(t0, t_len):
        t = t0 + jnp.arange(GRID_W)
        cols = []
        for k in POOL_WINDOWS:
            left = k // 2
            right = k - 1 - left
            cnt = jnp.minimum(t + right + 1, t_len) - jnp.maximum(t - left, 0)
            cols.append(jnp.broadcast_to((1.0 / cnt.astype(F32))[:, None], (GRID_W, gw)))
        return jnp.repeat(jnp.concatenate(cols, axis=1), SUB, axis=0)

    return jnp.stack([table(p * GRID_W, ctx_len) for p in range(ctx_len // GRID_W)] + [table(0, GRID_W)])


def _gelu_tanh(x):
    k0 = 0.7978845608028654
    hx = 0.5 * x
    return hx + hx * jnp.tanh(x * (k0 + (k0 * 0.044715) * (x * x)))


def _pool_mix(pad_ref, inv_ref, pw_ref, pb, ps):
    c = pad_ref.shape[1]
    gw = c // len(POOL_WINDOWS)
    outs = []
    for g, k in enumerate(POOL_WINDOWS):
        left = k // 2
        right = k - 1 - left
        lanes = slice(g * gw, (g + 1) * gw)
        s = None
        for o in range(-left, right + 1):
            v = pad_ref[POOL_HALO + o * SUB:POOL_HALO + o * SUB + TM, lanes]
            s = v if s is None else s + v
        outs.append(s * inv_ref[0, :, lanes] - pad_ref[POOL_HALO:POOL_HALO + TM, lanes])
    d = jnp.concatenate(outs, axis=1).astype(BF16)
    half = c // 2
    y = jnp.concatenate([_dot(d[:, :half], pw_ref[0]), _dot(d[:, half:], pw_ref[1])], axis=1)
    return (y + pb) * ps


def _scan_bwd_kernel(cfg, proj_ref, xp_ref, xn_ref, bp_ref, bn_ref, hf_ref, inv_ref, cw_ref, cb_ref,
                     wg_ref, ba_ref, bx_ref, lam_ref, h0_ref, pw_ref, pb_ref, ps_ref,
                     y_ref, st_ref, pad_ref, ppad_ref, a_ref, u_ref, hb_ref, carry_ref):
    n_tiles = cfg[2]
    j = n_tiles - 1 - pl.program_id(0)
    is_ctx, _, first, last = _seq_flags(cfg, j)
    c = hf_ref.shape[1]
    xc = _conv(pad_ref, proj_ref[:, 0:c], xp_ref[...], xn_ref[...], first, last, cw_ref[...], cb_ref[...])
    a, u = _gates(xc, wg_ref, ba_ref[...], bx_ref[...], lam_ref[...])
    a_ref[...] = a
    u_ref[...] = u

    @pl.when(last)
    def _():
        carry_ref[...] = h0_ref[0]

    h = _scan(a_ref, u_ref, hb_ref, carry_ref[...], reverse=True)
    carry_ref[...] = h
    st_ref[0] = h

    ga = proj_ref[:, c:2 * c]
    y_a = (hf_ref[...] + hb_ref[...]) * _gelu_tanh(ga)

    use_prev = jnp.logical_and(is_ctx, jnp.logical_not(first))
    use_next = jnp.logical_and(is_ctx, jnp.logical_not(last))
    ppad_ref[0:POOL_HALO, :] = jnp.where(use_prev, bp_ref[...], 0.0)
    ppad_ref[POOL_HALO:POOL_HALO + TM, :] = proj_ref[:, 2 * c:3 * c]
    ppad_ref[POOL_HALO + TM:, :] = jnp.where(use_next, bn_ref[...], 0.0)
    y_b = _pool_mix(ppad_ref, inv_ref, pw_ref, pb_ref[...], ps_ref[...])
    y_ref[...] = jnp.concatenate([y_a, y_b], axis=1).astype(BF16)


def _scan_bwd_call(cfg, inv_cnt, proj, hf, conv_w, conv_b, wg, ba, bx, lam, h0, pw, pb, ps):
    n_ctx_tiles, tiles_per_ctx, n_tiles, d = cfg
    n_groups = n_ctx_tiles // tiles_per_ctx
    c = conv_w.shape[1]
    rev = lambda j: n_tiles - 1 - j
    xp_spec, xn_spec = _halo_specs(cfg, 0, CONV_LEFT * SUB, CONV_RIGHT * SUB, rev)
    bp_spec, bn_spec = _halo_specs(cfg, 2, POOL_HALO, POOL_HALO, rev)

    def inv_idx(j):
        jj = rev(j)
        return (jnp.where(jj < n_ctx_tiles, lax.rem(jj, tiles_per_ctx), tiles_per_ctx), 0, 0)

    return pl.pallas_call(
        functools.partial(_scan_bwd_kernel, cfg),
        grid=(n_tiles,),
        in_specs=[
            pl.BlockSpec((TM, 3 * c), lambda j: (rev(j), 0)), xp_spec, xn_spec, bp_spec, bn_spec,
            pl.BlockSpec((TM, c), lambda j: (rev(j), 0)),
            pl.BlockSpec((1, TM, c), inv_idx),
            _const_spec((CONV_W, c)), _const_spec((1, c)), _const_spec((2, c // 2, c)),
            _const_spec((1, c)), _const_spec((1, c)), _const_spec((1, c)),
            _h0_spec(cfg, rev),
            _const_spec((2, c // 2, c // 2)), _const_spec((1, c)), _const_spec((1, c)),
        ],
        out_specs=[pl.BlockSpec((TM, d), lambda j: (rev(j), 0)), _state_spec(cfg, rev)],
        out_shape=[jax.ShapeDtypeStruct((n_tiles * TM, d), BF16),
                   jax.ShapeDtypeStruct((n_groups + 1, SUB, c), F32)],
        scratch_shapes=[
            pltpu.VMEM((TM + (CONV_W - 1) * SUB, c), F32),
            pltpu.VMEM((TM + 2 * POOL_HALO, c), F32),
            pltpu.VMEM((TM, c), F32), pltpu.VMEM((TM, c), F32), pltpu.VMEM((TM, c), F32),
            pltpu.VMEM((SUB, c), F32),
        ],
        compiler_params=_cparams(("arbitrary",)),
        name="lru_bwd_mix",
    )(proj, proj, proj, proj, proj, hf, inv_cnt, conv_w, conv_b.reshape(1, c), wg, ba.reshape(1, c),
      bx.reshape(1, c), lam.reshape(1, c), h0, pw, pb.reshape(1, c), ps.reshape(1, c))


def _out_proj(x_ref, ym_ref, mod_ref, wo_ref, g2_ref):
    xm = _gated_add(x_ref[...], mod_ref[0, 2], _dot(ym_ref[...], wo_ref[...]))
    h2 = _modulate(_rms(xm, g2_ref[...]), mod_ref[0, 4], mod_ref[0, 3])
    return xm, h2


def _swiglu_act(h2b, wg_ref, wu_ref, act_ref):
    n_chunks = act_ref.shape[1] // FF_CHUNK

    def body(f, carry):
        c0 = pl.multiple_of(f * FF_CHUNK, FF_CHUNK)
        g = _dot(h2b, wg_ref[:, pl.ds(c0, FF_CHUNK)])
        u = _dot(h2b, wu_ref[:, pl.ds(c0, FF_CHUNK)])
        act_ref[:, pl.ds(c0, FF_CHUNK)] = (g * _sigmoid(g) * u).astype(BF16)
        return carry

    lax.fori_loop(0, n_chunks, body, 0, unroll=True)


def _dense_ffn_kernel(x_ref, ym_ref, mod_ref, wo_ref, g2_ref, wg_ref, wu_ref, wd_ref, o_ref, act_ref):
    xm, h2 = _out_proj(x_ref, ym_ref, mod_ref, wo_ref, g2_ref)
    _swiglu_act(h2.astype(BF16), wg_ref, wu_ref, act_ref)
    o_ref[...] = _gated_add(xm, mod_ref[0, 5], _dot(act_ref[...], wd_ref[...]))


def _dense_ffn_call(cfg, x, ymix, mod_l, w_out, g2, wg, wu, wd):
    _, _, n_tiles, d = cfg
    d_ff = wg.shape[1]
    f = _wide(cfg)
    row_spec = pl.BlockSpec((TM * f, d), lambda j: (j, 0))
    return pl.pallas_call(
        _dense_ffn_kernel,
        grid=(n_tiles // f,),
        in_specs=[row_spec, row_spec, _mod_spec(cfg, f), _const_spec((d, d)), _const_spec((1, d)),
                  _const_spec((d, d_ff)), _const_spec((d, d_ff)), _const_spec((d_ff, d))],
        out_specs=row_spec,
        out_shape=jax.ShapeDtypeStruct((n_tiles * TM, d), F32),
        scratch_shapes=[pltpu.VMEM((TM * f, d_ff), BF16)],
        compiler_params=_cparams(("parallel",)),
        name="out_proj_dense_ffn",
    )(x, ymix, mod_l, w_out, g2.reshape(1, d), wg, wu, wd)


M_E1, M_E2, M_W1, M_W2, M_R1, M_R2 = range(6)


def _route_kernel(x_ref, ym_ref, mod_ref, wo_ref, g2_ref, rw_ref, rb_ref, tri_ref,
                  xm_ref, hp_ref, meta_ref, mt_ref, cnt_ref, run_ref):
    xm, h2 = _out_proj(x_ref, ym_ref, mod_ref, wo_ref, g2_ref)
    xm_ref[...] = xm
    h_hi = h2.astype(BF16)
    h_hi32 = h_hi.astype(F32)
    hp_ref[...] = _pack_pairs(h_hi32)

    h_lo = (h2 - h_hi32).astype(BF16)
    p = _dot(h_hi, rw_ref[...])
    logits = p[:, :LANES] + p[:, LANES:] + _dot(h_lo, rw_ref[:, :LANES])
    lane = lax.broadcasted_iota(jnp.int32, logits.shape, 1)
    neg = jnp.float32(-jnp.inf)
    lg = jnp.where(lane < N_EXPERTS, logits + rb_ref[...], neg)
    m1 = jnp.max(lg, axis=1, keepdims=True)
    i1 = jnp.min(jnp.where(lg == m1, lane, LANES), axis=1, keepdims=True)
    lg2 = jnp.where(lane == i1, neg, lg)
    m2 = jnp.max(lg2, axis=1, keepdims=True)
    i2 = jnp.min(jnp.where(lg2 == m2, lane, LANES), axis=1, keepdims=True)
    e2 = jnp.exp(m2 - m1)
    den = 1.0 + e2

    @pl.when(pl.program_id(0) == 0)
    def _():
        run_ref[...] = jnp.zeros_like(run_ref)

    sel1 = lane == i1
    sel2 = lane == i2
    onehot = jnp.where(jnp.logical_or(sel1, sel2), 1.0, 0.0)
    rank = _dot(tri_ref[...], onehot.astype(BF16)) + run_ref[...]
    r1 = jnp.sum(jnp.where(sel1, rank, 0.0), axis=1, keepdims=True)
    r2 = jnp.sum(jnp.where(sel2, rank, 0.0), axis=1, keepdims=True)
    run_ref[...] = run_ref[...] + jnp.sum(onehot, axis=0, keepdims=True)
    cnt_ref[...] = run_ref[...]

    meta = jnp.zeros(logits.shape, F32)
    for k, v in ((M_E1, i1.astype(F32)), (M_E2, i2.astype(F32)), (M_W1, 1.0 / den), (M_W2, e2 / den),
                 (M_R1, r1), (M_R2, r2)):
        meta = jnp.where(lane == k, v, meta)
    meta_ref[...] = meta
    mt_ref[0] = jnp.transpose(meta)[:SUB]


def _route_call(cfg, part, x, ymix, mod_l, w_out, g2, router_w, router_b):
    d = cfg[3]
    t0, nt = part
    n = nt * TM
    f = _wide(cfg)
    tm = TM * f
    s0 = t0 // f
    in_row_spec = pl.BlockSpec((tm, d), lambda j: (j + s0, 0))
    row_spec = pl.BlockSpec((tm, d), lambda j: (j, 0))
    rw = jnp.zeros((d, LANES), F32).at[:, :N_EXPERTS].set(router_w)
    rw_hi = rw.astype(BF16)
    rw = jnp.concatenate([rw_hi, (rw - rw_hi.astype(F32)).astype(BF16)], axis=1)
    rb = jnp.zeros((1, LANES), F32).at[0, :N_EXPERTS].set(router_b)
    tri = jnp.tril(jnp.ones((tm, tm), BF16), -1)
    return pl.pallas_call(
        _route_kernel,
        grid=(nt // f,),
        in_specs=[in_row_spec, in_row_spec, _mod_spec(cfg, f, t0), _const_spec((d, d)), _const_spec((1, d)),
                  _const_spec((d, 2 * LANES)), _const_spec((1, LANES)), _const_spec((tm, tm))],
        out_specs=[row_spec, pl.BlockSpec((tm, d // 2), lambda j: (j, 0)),
                   pl.BlockSpec((tm, LANES), lambda j: (j, 0)), pl.BlockSpec((1, SUB, tm), lambda j: (j, 0, 0)),
                   pl.BlockSpec((1, LANES), lambda j: (0, 0))],
        out_shape=[jax.ShapeDtypeStruct((n, d), F32), jax.ShapeDtypeStruct((n, d // 2), jnp.int32),
                   jax.ShapeDtypeStruct((n, LANES), F32), jax.ShapeDtypeStruct((n // tm, SUB, tm), F32),
                   jax.ShapeDtypeStruct((1, LANES), F32)],
        scratch_shapes=[pltpu.VMEM((1, LANES), F32)],
        compiler_params=_cparams(("arbitrary",)),
        name="out_proj_route",
    )(x, ymix, mod_l, w_out, g2.reshape(1, d), rw, rb, tri)


def _routing_tables(n_tiles, mt, cnt):
    counts = cnt[0, :N_EXPERTS].astype(jnp.int32)
    padded = ((counts + TM - 1) // TM) * TM
    ends = jnp.cumsum(padded)
    offs = ends - padded
    row = lambda k: mt[:, k, :].reshape(-1).astype(jnp.int32)
    pos1 = offs[row(M_E1)] + row(M_R1)
    pos2 = offs[row(M_E2)] + row(M_R2)
    n_sorted_tiles = 2 * n_tiles + N_EXPERTS
    starts = jnp.arange(n_sorted_tiles, dtype=jnp.int32) * TM
    tile_e = jnp.minimum(jnp.sum((starts[:, None] >= ends[None, :]).astype(jnp.int32), axis=1), N_EXPERTS - 1)
    n_active = (ends[-1] // TM).reshape(1)
    return jnp.concatenate([pos1, pos2]), tile_e, n_active


SC_CORES = 2
SC_SUBCORES = 16
SC_CHUNK = 64


def _sc_mesh():
    return plsc.VectorSubcoreMesh(core_axis_name="c", subcore_axis_name="s",
                                  num_cores=SC_CORES, num_subcores=SC_SUBCORES)


def _sc_scatter_rows(rows, idx, n_out):
    n_src, width = rows.shape
    n_idx = idx.shape[0]
    n_workers = SC_CORES * SC_SUBCORES
    per_w = n_idx // n_workers
    chunks = per_w // SC_CHUNK
    assert n_idx % (n_workers * SC_CHUNK) == 0 and n_src % per_w == 0
    idx3 = idx.reshape(n_workers, chunks, SC_CHUNK)

    def body(rows_hbm, idx_hbm, out_hbm, idx_v, rows_v, sem):
        wid = lax.axis_index("s") * SC_CORES + lax.axis_index("c")
        src_base = lax.rem(wid * per_w, n_src)
        pltpu.sync_copy(idx_hbm.at[wid], idx_v)

        @pl.loop(0, chunks)
        def _(i):
            off = pl.multiple_of(i * SC_CHUNK, SC_CHUNK)
            pltpu.sync_copy(rows_hbm.at[pl.ds(src_base + off, SC_CHUNK)], rows_v)
            pltpu.async_copy(rows_v, out_hbm.at[idx_v.at[i]], sem).wait()

    return pl.kernel(
        body,
        out_type=jax.ShapeDtypeStruct((n_out, width), rows.dtype),
        mesh=_sc_mesh(),
        scratch_types=[pltpu.VMEM((chunks, SC_CHUNK), jnp.int32), pltpu.VMEM((SC_CHUNK, width), rows.dtype),
                       pltpu.SemaphoreType.DMA],
        name="sc_scatter_rows",
    )(rows, idx3)


def _expert_kernel(te_ref, na_ref, s_ref, wgu_ref, wd_ref, o_ref):
    del te_ref

    @pl.when(pl.program_id(0) < na_ref[0])
    def _():
        h = _unpack_pairs(s_ref[...]).astype(BF16)
        gu = _dot(h, wgu_ref[0, 0])
        dfe = gu.shape[1] // 2
        g = gu[:, :dfe]
        act = (g * _sigmoid(g) * gu[:, dfe:]).astype(BF16)
        o_ref[...] = _pack_pairs(_dot(act, wd_ref[0, 0]).astype(BF16).astype(F32))

    @pl.when(pl.program_id(0) >= na_ref[0])
    def _():
        o_ref[...] = jnp.zeros_like(o_ref)


def _expert_call(cfg, tile_e, n_active, buf, wgu, wd, layer):
    d = cfg[3]
    dfe = wd.shape[2]
    n_sorted_tiles = buf.shape[0] // TM
    last = lambda t, na: jnp.maximum(jnp.minimum(t, na[0] - 1), 0)
    tile = lambda t, te, na: (last(t, na), 0)
    w_spec = lambda shape: pl.BlockSpec((1,) + shape, lambda t, te, na: (layer, te[last(t, na)], 0, 0))
    return pl.pallas_call(
        _expert_kernel,
        grid_spec=pltpu.PrefetchScalarGridSpec(
            num_scalar_prefetch=2,
            grid=(n_sorted_tiles,),
            in_specs=[pl.BlockSpec((TM, d // 2), tile), w_spec((1, d, 2 * dfe)), w_spec((1, dfe, d))],
            out_specs=pl.BlockSpec((TM, d // 2), lambda t, te, na: (t, 0)),
        ),
        out_shape=jax.ShapeDtypeStruct((n_sorted_tiles * TM, d // 2), jnp.int32),
        compiler_params=_cparams(("arbitrary",)),
        name="moe_experts",
    )(tile_e, n_active, buf, wgu, wd)


def _store_natural(n_ctx_tiles, part, y, o_refs):
    t0, nt = part
    has_ctx = t0 < n_ctx_tiles
    has_lat = t0 + nt > n_ctx_tiles
    is_ctx = pl.program_id(0) + t0 < n_ctx_tiles
    if has_ctx:
        @pl.when(is_ctx)
        def _():
            o_refs[0][0] = _from_time_major(y)

    if has_lat:
        @pl.when(jnp.logical_not(is_ctx))
        def _():
            o_refs[-1][...] = _from_time_major(y)


def _natural_out(cfg, part):
    n_ctx_tiles, tiles_per_ctx, n_tiles, d = cfg
    n_groups = n_ctx_tiles // tiles_per_ctx
    t0, nt = part
    specs = _natural_specs(cfg, t0)
    shapes = (jax.ShapeDtypeStruct((n_groups, SUB, tiles_per_ctx * GRID_W, d), F32),
              jax.ShapeDtypeStruct((SUB, (n_tiles - n_ctx_tiles) * GRID_W, d), F32))
    keep = [k for k, used in enumerate((t0 < n_ctx_tiles, t0 + nt > n_ctx_tiles)) if used]
    return [specs[k] for k in keep], [shapes[k] for k in keep], keep


def _sc_gather_rows(table, idx):
    n_rows = idx.shape[0]
    width = table.shape[1]
    n_workers = SC_CORES * SC_SUBCORES
    assert n_rows % (n_workers * SC_CHUNK) == 0
    per_w = n_rows // n_workers
    mesh = _sc_mesh()

    def body(table_hbm, idx_hbm, out_hbm, idx_v, rows_v, sem):
        wid = lax.axis_index("s") * SC_CORES + lax.axis_index("c")
        base = wid * per_w
        pltpu.sync_copy(idx_hbm.at[pl.ds(base, per_w)], idx_v)

        @pl.loop(0, per_w // SC_CHUNK)
        def _(i):
            off = pl.multiple_of(i * SC_CHUNK, SC_CHUNK)
            pltpu.async_copy(table_hbm.at[idx_v.at[pl.ds(off, SC_CHUNK)]], rows_v, sem).wait()
            pltpu.sync_copy(rows_v, out_hbm.at[pl.ds(base + off, SC_CHUNK)])

    return pl.kernel(
        body,
        out_type=jax.ShapeDtypeStruct((n_rows, width), table.dtype),
        mesh=mesh,
        scratch_types=[pltpu.VMEM((per_w,), jnp.int32), pltpu.VMEM((SC_CHUNK, width), table.dtype),
                       pltpu.SemaphoreType.DMA],
        name="sc_gather_rows",
    )(table, idx)


def _combine_kernel(final_ctx_tiles, part, n_filled, xm_ref, y1_ref, y2_ref, meta_ref, mod_ref, gfin_ref,
                    *refs):
    o_refs = refs[n_filled:]
    meta = meta_ref[...]
    y = (meta[:, M_W1:M_W1 + 1] * _unpack_pairs(y1_ref[...])
         + meta[:, M_W2:M_W2 + 1] * _unpack_pairs(y2_ref[...]))
    x = _gated_add(xm_ref[...], mod_ref[0, 5], y)
    if final_ctx_tiles is None:
        o_refs[0][...] = x
    else:
        _store_natural(final_ctx_tiles, part, _rms(x, gfin_ref[...]), o_refs)


def _combine_call(cfg, part, xm, yg, meta, mod_l, g_final, final, filled):
    n_ctx_tiles, _, n_tiles, d = cfg
    t0, nt = part
    row_spec = pl.BlockSpec((TM, d), lambda j: (j, 0))
    if final:
        out_specs, out_shape, keys = _natural_out(cfg, part)
    else:
        out_specs = [pl.BlockSpec((TM, d), lambda j: (j + t0, 0))]
        out_shape, keys = [jax.ShapeDtypeStruct((n_tiles * TM, d), F32)], [0]
    reuse = [k for k in keys if k in filled]
    base = [xm, yg, yg, meta, mod_l, g_final.reshape(1, d)]
    outs = pl.pallas_call(
        functools.partial(_combine_kernel, n_ctx_tiles if final else None, part, len(reuse)),
        grid=(nt,),
        in_specs=[row_spec, pl.BlockSpec((TM, d // 2), lambda j: (j, 0)),
                  pl.BlockSpec((TM, d // 2), lambda j: (nt + j, 0)),
                  pl.BlockSpec((TM, LANES), lambda j: (j, 0)), _mod_spec(cfg, 1, t0), _const_spec((1, d))]
                 + [pl.BlockSpec(memory_space=pl.ANY)] * len(reuse),
        out_specs=out_specs,
        out_shape=out_shape,
        input_output_aliases={len(base) + i: keys.index(k) for i, k in enumerate(reuse)},
        compiler_params=_cparams(("arbitrary",)),
        name="moe_combine",
    )(*base, *[filled[k] for k in reuse])
    return {**filled, **dict(zip(keys, outs))}


def _final_norm_kernel(n_ctx_tiles, part, x_ref, g_ref, *o_refs):
    _store_natural(n_ctx_tiles, part, _rms(x_ref[...], g_ref[...]), o_refs)


def _final_norm_call(cfg, x, g):
    n_ctx_tiles, _, n_tiles, d = cfg
    out_specs, out_shape, _ = _natural_out(cfg, (0, n_tiles))
    return pl.pallas_call(
        functools.partial(_final_norm_kernel, n_ctx_tiles, (0, n_tiles)),
        grid=(n_tiles,),
        in_specs=[pl.BlockSpec((TM, d), lambda j: (j, 0)), _const_spec((1, d))],
        out_specs=out_specs,
        out_shape=out_shape,
        compiler_params=_cparams(("arbitrary",)),
        name="final_norm",
    )(x, g.reshape(1, d))


def _block_diag(w, per_block):
    *lead, n, k, _ = w.shape
    nb = n // per_block
    w = w.reshape(*lead, nb, per_block, k, k)
    eye = jnp.eye(per_block, dtype=w.dtype)
    out = w[..., :, :, None, :] * eye[:, None, :, None]
    return out.reshape(*lead, nb, per_block * k, per_block * k)


def _pos_tables(n_ctx_tiles, n_lat_tiles, d):
    quarter = d // 4
    omega = 1.0 / (POS_BASE ** (jnp.arange(quarter, dtype=F32) / quarter))
    er = jnp.arange(n_lat_tiles, dtype=F32)[:, None] * omega
    ec = jnp.arange(GRID_W, dtype=F32)[:, None] * omega
    row_emb = jnp.concatenate([jnp.sin(er), jnp.cos(er)], axis=-1)
    col_emb = jnp.concatenate([jnp.sin(ec), jnp.cos(ec)], axis=-1)
    del n_ctx_tiles
    return row_emb[:, None, :], jnp.repeat(col_emb, SUB, axis=0)


def kernel(x_prompt, x_sample, state_lru, c, c_ctx, norm_mix_g, w_ada, b_ada, w_in, conv_w, conv_b, lru_wa, lru_ba, lru_wx, lru_bx, lru_lam, pool_w, pool_b, pool_scale, w_out, norm_ffn_g, ffn_wg, ffn_wu, ffn_wd, moe_router_w, moe_router_b, moe_wg, moe_wu, moe_wd, norm_final_g):
    bc, tc, d = x_prompt.shape
    bl, tl, _ = x_sample.shape
    depth = w_ada.shape[0]
    c_lru = conv_w.shape[-1]
    assert bl == SUB and bc % SUB == 0 and tc % GRID_W == 0 and tl % GRID_W == 0
    assert c_lru == 4 * LANES and d == 2 * c_lru
    n_groups = bc // SUB
    tiles_per_ctx = tc // GRID_W
    n_ctx_tiles = n_groups * tiles_per_ctx
    n_lat_tiles = tl // GRID_W
    n_tiles = n_ctx_tiles + n_lat_tiles
    n_ctx = n_ctx_tiles * TM
    cfg = (n_ctx_tiles, tiles_per_ctx, n_tiles, d)

    cond = jnp.concatenate([c, c_ctx[None], jnp.zeros((SUB - 1, d), F32)], axis=0)
    mod = _ada_call(cond, w_ada, b_ada)
    mod_lat = mod[:, :SUB].reshape(depth, SUB, 6, d).transpose(0, 2, 1, 3)
    mod_ctx = jnp.broadcast_to(mod[:, SUB].reshape(depth, 6, 1, d), (depth, 6, SUB, d))
    mod = jnp.stack([mod_ctx, mod_lat], axis=1)

    row_tab, col_rep = _pos_tables(n_ctx_tiles, n_lat_tiles, d)
    first = (x_prompt.reshape(n_groups, SUB, tc, d), x_sample, row_tab, col_rep)
    x = None
    wg_lru = jnp.concatenate([_block_diag(lru_wa, 4), _block_diag(lru_wx, 4)], axis=-1).astype(BF16)
    pw = _block_diag(pool_w, 2).astype(BF16)
    inv_cnt = _pool_inv_counts(tc, c_lru)
    moe_wgu = jnp.concatenate([moe_wg.astype(BF16), moe_wu.astype(BF16)], axis=3)
    moe_wd_b = moe_wd.astype(BF16)
    h0 = jnp.concatenate([jnp.zeros_like(state_lru[None]), state_lru[None]], axis=0)

    states = []
    for l in range(depth):
        jdx = l // 2
        if l == 0:
            proj, x = _inproj_call(cfg, x, mod[l], norm_mix_g[l], w_in[l].astype(BF16), first)
        else:
            (proj,) = _inproj_call(cfg, x, mod[l], norm_mix_g[l], w_in[l].astype(BF16))
        hf, st_f = _scan_fwd_call(cfg, proj, conv_w[l], conv_b[l], wg_lru[l, 0], lru_ba[l, 0],
                                  lru_bx[l, 0], lru_lam[l, 0], h0[:, :, l, 0])
        ymix, st_b = _scan_bwd_call(cfg, inv_cnt, proj, hf, conv_w[l], conv_b[l], wg_lru[l, 1], lru_ba[l, 1],
                                    lru_bx[l, 1], lru_lam[l, 1], h0[:, :, l, 1], pw[l], pool_b[l],
                                    pool_scale[l])
        states.append(jnp.stack([st_f[:n_groups].reshape(bc, c_lru), st_b[:n_groups].reshape(bc, c_lru)], axis=1))
        if l % 2 == 0:
            x = _dense_ffn_call(cfg, x, ymix, mod[l], w_out[l].astype(BF16), norm_ffn_g[l],
                                ffn_wg[jdx].astype(BF16), ffn_wu[jdx].astype(BF16), ffn_wd[jdx].astype(BF16))
        else:
            final = l == depth - 1
            w_out_b = w_out[l].astype(BF16)
            filled = {}
            for part in _moe_parts(cfg):
                xm, hp, meta, mt, cnt = _route_call(cfg, part, x, ymix, mod[l], w_out_b, norm_ffn_g[l],
                                                    moe_router_w[jdx], moe_router_b[jdx])
                pos_flat, tile_e, n_active = _routing_tables(part[1], mt, cnt)
                buf = _sc_scatter_rows(hp, pos_flat, (2 * part[1] + N_EXPERTS) * TM)
                ys = _expert_call(cfg, tile_e, n_active, buf, moe_wgu, moe_wd_b, jdx)
                yg = _sc_gather_rows(ys, pos_flat)
                filled = _combine_call(cfg, part, xm, yg, meta, mod[l], norm_final_g, final, filled)
            x = (filled[0], filled[1]) if final else filled[0]

    y_prompt, y_sample = x if depth % 2 == 0 else _final_norm_call(cfg, x, norm_final_g)
    new_state = jnp.stack(states, axis=1)
    return (y_prompt.reshape(bc, tc, d), y_sample, new_state)
```

```python
import functools

import jax
import jax.numpy as jnp
from jax import lax
from jax.experimental import pallas as pl
from jax.experimental.pallas import tpu as pltpu
from jax.experimental.pallas import tpu_sc as plsc

F32 = jnp.float32
BF16 = jnp.bfloat16

SUB = 8
LANES = 128
GRID_W = 64
TM = GRID_W * SUB
POS_BASE = 10000.0
N_LRU_HEADS = 8
CONV_W = 4
CONV_LEFT = CONV_W // 2
CONV_RIGHT = CONV_W - 1 - CONV_LEFT
LRU_C = 8.0
POOL_WINDOWS = (2, 4, 8, 16)
POOL_HALO = 8 * SUB
N_EXPERTS = 8
EPS = 1e-6
FF_CHUNK = 256
VMEM_LIMIT = 56 * 1024 * 1024


def _cparams(sem):
    return pltpu.CompilerParams(dimension_semantics=sem, vmem_limit_bytes=VMEM_LIMIT)


def _const_spec(shape):
    nd = len(shape)
    return pl.BlockSpec(shape, lambda *_: (0,) * nd, pipeline_mode=pl.Buffered(1))


def _rms(x, g):
    ms = jnp.mean(x * x, axis=-1, keepdims=True)
    return x * lax.rsqrt(ms + EPS) * g


def _per_seq(x, v, op):
    r, c = x.shape
    x3 = x.reshape(r // SUB, SUB, c)
    return op(x3, v[None]).reshape(r, c)


def _modulate(h, scale, shift):
    r, c = h.shape
    h3 = h.reshape(r // SUB, SUB, c)
    return (h3 * (1.0 + scale)[None] + shift[None]).reshape(r, c)


def _gated_add(x, gate, y):
    return x + _per_seq(y, gate, lambda a, b: a * b)


def _dot(a, b):
    return jnp.dot(a, b, preferred_element_type=F32)


def _pack_pairs(x):
    bits = lax.bitcast_convert_type(x, jnp.uint32)
    half = bits.shape[1] // 2
    w = lax.shift_right_logical(bits[:, :half], jnp.uint32(16)) | (bits[:, half:] & jnp.uint32(0xFFFF0000))
    return lax.bitcast_convert_type(w, jnp.int32)


def _unpack_pairs(w):
    w = lax.bitcast_convert_type(w, jnp.uint32)
    lo = lax.bitcast_convert_type(lax.shift_left(w, jnp.uint32(16)), F32)
    hi = lax.bitcast_convert_type(w & jnp.uint32(0xFFFF0000), F32)
    return jnp.concatenate([lo, hi], axis=1)


def _sigmoid(x):
    return 0.5 * jnp.tanh(0.5 * x) + 0.5


def _ada_kernel(c_ref, w_ref, b_ref, o_ref):
    c = c_ref[...]
    s = (c * jax.nn.sigmoid(c)).astype(BF16)
    o_ref[0] = _dot(s, w_ref[0].astype(BF16)) + b_ref[0]


def _ada_call(cond, w_ada, b_ada):
    depth, d, d6 = w_ada.shape
    nr = cond.shape[0]
    bn = d6 // 4
    return pl.pallas_call(
        _ada_kernel,
        grid=(depth, d6 // bn),
        in_specs=[
            pl.BlockSpec((nr, d), lambda l, n: (0, 0)),
            pl.BlockSpec((1, d, bn), lambda l, n: (l, 0, n)),
            pl.BlockSpec((1, 1, bn), lambda l, n: (l, 0, n)),
        ],
        out_specs=pl.BlockSpec((1, nr, bn), lambda l, n: (l, 0, n)),
        out_shape=jax.ShapeDtypeStruct((depth, nr, d6), F32),
        compiler_params=_cparams(("parallel", "parallel")),
        name="ada_mod",
    )(cond, w_ada, b_ada.reshape(depth, 1, d6))


def _to_time_major(x):
    s, t, d = x.shape
    return jnp.swapaxes(x, 0, 1).reshape(s * t, d)


def _from_time_major(x):
    r, d = x.shape
    return jnp.swapaxes(x.reshape(r // SUB, SUB, d), 0, 1)


def _inproj_kernel(n_ctx_tiles, *refs):
    if n_ctx_tiles is not None:
        xp_ref, xs_ref, row_ref, col_ref, mod_ref, g_ref, w_ref, o_ref, x0_ref = refs
        is_ctx = pl.program_id(0) < n_ctx_tiles

        @pl.when(is_ctx)
        def _():
            x0_ref[...] = _to_time_major(xp_ref[0])

        @pl.when(jnp.logical_not(is_ctx))
        def _():
            x = _to_time_major(xs_ref[...])
            half = x.shape[1] // 2
            x0_ref[...] = jnp.concatenate([x[:, :half] + row_ref[0], x[:, half:] + col_ref[...]], axis=1)

        x = x0_ref[...]
    else:
        x_ref, mod_ref, g_ref, w_ref, o_ref = refs
        x = x_ref[...]
    h = _modulate(_rms(x, g_ref[...]), mod_ref[0, 1], mod_ref[0, 0])
    o_ref[...] = _dot(h.astype(BF16), w_ref[...])


def _natural_specs(cfg, t0=0):
    n_ctx_tiles, tiles_per_ctx, _, d = cfg

    def ctx_idx(j):
        jc = jnp.minimum(j + t0, n_ctx_tiles - 1)
        return (jc // tiles_per_ctx, 0, lax.rem(jc, tiles_per_ctx), 0)

    return (pl.BlockSpec((1, SUB, GRID_W, d), ctx_idx),
            pl.BlockSpec((SUB, GRID_W, d), lambda j: (0, jnp.maximum(j + t0 - n_ctx_tiles, 0), 0)))


def _wide(cfg):
    n_ctx_tiles, _, n_tiles, _ = cfg
    return 2 if n_ctx_tiles % 2 == 0 and n_tiles % 2 == 0 else 1


def _moe_parts(cfg):
    n_tiles = cfg[2]
    half = n_tiles // 2
    if _wide(cfg) == 2 and half % 2 == 0:
        return [(0, half), (half, n_tiles - half)]
    return [(0, n_tiles)]


def _mod_spec(cfg, f=1, t0=0):
    n_ctx_steps = cfg[0] // f
    d = cfg[3]
    s0 = t0 // f
    return pl.BlockSpec((1, 6, SUB, d), lambda j: (jnp.where(j + s0 >= n_ctx_steps, 1, 0), 0, 0, 0))


def _inproj_call(cfg, x, mod_l, g, w_in, first=None):
    n_ctx_tiles, _, n_tiles, d = cfg
    d_in = w_in.shape[1]
    f = 1 if first is not None else _wide(cfg)
    tm = TM * f
    row_spec = pl.BlockSpec((tm, d), lambda j: (j, 0))
    out_specs = [pl.BlockSpec((tm, d_in), lambda j: (j, 0))]
    out_shape = [jax.ShapeDtypeStruct((n_tiles * TM, d_in), F32)]
    if first is not None:
        xp_spec, xs_spec = _natural_specs(cfg)
        in_specs = [xp_spec, xs_spec,
                    pl.BlockSpec((1, 1, d // 2), lambda j: (jnp.maximum(j - n_ctx_tiles, 0), 0, 0)),
                    _const_spec((TM, d // 2))]
        args = list(first)
        out_specs.append(row_spec)
        out_shape.append(jax.ShapeDtypeStruct((n_tiles * TM, d), F32))
    else:
        in_specs = [row_spec]
        args = [x]
    in_specs += [_mod_spec(cfg, f), _const_spec((1, d)), _const_spec((d, d_in))]
    args += [mod_l, g.reshape(1, d), w_in]
    return pl.pallas_call(
        functools.partial(_inproj_kernel, n_ctx_tiles if first is not None else None),
        grid=(n_tiles // f,),
        in_specs=in_specs,
        out_specs=out_specs,
        out_shape=out_shape,
        compiler_params=_cparams(("parallel",)),
        name="in_proj",
    )(*args)


def _seq_flags(cfg, j):
    n_ctx_tiles, tiles_per_ctx, n_tiles, _ = cfg
    is_ctx = j < n_ctx_tiles
    pos = lax.rem(j, tiles_per_ctx)
    first = jnp.where(is_ctx, pos == 0, j == n_ctx_tiles)
    last = jnp.where(is_ctx, pos == tiles_per_ctx - 1, j == n_tiles - 1)
    return is_ctx, pos, first, last


def _conv(pad_ref, xa, prev, nxt, first, last, cw, cb):
    lo = CONV_LEFT * SUB
    pad_ref[0:lo, :] = jnp.where(first, 0.0, prev)
    pad_ref[lo:lo + TM, :] = xa
    pad_ref[lo + TM:lo + TM + CONV_RIGHT * SUB, :] = jnp.where(last, 0.0, nxt)
    y = cb
    for k in range(CONV_W):
        y = y + pad_ref[k * SUB:k * SUB + TM, :] * cw[k:k + 1, :]
    return y


def _gates(xc, wg_ref, ba, bx, lam):
    half = xc.shape[1] // 2
    xb = xc.astype(BF16)
    g0 = _dot(xb[:, :half], wg_ref[0])
    g1 = _dot(xb[:, half:], wg_ref[1])
    t_r = jnp.tanh(jnp.concatenate([g0[:, :half], g1[:, :half]], axis=1) + 0.5 * ba)
    t_i = jnp.tanh(jnp.concatenate([g0[:, half:], g1[:, half:]], axis=1) + 0.5 * bx)
    z = -lam
    half_decay = (0.5 * LRU_C) * (jnp.maximum(z, 0.0) + jnp.log1p(jnp.exp(-jnp.abs(z))))
    neg_log_a = t_r * half_decay + half_decay
    a = jnp.exp(-neg_log_a)
    z = jnp.tanh(neg_log_a) * (a * a + 1.0)
    root = jnp.where(z > 0.0, z * lax.rsqrt(z), 0.0)
    half_xc = 0.5 * xc
    u = root * (t_i * half_xc + half_xc)
    return a, u


def _scan(a_ref, u_ref, h_ref, h, reverse):
    steps = TM // SUB

    def body(k, h):
        t = steps - 1 - k if reverse else k
        r0 = pl.multiple_of(t * SUB, SUB)
        h = a_ref[pl.ds(r0, SUB), :] * h + u_ref[pl.ds(r0, SUB), :]
        h_ref[pl.ds(r0, SUB), :] = h
        return h

    return lax.fori_loop(0, steps, body, h, unroll=8)


def _scan_fwd_kernel(cfg, xa_ref, xp_ref, xn_ref, cw_ref, cb_ref, wg_ref, ba_ref, bx_ref,
                     lam_ref, h0_ref, hf_ref, st_ref, pad_ref, a_ref, u_ref, carry_ref):
    j = pl.program_id(0)
    _, _, first, last = _seq_flags(cfg, j)
    xc = _conv(pad_ref, xa_ref[...], xp_ref[...], xn_ref[...], first, last, cw_ref[...], cb_ref[...])
    a, u = _gates(xc, wg_ref, ba_ref[...], bx_ref[...], lam_ref[...])
    a_ref[...] = a
    u_ref[...] = u

    @pl.when(first)
    def _():
        carry_ref[...] = h0_ref[0]

    h = _scan(a_ref, u_ref, hf_ref, carry_ref[...], reverse=False)
    carry_ref[...] = h
    st_ref[0] = h


def _halo_specs(cfg, col, rows_prev, rows_next, tile_of):
    n_tiles = cfg[2]
    c = 512
    nb_prev = TM // rows_prev
    nb_next = TM // rows_next
    last_next = n_tiles * nb_next - 1
    prev = pl.BlockSpec((rows_prev, c), lambda j: (jnp.maximum(tile_of(j) * nb_prev - 1, 0), col))
    nxt = pl.BlockSpec((rows_next, c), lambda j: (jnp.minimum((tile_of(j) + 1) * nb_next, last_next), col))
    return prev, nxt


def _state_spec(cfg, tile_of):
    n_ctx_tiles, tiles_per_ctx, _, _ = cfg
    n_groups = n_ctx_tiles // tiles_per_ctx
    c = 512
    return pl.BlockSpec((1, SUB, c), lambda j: (jnp.minimum(tile_of(j) // tiles_per_ctx, n_groups), 0, 0))


def _h0_spec(cfg, tile_of):
    n_ctx_tiles = cfg[0]
    return pl.BlockSpec((1, SUB, 512), lambda j: (jnp.where(tile_of(j) >= n_ctx_tiles, 1, 0), 0, 0))


def _scan_fwd_call(cfg, proj, conv_w, conv_b, wg, ba, bx, lam, h0):
    n_ctx_tiles, tiles_per_ctx, n_tiles, _ = cfg
    n_groups = n_ctx_tiles // tiles_per_ctx
    c = conv_w.shape[1]
    ident = lambda j: j
    xp_spec, xn_spec = _halo_specs(cfg, 0, CONV_LEFT * SUB, CONV_RIGHT * SUB, ident)
    return pl.pallas_call(
        functools.partial(_scan_fwd_kernel, cfg),
        grid=(n_tiles,),
        in_specs=[
            pl.BlockSpec((TM, c), lambda j: (j, 0)), xp_spec, xn_spec,
            _const_spec((CONV_W, c)), _const_spec((1, c)), _const_spec((2, c // 2, c)),
            _const_spec((1, c)), _const_spec((1, c)), _const_spec((1, c)),
            _h0_spec(cfg, ident),
        ],
        out_specs=[pl.BlockSpec((TM, c), lambda j: (j, 0)), _state_spec(cfg, ident)],
        out_shape=[jax.ShapeDtypeStruct((n_tiles * TM, c), F32),
                   jax.ShapeDtypeStruct((n_groups + 1, SUB, c), F32)],
        scratch_shapes=[
            pltpu.VMEM((TM + (CONV_W - 1) * SUB, c), F32),
            pltpu.VMEM((TM, c), F32), pltpu.VMEM((TM, c), F32), pltpu.VMEM((SUB, c), F32),
        ],
        compiler_params=_cparams(("arbitrary",)),
        name="lru_fwd",
    )(proj, proj, proj, conv_w, conv_b.reshape(1, c), wg, ba.reshape(1, c), bx.reshape(1, c),
      lam.reshape(1, c), h0)


def _pool_inv_counts(ctx_len, c):
    gw = c // len(POOL_WINDOWS)

    def table(t0, t_len):
        t = t0 + jnp.arange(GRID_W)
        cols = []
        for k in POOL_WINDOWS:
            left = k // 2
            right = k - 1 - left
            cnt = jnp.minimum(t + right + 1, t_len) - jnp.maximum(t - left, 0)
            cols.append(jnp.broadcast_to((1.0 / cnt.astype(F32))[:, None], (GRID_W, gw)))
        return jnp.repeat(jnp.concatenate(cols, axis=1), SUB, axis=0)

    return jnp.stack([table(p * GRID_W, ctx_len) for p in range(ctx_len // GRID_W)] + [table(0, GRID_W)])


def _gelu_tanh(x):
    k0 = 0.7978845608028654
    hx = 0.5 * x
    return hx + hx * jnp.tanh(x * (k0 + (k0 * 0.044715) * (x * x)))


def _pool_mix(pad_ref, inv_ref, pw_ref, pb, ps):
    c = pad_ref.shape[1]
    gw = c // len(POOL_WINDOWS)
    outs = []
    for g, k in enumerate(POOL_WINDOWS):
        left = k // 2
        right = k - 1 - left
        lanes = slice(g * gw, (g + 1) * gw)
        s = None
        for o in range(-left, right + 1):
            v = pad_ref[POOL_HALO + o * SUB:POOL_HALO + o * SUB + TM, lanes]
            s = v if s is None else s + v
        outs.append(s * inv_ref[0, :, lanes] - pad_ref[POOL_HALO:POOL_HALO + TM, lanes])
    d = jnp.concatenate(outs, axis=1).astype(BF16)
    half = c // 2
    y = jnp.concatenate([_dot(d[:, :half], pw_ref[0]), _dot(d[:, half:], pw_ref[1])], axis=1)
    return (y + pb) * ps


def _scan_bwd_kernel(cfg, proj_ref, xp_ref, xn_ref, bp_ref, bn_ref, hf_ref, inv_ref, cw_ref, cb_ref,
                     wg_ref, ba_ref, bx_ref, lam_ref, h0_ref, pw_ref, pb_ref, ps_ref,
                     y_ref, st_ref, pad_ref, ppad_ref, a_ref, u_ref, hb_ref, carry_ref):
    n_tiles = cfg[2]
    j = n_tiles - 1 - pl.program_id(0)
    is_ctx, _, first, last = _seq_flags(cfg, j)
    c = hf_ref.shape[1]
    xc = _conv(pad_ref, proj_ref[:, 0:c], xp_ref[...], xn_ref[...], first, last, cw_ref[...], cb_ref[...])
    a, u = _gates(xc, wg_ref, ba_ref[...], bx_ref[...], lam_ref[...])
    a_ref[...] = a
    u_ref[...] = u

    @pl.when(last)
    def _():
        carry_ref[...] = h0_ref[0]

    h = _scan(a_ref, u_ref, hb_ref, carry_ref[...], reverse=True)
    carry_ref[...] = h
    st_ref[0] = h

    ga = proj_ref[:, c:2 * c]
    y_a = (hf_ref[...] + hb_ref[...]) * _gelu_tanh(ga)

    use_prev = jnp.logical_and(is_ctx, jnp.logical_not(first))
    use_next = jnp.logical_and(is_ctx, jnp.logical_not(last))
    ppad_ref[0:POOL_HALO, :] = jnp.where(use_prev, bp_ref[...], 0.0)
    ppad_ref[POOL_HALO:POOL_HALO + TM, :] = proj_ref[:, 2 * c:3 * c]
    ppad_ref[POOL_HALO + TM:, :] = jnp.where(use_next, bn_ref[...], 0.0)
    y_b = _pool_mix(ppad_ref, inv_ref, pw_ref, pb_ref[...], ps_ref[...])
    y_ref[...] = jnp.concatenate([y_a, y_b], axis=1).astype(BF16)


def _scan_bwd_call(cfg, inv_cnt, proj, hf, conv_w, conv_b, wg, ba, bx, lam, h0, pw, pb, ps):
    n_ctx_tiles, tiles_per_ctx, n_tiles, d = cfg
    n_groups = n_ctx_tiles // tiles_per_ctx
    c = conv_w.shape[1]
    rev = lambda j: n_tiles - 1 - j
    xp_spec, xn_spec = _halo_specs(cfg, 0, CONV_LEFT * SUB, CONV_RIGHT * SUB, rev)
    bp_spec, bn_spec = _halo_specs(cfg, 2, POOL_HALO, POOL_HALO, rev)

    def inv_idx(j):
        jj = rev(j)
        return (jnp.where(jj < n_ctx_tiles, lax.rem(jj, tiles_per_ctx), tiles_per_ctx), 0, 0)

    return pl.pallas_call(
        functools.partial(_scan_bwd_kernel, cfg),
        grid=(n_tiles,),
        in_specs=[
            pl.BlockSpec((TM, 3 * c), lambda j: (rev(j), 0)), xp_spec, xn_spec, bp_spec, bn_spec,
            pl.BlockSpec((TM, c), lambda j: (rev(j), 0)),
            pl.BlockSpec((1, TM, c), inv_idx),
            _const_spec((CONV_W, c)), _const_spec((1, c)), _const_spec((2, c // 2, c)),
            _const_spec((1, c)), _const_spec((1, c)), _const_spec((1, c)),
            _h0_spec(cfg, rev),
            _const_spec((2, c // 2, c // 2)), _const_spec((1, c)), _const_spec((1, c)),
        ],
        out_specs=[pl.BlockSpec((TM, d), lambda j: (rev(j), 0)), _state_spec(cfg, rev)],
        out_shape=[jax.ShapeDtypeStruct((n_tiles * TM, d), BF16),
                   jax.ShapeDtypeStruct((n_groups + 1, SUB, c), F32)],
        scratch_shapes=[
            pltpu.VMEM((TM + (CONV_W - 1) * SUB, c), F32),
            pltpu.VMEM((TM + 2 * POOL_HALO, c), F32),
            pltpu.VMEM((TM, c), F32), pltpu.VMEM((TM, c), F32), pltpu.VMEM((TM, c), F32),
            pltpu.VMEM((SUB, c), F32),
        ],
        compiler_params=_cparams(("arbitrary",)),
        name="lru_bwd_mix",
    )(proj, proj, proj, proj, proj, hf, inv_cnt, conv_w, conv_b.reshape(1, c), wg, ba.reshape(1, c),
      bx.reshape(1, c), lam.reshape(1, c), h0, pw, pb.reshape(1, c), ps.reshape(1, c))


def _out_proj(x_ref, ym_ref, mod_ref, wo_ref, g2_ref):
    xm = _gated_add(x_ref[...], mod_ref[0, 2], _dot(ym_ref[...], wo_ref[...]))
    h2 = _modulate(_rms(xm, g2_ref[...]), mod_ref[0, 4], mod_ref[0, 3])
    return xm, h2


def _swiglu_act(h2b, wg_ref, wu_ref, act_ref):
    n_chunks = act_ref.shape[1] // FF_CHUNK

    def body(f, carry):
        c0 = pl.multiple_of(f * FF_CHUNK, FF_CHUNK)
        g = _dot(h2b, wg_ref[:, pl.ds(c0, FF_CHUNK)])
        u = _dot(h2b, wu_ref[:, pl.ds(c0, FF_CHUNK)])
        act_ref[:, pl.ds(c0, FF_CHUNK)] = (g * _sigmoid(g) * u).astype(BF16)
        return carry

    lax.fori_loop(0, n_chunks, body, 0, unroll=True)


def _dense_ffn_kernel(x_ref, ym_ref, mod_ref, wo_ref, g2_ref, wg_ref, wu_ref, wd_ref, o_ref, act_ref):
    xm, h2 = _out_proj(x_ref, ym_ref, mod_ref, wo_ref, g2_ref)
    _swiglu_act(h2.astype(BF16), wg_ref, wu_ref, act_ref)
    o_ref[...] = _gated_add(xm, mod_ref[0, 5], _dot(act_ref[...], wd_ref[...]))


def _dense_ffn_call(cfg, x, ymix, mod_l, w_out, g2, wg, wu, wd):
    _, _, n_tiles, d = cfg
    d_ff = wg.shape[1]
    f = _wide(cfg)
    row_spec = pl.BlockSpec((TM * f, d), lambda j: (j, 0))
    return pl.pallas_call(
        _dense_ffn_kernel,
        grid=(n_tiles // f,),
        in_specs=[row_spec, row_spec, _mod_spec(cfg, f), _const_spec((d, d)), _const_spec((1, d)),
                  _const_spec((d, d_ff)), _const_spec((d, d_ff)), _const_spec((d_ff, d))],
        out_specs=row_spec,
        out_shape=jax.ShapeDtypeStruct((n_tiles * TM, d), F32),
        scratch_shapes=[pltpu.VMEM((TM * f, d_ff), BF16)],
        compiler_params=_cparams(("parallel",)),
        name="out_proj_dense_ffn",
    )(x, ymix, mod_l, w_out, g2.reshape(1, d), wg, wu, wd)


M_E1, M_E2, M_W1, M_W2, M_R1, M_R2 = range(6)


def _route_kernel(x_ref, ym_ref, mod_ref, wo_ref, g2_ref, rw_ref, rb_ref, tri_ref,
                  xm_ref, hp_ref, meta_ref, mt_ref, cnt_ref, run_ref):
    @pl.when(pl.program_id(0) == 0)
    def _():
        run_ref[...] = jnp.zeros_like(run_ref)

    run = run_ref[...]
    for r0 in range(0, x_ref.shape[0], TM):
        rows = pl.ds(r0, TM)
        xm = _gated_add(x_ref[rows, :], mod_ref[0, 2], _dot(ym_ref[rows, :], wo_ref[...]))
        h2 = _modulate(_rms(xm, g2_ref[...]), mod_ref[0, 4], mod_ref[0, 3])
        xm_ref[rows, :] = xm
        h_hi = h2.astype(BF16)
        h_hi32 = h_hi.astype(F32)
        hp_ref[rows, :] = _pack_pairs(h_hi32)

        h_lo = (h2 - h_hi32).astype(BF16)
        p = _dot(h_hi, rw_ref[...])
        logits = p[:, :LANES] + p[:, LANES:] + _dot(h_lo, rw_ref[:, :LANES])
        lane = lax.broadcasted_iota(jnp.int32, logits.shape, 1)
        neg = jnp.float32(-jnp.inf)
        lg = jnp.where(lane < N_EXPERTS, logits + rb_ref[...], neg)
        m1 = jnp.max(lg, axis=1, keepdims=True)
        i1 = jnp.min(jnp.where(lg == m1, lane, LANES), axis=1, keepdims=True)
        lg2 = jnp.where(lane == i1, neg, lg)
        m2 = jnp.max(lg2, axis=1, keepdims=True)
        i2 = jnp.min(jnp.where(lg2 == m2, lane, LANES), axis=1, keepdims=True)
        e2 = jnp.exp(m2 - m1)
        den = 1.0 + e2

        sel1 = lane == i1
        sel2 = lane == i2
        onehot = jnp.where(jnp.logical_or(sel1, sel2), 1.0, 0.0)
        rank = _dot(tri_ref[...], onehot.astype(BF16)) + run
        r1 = jnp.sum(jnp.where(sel1, rank, 0.0), axis=1, keepdims=True)
        r2 = jnp.sum(jnp.where(sel2, rank, 0.0), axis=1, keepdims=True)
        run = run + jnp.sum(onehot, axis=0, keepdims=True)

        meta = jnp.zeros(logits.shape, F32)
        for k, v in ((M_E1, i1.astype(F32)), (M_E2, i2.astype(F32)), (M_W1, 1.0 / den), (M_W2, e2 / den),
                     (M_R1, r1), (M_R2, r2)):
            meta = jnp.where(lane == k, v, meta)
        meta_ref[rows, :] = meta
        mt_ref[0, :, rows] = jnp.transpose(meta)[:SUB]

    run_ref[...] = run
    cnt_ref[...] = run


def _route_call(cfg, part, x, ymix, mod_l, w_out, g2, router_w, router_b):
    d = cfg[3]
    t0, nt = part
    n = nt * TM
    f = _wide(cfg)
    tm = TM * f
    s0 = t0 // f
    in_row_spec = pl.BlockSpec((tm, d), lambda j: (j + s0, 0))
    row_spec = pl.BlockSpec((tm, d), lambda j: (j, 0))
    rw = jnp.zeros((d, LANES), F32).at[:, :N_EXPERTS].set(router_w)
    rw_hi = rw.astype(BF16)
    rw = jnp.concatenate([rw_hi, (rw - rw_hi.astype(F32)).astype(BF16)], axis=1)
    rb = jnp.zeros((1, LANES), F32).at[0, :N_EXPERTS].set(router_b)
    tri = jnp.tril(jnp.ones((TM, TM), BF16), -1)
    return pl.pallas_call(
        _route_kernel,
        grid=(nt // f,),
        in_specs=[in_row_spec, in_row_spec, _mod_spec(cfg, f, t0), _const_spec((d, d)), _const_spec((1, d)),
                  _const_spec((d, 2 * LANES)), _const_spec((1, LANES)), _const_spec((TM, TM))],
        out_specs=[row_spec, pl.BlockSpec((tm, d // 2), lambda j: (j, 0)),
                   pl.BlockSpec((tm, LANES), lambda j: (j, 0)), pl.BlockSpec((1, SUB, tm), lambda j: (j, 0, 0)),
                   pl.BlockSpec((1, LANES), lambda j: (0, 0))],
        out_shape=[jax.ShapeDtypeStruct((n, d), F32), jax.ShapeDtypeStruct((n, d // 2), jnp.int32),
                   jax.ShapeDtypeStruct((n, LANES), F32), jax.ShapeDtypeStruct((n // tm, SUB, tm), F32),
                   jax.ShapeDtypeStruct((1, LANES), F32)],
        scratch_shapes=[pltpu.VMEM((1, LANES), F32)],
        compiler_params=_cparams(("arbitrary",)),
        name="out_proj_route",
    )(x, ymix, mod_l, w_out, g2.reshape(1, d), rw, rb, tri)


def _routing_tables(n_tiles, mt, cnt):
    counts = cnt[0, :N_EXPERTS].astype(jnp.int32)
    padded = ((counts + TM - 1) // TM) * TM
    ends = jnp.cumsum(padded)
    offs = ends - padded
    row = lambda k: mt[:, k, :].reshape(-1).astype(jnp.int32)
    pos1 = offs[row(M_E1)] + row(M_R1)
    pos2 = offs[row(M_E2)] + row(M_R2)
    n_sorted_tiles = 2 * n_tiles + N_EXPERTS
    starts = jnp.arange(n_sorted_tiles, dtype=jnp.int32) * TM
    tile_e = jnp.minimum(jnp.sum((starts[:, None] >= ends[None, :]).astype(jnp.int32), axis=1), N_EXPERTS - 1)
    n_active = (ends[-1] // TM).reshape(1)
    return jnp.concatenate([pos1, pos2]), tile_e, n_active


SC_CORES = 2
SC_SUBCORES = 16
SC_CHUNK = 64


def _sc_mesh():
    return plsc.VectorSubcoreMesh(core_axis_name="c", subcore_axis_name="s",
                                  num_cores=SC_CORES, num_subcores=SC_SUBCORES)


def _sc_scatter_rows(rows, idx, n_out):
    n_src, width = rows.shape
    n_idx = idx.shape[0]
    n_workers = SC_CORES * SC_SUBCORES
    per_w = n_idx // n_workers
    chunks = per_w // SC_CHUNK
    assert n_idx % (n_workers * SC_CHUNK) == 0 and n_src % per_w == 0
    idx3 = idx.reshape(n_workers, chunks, SC_CHUNK)

    def body(rows_hbm, idx_hbm, out_hbm, idx_v, rows_v, sem):
        wid = lax.axis_index("s") * SC_CORES + lax.axis_index("c")
        src_base = lax.rem(wid * per_w, n_src)
        pltpu.sync_copy(idx_hbm.at[wid], idx_v)

        @pl.loop(0, chunks)
        def _(i):
            off = pl.multiple_of(i * SC_CHUNK, SC_CHUNK)
            pltpu.sync_copy(rows_hbm.at[pl.ds(src_base + off, SC_CHUNK)], rows_v)
            pltpu.async_copy(rows_v, out_hbm.at[idx_v.at[i]], sem).wait()

    return pl.kernel(
        body,
        out_type=jax.ShapeDtypeStruct((n_out, width), rows.dtype),
        mesh=_sc_mesh(),
        scratch_types=[pltpu.VMEM((chunks, SC_CHUNK), jnp.int32), pltpu.VMEM((SC_CHUNK, width), rows.dtype),
                       pltpu.SemaphoreType.DMA],
        name="sc_scatter_rows",
    )(rows, idx3)


def _expert_kernel(te_ref, na_ref, s_ref, wgu_ref, wd_ref, o_ref):
    del te_ref

    @pl.when(pl.program_id(0) < na_ref[0])
    def _():
        h = _unpack_pairs(s_ref[...]).astype(BF16)
        gu = _dot(h, wgu_ref[0, 0])
        dfe = gu.shape[1] // 2
        g = gu[:, :dfe]
        act = (g * _sigmoid(g) * gu[:, dfe:]).astype(BF16)
        o_ref[...] = _pack_pairs(_dot(act, wd_ref[0, 0]).astype(BF16).astype(F32))

    @pl.when(pl.program_id(0) >= na_ref[0])
    def _():
        o_ref[...] = jnp.zeros_like(o_ref)


def _expert_call(cfg, tile_e, n_active, buf, wgu, wd, layer):
    d = cfg[3]
    dfe = wd.shape[2]
    n_sorted_tiles = buf.shape[0] // TM
    last = lambda t, na: jnp.maximum(jnp.minimum(t, na[0] - 1), 0)
    tile = lambda t, te, na: (last(t, na), 0)
    w_spec = lambda shape: pl.BlockSpec((1,) + shape, lambda t, te, na: (layer, te[last(t, na)], 0, 0))
    return pl.pallas_call(
        _expert_kernel,
        grid_spec=pltpu.PrefetchScalarGridSpec(
            num_scalar_prefetch=2,
            grid=(n_sorted_tiles,),
            in_specs=[pl.BlockSpec((TM, d // 2), tile), w_spec((1, d, 2 * dfe)), w_spec((1, dfe, d))],
            out_specs=pl.BlockSpec((TM, d // 2), lambda t, te, na: (t, 0)),
        ),
        out_shape=jax.ShapeDtypeStruct((n_sorted_tiles * TM, d // 2), jnp.int32),
        compiler_params=_cparams(("arbitrary",)),
        name="moe_experts",
    )(tile_e, n_active, buf, wgu, wd)


def _store_natural(n_ctx_tiles, part, y, o_refs):
    t0, nt = part
    has_ctx = t0 < n_ctx_tiles
    has_lat = t0 + nt > n_ctx_tiles
    is_ctx = pl.program_id(0) + t0 < n_ctx_tiles
    if has_ctx:
        @pl.when(is_ctx)
        def _():
            o_refs[0][0] = _from_time_major(y)

    if has_lat:
        @pl.when(jnp.logical_not(is_ctx))
        def _():
            o_refs[-1][...] = _from_time_major(y)


def _natural_out(cfg, part):
    n_ctx_tiles, tiles_per_ctx, n_tiles, d = cfg
    n_groups = n_ctx_tiles // tiles_per_ctx
    t0, nt = part
    specs = _natural_specs(cfg, t0)
    shapes = (jax.ShapeDtypeStruct((n_groups, SUB, tiles_per_ctx * GRID_W, d), F32),
              jax.ShapeDtypeStruct((SUB, (n_tiles - n_ctx_tiles) * GRID_W, d), F32))
    keep = [k for k, used in enumerate((t0 < n_ctx_tiles, t0 + nt > n_ctx_tiles)) if used]
    return [specs[k] for k in keep], [shapes[k] for k in keep], keep


def _sc_gather_rows(table, idx):
    n_rows = idx.shape[0]
    width = table.shape[1]
    n_workers = SC_CORES * SC_SUBCORES
    assert n_rows % (n_workers * SC_CHUNK) == 0
    per_w = n_rows // n_workers
    mesh = _sc_mesh()

    def body(table_hbm, idx_hbm, out_hbm, idx_v, rows_v, sem):
        wid = lax.axis_index("s") * SC_CORES + lax.axis_index("c")
        base = wid * per_w
        pltpu.sync_copy(idx_hbm.at[pl.ds(base, per_w)], idx_v)

        @pl.loop(0, per_w // SC_CHUNK)
        def _(i):
            off = pl.multiple_of(i * SC_CHUNK, SC_CHUNK)
            pltpu.async_copy(table_hbm.at[idx_v.at[pl.ds(off, SC_CHUNK)]], rows_v, sem).wait()
            pltpu.sync_copy(rows_v, out_hbm.at[pl.ds(base + off, SC_CHUNK)])

    return pl.kernel(
        body,
        out_type=jax.ShapeDtypeStruct((n_rows, width), table.dtype),
        mesh=mesh,
        scratch_types=[pltpu.VMEM((per_w,), jnp.int32), pltpu.VMEM((SC_CHUNK, width), table.dtype),
                       pltpu.SemaphoreType.DMA],
        name="sc_gather_rows",
    )(table, idx)


def _combine_kernel(final_ctx_tiles, part, n_filled, xm_ref, y1_ref, y2_ref, meta_ref, mod_ref, gfin_ref,
                    *refs):
    o_refs = refs[n_filled:]
    meta = meta_ref[...]
    y = (meta[:, M_W1:M_W1 + 1] * _unpack_pairs(y1_ref[...])
         + meta[:, M_W2:M_W2 + 1] * _unpack_pairs(y2_ref[...]))
    x = _gated_add(xm_ref[...], mod_ref[0, 5], y)
    if final_ctx_tiles is None:
        o_refs[0][...] = x
    else:
        _store_natural(final_ctx_tiles, part, _rms(x, gfin_ref[...]), o_refs)


def _combine_call(cfg, part, xm, yg, meta, mod_l, g_final, final, filled):
    n_ctx_tiles, _, n_tiles, d = cfg
    t0, nt = part
    row_spec = pl.BlockSpec((TM, d), lambda j: (j, 0))
    if final:
        out_specs, out_shape, keys = _natural_out(cfg, part)
    else:
        out_specs = [pl.BlockSpec((TM, d), lambda j: (j + t0, 0))]
        out_shape, keys = [jax.ShapeDtypeStruct((n_tiles * TM, d), F32)], [0]
    reuse = [k for k in keys if k in filled]
    base = [xm, yg, yg, meta, mod_l, g_final.reshape(1, d)]
    outs = pl.pallas_call(
        functools.partial(_combine_kernel, n_ctx_tiles if final else None, part, len(reuse)),
        grid=(nt,),
        in_specs=[row_spec, pl.BlockSpec((TM, d // 2), lambda j: (j, 0)),
                  pl.BlockSpec((TM, d // 2), lambda j: (nt + j, 0)),
                  pl.BlockSpec((TM, LANES), lambda j: (j, 0)), _mod_spec(cfg, 1, t0), _const_spec((1, d))]
                 + [pl.BlockSpec(memory_space=pl.ANY)] * len(reuse),
        out_specs=out_specs,
        out_shape=out_shape,
        input_output_aliases={len(base) + i: keys.index(k) for i, k in enumerate(reuse)},
        compiler_params=_cparams(("arbitrary",)),
        name="moe_combine",
    )(*base, *[filled[k] for k in reuse])
    return {**filled, **dict(zip(keys, outs))}


def _final_norm_kernel(n_ctx_tiles, part, x_ref, g_ref, *o_refs):
    _store_natural(n_ctx_tiles, part, _rms(x_ref[...], g_ref[...]), o_refs)


def _final_norm_call(cfg, x, g):
    n_ctx_tiles, _, n_tiles, d = cfg
    out_specs, out_shape, _ = _natural_out(cfg, (0, n_tiles))
    return pl.pallas_call(
        functools.partial(_final_norm_kernel, n_ctx_tiles, (0, n_tiles)),
        grid=(n_tiles,),
        in_specs=[pl.BlockSpec((TM, d), lambda j: (j, 0)), _const_spec((1, d))],
        out_specs=out_specs,
        out_shape=out_shape,
        compiler_params=_cparams(("arbitrary",)),
        name="final_norm",
    )(x, g.reshape(1, d))


def _block_diag(w, per_block):
    *lead, n, k, _ = w.shape
    nb = n // per_block
    w = w.reshape(*lead, nb, per_block, k, k)
    eye = jnp.eye(per_block, dtype=w.dtype)
    out = w[..., :, :, None, :] * eye[:, None, :, None]
    return out.reshape(*lead, nb, per_block * k, per_block * k)


def _pos_tables(n_ctx_tiles, n_lat_tiles, d):
    quarter = d // 4
    omega = 1.0 / (POS_BASE ** (jnp.arange(quarter, dtype=F32) / quarter))
    er = jnp.arange(n_lat_tiles, dtype=F32)[:, None] * omega
    ec = jnp.arange(GRID_W, dtype=F32)[:, None] * omega
    row_emb = jnp.concatenate([jnp.sin(er), jnp.cos(er)], axis=-1)
    col_emb = jnp.concatenate([jnp.sin(ec), jnp.cos(ec)], axis=-1)
    del n_ctx_tiles
    return row_emb[:, None, :], jnp.repeat(col_emb, SUB, axis=0)


def kernel(x_prompt, x_sample, state_lru, c, c_ctx, norm_mix_g, w_ada, b_ada, w_in, conv_w, conv_b, lru_wa, lru_ba, lru_wx, lru_bx, lru_lam, pool_w, pool_b, pool_scale, w_out, norm_ffn_g, ffn_wg, ffn_wu, ffn_wd, moe_router_w, moe_router_b, moe_wg, moe_wu, moe_wd, norm_final_g):
    bc, tc, d = x_prompt.shape
    bl, tl, _ = x_sample.shape
    depth = w_ada.shape[0]
    c_lru = conv_w.shape[-1]
    assert bl == SUB and bc % SUB == 0 and tc % GRID_W == 0 and tl % GRID_W == 0
    assert c_lru == 4 * LANES and d == 2 * c_lru
    n_groups = bc // SUB
    tiles_per_ctx = tc // GRID_W
    n_ctx_tiles = n_groups * tiles_per_ctx
    n_lat_tiles = tl // GRID_W
    n_tiles = n_ctx_tiles + n_lat_tiles
    n_ctx = n_ctx_tiles * TM
    cfg = (n_ctx_tiles, tiles_per_ctx, n_tiles, d)

    cond = jnp.concatenate([c, c_ctx[None], jnp.zeros((SUB - 1, d), F32)], axis=0)
    mod = _ada_call(cond, w_ada, b_ada)
    mod_lat = mod[:, :SUB].reshape(depth, SUB, 6, d).transpose(0, 2, 1, 3)
    mod_ctx = jnp.broadcast_to(mod[:, SUB].reshape(depth, 6, 1, d), (depth, 6, SUB, d))
    mod = jnp.stack([mod_ctx, mod_lat], axis=1)

    row_tab, col_rep = _pos_tables(n_ctx_tiles, n_lat_tiles, d)
    first = (x_prompt.reshape(n_groups, SUB, tc, d), x_sample, row_tab, col_rep)
    x = None
    wg_lru = (0.5 * jnp.concatenate([_block_diag(lru_wa, 4), _block_diag(lru_wx, 4)], axis=-1)).astype(BF16)
    pw = _block_diag(pool_w, 2).astype(BF16)
    inv_cnt = _pool_inv_counts(tc, c_lru)
    moe_wgu = jnp.concatenate([moe_wg.astype(BF16), moe_wu.astype(BF16)], axis=3)
    moe_wd_b = moe_wd.astype(BF16)
    h0 = jnp.concatenate([jnp.zeros_like(state_lru[None]), state_lru[None]], axis=0)

    states = []
    for l in range(depth):
        jdx = l // 2
        if l == 0:
            proj, x = _inproj_call(cfg, x, mod[l], norm_mix_g[l], w_in[l].astype(BF16), first)
        else:
            (proj,) = _inproj_call(cfg, x, mod[l], norm_mix_g[l], w_in[l].astype(BF16))
        hf, st_f = _scan_fwd_call(cfg, proj, conv_w[l], conv_b[l], wg_lru[l, 0], lru_ba[l, 0],
                                  lru_bx[l, 0], lru_lam[l, 0], h0[:, :, l, 0])
        ymix, st_b = _scan_bwd_call(cfg, inv_cnt, proj, hf, conv_w[l], conv_b[l], wg_lru[l, 1], lru_ba[l, 1],
                                    lru_bx[l, 1], lru_lam[l, 1], h0[:, :, l, 1], pw[l], pool_b[l],
                                    pool_scale[l])
        states.append(jnp.stack([st_f[:n_groups].reshape(bc, c_lru), st_b[:n_groups].reshape(bc, c_lru)], axis=1))
        if l % 2 == 0:
            x = _dense_ffn_call(cfg, x, ymix, mod[l], w_out[l].astype(BF16), norm_ffn_g[l],
                                ffn_wg[jdx].astype(BF16), ffn_wu[jdx].astype(BF16), ffn_wd[jdx].astype(BF16))
        else:
            final = l == depth - 1
            w_out_b = w_out[l].astype(BF16)
            filled = {}
            for part in _moe_parts(cfg):
                xm, hp, meta, mt, cnt = _route_call(cfg, part, x, ymix, mod[l], w_out_b, norm_ffn_g[l],
                                                    moe_router_w[jdx], moe_router_b[jdx])
                pos_flat, tile_e, n_active = _routing_tables(part[1], mt, cnt)
                buf = _sc_scatter_rows(hp, pos_flat, (2 * part[1] + N_EXPERTS) * TM)
                ys = _expert_call(cfg, tile_e, n_active, buf, moe_wgu, moe_wd_b, jdx)
                yg = _sc_gather_rows(ys, pos_flat)
                filled = _combine_call(cfg, part, xm, yg, meta, mod[l], norm_final_g, final, filled)
            x = (filled[0], filled[1]) if final else filled[0]

    y_prompt, y_sample = x if depth % 2 == 0 else _final_norm_call(cfg, x, norm_final_g)
    new_state = jnp.stack(states, axis=1)
    return (y_prompt.reshape(bc, tc, d), y_sample, new_state)
```

```python
import functools

import jax
import jax.numpy as jnp
from jax import lax
from jax.experimental import pallas as pl
from jax.experimental.pallas import tpu as pltpu
from jax.experimental.pallas import tpu_sc as plsc

F32 = jnp.float32
BF16 = jnp.bfloat16

SUB = 8
LANES = 128
GRID_W = 64
TM = GRID_W * SUB
POS_BASE = 10000.0
N_LRU_HEADS = 8
CONV_W = 4
CONV_LEFT = CONV_W // 2
CONV_RIGHT = CONV_W - 1 - CONV_LEFT
LRU_C = 8.0
POOL_WINDOWS = (2, 4, 8, 16)
POOL_HALO = 8 * SUB
N_EXPERTS = 8
EPS = 1e-6
FF_CHUNK = 256
VMEM_LIMIT = 56 * 1024 * 1024


def _cparams(sem):
    return pltpu.CompilerParams(dimension_semantics=sem, vmem_limit_bytes=VMEM_LIMIT)


def _const_spec(shape):
    nd = len(shape)
    return pl.BlockSpec(shape, lambda *_: (0,) * nd, pipeline_mode=pl.Buffered(1))


def _rms(x, g):
    ms = jnp.mean(x * x, axis=-1, keepdims=True)
    return x * lax.rsqrt(ms + EPS) * g


def _per_seq(x, v, op):
    r, c = x.shape
    x3 = x.reshape(r // SUB, SUB, c)
    return op(x3, v[None]).reshape(r, c)


def _modulate(h, scale, shift):
    r, c = h.shape
    h3 = h.reshape(r // SUB, SUB, c)
    return (h3 * (1.0 + scale)[None] + shift[None]).reshape(r, c)


def _gated_add(x, gate, y):
    return x + _per_seq(y, gate, lambda a, b: a * b)


def _dot(a, b):
    return jnp.dot(a, b, preferred_element_type=F32)


def _pack_pairs(x):
    bits = lax.bitcast_convert_type(x, jnp.uint32)
    half = bits.shape[1] // 2
    w = lax.shift_right_logical(bits[:, :half], jnp.uint32(16)) | (bits[:, half:] & jnp.uint32(0xFFFF0000))
    return lax.bitcast_convert_type(w, jnp.int32)


def _unpack_pairs(w):
    w = lax.bitcast_convert_type(w, jnp.uint32)
    lo = lax.bitcast_convert_type(lax.shift_left(w, jnp.uint32(16)), F32)
    hi = lax.bitcast_convert_type(w & jnp.uint32(0xFFFF0000), F32)
    return jnp.concatenate([lo, hi], axis=1)


def _sigmoid(x):
    return 0.5 * jnp.tanh(0.5 * x) + 0.5


def _ada_kernel(c_ref, w_ref, b_ref, o_ref):
    c = c_ref[...]
    s = (c * jax.nn.sigmoid(c)).astype(BF16)
    o_ref[0] = _dot(s, w_ref[0].astype(BF16)) + b_ref[0]


def _ada_call(cond, w_ada, b_ada):
    depth, d, d6 = w_ada.shape
    nr = cond.shape[0]
    bn = d6 // 4
    return pl.pallas_call(
        _ada_kernel,
        grid=(depth, d6 // bn),
        in_specs=[
            pl.BlockSpec((nr, d), lambda l, n: (0, 0)),
            pl.BlockSpec((1, d, bn), lambda l, n: (l, 0, n)),
            pl.BlockSpec((1, 1, bn), lambda l, n: (l, 0, n)),
        ],
        out_specs=pl.BlockSpec((1, nr, bn), lambda l, n: (l, 0, n)),
        out_shape=jax.ShapeDtypeStruct((depth, nr, d6), F32),
        compiler_params=_cparams(("parallel", "parallel")),
        name="ada_mod",
    )(cond, w_ada, b_ada.reshape(depth, 1, d6))


def _to_time_major(x):
    s, t, d = x.shape
    return jnp.swapaxes(x, 0, 1).reshape(s * t, d)


def _from_time_major(x):
    r, d = x.shape
    return jnp.swapaxes(x.reshape(r // SUB, SUB, d), 0, 1)


def _inproj_kernel(n_ctx_tiles, *refs):
    if n_ctx_tiles is not None:
        xp_ref, xs_ref, row_ref, col_ref, mod_ref, g_ref, w_ref, o_ref, x0_ref = refs
        is_ctx = pl.program_id(0) < n_ctx_tiles

        @pl.when(is_ctx)
        def _():
            x0_ref[...] = _to_time_major(xp_ref[0])

        @pl.when(jnp.logical_not(is_ctx))
        def _():
            x = _to_time_major(xs_ref[...])
            half = x.shape[1] // 2
            x0_ref[...] = jnp.concatenate([x[:, :half] + row_ref[0], x[:, half:] + col_ref[...]], axis=1)

        x = x0_ref[...]
    else:
        x_ref, mod_ref, g_ref, w_ref, o_ref = refs
        x = x_ref[...]
    h = _modulate(_rms(x, g_ref[...]), mod_ref[0, 1], mod_ref[0, 0])
    o_ref[...] = _dot(h.astype(BF16), w_ref[...])


def _natural_specs(cfg, t0=0):
    n_ctx_tiles, tiles_per_ctx, _, d = cfg

    def ctx_idx(j):
        jc = jnp.minimum(j + t0, n_ctx_tiles - 1)
        return (jc // tiles_per_ctx, 0, lax.rem(jc, tiles_per_ctx), 0)

    return (pl.BlockSpec((1, SUB, GRID_W, d), ctx_idx),
            pl.BlockSpec((SUB, GRID_W, d), lambda j: (0, jnp.maximum(j + t0 - n_ctx_tiles, 0), 0)))


def _wide(cfg):
    n_ctx_tiles, _, n_tiles, _ = cfg
    return 2 if n_ctx_tiles % 2 == 0 and n_tiles % 2 == 0 else 1


def _moe_parts(cfg):
    n_tiles = cfg[2]
    half = n_tiles // 2
    if _wide(cfg) == 2 and half % 2 == 0:
        return [(0, half), (half, n_tiles - half)]
    return [(0, n_tiles)]


def _mod_spec(cfg, f=1, t0=0):
    n_ctx_steps = cfg[0] // f
    d = cfg[3]
    s0 = t0 // f
    return pl.BlockSpec((1, 6, SUB, d), lambda j: (jnp.where(j + s0 >= n_ctx_steps, 1, 0), 0, 0, 0))


def _inproj_call(cfg, x, mod_l, g, w_in, first=None):
    n_ctx_tiles, _, n_tiles, d = cfg
    d_in = w_in.shape[1]
    f = 1 if first is not None else _wide(cfg)
    tm = TM * f
    row_spec = pl.BlockSpec((tm, d), lambda j: (j, 0))
    out_specs = [pl.BlockSpec((tm, d_in), lambda j: (j, 0))]
    out_shape = [jax.ShapeDtypeStruct((n_tiles * TM, d_in), F32)]
    if first is not None:
        xp_spec, xs_spec = _natural_specs(cfg)
        in_specs = [xp_spec, xs_spec,
                    pl.BlockSpec((1, 1, d // 2), lambda j: (jnp.maximum(j - n_ctx_tiles, 0), 0, 0)),
                    _const_spec((TM, d // 2))]
        args = list(first)
        out_specs.append(row_spec)
        out_shape.append(jax.ShapeDtypeStruct((n_tiles * TM, d), F32))
    else:
        in_specs = [row_spec]
        args = [x]
    in_specs += [_mod_spec(cfg, f), _const_spec((1, d)), _const_spec((d, d_in))]
    args += [mod_l, g.reshape(1, d), w_in]
    return pl.pallas_call(
        functools.partial(_inproj_kernel, n_ctx_tiles if first is not None else None),
        grid=(n_tiles // f,),
        in_specs=in_specs,
        out_specs=out_specs,
        out_shape=out_shape,
        compiler_params=_cparams(("parallel",)),
        name="in_proj",
    )(*args)


def _seq_flags(cfg, j):
    n_ctx_tiles, tiles_per_ctx, n_tiles, _ = cfg
    is_ctx = j < n_ctx_tiles
    pos = lax.rem(j, tiles_per_ctx)
    first = jnp.where(is_ctx, pos == 0, j == n_ctx_tiles)
    last = jnp.where(is_ctx, pos == tiles_per_ctx - 1, j == n_tiles - 1)
    return is_ctx, pos, first, last


def _conv(pad_ref, xa, prev, nxt, first, last, cw, cb):
    lo = CONV_LEFT * SUB
    pad_ref[0:lo, :] = jnp.where(first, 0.0, prev)
    pad_ref[lo:lo + TM, :] = xa
    pad_ref[lo + TM:lo + TM + CONV_RIGHT * SUB, :] = jnp.where(last, 0.0, nxt)
    y = cb
    for k in range(CONV_W):
        y = y + pad_ref[k * SUB:k * SUB + TM, :] * cw[k:k + 1, :]
    return y


def _gates(xc, wg_ref, ba, bx, lam):
    half = xc.shape[1] // 2
    xb = xc.astype(BF16)
    g0 = _dot(xb[:, :half], wg_ref[0])
    g1 = _dot(xb[:, half:], wg_ref[1])
    t_r = jnp.tanh(jnp.concatenate([g0[:, :half], g1[:, :half]], axis=1) + 0.5 * ba)
    t_i = jnp.tanh(jnp.concatenate([g0[:, half:], g1[:, half:]], axis=1) + 0.5 * bx)
    z = -lam
    half_decay = (0.5 * LRU_C) * (jnp.maximum(z, 0.0) + jnp.log1p(jnp.exp(-jnp.abs(z))))
    neg_log_a = t_r * half_decay + half_decay
    a = jnp.exp(-neg_log_a)
    z = jnp.tanh(neg_log_a) * (a * a + 1.0)
    root = jnp.where(z > 0.0, z * lax.rsqrt(z), 0.0)
    half_xc = 0.5 * xc
    u = root * (t_i * half_xc + half_xc)
    return a, u


def _scan(a_ref, u_ref, h_ref, h, reverse):
    steps = TM // SUB

    def body(k, h):
        t = steps - 1 - k if reverse else k
        r0 = pl.multiple_of(t * SUB, SUB)
        h = a_ref[pl.ds(r0, SUB), :] * h + u_ref[pl.ds(r0, SUB), :]
        h_ref[pl.ds(r0, SUB), :] = h
        return h

    return lax.fori_loop(0, steps, body, h, unroll=8)


def _scan_fwd_kernel(cfg, xa_ref, xp_ref, xn_ref, cw_ref, cb_ref, wg_ref, ba_ref, bx_ref,
                     lam_ref, h0_ref, hf_ref, st_ref, pad_ref, a_ref, u_ref, carry_ref):
    j = pl.program_id(0)
    _, _, first, last = _seq_flags(cfg, j)
    xc = _conv(pad_ref, xa_ref[...], xp_ref[...], xn_ref[...], first, last, cw_ref[...], cb_ref[...])
    a, u = _gates(xc, wg_ref, ba_ref[...], bx_ref[...], lam_ref[...])
    a_ref[...] = a
    u_ref[...] = u

    @pl.when(first)
    def _():
        carry_ref[...] = h0_ref[0]

    h = _scan(a_ref, u_ref, hf_ref, carry_ref[...], reverse=False)
    carry_ref[...] = h
    st_ref[0] = h


def _halo_specs(cfg, col, rows_prev, rows_next, tile_of):
    n_tiles = cfg[2]
    c = 512
    nb_prev = TM // rows_prev
    nb_next = TM // rows_next
    last_next = n_tiles * nb_next - 1
    prev = pl.BlockSpec((rows_prev, c), lambda j: (jnp.maximum(tile_of(j) * nb_prev - 1, 0), col))
    nxt = pl.BlockSpec((rows_next, c), lambda j: (jnp.minimum((tile_of(j) + 1) * nb_next, last_next), col))
    return prev, nxt


def _state_spec(cfg, tile_of):
    n_ctx_tiles, tiles_per_ctx, _, _ = cfg
    n_groups = n_ctx_tiles // tiles_per_ctx
    c = 512
    return pl.BlockSpec((1, SUB, c), lambda j: (jnp.minimum(tile_of(j) // tiles_per_ctx, n_groups), 0, 0))


def _h0_spec(cfg, tile_of):
    n_ctx_tiles = cfg[0]
    return pl.BlockSpec((1, SUB, 512), lambda j: (jnp.where(tile_of(j) >= n_ctx_tiles, 1, 0), 0, 0))


def _scan_fwd_call(cfg, proj, conv_w, conv_b, wg, ba, bx, lam, h0):
    n_ctx_tiles, tiles_per_ctx, n_tiles, _ = cfg
    n_groups = n_ctx_tiles // tiles_per_ctx
    c = conv_w.shape[1]
    ident = lambda j: j
    xp_spec, xn_spec = _halo_specs(cfg, 0, CONV_LEFT * SUB, CONV_RIGHT * SUB, ident)
    return pl.pallas_call(
        functools.partial(_scan_fwd_kernel, cfg),
        grid=(n_tiles,),
        in_specs=[
            pl.BlockSpec((TM, c), lambda j: (j, 0)), xp_spec, xn_spec,
            _const_spec((CONV_W, c)), _const_spec((1, c)), _const_spec((2, c // 2, c)),
            _const_spec((1, c)), _const_spec((1, c)), _const_spec((1, c)),
            _h0_spec(cfg, ident),
        ],
        out_specs=[pl.BlockSpec((TM, c), lambda j: (j, 0)), _state_spec(cfg, ident)],
        out_shape=[jax.ShapeDtypeStruct((n_tiles * TM, c), F32),
                   jax.ShapeDtypeStruct((n_groups + 1, SUB, c), F32)],
        scratch_shapes=[
            pltpu.VMEM((TM + (CONV_W - 1) * SUB, c), F32),
            pltpu.VMEM((TM, c), F32), pltpu.VMEM((TM, c), F32), pltpu.VMEM((SUB, c), F32),
        ],
        compiler_params=_cparams(("arbitrary",)),
        name="lru_fwd",
    )(proj, proj, proj, conv_w, conv_b.reshape(1, c), wg, ba.reshape(1, c), bx.reshape(1, c),
      lam.reshape(1, c), h0)


def _pool_inv_counts(ctx_len, c):
    gw = c // len(POOL_WINDOWS)

    def table(t0, t_len):
        t = t0 + jnp.arange(GRID_W)
        cols = []
        for k in POOL_WINDOWS:
            left = k // 2
            right = k - 1 - left
            cnt = jnp.minimum(t + right + 1, t_len) - jnp.maximum(t - left, 0)
            cols.append(jnp.broadcast_to((1.0 / cnt.astype(F32))[:, None], (GRID_W, gw)))
        return jnp.repeat(jnp.concatenate(cols, axis=1), SUB, axis=0)

    return jnp.stack([table(p * GRID_W, ctx_len) for p in range(ctx_len // GRID_W)] + [table(0, GRID_W)])


def _gelu_tanh(x):
    k0 = 0.7978845608028654
    hx = 0.5 * x
    return hx + hx * jnp.tanh(x * (k0 + (k0 * 0.044715) * (x * x)))


def _pool_mix(pad_ref, inv_ref, pw_ref, pb, ps):
    c = pad_ref.shape[1]
    gw = c // len(POOL_WINDOWS)
    outs = []
    for g, k in enumerate(POOL_WINDOWS):
        left = k // 2
        right = k - 1 - left
        lanes = slice(g * gw, (g + 1) * gw)
        s = None
        for o in range(-left, right + 1):
            v = pad_ref[POOL_HALO + o * SUB:POOL_HALO + o * SUB + TM, lanes]
            s = v if s is None else s + v
        outs.append(s * inv_ref[0, :, lanes] - pad_ref[POOL_HALO:POOL_HALO + TM, lanes])
    d = jnp.concatenate(outs, axis=1).astype(BF16)
    half = c // 2
    y = jnp.concatenate([_dot(d[:, :half], pw_ref[0]), _dot(d[:, half:], pw_ref[1])], axis=1)
    return (y + pb) * ps


def _scan_bwd_kernel(cfg, proj_ref, xp_ref, xn_ref, bp_ref, bn_ref, hf_ref, inv_ref, cw_ref, cb_ref,
                     wg_ref, ba_ref, bx_ref, lam_ref, h0_ref, pw_ref, pb_ref, ps_ref,
                     y_ref, st_ref, pad_ref, ppad_ref, a_ref, u_ref, hb_ref, carry_ref):
    n_tiles = cfg[2]
    j = n_tiles - 1 - pl.program_id(0)
    is_ctx, _, first, last = _seq_flags(cfg, j)
    c = hf_ref.shape[1]
    xc = _conv(pad_ref, proj_ref[:, 0:c], xp_ref[...], xn_ref[...], first, last, cw_ref[...], cb_ref[...])
    a, u = _gates(xc, wg_ref, ba_ref[...], bx_ref[...], lam_ref[...])
    a_ref[...] = a
    u_ref[...] = u

    @pl.when(last)
    def _():
        carry_ref[...] = h0_ref[0]

    h = _scan(a_ref, u_ref, hb_ref, carry_ref[...], reverse=True)
    carry_ref[...] = h
    st_ref[0] = h

    ga = proj_ref[:, c:2 * c]
    y_a = (hf_ref[...] + hb_ref[...]) * _gelu_tanh(ga)

    use_prev = jnp.logical_and(is_ctx, jnp.logical_not(first))
    use_next = jnp.logical_and(is_ctx, jnp.logical_not(last))
    ppad_ref[0:POOL_HALO, :] = jnp.where(use_prev, bp_ref[...], 0.0)
    ppad_ref[POOL_HALO:POOL_HALO + TM, :] = proj_ref[:, 2 * c:3 * c]
    ppad_ref[POOL_HALO + TM:, :] = jnp.where(use_next, bn_ref[...], 0.0)
    y_b = _pool_mix(ppad_ref, inv_ref, pw_ref, pb_ref[...], ps_ref[...])
    y_ref[...] = jnp.concatenate([y_a, y_b], axis=1).astype(BF16)


def _scan_bwd_call(cfg, inv_cnt, proj, hf, conv_w, conv_b, wg, ba, bx, lam, h0, pw, pb, ps):
    n_ctx_tiles, tiles_per_ctx, n_tiles, d = cfg
    n_groups = n_ctx_tiles // tiles_per_ctx
    c = conv_w.shape[1]
    rev = lambda j: n_tiles - 1 - j
    xp_spec, xn_spec = _halo_specs(cfg, 0, CONV_LEFT * SUB, CONV_RIGHT * SUB, rev)
    bp_spec, bn_spec = _halo_specs(cfg, 2, POOL_HALO, POOL_HALO, rev)

    def inv_idx(j):
        jj = rev(j)
        return (jnp.where(jj < n_ctx_tiles, lax.rem(jj, tiles_per_ctx), tiles_per_ctx), 0, 0)

    return pl.pallas_call(
        functools.partial(_scan_bwd_kernel, cfg),
        grid=(n_tiles,),
        in_specs=[
            pl.BlockSpec((TM, 3 * c), lambda j: (rev(j), 0)), xp_spec, xn_spec, bp_spec, bn_spec,
            pl.BlockSpec((TM, c), lambda j: (rev(j), 0)),
            pl.BlockSpec((1, TM, c), inv_idx),
            _const_spec((CONV_W, c)), _const_spec((1, c)), _const_spec((2, c // 2, c)),
            _const_spec((1, c)), _const_spec((1, c)), _const_spec((1, c)),
            _h0_spec(cfg, rev),
            _const_spec((2, c // 2, c // 2)), _const_spec((1, c)), _const_spec((1, c)),
        ],
        out_specs=[pl.BlockSpec((TM, d), lambda j: (rev(j), 0)), _state_spec(cfg, rev)],
        out_shape=[jax.ShapeDtypeStruct((n_tiles * TM, d), BF16),
                   jax.ShapeDtypeStruct((n_groups + 1, SUB, c), F32)],
        scratch_shapes=[
            pltpu.VMEM((TM + (CONV_W - 1) * SUB, c), F32),
            pltpu.VMEM((TM + 2 * POOL_HALO, c), F32),
            pltpu.VMEM((TM, c), F32), pltpu.VMEM((TM, c), F32), pltpu.VMEM((TM, c), F32),
            pltpu.VMEM((SUB, c), F32),
        ],
        compiler_params=_cparams(("arbitrary",)),
        name="lru_bwd_mix",
    )(proj, proj, proj, proj, proj, hf, inv_cnt, conv_w, conv_b.reshape(1, c), wg, ba.reshape(1, c),
      bx.reshape(1, c), lam.reshape(1, c), h0, pw, pb.reshape(1, c), ps.reshape(1, c))


def _out_proj(x_ref, ym_ref, mod_ref, wo_ref, g2_ref):
    xm = _gated_add(x_ref[...], mod_ref[0, 2], _dot(ym_ref[...], wo_ref[...]))
    h2 = _modulate(_rms(xm, g2_ref[...]), mod_ref[0, 4], mod_ref[0, 3])
    return xm, h2


def _swiglu_act(h2b, wg_ref, wu_ref, act_ref):
    n_chunks = act_ref.shape[1] // FF_CHUNK

    def body(f, carry):
        c0 = pl.multiple_of(f * FF_CHUNK, FF_CHUNK)
        g = _dot(h2b, wg_ref[:, pl.ds(c0, FF_CHUNK)])
        u = _dot(h2b, wu_ref[:, pl.ds(c0, FF_CHUNK)])
        act_ref[:, pl.ds(c0, FF_CHUNK)] = (g * _sigmoid(g) * u).astype(BF16)
        return carry

    lax.fori_loop(0, n_chunks, body, 0, unroll=True)


def _dense_ffn_kernel(x_ref, ym_ref, mod_ref, wo_ref, g2_ref, wg_ref, wu_ref, wd_ref, o_ref, act_ref):
    xm, h2 = _out_proj(x_ref, ym_ref, mod_ref, wo_ref, g2_ref)
    _swiglu_act(h2.astype(BF16), wg_ref, wu_ref, act_ref)
    o_ref[...] = _gated_add(xm, mod_ref[0, 5], _dot(act_ref[...], wd_ref[...]))


def _dense_ffn_call(cfg, x, ymix, mod_l, w_out, g2, wg, wu, wd):
    _, _, n_tiles, d = cfg
    d_ff = wg.shape[1]
    f = _wide(cfg)
    row_spec = pl.BlockSpec((TM * f, d), lambda j: (j, 0))
    return pl.pallas_call(
        _dense_ffn_kernel,
        grid=(n_tiles // f,),
        in_specs=[row_spec, row_spec, _mod_spec(cfg, f), _const_spec((d, d)), _const_spec((1, d)),
                  _const_spec((d, d_ff)), _const_spec((d, d_ff)), _const_spec((d_ff, d))],
        out_specs=row_spec,
        out_shape=jax.ShapeDtypeStruct((n_tiles * TM, d), F32),
        scratch_shapes=[pltpu.VMEM((TM * f, d_ff), BF16)],
        compiler_params=_cparams(("parallel",)),
        name="out_proj_dense_ffn",
    )(x, ymix, mod_l, w_out, g2.reshape(1, d), wg, wu, wd)


M_E1, M_E2, M_W1, M_W2, M_R1, M_R2 = range(6)


def _route_kernel(x_ref, ym_ref, mod_ref, wo_ref, g2_ref, rw_ref, rb_ref, tri_ref,
                  xm_ref, hp_ref, meta_ref, mt_ref, cnt_ref, run_ref):
    @pl.when(pl.program_id(0) == 0)
    def _():
        run_ref[...] = jnp.zeros_like(run_ref)

    run = run_ref[...]
    for r0 in range(0, x_ref.shape[0], TM):
        rows = pl.ds(r0, TM)
        xm = _gated_add(x_ref[rows, :], mod_ref[0, 2], _dot(ym_ref[rows, :], wo_ref[...]))
        h2 = _modulate(_rms(xm, g2_ref[...]), mod_ref[0, 4], mod_ref[0, 3])
        xm_ref[rows, :] = xm
        h_hi = h2.astype(BF16)
        h_hi32 = h_hi.astype(F32)
        hp_ref[rows, :] = _pack_pairs(h_hi32)

        h_lo = (h2 - h_hi32).astype(BF16)
        p = _dot(h_hi, rw_ref[...])
        logits = p[:, :LANES] + p[:, LANES:] + _dot(h_lo, rw_ref[:, :LANES])
        lane = lax.broadcasted_iota(jnp.int32, logits.shape, 1)
        neg = jnp.float32(-jnp.inf)
        lg = jnp.where(lane < N_EXPERTS, logits + rb_ref[...], neg)
        m1 = jnp.max(lg, axis=1, keepdims=True)
        i1 = jnp.min(jnp.where(lg == m1, lane, LANES), axis=1, keepdims=True)
        lg2 = jnp.where(lane == i1, neg, lg)
        m2 = jnp.max(lg2, axis=1, keepdims=True)
        i2 = jnp.min(jnp.where(lg2 == m2, lane, LANES), axis=1, keepdims=True)
        e2 = jnp.exp(m2 - m1)
        den = 1.0 + e2

        sel1 = lane == i1
        sel2 = lane == i2
        onehot = jnp.where(jnp.logical_or(sel1, sel2), 1.0, 0.0)
        rank = _dot(tri_ref[...], onehot.astype(BF16)) + run
        r1 = jnp.sum(jnp.where(sel1, rank, 0.0), axis=1, keepdims=True)
        r2 = jnp.sum(jnp.where(sel2, rank, 0.0), axis=1, keepdims=True)
        run = run + jnp.sum(onehot, axis=0, keepdims=True)

        meta = jnp.zeros(logits.shape, F32)
        for k, v in ((M_E1, i1.astype(F32)), (M_E2, i2.astype(F32)), (M_W1, 1.0 / den), (M_W2, e2 / den),
                     (M_R1, r1), (M_R2, r2)):
            meta = jnp.where(lane == k, v, meta)
        meta_ref[rows, :] = meta
        mt_ref[0, :, rows] = jnp.transpose(meta)[:SUB]

    run_ref[...] = run
    cnt_ref[...] = run


def _route_call(cfg, part, x, ymix, mod_l, w_out, g2, router_w, router_b):
    d = cfg[3]
    t0, nt = part
    n = nt * TM
    f = _wide(cfg)
    tm = TM * f
    s0 = t0 // f
    in_row_spec = pl.BlockSpec((tm, d), lambda j: (j + s0, 0))
    row_spec = pl.BlockSpec((tm, d), lambda j: (j, 0))
    rw = jnp.zeros((d, LANES), F32).at[:, :N_EXPERTS].set(router_w)
    rw_hi = rw.astype(BF16)
    rw = jnp.concatenate([rw_hi, (rw - rw_hi.astype(F32)).astype(BF16)], axis=1)
    rb = jnp.zeros((1, LANES), F32).at[0, :N_EXPERTS].set(router_b)
    tri = jnp.tril(jnp.ones((TM, TM), BF16), -1)
    return pl.pallas_call(
        _route_kernel,
        grid=(nt // f,),
        in_specs=[in_row_spec, in_row_spec, _mod_spec(cfg, f, t0), _const_spec((d, d)), _const_spec((1, d)),
                  _const_spec((d, 2 * LANES)), _const_spec((1, LANES)), _const_spec((TM, TM))],
        out_specs=[row_spec, pl.BlockSpec((tm, d // 2), lambda j: (j, 0)),
                   pl.BlockSpec((tm, LANES), lambda j: (j, 0)), pl.BlockSpec((1, SUB, tm), lambda j: (j, 0, 0)),
                   pl.BlockSpec((1, LANES), lambda j: (0, 0))],
        out_shape=[jax.ShapeDtypeStruct((n, d), F32), jax.ShapeDtypeStruct((n, d // 2), jnp.int32),
                   jax.ShapeDtypeStruct((n, LANES), F32), jax.ShapeDtypeStruct((n // tm, SUB, tm), F32),
                   jax.ShapeDtypeStruct((1, LANES), F32)],
        scratch_shapes=[pltpu.VMEM((1, LANES), F32)],
        compiler_params=_cparams(("arbitrary",)),
        name="out_proj_route",
    )(x, ymix, mod_l, w_out, g2.reshape(1, d), rw, rb, tri)


def _routing_tables(n_tiles, mt, cnt):
    counts = cnt[0, :N_EXPERTS].astype(jnp.int32)
    padded = ((counts + TM - 1) // TM) * TM
    ends = jnp.cumsum(padded)
    offs = ends - padded
    row = lambda k: mt[:, k, :].reshape(-1).astype(jnp.int32)
    pos1 = offs[row(M_E1)] + row(M_R1)
    pos2 = offs[row(M_E2)] + row(M_R2)
    n_sorted_tiles = 2 * n_tiles + N_EXPERTS
    starts = jnp.arange(n_sorted_tiles, dtype=jnp.int32) * TM
    tile_e = jnp.minimum(jnp.sum((starts[:, None] >= ends[None, :]).astype(jnp.int32), axis=1), N_EXPERTS - 1)
    n_active = (ends[-1] // TM).reshape(1)
    return jnp.concatenate([pos1, pos2]), tile_e, n_active


SC_CORES = 2
SC_SUBCORES = 16
SC_CHUNK = 64


def _sc_mesh():
    return plsc.VectorSubcoreMesh(core_axis_name="c", subcore_axis_name="s",
                                  num_cores=SC_CORES, num_subcores=SC_SUBCORES)


def _sc_scatter_rows(rows, idx, n_out):
    n_src, width = rows.shape
    n_idx = idx.shape[0]
    n_workers = SC_CORES * SC_SUBCORES
    per_w = n_idx // n_workers
    chunks = per_w // SC_CHUNK
    assert n_idx % (n_workers * SC_CHUNK) == 0 and n_src % per_w == 0
    idx3 = idx.reshape(n_workers, chunks, SC_CHUNK)

    def body(rows_hbm, idx_hbm, out_hbm, idx_v, rows_v, sem):
        wid = lax.axis_index("s") * SC_CORES + lax.axis_index("c")
        src_base = lax.rem(wid * per_w, n_src)
        pltpu.sync_copy(idx_hbm.at[wid], idx_v)

        @pl.loop(0, chunks)
        def _(i):
            off = pl.multiple_of(i * SC_CHUNK, SC_CHUNK)
            pltpu.sync_copy(rows_hbm.at[pl.ds(src_base + off, SC_CHUNK)], rows_v)
            pltpu.async_copy(rows_v, out_hbm.at[idx_v.at[i]], sem).wait()

    return pl.kernel(
        body,
        out_type=jax.ShapeDtypeStruct((n_out, width), rows.dtype),
        mesh=_sc_mesh(),
        scratch_types=[pltpu.VMEM((chunks, SC_CHUNK), jnp.int32), pltpu.VMEM((SC_CHUNK, width), rows.dtype),
                       pltpu.SemaphoreType.DMA],
        name="sc_scatter_rows",
    )(rows, idx3)


def _expert_kernel(te_ref, na_ref, s_ref, wgu_ref, wd_ref, o_ref):
    del te_ref

    @pl.when(pl.program_id(0) < na_ref[0])
    def _():
        h = _unpack_pairs(s_ref[...]).astype(BF16)
        gu = _dot(h, wgu_ref[0, 0])
        dfe = gu.shape[1] // 2
        g = gu[:, :dfe]
        act = (g * _sigmoid(g) * gu[:, dfe:]).astype(BF16)
        o_ref[...] = _pack_pairs(_dot(act, wd_ref[0, 0]).astype(BF16).astype(F32))

    @pl.when(pl.program_id(0) >= na_ref[0])
    def _():
        o_ref[...] = jnp.zeros_like(o_ref)


def _expert_call(cfg, tile_e, n_active, buf, wgu, wd, layer):
    d = cfg[3]
    dfe = wd.shape[2]
    n_sorted_tiles = buf.shape[0] // TM
    last = lambda t, na: jnp.maximum(jnp.minimum(t, na[0] - 1), 0)
    tile = lambda t, te, na: (last(t, na), 0)
    w_spec = lambda shape: pl.BlockSpec((1,) + shape, lambda t, te, na: (layer, te[last(t, na)], 0, 0))
    return pl.pallas_call(
        _expert_kernel,
        grid_spec=pltpu.PrefetchScalarGridSpec(
            num_scalar_prefetch=2,
            grid=(n_sorted_tiles,),
            in_specs=[pl.BlockSpec((TM, d // 2), tile), w_spec((1, d, 2 * dfe)), w_spec((1, dfe, d))],
            out_specs=pl.BlockSpec((TM, d // 2), lambda t, te, na: (t, 0)),
        ),
        out_shape=jax.ShapeDtypeStruct((n_sorted_tiles * TM, d // 2), jnp.int32),
        compiler_params=_cparams(("arbitrary",)),
        name="moe_experts",
    )(tile_e, n_active, buf, wgu, wd)


def _store_natural(n_ctx_tiles, part, y, o_refs):
    t0, nt = part
    has_ctx = t0 < n_ctx_tiles
    has_lat = t0 + nt > n_ctx_tiles
    is_ctx = pl.program_id(0) + t0 < n_ctx_tiles
    if has_ctx:
        @pl.when(is_ctx)
        def _():
            o_refs[0][0] = _from_time_major(y)

    if has_lat:
        @pl.when(jnp.logical_not(is_ctx))
        def _():
            o_refs[-1][...] = _from_time_major(y)


def _natural_out(cfg, part):
    n_ctx_tiles, tiles_per_ctx, n_tiles, d = cfg
    n_groups = n_ctx_tiles // tiles_per_ctx
    t0, nt = part
    specs = _natural_specs(cfg, t0)
    shapes = (jax.ShapeDtypeStruct((n_groups, SUB, tiles_per_ctx * GRID_W, d), F32),
              jax.ShapeDtypeStruct((SUB, (n_tiles - n_ctx_tiles) * GRID_W, d), F32))
    keep = [k for k, used in enumerate((t0 < n_ctx_tiles, t0 + nt > n_ctx_tiles)) if used]
    return [specs[k] for k in keep], [shapes[k] for k in keep], keep


def _sc_gather_rows(table, idx):
    n_rows = idx.shape[0]
    width = table.shape[1]
    n_workers = SC_CORES * SC_SUBCORES
    assert n_rows % (n_workers * SC_CHUNK) == 0
    per_w = n_rows // n_workers
    mesh = _sc_mesh()

    def body(table_hbm, idx_hbm, out_hbm, idx_v, rows_v, sem):
        wid = lax.axis_index("s") * SC_CORES + lax.axis_index("c")
        base = wid * per_w
        pltpu.sync_copy(idx_hbm.at[pl.ds(base, per_w)], idx_v)

        @pl.loop(0, per_w // SC_CHUNK)
        def _(i):
            off = pl.multiple_of(i * SC_CHUNK, SC_CHUNK)
            pltpu.async_copy(table_hbm.at[idx_v.at[pl.ds(off, SC_CHUNK)]], rows_v, sem).wait()
            pltpu.sync_copy(rows_v, out_hbm.at[pl.ds(base + off, SC_CHUNK)])

    return pl.kernel(
        body,
        out_type=jax.ShapeDtypeStruct((n_rows, width), table.dtype),
        mesh=mesh,
        scratch_types=[pltpu.VMEM((per_w,), jnp.int32), pltpu.VMEM((SC_CHUNK, width), table.dtype),
                       pltpu.SemaphoreType.DMA],
        name="sc_gather_rows",
    )(table, idx)


def _combine_kernel(final_ctx_tiles, part, n_filled, xm_ref, y1_ref, y2_ref, meta_ref, mod_ref, gfin_ref,
                    *refs):
    o_refs = refs[n_filled:]
    meta = meta_ref[...]
    y = (meta[:, M_W1:M_W1 + 1] * _unpack_pairs(y1_ref[...])
         + meta[:, M_W2:M_W2 + 1] * _unpack_pairs(y2_ref[...]))
    x = _gated_add(xm_ref[...], mod_ref[0, 5], y)
    if final_ctx_tiles is None:
        o_refs[0][...] = x
    else:
        _store_natural(final_ctx_tiles, part, _rms(x, gfin_ref[...]), o_refs)


def _combine_call(cfg, part, xm, yg, meta, mod_l, g_final, final, filled):
    n_ctx_tiles, _, n_tiles, d = cfg
    t0, nt = part
    row_spec = pl.BlockSpec((TM, d), lambda j: (j, 0))
    if final:
        out_specs, out_shape, keys = _natural_out(cfg, part)
    else:
        out_specs = [pl.BlockSpec((TM, d), lambda j: (j + t0, 0))]
        out_shape, keys = [jax.ShapeDtypeStruct((n_tiles * TM, d), F32)], [0]
    reuse = [k for k in keys if k in filled]
    base = [xm, yg, yg, meta, mod_l, g_final.reshape(1, d)]
    outs = pl.pallas_call(
        functools.partial(_combine_kernel, n_ctx_tiles if final else None, part, len(reuse)),
        grid=(nt,),
        in_specs=[row_spec, pl.BlockSpec((TM, d // 2), lambda j: (j, 0)),
                  pl.BlockSpec((TM, d // 2), lambda j: (nt + j, 0)),
                  pl.BlockSpec((TM, LANES), lambda j: (j, 0)), _mod_spec(cfg, 1, t0), _const_spec((1, d))]
                 + [pl.BlockSpec(memory_space=pl.ANY)] * len(reuse),
        out_specs=out_specs,
        out_shape=out_shape,
        input_output_aliases={len(base) + i: keys.index(k) for i, k in enumerate(reuse)},
        compiler_params=_cparams(("arbitrary",)),
        name="moe_combine",
    )(*base, *[filled[k] for k in reuse])
    return {**filled, **dict(zip(keys, outs))}


def _final_norm_kernel(n_ctx_tiles, part, x_ref, g_ref, *o_refs):
    _store_natural(n_ctx_tiles, part, _rms(x_ref[...], g_ref[...]), o_refs)


def _final_norm_call(cfg, x, g):
    n_ctx_tiles, _, n_tiles, d = cfg
    out_specs, out_shape, _ = _natural_out(cfg, (0, n_tiles))
    return pl.pallas_call(
        functools.partial(_final_norm_kernel, n_ctx_tiles, (0, n_tiles)),
        grid=(n_tiles,),
        in_specs=[pl.BlockSpec((TM, d), lambda j: (j, 0)), _const_spec((1, d))],
        out_specs=out_specs,
        out_shape=out_shape,
        compiler_params=_cparams(("arbitrary",)),
        name="final_norm",
    )(x, g.reshape(1, d))


def _block_diag(w, per_block):
    *lead, n, k, _ = w.shape
    nb = n // per_block
    w = w.reshape(*lead, nb, per_block, k, k)
    eye = jnp.eye(per_block, dtype=w.dtype)
    out = w[..., :, :, None, :] * eye[:, None, :, None]
    return out.reshape(*lead, nb, per_block * k, per_block * k)


def _pos_tables(n_ctx_tiles, n_lat_tiles, d):
    quarter = d // 4
    omega = 1.0 / (POS_BASE ** (jnp.arange(quarter, dtype=F32) / quarter))
    er = jnp.arange(n_lat_tiles, dtype=F32)[:, None] * omega
    ec = jnp.arange(GRID_W, dtype=F32)[:, None] * omega
    row_emb = jnp.concatenate([jnp.sin(er), jnp.cos(er)], axis=-1)
    col_emb = jnp.concatenate([jnp.sin(ec), jnp.cos(ec)], axis=-1)
    del n_ctx_tiles
    return row_emb[:, None, :], jnp.repeat(col_emb, SUB, axis=0)


def kernel(x_prompt, x_sample, state_lru, c, c_ctx, norm_mix_g, w_ada, b_ada, w_in, conv_w, conv_b, lru_wa, lru_ba, lru_wx, lru_bx, lru_lam, pool_w, pool_b, pool_scale, w_out, norm_ffn_g, ffn_wg, ffn_wu, ffn_wd, moe_router_w, moe_router_b, moe_wg, moe_wu, moe_wd, norm_final_g):
    bc, tc, d = x_prompt.shape
    bl, tl, _ = x_sample.shape
    depth = w_ada.shape[0]
    c_lru = conv_w.shape[-1]
    assert bl == SUB and bc % SUB == 0 and tc % GRID_W == 0 and tl % GRID_W == 0
    assert c_lru == 4 * LANES and d == 2 * c_lru
    n_groups = bc // SUB
    tiles_per_ctx = tc // GRID_W
    n_ctx_tiles = n_groups * tiles_per_ctx
    n_lat_tiles = tl // GRID_W
    n_tiles = n_ctx_tiles + n_lat_tiles
    n_ctx = n_ctx_tiles * TM
    cfg = (n_ctx_tiles, tiles_per_ctx, n_tiles, d)

    cond = jnp.concatenate([c, c_ctx[None], jnp.zeros((SUB - 1, d), F32)], axis=0)
    mod = _ada_call(cond, w_ada, b_ada)
    mod_lat = mod[:, :SUB].reshape(depth, SUB, 6, d).transpose(0, 2, 1, 3)
    mod_ctx = jnp.broadcast_to(mod[:, SUB].reshape(depth, 6, 1, d), (depth, 6, SUB, d))
    mod = jnp.stack([mod_ctx, mod_lat], axis=1)

    row_tab, col_rep = _pos_tables(n_ctx_tiles, n_lat_tiles, d)
    first = (x_prompt.reshape(n_groups, SUB, tc, d), x_sample, row_tab, col_rep)
    x = None
    wg_lru = (0.5 * jnp.concatenate([_block_diag(lru_wa, 4), _block_diag(lru_wx, 4)], axis=-1)).astype(BF16)
    pw = _block_diag(pool_w, 2).astype(BF16)
    inv_cnt = _pool_inv_counts(tc, c_lru)
    moe_wgu = jnp.concatenate([moe_wg, moe_wu], axis=3).astype(BF16)
    moe_wd_b = moe_wd.astype(BF16)
    h0 = jnp.concatenate([jnp.zeros_like(state_lru[None]), state_lru[None]], axis=0)

    states = []
    for l in range(depth):
        jdx = l // 2
        if l == 0:
            proj, x = _inproj_call(cfg, x, mod[l], norm_mix_g[l], w_in[l].astype(BF16), first)
        else:
            (proj,) = _inproj_call(cfg, x, mod[l], norm_mix_g[l], w_in[l].astype(BF16))
        hf, st_f = _scan_fwd_call(cfg, proj, conv_w[l], conv_b[l], wg_lru[l, 0], lru_ba[l, 0],
                                  lru_bx[l, 0], lru_lam[l, 0], h0[:, :, l, 0])
        ymix, st_b = _scan_bwd_call(cfg, inv_cnt, proj, hf, conv_w[l], conv_b[l], wg_lru[l, 1], lru_ba[l, 1],
                                    lru_bx[l, 1], lru_lam[l, 1], h0[:, :, l, 1], pw[l], pool_b[l],
                                    pool_scale[l])
        states.append(jnp.stack([st_f[:n_groups].reshape(bc, c_lru), st_b[:n_groups].reshape(bc, c_lru)], axis=1))
        if l % 2 == 0:
            x = _dense_ffn_call(cfg, x, ymix, mod[l], w_out[l].astype(BF16), norm_ffn_g[l],
                                ffn_wg[jdx].astype(BF16), ffn_wu[jdx].astype(BF16), ffn_wd[jdx].astype(BF16))
        else:
            final = l == depth - 1
            w_out_b = w_out[l].astype(BF16)
            filled = {}
            for part in _moe_parts(cfg):
                xm, hp, meta, mt, cnt = _route_call(cfg, part, x, ymix, mod[l], w_out_b, norm_ffn_g[l],
                                                    moe_router_w[jdx], moe_router_b[jdx])
                pos_flat, tile_e, n_active = _routing_tables(part[1], mt, cnt)
                buf = _sc_scatter_rows(hp, pos_flat, (2 * part[1] + N_EXPERTS) * TM)
                ys = _expert_call(cfg, tile_e, n_active, buf, moe_wgu, moe_wd_b, jdx)
                yg = _sc_gather_rows(ys, pos_flat)
                filled = _combine_call(cfg, part, xm, yg, meta, mod[l], norm_final_g, final, filled)
            x = (filled[0], filled[1]) if final else filled[0]

    y_prompt, y_sample = x if depth % 2 == 0 else _final_norm_call(cfg, x, norm_final_g)
    new_state = jnp.stack(states, axis=1)
    return (y_prompt.reshape(bc, tc, d), y_sample, new_state)
```

```python
import functools

import jax
import jax.numpy as jnp
from jax import lax
from jax.experimental import pallas as pl
from jax.experimental.pallas import tpu as pltpu
from jax.experimental.pallas import tpu_sc as plsc

F32 = jnp.float32
BF16 = jnp.bfloat16

SUB = 8
LANES = 128
GRID_W = 64
TM = GRID_W * SUB
POS_BASE = 10000.0
N_LRU_HEADS = 8
CONV_W = 4
CONV_LEFT = CONV_W // 2
CONV_RIGHT = CONV_W - 1 - CONV_LEFT
LRU_C = 8.0
POOL_WINDOWS = (2, 4, 8, 16)
POOL_HALO = 8 * SUB
N_EXPERTS = 8
EPS = 1e-6
FF_CHUNK = 256
VMEM_LIMIT = 56 * 1024 * 1024


def _cparams(sem):
    return pltpu.CompilerParams(dimension_semantics=sem, vmem_limit_bytes=VMEM_LIMIT)


def _const_spec(shape):
    nd = len(shape)
    return pl.BlockSpec(shape, lambda *_: (0,) * nd, pipeline_mode=pl.Buffered(1))


def _rms(x, g):
    ms = jnp.mean(x * x, axis=-1, keepdims=True)
    return x * lax.rsqrt(ms + EPS) * g


def _per_seq(x, v, op):
    r, c = x.shape
    x3 = x.reshape(r // SUB, SUB, c)
    return op(x3, v[None]).reshape(r, c)


def _modulate(h, scale, shift):
    r, c = h.shape
    h3 = h.reshape(r // SUB, SUB, c)
    return (h3 * (1.0 + scale)[None] + shift[None]).reshape(r, c)


def _gated_add(x, gate, y):
    return x + _per_seq(y, gate, lambda a, b: a * b)


def _dot(a, b):
    return jnp.dot(a, b, preferred_element_type=F32)


def _pack_pairs(x):
    bits = lax.bitcast_convert_type(x, jnp.uint32)
    half = bits.shape[1] // 2
    w = lax.shift_right_logical(bits[:, :half], jnp.uint32(16)) | (bits[:, half:] & jnp.uint32(0xFFFF0000))
    return lax.bitcast_convert_type(w, jnp.int32)


def _unpack_pairs(w):
    w = lax.bitcast_convert_type(w, jnp.uint32)
    lo = lax.bitcast_convert_type(lax.shift_left(w, jnp.uint32(16)), F32)
    hi = lax.bitcast_convert_type(w & jnp.uint32(0xFFFF0000), F32)
    return jnp.concatenate([lo, hi], axis=1)


def _sigmoid(x):
    return 0.5 * jnp.tanh(0.5 * x) + 0.5


def _ada_kernel(c_ref, w_ref, b_ref, o_ref):
    c = c_ref[...]
    s = (c * jax.nn.sigmoid(c)).astype(BF16)
    o_ref[0] = _dot(s, w_ref[0].astype(BF16)) + b_ref[0]


def _ada_call(cond, w_ada, b_ada):
    depth, d, d6 = w_ada.shape
    nr = cond.shape[0]
    bn = d6 // 4
    return pl.pallas_call(
        _ada_kernel,
        grid=(depth, d6 // bn),
        in_specs=[
            pl.BlockSpec((nr, d), lambda l, n: (0, 0)),
            pl.BlockSpec((1, d, bn), lambda l, n: (l, 0, n)),
            pl.BlockSpec((1, 1, bn), lambda l, n: (l, 0, n)),
        ],
        out_specs=pl.BlockSpec((1, nr, bn), lambda l, n: (l, 0, n)),
        out_shape=jax.ShapeDtypeStruct((depth, nr, d6), F32),
        compiler_params=_cparams(("parallel", "parallel")),
        name="ada_mod",
    )(cond, w_ada, b_ada.reshape(depth, 1, d6))


def _to_time_major(x):
    s, t, d = x.shape
    return jnp.swapaxes(x, 0, 1).reshape(s * t, d)


def _from_time_major(x):
    r, d = x.shape
    return jnp.swapaxes(x.reshape(r // SUB, SUB, d), 0, 1)


def _inproj_kernel(n_ctx_tiles, *refs):
    if n_ctx_tiles is not None:
        xp_ref, xs_ref, row_ref, col_ref, mod_ref, g_ref, w_ref, o_ref, x0_ref = refs
        is_ctx = pl.program_id(0) < n_ctx_tiles

        @pl.when(is_ctx)
        def _():
            x0_ref[...] = _to_time_major(xp_ref[0])

        @pl.when(jnp.logical_not(is_ctx))
        def _():
            x = _to_time_major(xs_ref[...])
            half = x.shape[1] // 2
            x0_ref[...] = jnp.concatenate([x[:, :half] + row_ref[0], x[:, half:] + col_ref[...]], axis=1)

        x = x0_ref[...]
    else:
        x_ref, mod_ref, g_ref, w_ref, o_ref = refs
        x = x_ref[...]
    h = _modulate(_rms(x, g_ref[...]), mod_ref[0, 1], mod_ref[0, 0])
    o_ref[...] = _dot(h.astype(BF16), w_ref[...])


def _natural_specs(cfg, t0=0):
    n_ctx_tiles, tiles_per_ctx, _, d = cfg

    def ctx_idx(j):
        jc = jnp.minimum(j + t0, n_ctx_tiles - 1)
        return (jc // tiles_per_ctx, 0, lax.rem(jc, tiles_per_ctx), 0)

    return (pl.BlockSpec((1, SUB, GRID_W, d), ctx_idx),
            pl.BlockSpec((SUB, GRID_W, d), lambda j: (0, jnp.maximum(j + t0 - n_ctx_tiles, 0), 0)))


def _wide(cfg):
    n_ctx_tiles, _, n_tiles, _ = cfg
    return 2 if n_ctx_tiles % 2 == 0 and n_tiles % 2 == 0 else 1


def _moe_parts(cfg):
    n_tiles = cfg[2]
    half = n_tiles // 2
    if _wide(cfg) == 2 and half % 2 == 0:
        return [(0, half), (half, n_tiles - half)]
    return [(0, n_tiles)]


def _mod_spec(cfg, f=1, t0=0):
    n_ctx_steps = cfg[0] // f
    d = cfg[3]
    s0 = t0 // f
    return pl.BlockSpec((1, 6, SUB, d), lambda j: (jnp.where(j + s0 >= n_ctx_steps, 1, 0), 0, 0, 0))


def _inproj_call(cfg, x, mod_l, g, w_in, first=None):
    n_ctx_tiles, _, n_tiles, d = cfg
    d_in = w_in.shape[1]
    f = 1 if first is not None else _wide(cfg)
    tm = TM * f
    row_spec = pl.BlockSpec((tm, d), lambda j: (j, 0))
    out_specs = [pl.BlockSpec((tm, d_in), lambda j: (j, 0))]
    out_shape = [jax.ShapeDtypeStruct((n_tiles * TM, d_in), F32)]
    if first is not None:
        xp_spec, xs_spec = _natural_specs(cfg)
        in_specs = [xp_spec, xs_spec,
                    pl.BlockSpec((1, 1, d // 2), lambda j: (jnp.maximum(j - n_ctx_tiles, 0), 0, 0)),
                    _const_spec((TM, d // 2))]
        args = list(first)
        out_specs.append(row_spec)
        out_shape.append(jax.ShapeDtypeStruct((n_tiles * TM, d), F32))
    else:
        in_specs = [row_spec]
        args = [x]
    in_specs += [_mod_spec(cfg, f), _const_spec((1, d)), _const_spec((d, d_in))]
    args += [mod_l, g.reshape(1, d), w_in]
    return pl.pallas_call(
        functools.partial(_inproj_kernel, n_ctx_tiles if first is not None else None),
        grid=(n_tiles // f,),
        in_specs=in_specs,
        out_specs=out_specs,
        out_shape=out_shape,
        compiler_params=_cparams(("parallel",)),
        name="in_proj",
    )(*args)


def _seq_flags(cfg, j):
    n_ctx_tiles, tiles_per_ctx, n_tiles, _ = cfg
    is_ctx = j < n_ctx_tiles
    pos = lax.rem(j, tiles_per_ctx)
    first = jnp.where(is_ctx, pos == 0, j == n_ctx_tiles)
    last = jnp.where(is_ctx, pos == tiles_per_ctx - 1, j == n_tiles - 1)
    return is_ctx, pos, first, last


def _conv(pad_ref, xa, prev, nxt, first, last, cw, cb):
    lo = CONV_LEFT * SUB
    pad_ref[0:lo, :] = jnp.where(first, 0.0, prev)
    pad_ref[lo:lo + TM, :] = xa
    pad_ref[lo + TM:lo + TM + CONV_RIGHT * SUB, :] = jnp.where(last, 0.0, nxt)
    y = cb
    for k in range(CONV_W):
        y = y + pad_ref[k * SUB:k * SUB + TM, :] * cw[k:k + 1, :]
    return y


def _gates(xc, wg_ref, ba, bx, lam):
    half = xc.shape[1] // 2
    xb = xc.astype(BF16)
    g0 = _dot(xb[:, :half], wg_ref[0])
    g1 = _dot(xb[:, half:], wg_ref[1])
    t_r = jnp.tanh(jnp.concatenate([g0[:, :half], g1[:, :half]], axis=1) + 0.5 * ba)
    t_i = jnp.tanh(jnp.concatenate([g0[:, half:], g1[:, half:]], axis=1) + 0.5 * bx)
    z = -lam
    half_decay = (0.5 * LRU_C) * (jnp.maximum(z, 0.0) + jnp.log1p(jnp.exp(-jnp.abs(z))))
    neg_log_a = t_r * half_decay + half_decay
    a = jnp.exp(-neg_log_a)
    z = jnp.tanh(neg_log_a) * (a * a + 1.0)
    root = jnp.where(z > 0.0, z * lax.rsqrt(z), 0.0)
    half_xc = 0.5 * xc
    u = root * (t_i * half_xc + half_xc)
    return a, u


def _scan(a_ref, u_ref, h_ref, row0, h, reverse):
    steps = TM // SUB

    def body(k, h):
        t = steps - 1 - k if reverse else k
        r0 = pl.multiple_of(t * SUB, SUB)
        h = a_ref[pl.ds(r0, SUB), :] * h + u_ref[pl.ds(r0, SUB), :]
        h_ref[pl.ds(row0 + r0, SUB), :] = h
        return h

    return lax.fori_loop(0, steps, body, h, unroll=8)


def _seq_tiles(cfg):
    n_ctx_tiles, tiles_per_ctx, n_tiles, _ = cfg
    return 2 if n_ctx_tiles % 2 == 0 and tiles_per_ctx % 2 == 0 and n_tiles % 2 == 0 else 1


def _inner_halo(ref, sub, f, col0, c, rows, outer_prev, outer_next):
    prev = outer_prev[...] if sub == 0 else ref[sub * TM - rows[0]:sub * TM, col0:col0 + c]
    nxt = outer_next[...] if sub == f - 1 else ref[(sub + 1) * TM:(sub + 1) * TM + rows[1], col0:col0 + c]
    return prev, nxt


def _scan_fwd_kernel(cfg, f, xa_ref, xp_ref, xn_ref, cw_ref, cb_ref, wg_ref, ba_ref, bx_ref,
                     lam_ref, h0_ref, hf_ref, st_ref, pad_ref, a_ref, u_ref, carry_ref):
    c = xa_ref.shape[1]
    for sub in range(f):
        _, _, first, last = _seq_flags(cfg, pl.program_id(0) * f + sub)
        prev, nxt = _inner_halo(xa_ref, sub, f, 0, c, (CONV_LEFT * SUB, CONV_RIGHT * SUB), xp_ref, xn_ref)
        xc = _conv(pad_ref, xa_ref[sub * TM:(sub + 1) * TM, :], prev, nxt, first, last, cw_ref[...], cb_ref[...])
        a, u = _gates(xc, wg_ref, ba_ref[...], bx_ref[...], lam_ref[...])
        a_ref[...] = a
        u_ref[...] = u

        @pl.when(first)
        def _():
            carry_ref[...] = h0_ref[0]

        h = _scan(a_ref, u_ref, hf_ref, sub * TM, carry_ref[...], reverse=False)
        carry_ref[...] = h
        st_ref[0] = h


def _halo_specs(cfg, col, rows_prev, rows_next, blk_of, f):
    n_tiles, d = cfg[2], cfg[3]
    c = d // 2
    nb_prev = TM // rows_prev
    nb_next = TM // rows_next
    last_next = n_tiles * nb_next - 1
    prev = pl.BlockSpec((rows_prev, c), lambda j: (jnp.maximum(blk_of(j) * f * nb_prev - 1, 0), col))
    nxt = pl.BlockSpec((rows_next, c), lambda j: (jnp.minimum((blk_of(j) + 1) * f * nb_next, last_next), col))
    return prev, nxt


def _state_spec(cfg, blk_of, f):
    n_ctx_tiles, tiles_per_ctx, _, d = cfg
    n_groups = n_ctx_tiles // tiles_per_ctx
    return pl.BlockSpec((1, SUB, d // 2),
                        lambda j: (jnp.minimum(blk_of(j) * f // tiles_per_ctx, n_groups), 0, 0))


def _h0_spec(cfg, blk_of, f):
    n_ctx_tiles, d = cfg[0], cfg[3]
    return pl.BlockSpec((1, SUB, d // 2), lambda j: (jnp.where(blk_of(j) * f >= n_ctx_tiles, 1, 0), 0, 0))


def _scan_fwd_call(cfg, proj, conv_w, conv_b, wg, ba, bx, lam, h0):
    n_ctx_tiles, tiles_per_ctx, n_tiles, _ = cfg
    n_groups = n_ctx_tiles // tiles_per_ctx
    c = conv_w.shape[1]
    f = _seq_tiles(cfg)
    ident = lambda j: j
    xp_spec, xn_spec = _halo_specs(cfg, 0, CONV_LEFT * SUB, CONV_RIGHT * SUB, ident, f)
    return pl.pallas_call(
        functools.partial(_scan_fwd_kernel, cfg, f),
        grid=(n_tiles // f,),
        in_specs=[
            pl.BlockSpec((f * TM, c), lambda j: (j, 0)), xp_spec, xn_spec,
            _const_spec((CONV_W, c)), _const_spec((1, c)), _const_spec((2, c // 2, c)),
            _const_spec((1, c)), _const_spec((1, c)), _const_spec((1, c)),
            _h0_spec(cfg, ident, f),
        ],
        out_specs=[pl.BlockSpec((f * TM, c), lambda j: (j, 0)), _state_spec(cfg, ident, f)],
        out_shape=[jax.ShapeDtypeStruct((n_tiles * TM, c), F32),
                   jax.ShapeDtypeStruct((n_groups + 1, SUB, c), F32)],
        scratch_shapes=[
            pltpu.VMEM((TM + (CONV_W - 1) * SUB, c), F32),
            pltpu.VMEM((TM, c), F32), pltpu.VMEM((TM, c), F32), pltpu.VMEM((SUB, c), F32),
        ],
        compiler_params=_cparams(("arbitrary",)),
        name="lru_fwd",
    )(proj, proj, proj, conv_w, conv_b.reshape(1, c), wg, ba.reshape(1, c), bx.reshape(1, c),
      lam.reshape(1, c), h0)


def _pool_inv_counts(ctx_len, c):
    gw = c // len(POOL_WINDOWS)

    def table(t0, t_len):
        t = t0 + jnp.arange(GRID_W)
        cols = []
        for k in POOL_WINDOWS:
            left = k // 2
            right = k - 1 - left
            cnt = jnp.minimum(t + right + 1, t_len) - jnp.maximum(t - left, 0)
            cols.append(jnp.broadcast_to((1.0 / cnt.astype(F32))[:, None], (GRID_W, gw)))
        return jnp.repeat(jnp.concatenate(cols, axis=1), SUB, axis=0)

    return jnp.stack([table(p * GRID_W, ctx_len) for p in range(ctx_len // GRID_W)] + [table(0, GRID_W)])


def _gelu_tanh(x):
    k0 = 0.7978845608028654
    hx = 0.5 * x
    return hx + hx * jnp.tanh(x * (k0 + (k0 * 0.044715) * (x * x)))


def _pool_mix(pad_ref, inv_ref, pw_ref, pb, ps):
    c = pad_ref.shape[1]
    gw = c // len(POOL_WINDOWS)
    outs = []
    for g, k in enumerate(POOL_WINDOWS):
        left = k // 2
        right = k - 1 - left
        lanes = slice(g * gw, (g + 1) * gw)
        s = None
        for o in range(-left, right + 1):
            v = pad_ref[POOL_HALO + o * SUB:POOL_HALO + o * SUB + TM, lanes]
            s = v if s is None else s + v
        outs.append(s * inv_ref[0, :, lanes] - pad_ref[POOL_HALO:POOL_HALO + TM, lanes])
    d = jnp.concatenate(outs, axis=1).astype(BF16)
    half = c // 2
    y = jnp.concatenate([_dot(d[:, :half], pw_ref[0]), _dot(d[:, half:], pw_ref[1])], axis=1)
    return (y + pb) * ps


def _scan_bwd_kernel(cfg, f, proj_ref, xp_ref, xn_ref, bp_ref, bn_ref, hf_ref, *refs):
    inv_refs, refs = refs[:f], refs[f:]
    (cw_ref, cb_ref, wg_ref, ba_ref, bx_ref, lam_ref, h0_ref, pw_ref, pb_ref, ps_ref,
     y_ref, st_ref, pad_ref, ppad_ref, a_ref, u_ref, hb_ref, carry_ref) = refs
    n_blocks = cfg[2] // f
    blk = n_blocks - 1 - pl.program_id(0)
    c = hf_ref.shape[1]
    for sub in reversed(range(f)):
        rows = slice(sub * TM, (sub + 1) * TM)
        is_ctx, _, first, last = _seq_flags(cfg, blk * f + sub)
        prev, nxt = _inner_halo(proj_ref, sub, f, 0, c, (CONV_LEFT * SUB, CONV_RIGHT * SUB), xp_ref, xn_ref)
        xc = _conv(pad_ref, proj_ref[rows, 0:c], prev, nxt, first, last, cw_ref[...], cb_ref[...])
        a, u = _gates(xc, wg_ref, ba_ref[...], bx_ref[...], lam_ref[...])
        a_ref[...] = a
        u_ref[...] = u

        @pl.when(last)
        def _():
            carry_ref[...] = h0_ref[0]

        h = _scan(a_ref, u_ref, hb_ref, 0, carry_ref[...], reverse=True)
        carry_ref[...] = h
        st_ref[0] = h

        ga = proj_ref[rows, c:2 * c]
        y_a = (hf_ref[rows, :] + hb_ref[...]) * _gelu_tanh(ga)

        use_prev = jnp.logical_and(is_ctx, jnp.logical_not(first))
        use_next = jnp.logical_and(is_ctx, jnp.logical_not(last))
        bprev, bnext = _inner_halo(proj_ref, sub, f, 2 * c, c, (POOL_HALO, POOL_HALO), bp_ref, bn_ref)
        ppad_ref[0:POOL_HALO, :] = jnp.where(use_prev, bprev, 0.0)
        ppad_ref[POOL_HALO:POOL_HALO + TM, :] = proj_ref[rows, 2 * c:3 * c]
        ppad_ref[POOL_HALO + TM:, :] = jnp.where(use_next, bnext, 0.0)
        y_b = _pool_mix(ppad_ref, inv_refs[sub], pw_ref, pb_ref[...], ps_ref[...])
        y_ref[rows, :] = jnp.concatenate([y_a, y_b], axis=1).astype(BF16)


def _scan_bwd_call(cfg, inv_cnt, proj, hf, conv_w, conv_b, wg, ba, bx, lam, h0, pw, pb, ps):
    n_ctx_tiles, tiles_per_ctx, n_tiles, d = cfg
    n_groups = n_ctx_tiles // tiles_per_ctx
    c = conv_w.shape[1]
    f = _seq_tiles(cfg)
    n_blocks = n_tiles // f
    rev = lambda j: n_blocks - 1 - j
    xp_spec, xn_spec = _halo_specs(cfg, 0, CONV_LEFT * SUB, CONV_RIGHT * SUB, rev, f)
    bp_spec, bn_spec = _halo_specs(cfg, 2, POOL_HALO, POOL_HALO, rev, f)

    def inv_spec(sub):
        def idx(j):
            tile = rev(j) * f + sub
            return (jnp.where(tile < n_ctx_tiles, lax.rem(tile, tiles_per_ctx), tiles_per_ctx), 0, 0)
        return pl.BlockSpec((1, TM, c), idx)

    return pl.pallas_call(
        functools.partial(_scan_bwd_kernel, cfg, f),
        grid=(n_blocks,),
        in_specs=[
            pl.BlockSpec((f * TM, 3 * c), lambda j: (rev(j), 0)), xp_spec, xn_spec, bp_spec, bn_spec,
            pl.BlockSpec((f * TM, c), lambda j: (rev(j), 0)),
            *[inv_spec(sub) for sub in range(f)],
            _const_spec((CONV_W, c)), _const_spec((1, c)), _const_spec((2, c // 2, c)),
            _const_spec((1, c)), _const_spec((1, c)), _const_spec((1, c)),
            _h0_spec(cfg, rev, f),
            _const_spec((2, c // 2, c // 2)), _const_spec((1, c)), _const_spec((1, c)),
        ],
        out_specs=[pl.BlockSpec((f * TM, d), lambda j: (rev(j), 0)), _state_spec(cfg, rev, f)],
        out_shape=[jax.ShapeDtypeStruct((n_tiles * TM, d), BF16),
                   jax.ShapeDtypeStruct((n_groups + 1, SUB, c), F32)],
        scratch_shapes=[
            pltpu.VMEM((TM + (CONV_W - 1) * SUB, c), F32),
            pltpu.VMEM((TM + 2 * POOL_HALO, c), F32),
            pltpu.VMEM((TM, c), F32), pltpu.VMEM((TM, c), F32), pltpu.VMEM((TM, c), F32),
            pltpu.VMEM((SUB, c), F32),
        ],
        compiler_params=_cparams(("arbitrary",)),
        name="lru_bwd_mix",
    )(proj, proj, proj, proj, proj, hf, *([inv_cnt] * f), conv_w, conv_b.reshape(1, c), wg, ba.reshape(1, c),
      bx.reshape(1, c), lam.reshape(1, c), h0, pw, pb.reshape(1, c), ps.reshape(1, c))


def _out_proj(x_ref, ym_ref, mod_ref, wo_ref, g2_ref):
    xm = _gated_add(x_ref[...], mod_ref[0, 2], _dot(ym_ref[...], wo_ref[...]))
    h2 = _modulate(_rms(xm, g2_ref[...]), mod_ref[0, 4], mod_ref[0, 3])
    return xm, h2


def _swiglu_act(h2b, wg_ref, wu_ref, act_ref):
    n_chunks = act_ref.shape[1] // FF_CHUNK

    def body(f, carry):
        c0 = pl.multiple_of(f * FF_CHUNK, FF_CHUNK)
        g = _dot(h2b, wg_ref[:, pl.ds(c0, FF_CHUNK)])
        u = _dot(h2b, wu_ref[:, pl.ds(c0, FF_CHUNK)])
        act_ref[:, pl.ds(c0, FF_CHUNK)] = (g * _sigmoid(g) * u).astype(BF16)
        return carry

    lax.fori_loop(0, n_chunks, body, 0, unroll=True)


def _dense_ffn_kernel(x_ref, ym_ref, mod_ref, wo_ref, g2_ref, wg_ref, wu_ref, wd_ref, o_ref, act_ref):
    xm, h2 = _out_proj(x_ref, ym_ref, mod_ref, wo_ref, g2_ref)
    _swiglu_act(h2.astype(BF16), wg_ref, wu_ref, act_ref)
    o_ref[...] = _gated_add(xm, mod_ref[0, 5], _dot(act_ref[...], wd_ref[...]))


def _dense_ffn_call(cfg, x, ymix, mod_l, w_out, g2, wg, wu, wd):
    _, _, n_tiles, d = cfg
    d_ff = wg.shape[1]
    f = _wide(cfg)
    row_spec = pl.BlockSpec((TM * f, d), lambda j: (j, 0))
    return pl.pallas_call(
        _dense_ffn_kernel,
        grid=(n_tiles // f,),
        in_specs=[row_spec, row_spec, _mod_spec(cfg, f), _const_spec((d, d)), _const_spec((1, d)),
                  _const_spec((d, d_ff)), _const_spec((d, d_ff)), _const_spec((d_ff, d))],
        out_specs=row_spec,
        out_shape=jax.ShapeDtypeStruct((n_tiles * TM, d), F32),
        scratch_shapes=[pltpu.VMEM((TM * f, d_ff), BF16)],
        compiler_params=_cparams(("parallel",)),
        name="out_proj_dense_ffn",
    )(x, ymix, mod_l, w_out, g2.reshape(1, d), wg, wu, wd)


M_E1, M_E2, M_W1, M_W2, M_R1, M_R2 = range(6)


def _route_kernel(x_ref, ym_ref, mod_ref, wo_ref, g2_ref, rw_ref, rb_ref, tri_ref,
                  xm_ref, hp_ref, meta_ref, mt_ref, cnt_ref, run_ref):
    @pl.when(pl.program_id(0) == 0)
    def _():
        run_ref[...] = jnp.zeros_like(run_ref)

    run = run_ref[...]
    for r0 in range(0, x_ref.shape[0], TM):
        rows = pl.ds(r0, TM)
        xm = _gated_add(x_ref[rows, :], mod_ref[0, 2], _dot(ym_ref[rows, :], wo_ref[...]))
        h2 = _modulate(_rms(xm, g2_ref[...]), mod_ref[0, 4], mod_ref[0, 3])
        xm_ref[rows, :] = xm
        h_hi = h2.astype(BF16)
        h_hi32 = h_hi.astype(F32)
        hp_ref[rows, :] = _pack_pairs(h_hi32)

        h_lo = (h2 - h_hi32).astype(BF16)
        p = _dot(h_hi, rw_ref[...])
        logits = p[:, :LANES] + p[:, LANES:] + _dot(h_lo, rw_ref[:, :LANES])
        lane = lax.broadcasted_iota(jnp.int32, logits.shape, 1)
        neg = jnp.float32(-jnp.inf)
        lg = jnp.where(lane < N_EXPERTS, logits + rb_ref[...], neg)
        m1 = jnp.max(lg, axis=1, keepdims=True)
        i1 = jnp.min(jnp.where(lg == m1, lane, LANES), axis=1, keepdims=True)
        lg2 = jnp.where(lane == i1, neg, lg)
        m2 = jnp.max(lg2, axis=1, keepdims=True)
        i2 = jnp.min(jnp.where(lg2 == m2, lane, LANES), axis=1, keepdims=True)
        e2 = jnp.exp(m2 - m1)
        den = 1.0 + e2

        sel1 = lane == i1
        sel2 = lane == i2
        onehot = jnp.where(jnp.logical_or(sel1, sel2), 1.0, 0.0)
        rank = _dot(tri_ref[...], onehot.astype(BF16)) + run
        r1 = jnp.sum(jnp.where(sel1, rank, 0.0), axis=1, keepdims=True)
        r2 = jnp.sum(jnp.where(sel2, rank, 0.0), axis=1, keepdims=True)
        run = run + jnp.sum(onehot, axis=0, keepdims=True)

        meta = jnp.zeros(logits.shape, F32)
        for k, v in ((M_E1, i1.astype(F32)), (M_E2, i2.astype(F32)), (M_W1, 1.0 / den), (M_W2, e2 / den),
                     (M_R1, r1), (M_R2, r2)):
            meta = jnp.where(lane == k, v, meta)
        meta_ref[rows, :] = meta
        mt_ref[0, :, rows] = jnp.transpose(meta)[:SUB]

    run_ref[...] = run
    cnt_ref[...] = run


def _route_call(cfg, part, x, ymix, mod_l, w_out, g2, router_w, router_b):
    d = cfg[3]
    t0, nt = part
    n = nt * TM
    f = _wide(cfg)
    tm = TM * f
    s0 = t0 // f
    in_row_spec = pl.BlockSpec((tm, d), lambda j: (j + s0, 0))
    row_spec = pl.BlockSpec((tm, d), lambda j: (j, 0))
    rw = jnp.zeros((d, LANES), F32).at[:, :N_EXPERTS].set(router_w)
    rw_hi = rw.astype(BF16)
    rw = jnp.concatenate([rw_hi, (rw - rw_hi.astype(F32)).astype(BF16)], axis=1)
    rb = jnp.zeros((1, LANES), F32).at[0, :N_EXPERTS].set(router_b)
    tri = jnp.tril(jnp.ones((TM, TM), BF16), -1)
    return pl.pallas_call(
        _route_kernel,
        grid=(nt // f,),
        in_specs=[in_row_spec, in_row_spec, _mod_spec(cfg, f, t0), _const_spec((d, d)), _const_spec((1, d)),
                  _const_spec((d, 2 * LANES)), _const_spec((1, LANES)), _const_spec((TM, TM))],
        out_specs=[row_spec, pl.BlockSpec((tm, d // 2), lambda j: (j, 0)),
                   pl.BlockSpec((tm, LANES), lambda j: (j, 0)), pl.BlockSpec((1, SUB, tm), lambda j: (j, 0, 0)),
                   pl.BlockSpec((1, LANES), lambda j: (0, 0))],
        out_shape=[jax.ShapeDtypeStruct((n, d), F32), jax.ShapeDtypeStruct((n, d // 2), jnp.int32),
                   jax.ShapeDtypeStruct((n, LANES), F32), jax.ShapeDtypeStruct((n // tm, SUB, tm), F32),
                   jax.ShapeDtypeStruct((1, LANES), F32)],
        scratch_shapes=[pltpu.VMEM((1, LANES), F32)],
        compiler_params=_cparams(("arbitrary",)),
        name="out_proj_route",
    )(x, ymix, mod_l, w_out, g2.reshape(1, d), rw, rb, tri)


def _routing_tables(n_tiles, mt, cnt):
    counts = cnt[0, :N_EXPERTS].astype(jnp.int32)
    padded = ((counts + TM - 1) // TM) * TM
    ends = jnp.cumsum(padded)
    offs = ends - padded
    row = lambda k: mt[:, k, :].reshape(-1).astype(jnp.int32)
    pos1 = offs[row(M_E1)] + row(M_R1)
    pos2 = offs[row(M_E2)] + row(M_R2)
    n_sorted_tiles = 2 * n_tiles + N_EXPERTS
    starts = jnp.arange(n_sorted_tiles, dtype=jnp.int32) * TM
    tile_e = jnp.minimum(jnp.sum((starts[:, None] >= ends[None, :]).astype(jnp.int32), axis=1), N_EXPERTS - 1)
    n_active = (ends[-1] // TM).reshape(1)
    return jnp.concatenate([pos1, pos2]), tile_e, n_active


SC_CORES = 2
SC_SUBCORES = 16
SC_CHUNK = 64


def _sc_mesh():
    return plsc.VectorSubcoreMesh(core_axis_name="c", subcore_axis_name="s",
                                  num_cores=SC_CORES, num_subcores=SC_SUBCORES)


def _sc_scatter_rows(rows, idx, n_out):
    n_src, width = rows.shape
    n_idx = idx.shape[0]
    n_workers = SC_CORES * SC_SUBCORES
    per_w = n_idx // n_workers
    chunks = per_w // SC_CHUNK
    assert n_idx % (n_workers * SC_CHUNK) == 0 and n_src % per_w == 0
    idx3 = idx.reshape(n_workers, chunks, SC_CHUNK)

    def body(rows_hbm, idx_hbm, out_hbm, idx_v, rows_v, sem):
        wid = lax.axis_index("s") * SC_CORES + lax.axis_index("c")
        src_base = lax.rem(wid * per_w, n_src)
        pltpu.sync_copy(idx_hbm.at[wid], idx_v)

        @pl.loop(0, chunks)
        def _(i):
            off = pl.multiple_of(i * SC_CHUNK, SC_CHUNK)
            pltpu.sync_copy(rows_hbm.at[pl.ds(src_base + off, SC_CHUNK)], rows_v)
            pltpu.async_copy(rows_v, out_hbm.at[idx_v.at[i]], sem).wait()

    return pl.kernel(
        body,
        out_type=jax.ShapeDtypeStruct((n_out, width), rows.dtype),
        mesh=_sc_mesh(),
        scratch_types=[pltpu.VMEM((chunks, SC_CHUNK), jnp.int32), pltpu.VMEM((SC_CHUNK, width), rows.dtype),
                       pltpu.SemaphoreType.DMA],
        name="sc_scatter_rows",
    )(rows, idx3)


def _expert_kernel(te_ref, na_ref, s_ref, wgu_ref, wd_ref, o_ref):
    del te_ref

    @pl.when(pl.program_id(0) < na_ref[0])
    def _():
        h = _unpack_pairs(s_ref[...]).astype(BF16)
        gu = _dot(h, wgu_ref[0, 0])
        dfe = gu.shape[1] // 2
        g = gu[:, :dfe]
        act = (g * _sigmoid(g) * gu[:, dfe:]).astype(BF16)
        o_ref[...] = _pack_pairs(_dot(act, wd_ref[0, 0]).astype(BF16).astype(F32))

    @pl.when(pl.program_id(0) >= na_ref[0])
    def _():
        o_ref[...] = jnp.zeros_like(o_ref)


def _expert_call(cfg, tile_e, n_active, buf, wgu, wd, layer):
    d = cfg[3]
    dfe = wd.shape[2]
    n_sorted_tiles = buf.shape[0] // TM
    last = lambda t, na: jnp.maximum(jnp.minimum(t, na[0] - 1), 0)
    tile = lambda t, te, na: (last(t, na), 0)
    w_spec = lambda shape: pl.BlockSpec((1,) + shape, lambda t, te, na: (layer, te[last(t, na)], 0, 0))
    return pl.pallas_call(
        _expert_kernel,
        grid_spec=pltpu.PrefetchScalarGridSpec(
            num_scalar_prefetch=2,
            grid=(n_sorted_tiles,),
            in_specs=[pl.BlockSpec((TM, d // 2), tile), w_spec((1, d, 2 * dfe)), w_spec((1, dfe, d))],
            out_specs=pl.BlockSpec((TM, d // 2), lambda t, te, na: (t, 0)),
        ),
        out_shape=jax.ShapeDtypeStruct((n_sorted_tiles * TM, d // 2), jnp.int32),
        compiler_params=_cparams(("arbitrary",)),
        name="moe_experts",
    )(tile_e, n_active, buf, wgu, wd)


def _store_natural(n_ctx_tiles, part, y, o_refs):
    t0, nt = part
    has_ctx = t0 < n_ctx_tiles
    has_lat = t0 + nt > n_ctx_tiles
    is_ctx = pl.program_id(0) + t0 < n_ctx_tiles
    if has_ctx:
        @pl.when(is_ctx)
        def _():
            o_refs[0][0] = _from_time_major(y)

    if has_lat:
        @pl.when(jnp.logical_not(is_ctx))
        def _():
            o_refs[-1][...] = _from_time_major(y)


def _natural_out(cfg, part):
    n_ctx_tiles, tiles_per_ctx, n_tiles, d = cfg
    n_groups = n_ctx_tiles // tiles_per_ctx
    t0, nt = part
    specs = _natural_specs(cfg, t0)
    shapes = (jax.ShapeDtypeStruct((n_groups, SUB, tiles_per_ctx * GRID_W, d), F32),
              jax.ShapeDtypeStruct((SUB, (n_tiles - n_ctx_tiles) * GRID_W, d), F32))
    keep = [k for k, used in enumerate((t0 < n_ctx_tiles, t0 + nt > n_ctx_tiles)) if used]
    return [specs[k] for k in keep], [shapes[k] for k in keep], keep


def _sc_gather_rows(table, idx):
    n_rows = idx.shape[0]
    width = table.shape[1]
    n_workers = SC_CORES * SC_SUBCORES
    assert n_rows % (n_workers * SC_CHUNK) == 0
    per_w = n_rows // n_workers
    mesh = _sc_mesh()

    def body(table_hbm, idx_hbm, out_hbm, idx_v, rows_v, sem):
        wid = lax.axis_index("s") * SC_CORES + lax.axis_index("c")
        base = wid * per_w
        pltpu.sync_copy(idx_hbm.at[pl.ds(base, per_w)], idx_v)

        @pl.loop(0, per_w // SC_CHUNK)
        def _(i):
            off = pl.multiple_of(i * SC_CHUNK, SC_CHUNK)
            pltpu.async_copy(table_hbm.at[idx_v.at[pl.ds(off, SC_CHUNK)]], rows_v, sem).wait()
            pltpu.sync_copy(rows_v, out_hbm.at[pl.ds(base + off, SC_CHUNK)])

    return pl.kernel(
        body,
        out_type=jax.ShapeDtypeStruct((n_rows, width), table.dtype),
        mesh=mesh,
        scratch_types=[pltpu.VMEM((per_w,), jnp.int32), pltpu.VMEM((SC_CHUNK, width), table.dtype),
                       pltpu.SemaphoreType.DMA],
        name="sc_gather_rows",
    )(table, idx)


def _combine_kernel(final_ctx_tiles, part, n_filled, xm_ref, y1_ref, y2_ref, meta_ref, mod_ref, gfin_ref,
                    *refs):
    o_refs = refs[n_filled:]
    meta = meta_ref[...]
    y = (meta[:, M_W1:M_W1 + 1] * _unpack_pairs(y1_ref[...])
         + meta[:, M_W2:M_W2 + 1] * _unpack_pairs(y2_ref[...]))
    x = _gated_add(xm_ref[...], mod_ref[0, 5], y)
    if final_ctx_tiles is None:
        o_refs[0][...] = x
    else:
        _store_natural(final_ctx_tiles, part, _rms(x, gfin_ref[...]), o_refs)


def _combine_call(cfg, part, xm, yg, meta, mod_l, g_final, final, filled):
    n_ctx_tiles, _, n_tiles, d = cfg
    t0, nt = part
    row_spec = pl.BlockSpec((TM, d), lambda j: (j, 0))
    if final:
        out_specs, out_shape, keys = _natural_out(cfg, part)
    else:
        out_specs = [pl.BlockSpec((TM, d), lambda j: (j + t0, 0))]
        out_shape, keys = [jax.ShapeDtypeStruct((n_tiles * TM, d), F32)], [0]
    reuse = [k for k in keys if k in filled]
    base = [xm, yg, yg, meta, mod_l, g_final.reshape(1, d)]
    outs = pl.pallas_call(
        functools.partial(_combine_kernel, n_ctx_tiles if final else None, part, len(reuse)),
        grid=(nt,),
        in_specs=[row_spec, pl.BlockSpec((TM, d // 2), lambda j: (j, 0)),
                  pl.BlockSpec((TM, d // 2), lambda j: (nt + j, 0)),
                  pl.BlockSpec((TM, LANES), lambda j: (j, 0)), _mod_spec(cfg, 1, t0), _const_spec((1, d))]
                 + [pl.BlockSpec(memory_space=pl.ANY)] * len(reuse),
        out_specs=out_specs,
        out_shape=out_shape,
        input_output_aliases={len(base) + i: keys.index(k) for i, k in enumerate(reuse)},
        compiler_params=_cparams(("arbitrary",)),
        name="moe_combine",
    )(*base, *[filled[k] for k in reuse])
    return {**filled, **dict(zip(keys, outs))}


def _final_norm_kernel(n_ctx_tiles, part, x_ref, g_ref, *o_refs):
    _store_natural(n_ctx_tiles, part, _rms(x_ref[...], g_ref[...]), o_refs)


def _final_norm_call(cfg, x, g):
    n_ctx_tiles, _, n_tiles, d = cfg
    out_specs, out_shape, _ = _natural_out(cfg, (0, n_tiles))
    return pl.pallas_call(
        functools.partial(_final_norm_kernel, n_ctx_tiles, (0, n_tiles)),
        grid=(n_tiles,),
        in_specs=[pl.BlockSpec((TM, d), lambda j: (j, 0)), _const_spec((1, d))],
        out_specs=out_specs,
        out_shape=out_shape,
        compiler_params=_cparams(("arbitrary",)),
        name="final_norm",
    )(x, g.reshape(1, d))


def _block_diag(w, per_block):
    *lead, n, k, _ = w.shape
    nb = n // per_block
    w = w.reshape(*lead, nb, per_block, k, k)
    eye = jnp.eye(per_block, dtype=w.dtype)
    out = w[..., :, :, None, :] * eye[:, None, :, None]
    return out.reshape(*lead, nb, per_block * k, per_block * k)


def _pos_tables(n_ctx_tiles, n_lat_tiles, d):
    quarter = d // 4
    omega = 1.0 / (POS_BASE ** (jnp.arange(quarter, dtype=F32) / quarter))
    er = jnp.arange(n_lat_tiles, dtype=F32)[:, None] * omega
    ec = jnp.arange(GRID_W, dtype=F32)[:, None] * omega
    row_emb = jnp.concatenate([jnp.sin(er), jnp.cos(er)], axis=-1)
    col_emb = jnp.concatenate([jnp.sin(ec), jnp.cos(ec)], axis=-1)
    del n_ctx_tiles
    return row_emb[:, None, :], jnp.repeat(col_emb, SUB, axis=0)


def kernel(x_prompt, x_sample, state_lru, c, c_ctx, norm_mix_g, w_ada, b_ada, w_in, conv_w, conv_b, lru_wa, lru_ba, lru_wx, lru_bx, lru_lam, pool_w, pool_b, pool_scale, w_out, norm_ffn_g, ffn_wg, ffn_wu, ffn_wd, moe_router_w, moe_router_b, moe_wg, moe_wu, moe_wd, norm_final_g):
    bc, tc, d = x_prompt.shape
    bl, tl, _ = x_sample.shape
    depth = w_ada.shape[0]
    c_lru = conv_w.shape[-1]
    assert bl == SUB and bc % SUB == 0 and tc % GRID_W == 0 and tl % GRID_W == 0
    assert c_lru == 4 * LANES and d == 2 * c_lru
    n_groups = bc // SUB
    tiles_per_ctx = tc // GRID_W
    n_ctx_tiles = n_groups * tiles_per_ctx
    n_lat_tiles = tl // GRID_W
    n_tiles = n_ctx_tiles + n_lat_tiles
    n_ctx = n_ctx_tiles * TM
    cfg = (n_ctx_tiles, tiles_per_ctx, n_tiles, d)

    cond = jnp.concatenate([c, c_ctx[None], jnp.zeros((SUB - 1, d), F32)], axis=0)
    mod = _ada_call(cond, w_ada, b_ada)
    mod_lat = mod[:, :SUB].reshape(depth, SUB, 6, d).transpose(0, 2, 1, 3)
    mod_ctx = jnp.broadcast_to(mod[:, SUB].reshape(depth, 6, 1, d), (depth, 6, SUB, d))
    mod = jnp.stack([mod_ctx, mod_lat], axis=1)

    row_tab, col_rep = _pos_tables(n_ctx_tiles, n_lat_tiles, d)
    first = (x_prompt.reshape(n_groups, SUB, tc, d), x_sample, row_tab, col_rep)
    x = None
    wg_lru = (0.5 * jnp.concatenate([_block_diag(lru_wa, 4), _block_diag(lru_wx, 4)], axis=-1)).astype(BF16)
    pw = _block_diag(pool_w, 2).astype(BF16)
    inv_cnt = _pool_inv_counts(tc, c_lru)
    moe_wgu = jnp.concatenate([moe_wg, moe_wu], axis=3).astype(BF16)
    moe_wd_b = moe_wd.astype(BF16)
    h0 = jnp.concatenate([jnp.zeros_like(state_lru[None]), state_lru[None]], axis=0)

    states = []
    for l in range(depth):
        jdx = l // 2
        if l == 0:
            proj, x = _inproj_call(cfg, x, mod[l], norm_mix_g[l], w_in[l].astype(BF16), first)
        else:
            (proj,) = _inproj_call(cfg, x, mod[l], norm_mix_g[l], w_in[l].astype(BF16))
        hf, st_f = _scan_fwd_call(cfg, proj, conv_w[l], conv_b[l], wg_lru[l, 0], lru_ba[l, 0],
                                  lru_bx[l, 0], lru_lam[l, 0], h0[:, :, l, 0])
        ymix, st_b = _scan_bwd_call(cfg, inv_cnt, proj, hf, conv_w[l], conv_b[l], wg_lru[l, 1], lru_ba[l, 1],
                                    lru_bx[l, 1], lru_lam[l, 1], h0[:, :, l, 1], pw[l], pool_b[l],
                                    pool_scale[l])
        states.append(jnp.stack([st_f[:n_groups].reshape(bc, c_lru), st_b[:n_groups].reshape(bc, c_lru)], axis=1))
        if l % 2 == 0:
            x = _dense_ffn_call(cfg, x, ymix, mod[l], w_out[l].astype(BF16), norm_ffn_g[l],
                                ffn_wg[jdx].astype(BF16), ffn_wu[jdx].astype(BF16), ffn_wd[jdx].astype(BF16))
        else:
            final = l == depth - 1
            w_out_b = w_out[l].astype(BF16)
            filled = {}
            for part in _moe_parts(cfg):
                xm, hp, meta, mt, cnt = _route_call(cfg, part, x, ymix, mod[l], w_out_b, norm_ffn_g[l],
                                                    moe_router_w[jdx], moe_router_b[jdx])
                pos_flat, tile_e, n_active = _routing_tables(part[1], mt, cnt)
                buf = _sc_scatter_rows(hp, pos_flat, (2 * part[1] + N_EXPERTS) * TM)
                ys = _expert_call(cfg, tile_e, n_active, buf, moe_wgu, moe_wd_b, jdx)
                yg = _sc_gather_rows(ys, pos_flat)
                filled = _combine_call(cfg, part, xm, yg, meta, mod[l], norm_final_g, final, filled)
            x = (filled[0], filled[1]) if final else filled[0]

    y_prompt, y_sample = x if depth % 2 == 0 else _final_norm_call(cfg, x, norm_final_g)
    new_state = jnp.stack(states, axis=1)
    return (y_prompt.reshape(bc, tc, d), y_sample, new_state)
```

```python
import functools

import jax
import jax.numpy as jnp
from jax import lax
from jax.experimental import pallas as pl
from jax.experimental.pallas import tpu as pltpu
from jax.experimental.pallas import tpu_sc as plsc

F32 = jnp.float32
BF16 = jnp.bfloat16

SUB = 8
LANES = 128
GRID_W = 64
TM = GRID_W * SUB
POS_BASE = 10000.0
N_LRU_HEADS = 8
CONV_W = 4
CONV_LEFT = CONV_W // 2
CONV_RIGHT = CONV_W - 1 - CONV_LEFT
LRU_C = 8.0
POOL_WINDOWS = (2, 4, 8, 16)
POOL_HALO = 8 * SUB
N_EXPERTS = 8
EPS = 1e-6
FF_CHUNK = 256
VMEM_LIMIT = 56 * 1024 * 1024


def _cparams(sem):
    return pltpu.CompilerParams(dimension_semantics=sem, vmem_limit_bytes=VMEM_LIMIT)


def _const_spec(shape):
    nd = len(shape)
    return pl.BlockSpec(shape, lambda *_: (0,) * nd, pipeline_mode=pl.Buffered(1))


def _rms(x, g):
    ms = jnp.mean(x * x, axis=-1, keepdims=True)
    return x * lax.rsqrt(ms + EPS) * g


def _per_seq(x, v, op):
    r, c = x.shape
    x3 = x.reshape(r // SUB, SUB, c)
    return op(x3, v[None]).reshape(r, c)


def _modulate(h, scale, shift):
    r, c = h.shape
    h3 = h.reshape(r // SUB, SUB, c)
    return (h3 * (1.0 + scale)[None] + shift[None]).reshape(r, c)


def _gated_add(x, gate, y):
    return x + _per_seq(y, gate, lambda a, b: a * b)


def _dot(a, b):
    return jnp.dot(a, b, preferred_element_type=F32)


def _pack_pairs(x):
    bits = lax.bitcast_convert_type(x, jnp.uint32)
    half = bits.shape[1] // 2
    w = lax.shift_right_logical(bits[:, :half], jnp.uint32(16)) | (bits[:, half:] & jnp.uint32(0xFFFF0000))
    return lax.bitcast_convert_type(w, jnp.int32)


def _unpack_pairs(w):
    w = lax.bitcast_convert_type(w, jnp.uint32)
    lo = lax.bitcast_convert_type(lax.shift_left(w, jnp.uint32(16)), F32)
    hi = lax.bitcast_convert_type(w & jnp.uint32(0xFFFF0000), F32)
    return jnp.concatenate([lo, hi], axis=1)


def _sigmoid(x):
    return 0.5 * jnp.tanh(0.5 * x) + 0.5


def _ada_kernel(c_ref, w_ref, b_ref, o_ref):
    c = c_ref[...]
    s = (c * jax.nn.sigmoid(c)).astype(BF16)
    o_ref[0] = _dot(s, w_ref[0].astype(BF16)) + b_ref[0]


def _ada_call(cond, w_ada, b_ada):
    depth, d, d6 = w_ada.shape
    nr = cond.shape[0]
    bn = d6 // 4
    return pl.pallas_call(
        _ada_kernel,
        grid=(depth, d6 // bn),
        in_specs=[
            pl.BlockSpec((nr, d), lambda l, n: (0, 0)),
            pl.BlockSpec((1, d, bn), lambda l, n: (l, 0, n)),
            pl.BlockSpec((1, 1, bn), lambda l, n: (l, 0, n)),
        ],
        out_specs=pl.BlockSpec((1, nr, bn), lambda l, n: (l, 0, n)),
        out_shape=jax.ShapeDtypeStruct((depth, nr, d6), F32),
        compiler_params=_cparams(("parallel", "parallel")),
        name="ada_mod",
    )(cond, w_ada, b_ada.reshape(depth, 1, d6))


def _to_time_major(x):
    s, t, d = x.shape
    return jnp.swapaxes(x, 0, 1).reshape(s * t, d)


def _from_time_major(x):
    r, d = x.shape
    return jnp.swapaxes(x.reshape(r // SUB, SUB, d), 0, 1)


def _inproj_kernel(n_ctx_tiles, *refs):
    if n_ctx_tiles is not None:
        xp_ref, xs_ref, row_ref, col_ref, mod_ref, g_ref, w_ref, o_ref, x0_ref = refs
        f = row_ref.shape[0]
        is_ctx = pl.program_id(0) * f < n_ctx_tiles

        @pl.when(is_ctx)
        def _():
            x0_ref[...] = _to_time_major(xp_ref[0])

        @pl.when(jnp.logical_not(is_ctx))
        def _():
            x = _to_time_major(xs_ref[...])
            half = x.shape[1] // 2
            for k in range(f):
                rows = slice(k * TM, (k + 1) * TM)
                x0_ref[rows, :] = jnp.concatenate([x[rows, :half] + row_ref[k], x[rows, half:] + col_ref[...]],
                                                  axis=1)

        x = x0_ref[...]
    else:
        x_ref, mod_ref, g_ref, w_ref, o_ref = refs
        x = x_ref[...]
    h = _modulate(_rms(x, g_ref[...]), mod_ref[0, 1], mod_ref[0, 0])
    o_ref[...] = _dot(h.astype(BF16), w_ref[...])


def _natural_specs(cfg, t0=0, f=1):
    n_ctx_tiles, tiles_per_ctx, _, d = cfg

    def ctx_idx(j):
        jc = jnp.minimum(j * f + t0, n_ctx_tiles - f)
        return (jc // tiles_per_ctx, 0, lax.rem(jc, tiles_per_ctx) // f, 0)

    return (pl.BlockSpec((1, SUB, f * GRID_W, d), ctx_idx),
            pl.BlockSpec((SUB, f * GRID_W, d), lambda j: (0, jnp.maximum(j * f + t0 - n_ctx_tiles, 0) // f, 0)))


def _wide(cfg):
    n_ctx_tiles, _, n_tiles, _ = cfg
    return 2 if n_ctx_tiles % 2 == 0 and n_tiles % 2 == 0 else 1


def _moe_parts(cfg):
    n_tiles = cfg[2]
    half = n_tiles // 2
    if _wide(cfg) == 2 and half % 2 == 0:
        return [(0, half), (half, n_tiles - half)]
    return [(0, n_tiles)]


def _mod_spec(cfg, f=1, t0=0):
    n_ctx_steps = cfg[0] // f
    d = cfg[3]
    s0 = t0 // f
    return pl.BlockSpec((1, 6, SUB, d), lambda j: (jnp.where(j + s0 >= n_ctx_steps, 1, 0), 0, 0, 0))


def _inproj_call(cfg, x, mod_l, g, w_in, first=None):
    n_ctx_tiles, _, n_tiles, d = cfg
    d_in = w_in.shape[1]
    f = _seq_tiles(cfg) if first is not None else _wide(cfg)
    tm = TM * f
    row_spec = pl.BlockSpec((tm, d), lambda j: (j, 0))
    out_specs = [pl.BlockSpec((tm, d_in), lambda j: (j, 0))]
    out_shape = [jax.ShapeDtypeStruct((n_tiles * TM, d_in), F32)]
    if first is not None:
        xp_spec, xs_spec = _natural_specs(cfg, 0, f)
        in_specs = [xp_spec, xs_spec,
                    pl.BlockSpec((f, 1, d // 2), lambda j: (jnp.maximum(j - n_ctx_tiles // f, 0), 0, 0)),
                    _const_spec((TM, d // 2))]
        args = list(first)
        out_specs.append(row_spec)
        out_shape.append(jax.ShapeDtypeStruct((n_tiles * TM, d), F32))
    else:
        in_specs = [row_spec]
        args = [x]
    in_specs += [_mod_spec(cfg, f), _const_spec((1, d)), _const_spec((d, d_in))]
    args += [mod_l, g.reshape(1, d), w_in]
    return pl.pallas_call(
        functools.partial(_inproj_kernel, n_ctx_tiles if first is not None else None),
        grid=(n_tiles // f,),
        in_specs=in_specs,
        out_specs=out_specs,
        out_shape=out_shape,
        compiler_params=_cparams(("parallel",)),
        name="in_proj",
    )(*args)


def _seq_flags(cfg, j):
    n_ctx_tiles, tiles_per_ctx, n_tiles, _ = cfg
    is_ctx = j < n_ctx_tiles
    pos = lax.rem(j, tiles_per_ctx)
    first = jnp.where(is_ctx, pos == 0, j == n_ctx_tiles)
    last = jnp.where(is_ctx, pos == tiles_per_ctx - 1, j == n_tiles - 1)
    return is_ctx, pos, first, last


def _conv(pad_ref, xa, prev, nxt, first, last, cw, cb):
    lo = CONV_LEFT * SUB
    pad_ref[0:lo, :] = jnp.where(first, 0.0, prev)
    pad_ref[lo:lo + TM, :] = xa
    pad_ref[lo + TM:lo + TM + CONV_RIGHT * SUB, :] = jnp.where(last, 0.0, nxt)
    y = cb
    for k in range(CONV_W):
        y = y + pad_ref[k * SUB:k * SUB + TM, :] * cw[k:k + 1, :]
    return y


def _gates(xc, wg_ref, ba, bx, lam):
    half = xc.shape[1] // 2
    xb = xc.astype(BF16)
    g0 = _dot(xb[:, :half], wg_ref[0])
    g1 = _dot(xb[:, half:], wg_ref[1])
    t_r = jnp.tanh(jnp.concatenate([g0[:, :half], g1[:, :half]], axis=1) + 0.5 * ba)
    t_i = jnp.tanh(jnp.concatenate([g0[:, half:], g1[:, half:]], axis=1) + 0.5 * bx)
    z = -lam
    half_decay = (0.5 * LRU_C) * (jnp.maximum(z, 0.0) + jnp.log1p(jnp.exp(-jnp.abs(z))))
    neg_log_a = t_r * half_decay + half_decay
    a = jnp.exp(-neg_log_a)
    z = jnp.tanh(neg_log_a) * (a * a + 1.0)
    root = jnp.where(z > 0.0, z * lax.rsqrt(z), 0.0)
    half_xc = 0.5 * xc
    u = root * (t_i * half_xc + half_xc)
    return a, u


def _scan(a_ref, u_ref, h_ref, row0, h, reverse):
    steps = TM // SUB

    def body(k, h):
        t = steps - 1 - k if reverse else k
        r0 = pl.multiple_of(t * SUB, SUB)
        h = a_ref[pl.ds(r0, SUB), :] * h + u_ref[pl.ds(r0, SUB), :]
        h_ref[pl.ds(row0 + r0, SUB), :] = h
        return h

    return lax.fori_loop(0, steps, body, h, unroll=8)


def _seq_tiles(cfg):
    n_ctx_tiles, tiles_per_ctx, n_tiles, _ = cfg
    return 2 if n_ctx_tiles % 2 == 0 and tiles_per_ctx % 2 == 0 and n_tiles % 2 == 0 else 1


def _inner_halo(ref, sub, f, col0, c, rows, outer_prev, outer_next):
    prev = outer_prev[...] if sub == 0 else ref[sub * TM - rows[0]:sub * TM, col0:col0 + c]
    nxt = outer_next[...] if sub == f - 1 else ref[(sub + 1) * TM:(sub + 1) * TM + rows[1], col0:col0 + c]
    return prev, nxt


def _scan_fwd_kernel(cfg, f, xa_ref, xp_ref, xn_ref, cw_ref, cb_ref, wg_ref, ba_ref, bx_ref,
                     lam_ref, h0_ref, hf_ref, st_ref, pad_ref, a_ref, u_ref, carry_ref):
    c = xa_ref.shape[1]
    for sub in range(f):
        _, _, first, last = _seq_flags(cfg, pl.program_id(0) * f + sub)
        prev, nxt = _inner_halo(xa_ref, sub, f, 0, c, (CONV_LEFT * SUB, CONV_RIGHT * SUB), xp_ref, xn_ref)
        xc = _conv(pad_ref, xa_ref[sub * TM:(sub + 1) * TM, :], prev, nxt, first, last, cw_ref[...], cb_ref[...])
        a, u = _gates(xc, wg_ref, ba_ref[...], bx_ref[...], lam_ref[...])
        a_ref[...] = a
        u_ref[...] = u

        @pl.when(first)
        def _():
            carry_ref[...] = h0_ref[0]

        h = _scan(a_ref, u_ref, hf_ref, sub * TM, carry_ref[...], reverse=False)
        carry_ref[...] = h
        st_ref[0] = h


def _halo_specs(cfg, col, rows_prev, rows_next, blk_of, f):
    n_tiles, d = cfg[2], cfg[3]
    c = d // 2
    nb_prev = TM // rows_prev
    nb_next = TM // rows_next
    last_next = n_tiles * nb_next - 1
    prev = pl.BlockSpec((rows_prev, c), lambda j: (jnp.maximum(blk_of(j) * f * nb_prev - 1, 0), col))
    nxt = pl.BlockSpec((rows_next, c), lambda j: (jnp.minimum((blk_of(j) + 1) * f * nb_next, last_next), col))
    return prev, nxt


def _state_spec(cfg, blk_of, f):
    n_ctx_tiles, tiles_per_ctx, _, d = cfg
    n_groups = n_ctx_tiles // tiles_per_ctx
    return pl.BlockSpec((1, SUB, d // 2),
                        lambda j: (jnp.minimum(blk_of(j) * f // tiles_per_ctx, n_groups), 0, 0))


def _h0_spec(cfg, blk_of, f):
    n_ctx_tiles, d = cfg[0], cfg[3]
    return pl.BlockSpec((1, SUB, d // 2), lambda j: (jnp.where(blk_of(j) * f >= n_ctx_tiles, 1, 0), 0, 0))


def _scan_fwd_call(cfg, proj, conv_w, conv_b, wg, ba, bx, lam, h0):
    n_ctx_tiles, tiles_per_ctx, n_tiles, _ = cfg
    n_groups = n_ctx_tiles // tiles_per_ctx
    c = conv_w.shape[1]
    f = _seq_tiles(cfg)
    ident = lambda j: j
    xp_spec, xn_spec = _halo_specs(cfg, 0, CONV_LEFT * SUB, CONV_RIGHT * SUB, ident, f)
    return pl.pallas_call(
        functools.partial(_scan_fwd_kernel, cfg, f),
        grid=(n_tiles // f,),
        in_specs=[
            pl.BlockSpec((f * TM, c), lambda j: (j, 0)), xp_spec, xn_spec,
            _const_spec((CONV_W, c)), _const_spec((1, c)), _const_spec((2, c // 2, c)),
            _const_spec((1, c)), _const_spec((1, c)), _const_spec((1, c)),
            _h0_spec(cfg, ident, f),
        ],
        out_specs=[pl.BlockSpec((f * TM, c), lambda j: (j, 0)), _state_spec(cfg, ident, f)],
        out_shape=[jax.ShapeDtypeStruct((n_tiles * TM, c), F32),
                   jax.ShapeDtypeStruct((n_groups + 1, SUB, c), F32)],
        scratch_shapes=[
            pltpu.VMEM((TM + (CONV_W - 1) * SUB, c), F32),
            pltpu.VMEM((TM, c), F32), pltpu.VMEM((TM, c), F32), pltpu.VMEM((SUB, c), F32),
        ],
        compiler_params=_cparams(("arbitrary",)),
        name="lru_fwd",
    )(proj, proj, proj, conv_w, conv_b.reshape(1, c), wg, ba.reshape(1, c), bx.reshape(1, c),
      lam.reshape(1, c), h0)


def _pool_inv_counts(ctx_len, c):
    gw = c // len(POOL_WINDOWS)

    def table(t0, t_len):
        t = t0 + jnp.arange(GRID_W)
        cols = []
        for k in POOL_WINDOWS:
            left = k // 2
            right = k - 1 - left
            cnt = jnp.minimum(t + right + 1, t_len) - jnp.maximum(t - left, 0)
            cols.append(jnp.broadcast_to((1.0 / cnt.astype(F32))[:, None], (GRID_W, gw)))
        return jnp.repeat(jnp.concatenate(cols, axis=1), SUB, axis=0)

    return jnp.stack([table(p * GRID_W, ctx_len) for p in range(ctx_len // GRID_W)] + [table(0, GRID_W)])


def _gelu_tanh(x):
    k0 = 0.7978845608028654
    hx = 0.5 * x
    return hx + hx * jnp.tanh(x * (k0 + (k0 * 0.044715) * (x * x)))


def _pool_mix(pad_ref, inv_ref, pw_ref, pb, ps):
    c = pad_ref.shape[1]
    gw = c // len(POOL_WINDOWS)
    outs = []
    for g, k in enumerate(POOL_WINDOWS):
        left = k // 2
        right = k - 1 - left
        lanes = slice(g * gw, (g + 1) * gw)
        s = None
        for o in range(-left, right + 1):
            v = pad_ref[POOL_HALO + o * SUB:POOL_HALO + o * SUB + TM, lanes]
            s = v if s is None else s + v
        outs.append(s * inv_ref[0, :, lanes] - pad_ref[POOL_HALO:POOL_HALO + TM, lanes])
    d = jnp.concatenate(outs, axis=1).astype(BF16)
    half = c // 2
    y = jnp.concatenate([_dot(d[:, :half], pw_ref[0]), _dot(d[:, half:], pw_ref[1])], axis=1)
    return (y + pb) * ps


def _scan_bwd_kernel(cfg, f, proj_ref, xp_ref, xn_ref, bp_ref, bn_ref, hf_ref, *refs):
    inv_refs, refs = refs[:f], refs[f:]
    (cw_ref, cb_ref, wg_ref, ba_ref, bx_ref, lam_ref, h0_ref, pw_ref, pb_ref, ps_ref,
     y_ref, st_ref, pad_ref, ppad_ref, a_ref, u_ref, hb_ref, carry_ref) = refs
    n_blocks = cfg[2] // f
    blk = n_blocks - 1 - pl.program_id(0)
    c = hf_ref.shape[1]
    for sub in reversed(range(f)):
        rows = slice(sub * TM, (sub + 1) * TM)
        is_ctx, _, first, last = _seq_flags(cfg, blk * f + sub)
        prev, nxt = _inner_halo(proj_ref, sub, f, 0, c, (CONV_LEFT * SUB, CONV_RIGHT * SUB), xp_ref, xn_ref)
        xc = _conv(pad_ref, proj_ref[rows, 0:c], prev, nxt, first, last, cw_ref[...], cb_ref[...])
        a, u = _gates(xc, wg_ref, ba_ref[...], bx_ref[...], lam_ref[...])
        a_ref[...] = a
        u_ref[...] = u

        @pl.when(last)
        def _():
            carry_ref[...] = h0_ref[0]

        h = _scan(a_ref, u_ref, hb_ref, 0, carry_ref[...], reverse=True)
        carry_ref[...] = h
        st_ref[0] = h

        ga = proj_ref[rows, c:2 * c]
        y_a = (hf_ref[rows, :] + hb_ref[...]) * _gelu_tanh(ga)

        use_prev = jnp.logical_and(is_ctx, jnp.logical_not(first))
        use_next = jnp.logical_and(is_ctx, jnp.logical_not(last))
        bprev, bnext = _inner_halo(proj_ref, sub, f, 2 * c, c, (POOL_HALO, POOL_HALO), bp_ref, bn_ref)
        ppad_ref[0:POOL_HALO, :] = jnp.where(use_prev, bprev, 0.0)
        ppad_ref[POOL_HALO:POOL_HALO + TM, :] = proj_ref[rows, 2 * c:3 * c]
        ppad_ref[POOL_HALO + TM:, :] = jnp.where(use_next, bnext, 0.0)
        y_b = _pool_mix(ppad_ref, inv_refs[sub], pw_ref, pb_ref[...], ps_ref[...])
        y_ref[rows, :] = jnp.concatenate([y_a, y_b], axis=1).astype(BF16)


def _scan_bwd_call(cfg, inv_cnt, proj, hf, conv_w, conv_b, wg, ba, bx, lam, h0, pw, pb, ps):
    n_ctx_tiles, tiles_per_ctx, n_tiles, d = cfg
    n_groups = n_ctx_tiles // tiles_per_ctx
    c = conv_w.shape[1]
    f = _seq_tiles(cfg)
    n_blocks = n_tiles // f
    rev = lambda j: n_blocks - 1 - j
    xp_spec, xn_spec = _halo_specs(cfg, 0, CONV_LEFT * SUB, CONV_RIGHT * SUB, rev, f)
    bp_spec, bn_spec = _halo_specs(cfg, 2, POOL_HALO, POOL_HALO, rev, f)

    def inv_spec(sub):
        def idx(j):
            tile = rev(j) * f + sub
            return (jnp.where(tile < n_ctx_tiles, lax.rem(tile, tiles_per_ctx), tiles_per_ctx), 0, 0)
        return pl.BlockSpec((1, TM, c), idx)

    return pl.pallas_call(
        functools.partial(_scan_bwd_kernel, cfg, f),
        grid=(n_blocks,),
        in_specs=[
            pl.BlockSpec((f * TM, 3 * c), lambda j: (rev(j), 0)), xp_spec, xn_spec, bp_spec, bn_spec,
            pl.BlockSpec((f * TM, c), lambda j: (rev(j), 0)),
            *[inv_spec(sub) for sub in range(f)],
            _const_spec((CONV_W, c)), _const_spec((1, c)), _const_spec((2, c // 2, c)),
            _const_spec((1, c)), _const_spec((1, c)), _const_spec((1, c)),
            _h0_spec(cfg, rev, f),
            _const_spec((2, c // 2, c // 2)), _const_spec((1, c)), _const_spec((1, c)),
        ],
        out_specs=[pl.BlockSpec((f * TM, d), lambda j: (rev(j), 0)), _state_spec(cfg, rev, f)],
        out_shape=[jax.ShapeDtypeStruct((n_tiles * TM, d), BF16),
                   jax.ShapeDtypeStruct((n_groups + 1, SUB, c), F32)],
        scratch_shapes=[
            pltpu.VMEM((TM + (CONV_W - 1) * SUB, c), F32),
            pltpu.VMEM((TM + 2 * POOL_HALO, c), F32),
            pltpu.VMEM((TM, c), F32), pltpu.VMEM((TM, c), F32), pltpu.VMEM((TM, c), F32),
            pltpu.VMEM((SUB, c), F32),
        ],
        compiler_params=_cparams(("arbitrary",)),
        name="lru_bwd_mix",
    )(proj, proj, proj, proj, proj, hf, *([inv_cnt] * f), conv_w, conv_b.reshape(1, c), wg, ba.reshape(1, c),
      bx.reshape(1, c), lam.reshape(1, c), h0, pw, pb.reshape(1, c), ps.reshape(1, c))


def _out_proj(x_ref, ym_ref, mod_ref, wo_ref, g2_ref):
    xm = _gated_add(x_ref[...], mod_ref[0, 2], _dot(ym_ref[...], wo_ref[...]))
    h2 = _modulate(_rms(xm, g2_ref[...]), mod_ref[0, 4], mod_ref[0, 3])
    return xm, h2


def _swiglu_act(h2b, wg_ref, wu_ref, act_ref):
    n_chunks = act_ref.shape[1] // FF_CHUNK

    def body(f, carry):
        c0 = pl.multiple_of(f * FF_CHUNK, FF_CHUNK)
        g = _dot(h2b, wg_ref[:, pl.ds(c0, FF_CHUNK)])
        u = _dot(h2b, wu_ref[:, pl.ds(c0, FF_CHUNK)])
        act_ref[:, pl.ds(c0, FF_CHUNK)] = (g * _sigmoid(g) * u).astype(BF16)
        return carry

    lax.fori_loop(0, n_chunks, body, 0, unroll=True)


def _dense_ffn_kernel(x_ref, ym_ref, mod_ref, wo_ref, g2_ref, wg_ref, wu_ref, wd_ref, o_ref, act_ref):
    xm, h2 = _out_proj(x_ref, ym_ref, mod_ref, wo_ref, g2_ref)
    _swiglu_act(h2.astype(BF16), wg_ref, wu_ref, act_ref)
    o_ref[...] = _gated_add(xm, mod_ref[0, 5], _dot(act_ref[...], wd_ref[...]))


def _dense_ffn_call(cfg, x, ymix, mod_l, w_out, g2, wg, wu, wd):
    _, _, n_tiles, d = cfg
    d_ff = wg.shape[1]
    f = _wide(cfg)
    row_spec = pl.BlockSpec((TM * f, d), lambda j: (j, 0))
    return pl.pallas_call(
        _dense_ffn_kernel,
        grid=(n_tiles // f,),
        in_specs=[row_spec, row_spec, _mod_spec(cfg, f), _const_spec((d, d)), _const_spec((1, d)),
                  _const_spec((d, d_ff)), _const_spec((d, d_ff)), _const_spec((d_ff, d))],
        out_specs=row_spec,
        out_shape=jax.ShapeDtypeStruct((n_tiles * TM, d), F32),
        scratch_shapes=[pltpu.VMEM((TM * f, d_ff), BF16)],
        compiler_params=_cparams(("parallel",)),
        name="out_proj_dense_ffn",
    )(x, ymix, mod_l, w_out, g2.reshape(1, d), wg, wu, wd)


M_E1, M_E2, M_W1, M_W2, M_R1, M_R2 = range(6)


def _route_kernel(cast_weights, x_ref, ym_ref, mod_ref, wo_ref, g2_ref, rw_ref, rb_ref, tri_ref, *refs):
    if cast_weights:
        wg_ref, wu_ref, wd_ref, *refs = refs
        wgu_ref, wdb_ref = refs[5:7]
        wgu_ref[...] = jnp.concatenate([wg_ref[0], wu_ref[0]], axis=1).astype(BF16)
        wdb_ref[...] = wd_ref[0].astype(BF16)
        refs = refs[:5] + refs[7:]
    xm_ref, hp_ref, meta_ref, mt_ref, cnt_ref, run_ref = refs

    @pl.when(pl.program_id(0) == 0)
    def _():
        run_ref[...] = jnp.zeros_like(run_ref)

    run = run_ref[...]
    for r0 in range(0, x_ref.shape[0], TM):
        rows = pl.ds(r0, TM)
        xm = _gated_add(x_ref[rows, :], mod_ref[0, 2], _dot(ym_ref[rows, :], wo_ref[...]))
        h2 = _modulate(_rms(xm, g2_ref[...]), mod_ref[0, 4], mod_ref[0, 3])
        xm_ref[rows, :] = xm
        h_hi = h2.astype(BF16)
        h_hi32 = h_hi.astype(F32)
        hp_ref[rows, :] = _pack_pairs(h_hi32)

        h_lo = (h2 - h_hi32).astype(BF16)
        p = _dot(h_hi, rw_ref[...])
        logits = p[:, :LANES] + p[:, LANES:] + _dot(h_lo, rw_ref[:, :LANES])
        lane = lax.broadcasted_iota(jnp.int32, logits.shape, 1)
        neg = jnp.float32(-jnp.inf)
        lg = jnp.where(lane < N_EXPERTS, logits + rb_ref[...], neg)
        m1 = jnp.max(lg, axis=1, keepdims=True)
        i1 = jnp.min(jnp.where(lg == m1, lane, LANES), axis=1, keepdims=True)
        lg2 = jnp.where(lane == i1, neg, lg)
        m2 = jnp.max(lg2, axis=1, keepdims=True)
        i2 = jnp.min(jnp.where(lg2 == m2, lane, LANES), axis=1, keepdims=True)
        e2 = jnp.exp(m2 - m1)
        den = 1.0 + e2

        sel1 = lane == i1
        sel2 = lane == i2
        onehot = jnp.where(jnp.logical_or(sel1, sel2), 1.0, 0.0)
        rank = _dot(tri_ref[...], onehot.astype(BF16)) + run
        r1 = jnp.sum(jnp.where(sel1, rank, 0.0), axis=1, keepdims=True)
        r2 = jnp.sum(jnp.where(sel2, rank, 0.0), axis=1, keepdims=True)
        run = run + jnp.sum(onehot, axis=0, keepdims=True)

        meta = jnp.zeros(logits.shape, F32)
        for k, v in ((M_E1, i1.astype(F32)), (M_E2, i2.astype(F32)), (M_W1, 1.0 / den), (M_W2, e2 / den),
                     (M_R1, r1), (M_R2, r2)):
            meta = jnp.where(lane == k, v, meta)
        meta_ref[rows, :] = meta
        mt_ref[0, :, rows] = jnp.transpose(meta)[:SUB]

    run_ref[...] = run
    cnt_ref[...] = run


def _cast_blocks(steps, rows):
    nblk = 1
    while nblk * 2 <= steps and rows % (nblk * 2 * 2 * SUB) == 0:
        nblk *= 2
    return nblk


def _route_call(cfg, part, x, ymix, mod_l, w_out, g2, router_w, router_b, expert_w=None):
    d = cfg[3]
    t0, nt = part
    n = nt * TM
    f = _wide(cfg)
    tm = TM * f
    s0 = t0 // f
    steps = nt // f
    in_row_spec = pl.BlockSpec((tm, d), lambda j: (j + s0, 0))
    row_spec = pl.BlockSpec((tm, d), lambda j: (j, 0))
    rw = jnp.zeros((d, LANES), F32).at[:, :N_EXPERTS].set(router_w)
    rw_hi = rw.astype(BF16)
    rw = jnp.concatenate([rw_hi, (rw - rw_hi.astype(F32)).astype(BF16)], axis=1)
    rb = jnp.zeros((1, LANES), F32).at[0, :N_EXPERTS].set(router_b)
    tri = jnp.tril(jnp.ones((TM, TM), BF16), -1)
    in_specs = [in_row_spec, in_row_spec, _mod_spec(cfg, f, t0), _const_spec((d, d)), _const_spec((1, d)),
                _const_spec((d, 2 * LANES)), _const_spec((1, LANES)), _const_spec((TM, TM))]
    args = [x, ymix, mod_l, w_out, g2.reshape(1, d), rw, rb, tri]
    out_specs = [row_spec, pl.BlockSpec((tm, d // 2), lambda j: (j, 0)),
                 pl.BlockSpec((tm, LANES), lambda j: (j, 0)), pl.BlockSpec((1, SUB, tm), lambda j: (j, 0, 0)),
                 pl.BlockSpec((1, LANES), lambda j: (0, 0))]
    out_shape = [jax.ShapeDtypeStruct((n, d), F32), jax.ShapeDtypeStruct((n, d // 2), jnp.int32),
                 jax.ShapeDtypeStruct((n, LANES), F32), jax.ShapeDtypeStruct((n // tm, SUB, tm), F32),
                 jax.ShapeDtypeStruct((1, LANES), F32)]
    if expert_w is not None:
        wg, wu, wd, layer = expert_w
        n_l, n_e, _, dfe = wg.shape
        nblk = min(_cast_blocks(steps, n_e * d), _cast_blocks(steps, n_e * dfe))
        blk = lambda j: (jnp.minimum(j, nblk - 1), 0)
        blk_in = lambda j: (layer, jnp.minimum(j, nblk - 1), 0)
        ru, rd = n_e * d // nblk, n_e * dfe // nblk
        in_specs += [pl.BlockSpec((1, ru, dfe), blk_in), pl.BlockSpec((1, ru, dfe), blk_in),
                     pl.BlockSpec((1, rd, d), blk_in)]
        args += [wg.reshape(n_l, n_e * d, dfe), wu.reshape(n_l, n_e * d, dfe), wd.reshape(n_l, n_e * dfe, d)]
        out_specs += [pl.BlockSpec((ru, 2 * dfe), blk), pl.BlockSpec((rd, d), blk)]
        out_shape += [jax.ShapeDtypeStruct((n_e * d, 2 * dfe), BF16), jax.ShapeDtypeStruct((n_e * dfe, d), BF16)]
    outs = pl.pallas_call(
        functools.partial(_route_kernel, expert_w is not None),
        grid=(steps,),
        in_specs=in_specs,
        out_specs=out_specs,
        out_shape=out_shape,
        scratch_shapes=[pltpu.VMEM((1, LANES), F32)],
        compiler_params=_cparams(("arbitrary",)),
        name="out_proj_route",
    )(*args)
    if expert_w is None:
        return outs, None
    return outs[:5], (outs[5].reshape(n_e, d, 2 * dfe), outs[6].reshape(n_e, dfe, d))


def _routing_tables(n_tiles, mt, cnt):
    counts = cnt[0, :N_EXPERTS].astype(jnp.int32)
    padded = ((counts + TM - 1) // TM) * TM
    ends = jnp.cumsum(padded)
    offs = ends - padded
    row = lambda k: mt[:, k, :].reshape(-1).astype(jnp.int32)
    pos1 = offs[row(M_E1)] + row(M_R1)
    pos2 = offs[row(M_E2)] + row(M_R2)
    n_sorted_tiles = 2 * n_tiles + N_EXPERTS
    starts = jnp.arange(n_sorted_tiles, dtype=jnp.int32) * TM
    tile_e = jnp.minimum(jnp.sum((starts[:, None] >= ends[None, :]).astype(jnp.int32), axis=1), N_EXPERTS - 1)
    n_active = (ends[-1] // TM).reshape(1)
    return jnp.concatenate([pos1, pos2]), tile_e, n_active


SC_CORES = 2
SC_SUBCORES = 16
SC_CHUNK = 64


def _sc_mesh():
    return plsc.VectorSubcoreMesh(core_axis_name="c", subcore_axis_name="s",
                                  num_cores=SC_CORES, num_subcores=SC_SUBCORES)


def _sc_scatter_rows(rows, idx, n_out):
    n_src, width = rows.shape
    n_idx = idx.shape[0]
    n_workers = SC_CORES * SC_SUBCORES
    per_w = n_idx // n_workers
    chunks = per_w // SC_CHUNK
    assert n_idx % (n_workers * SC_CHUNK) == 0 and n_src % per_w == 0
    idx3 = idx.reshape(n_workers, chunks, SC_CHUNK)

    def body(rows_hbm, idx_hbm, out_hbm, idx_v, rows_v, sem):
        wid = lax.axis_index("s") * SC_CORES + lax.axis_index("c")
        src_base = lax.rem(wid * per_w, n_src)
        pltpu.sync_copy(idx_hbm.at[wid], idx_v)

        @pl.loop(0, chunks)
        def _(i):
            off = pl.multiple_of(i * SC_CHUNK, SC_CHUNK)
            pltpu.sync_copy(rows_hbm.at[pl.ds(src_base + off, SC_CHUNK)], rows_v)
            pltpu.async_copy(rows_v, out_hbm.at[idx_v.at[i]], sem).wait()

    return pl.kernel(
        body,
        out_type=jax.ShapeDtypeStruct((n_out, width), rows.dtype),
        mesh=_sc_mesh(),
        scratch_types=[pltpu.VMEM((chunks, SC_CHUNK), jnp.int32), pltpu.VMEM((SC_CHUNK, width), rows.dtype),
                       pltpu.SemaphoreType.DMA],
        name="sc_scatter_rows",
    )(rows, idx3)


def _expert_kernel(te_ref, na_ref, s_ref, wgu_ref, wd_ref, o_ref):
    del te_ref

    @pl.when(pl.program_id(0) < na_ref[0])
    def _():
        h = _unpack_pairs(s_ref[...]).astype(BF16)
        gu = _dot(h, wgu_ref[0])
        dfe = gu.shape[1] // 2
        g = gu[:, :dfe]
        act = (g * _sigmoid(g) * gu[:, dfe:]).astype(BF16)
        o_ref[...] = _pack_pairs(_dot(act, wd_ref[0]).astype(BF16).astype(F32))

    @pl.when(pl.program_id(0) >= na_ref[0])
    def _():
        o_ref[...] = jnp.zeros_like(o_ref)


def _expert_call(cfg, tile_e, n_active, buf, wgu, wd):
    d = cfg[3]
    dfe = wd.shape[1]
    n_sorted_tiles = buf.shape[0] // TM
    last = lambda t, na: jnp.maximum(jnp.minimum(t, na[0] - 1), 0)
    tile = lambda t, te, na: (last(t, na), 0)
    w_spec = lambda shape: pl.BlockSpec(shape, lambda t, te, na: (te[last(t, na)], 0, 0))
    return pl.pallas_call(
        _expert_kernel,
        grid_spec=pltpu.PrefetchScalarGridSpec(
            num_scalar_prefetch=2,
            grid=(n_sorted_tiles,),
            in_specs=[pl.BlockSpec((TM, d // 2), tile), w_spec((1, d, 2 * dfe)), w_spec((1, dfe, d))],
            out_specs=pl.BlockSpec((TM, d // 2), lambda t, te, na: (t, 0)),
        ),
        out_shape=jax.ShapeDtypeStruct((n_sorted_tiles * TM, d // 2), jnp.int32),
        compiler_params=_cparams(("arbitrary",)),
        name="moe_experts",
    )(tile_e, n_active, buf, wgu, wd)


def _store_natural(n_ctx_tiles, part, y, o_refs):
    t0, nt = part
    has_ctx = t0 < n_ctx_tiles
    has_lat = t0 + nt > n_ctx_tiles
    is_ctx = pl.program_id(0) + t0 < n_ctx_tiles
    if has_ctx:
        @pl.when(is_ctx)
        def _():
            o_refs[0][0] = _from_time_major(y)

    if has_lat:
        @pl.when(jnp.logical_not(is_ctx))
        def _():
            o_refs[-1][...] = _from_time_major(y)


def _natural_out(cfg, part):
    n_ctx_tiles, tiles_per_ctx, n_tiles, d = cfg
    n_groups = n_ctx_tiles // tiles_per_ctx
    t0, nt = part
    specs = _natural_specs(cfg, t0)
    shapes = (jax.ShapeDtypeStruct((n_groups, SUB, tiles_per_ctx * GRID_W, d), F32),
              jax.ShapeDtypeStruct((SUB, (n_tiles - n_ctx_tiles) * GRID_W, d), F32))
    keep = [k for k, used in enumerate((t0 < n_ctx_tiles, t0 + nt > n_ctx_tiles)) if used]
    return [specs[k] for k in keep], [shapes[k] for k in keep], keep


def _sc_gather_rows(table, idx):
    n_rows = idx.shape[0]
    width = table.shape[1]
    n_workers = SC_CORES * SC_SUBCORES
    assert n_rows % (n_workers * SC_CHUNK) == 0
    per_w = n_rows // n_workers
    mesh = _sc_mesh()

    def body(table_hbm, idx_hbm, out_hbm, idx_v, rows_v, sem):
        wid = lax.axis_index("s") * SC_CORES + lax.axis_index("c")
        base = wid * per_w
        pltpu.sync_copy(idx_hbm.at[pl.ds(base, per_w)], idx_v)

        @pl.loop(0, per_w // SC_CHUNK)
        def _(i):
            off = pl.multiple_of(i * SC_CHUNK, SC_CHUNK)
            pltpu.async_copy(table_hbm.at[idx_v.at[pl.ds(off, SC_CHUNK)]], rows_v, sem).wait()
            pltpu.sync_copy(rows_v, out_hbm.at[pl.ds(base + off, SC_CHUNK)])

    return pl.kernel(
        body,
        out_type=jax.ShapeDtypeStruct((n_rows, width), table.dtype),
        mesh=mesh,
        scratch_types=[pltpu.VMEM((per_w,), jnp.int32), pltpu.VMEM((SC_CHUNK, width), table.dtype),
                       pltpu.SemaphoreType.DMA],
        name="sc_gather_rows",
    )(table, idx)


def _combine_kernel(final_ctx_tiles, part, n_filled, xm_ref, y1_ref, y2_ref, meta_ref, mod_ref, gfin_ref,
                    *refs):
    o_refs = refs[n_filled:]
    meta = meta_ref[...]
    y = (meta[:, M_W1:M_W1 + 1] * _unpack_pairs(y1_ref[...])
         + meta[:, M_W2:M_W2 + 1] * _unpack_pairs(y2_ref[...]))
    x = _gated_add(xm_ref[...], mod_ref[0, 5], y)
    if final_ctx_tiles is None:
        o_refs[0][...] = x
    else:
        _store_natural(final_ctx_tiles, part, _rms(x, gfin_ref[...]), o_refs)


def _combine_call(cfg, part, xm, yg, meta, mod_l, g_final, final, filled):
    n_ctx_tiles, _, n_tiles, d = cfg
    t0, nt = part
    row_spec = pl.BlockSpec((TM, d), lambda j: (j, 0))
    if final:
        out_specs, out_shape, keys = _natural_out(cfg, part)
    else:
        out_specs = [pl.BlockSpec((TM, d), lambda j: (j + t0, 0))]
        out_shape, keys = [jax.ShapeDtypeStruct((n_tiles * TM, d), F32)], [0]
    reuse = [k for k in keys if k in filled]
    base = [xm, yg, yg, meta, mod_l, g_final.reshape(1, d)]
    outs = pl.pallas_call(
        functools.partial(_combine_kernel, n_ctx_tiles if final else None, part, len(reuse)),
        grid=(nt,),
        in_specs=[row_spec, pl.BlockSpec((TM, d // 2), lambda j: (j, 0)),
                  pl.BlockSpec((TM, d // 2), lambda j: (nt + j, 0)),
                  pl.BlockSpec((TM, LANES), lambda j: (j, 0)), _mod_spec(cfg, 1, t0), _const_spec((1, d))]
                 + [pl.BlockSpec(memory_space=pl.ANY)] * len(reuse),
        out_specs=out_specs,
        out_shape=out_shape,
        input_output_aliases={len(base) + i: keys.index(k) for i, k in enumerate(reuse)},
        compiler_params=_cparams(("arbitrary",)),
        name="moe_combine",
    )(*base, *[filled[k] for k in reuse])
    return {**filled, **dict(zip(keys, outs))}


def _final_norm_kernel(n_ctx_tiles, part, x_ref, g_ref, *o_refs):
    _store_natural(n_ctx_tiles, part, _rms(x_ref[...], g_ref[...]), o_refs)


def _final_norm_call(cfg, x, g):
    n_ctx_tiles, _, n_tiles, d = cfg
    out_specs, out_shape, _ = _natural_out(cfg, (0, n_tiles))
    return pl.pallas_call(
        functools.partial(_final_norm_kernel, n_ctx_tiles, (0, n_tiles)),
        grid=(n_tiles,),
        in_specs=[pl.BlockSpec((TM, d), lambda j: (j, 0)), _const_spec((1, d))],
        out_specs=out_specs,
        out_shape=out_shape,
        compiler_params=_cparams(("arbitrary",)),
        name="final_norm",
    )(x, g.reshape(1, d))


def _block_diag(w, per_block):
    *lead, n, k, _ = w.shape
    nb = n // per_block
    w = w.reshape(*lead, nb, per_block, k, k)
    eye = jnp.eye(per_block, dtype=w.dtype)
    out = w[..., :, :, None, :] * eye[:, None, :, None]
    return out.reshape(*lead, nb, per_block * k, per_block * k)


def _pos_tables(n_ctx_tiles, n_lat_tiles, d):
    quarter = d // 4
    omega = 1.0 / (POS_BASE ** (jnp.arange(quarter, dtype=F32) / quarter))
    er = jnp.arange(n_lat_tiles, dtype=F32)[:, None] * omega
    ec = jnp.arange(GRID_W, dtype=F32)[:, None] * omega
    row_emb = jnp.concatenate([jnp.sin(er), jnp.cos(er)], axis=-1)
    col_emb = jnp.concatenate([jnp.sin(ec), jnp.cos(ec)], axis=-1)
    del n_ctx_tiles
    return row_emb[:, None, :], jnp.repeat(col_emb, SUB, axis=0)


def kernel(x_prompt, x_sample, state_lru, c, c_ctx, norm_mix_g, w_ada, b_ada, w_in, conv_w, conv_b, lru_wa, lru_ba, lru_wx, lru_bx, lru_lam, pool_w, pool_b, pool_scale, w_out, norm_ffn_g, ffn_wg, ffn_wu, ffn_wd, moe_router_w, moe_router_b, moe_wg, moe_wu, moe_wd, norm_final_g):
    bc, tc, d = x_prompt.shape
    bl, tl, _ = x_sample.shape
    depth = w_ada.shape[0]
    c_lru = conv_w.shape[-1]
    assert bl == SUB and bc % SUB == 0 and tc % GRID_W == 0 and tl % GRID_W == 0
    assert c_lru == 4 * LANES and d == 2 * c_lru
    n_groups = bc // SUB
    tiles_per_ctx = tc // GRID_W
    n_ctx_tiles = n_groups * tiles_per_ctx
    n_lat_tiles = tl // GRID_W
    n_tiles = n_ctx_tiles + n_lat_tiles
    n_ctx = n_ctx_tiles * TM
    cfg = (n_ctx_tiles, tiles_per_ctx, n_tiles, d)

    cond = jnp.concatenate([c, c_ctx[None], jnp.zeros((SUB - 1, d), F32)], axis=0)
    mod = _ada_call(cond, w_ada, b_ada)
    mod_lat = mod[:, :SUB].reshape(depth, SUB, 6, d).transpose(0, 2, 1, 3)
    mod_ctx = jnp.broadcast_to(mod[:, SUB].reshape(depth, 6, 1, d), (depth, 6, SUB, d))
    mod = jnp.stack([mod_ctx, mod_lat], axis=1)

    row_tab, col_rep = _pos_tables(n_ctx_tiles, n_lat_tiles, d)
    first = (x_prompt.reshape(n_groups, SUB, tc, d), x_sample, row_tab, col_rep)
    x = None
    wg_lru = (0.5 * jnp.concatenate([_block_diag(lru_wa, 4), _block_diag(lru_wx, 4)], axis=-1)).astype(BF16)
    pw = _block_diag(pool_w, 2).astype(BF16)
    inv_cnt = _pool_inv_counts(tc, c_lru)
    h0 = jnp.concatenate([jnp.zeros_like(state_lru[None]), state_lru[None]], axis=0)

    states = []
    for l in range(depth):
        jdx = l // 2
        if l == 0:
            proj, x = _inproj_call(cfg, x, mod[l], norm_mix_g[l], w_in[l].astype(BF16), first)
        else:
            (proj,) = _inproj_call(cfg, x, mod[l], norm_mix_g[l], w_in[l].astype(BF16))
        hf, st_f = _scan_fwd_call(cfg, proj, conv_w[l], conv_b[l], wg_lru[l, 0], lru_ba[l, 0],
                                  lru_bx[l, 0], lru_lam[l, 0], h0[:, :, l, 0])
        ymix, st_b = _scan_bwd_call(cfg, inv_cnt, proj, hf, conv_w[l], conv_b[l], wg_lru[l, 1], lru_ba[l, 1],
                                    lru_bx[l, 1], lru_lam[l, 1], h0[:, :, l, 1], pw[l], pool_b[l],
                                    pool_scale[l])
        states.append(jnp.stack([st_f[:n_groups].reshape(bc, c_lru), st_b[:n_groups].reshape(bc, c_lru)], axis=1))
        if l % 2 == 0:
            x = _dense_ffn_call(cfg, x, ymix, mod[l], w_out[l].astype(BF16), norm_ffn_g[l],
                                ffn_wg[jdx].astype(BF16), ffn_wu[jdx].astype(BF16), ffn_wd[jdx].astype(BF16))
        else:
            final = l == depth - 1
            w_out_b = w_out[l].astype(BF16)
            filled = {}
            expert_w = None
            for part in _moe_parts(cfg):
                to_cast = (moe_wg, moe_wu, moe_wd, jdx) if expert_w is None else None
                (xm, hp, meta, mt, cnt), cast = _route_call(cfg, part, x, ymix, mod[l], w_out_b, norm_ffn_g[l],
                                                            moe_router_w[jdx], moe_router_b[jdx], to_cast)
                expert_w = expert_w or cast
                pos_flat, tile_e, n_active = _routing_tables(part[1], mt, cnt)
                buf = _sc_scatter_rows(hp, pos_flat, (2 * part[1] + N_EXPERTS) * TM)
                ys = _expert_call(cfg, tile_e, n_active, buf, *expert_w)
                yg = _sc_gather_rows(ys, pos_flat)
                filled = _combine_call(cfg, part, xm, yg, meta, mod[l], norm_final_g, final, filled)
            x = (filled[0], filled[1]) if final else filled[0]

    y_prompt, y_sample = x if depth % 2 == 0 else _final_norm_call(cfg, x, norm_final_g)
    new_state = jnp.stack(states, axis=1)
    return (y_prompt.reshape(bc, tc, d), y_sample, new_state)
```

```python
import functools

import jax
import jax.numpy as jnp
from jax import lax
from jax.experimental import pallas as pl
from jax.experimental.pallas import tpu as pltpu
from jax.experimental.pallas import tpu_sc as plsc

F32 = jnp.float32
BF16 = jnp.bfloat16

SUB = 8
LANES = 128
GRID_W = 64
TM = GRID_W * SUB
POS_BASE = 10000.0
N_LRU_HEADS = 8
CONV_W = 4
CONV_LEFT = CONV_W // 2
CONV_RIGHT = CONV_W - 1 - CONV_LEFT
LRU_C = 8.0
POOL_WINDOWS = (2, 4, 8, 16)
POOL_HALO = 8 * SUB
N_EXPERTS = 8
EPS = 1e-6
FF_CHUNK = 256
VMEM_LIMIT = 56 * 1024 * 1024


def _cparams(sem):
    return pltpu.CompilerParams(dimension_semantics=sem, vmem_limit_bytes=VMEM_LIMIT)


def _const_spec(shape):
    nd = len(shape)
    return pl.BlockSpec(shape, lambda *_: (0,) * nd, pipeline_mode=pl.Buffered(1))


def _rms(x, g):
    ms = jnp.mean(x * x, axis=-1, keepdims=True)
    return x * lax.rsqrt(ms + EPS) * g


def _per_seq(x, v, op):
    r, c = x.shape
    x3 = x.reshape(r // SUB, SUB, c)
    return op(x3, v[None]).reshape(r, c)


def _modulate(h, scale, shift):
    r, c = h.shape
    h3 = h.reshape(r // SUB, SUB, c)
    return (h3 * (1.0 + scale)[None] + shift[None]).reshape(r, c)


def _gated_add(x, gate, y):
    return x + _per_seq(y, gate, lambda a, b: a * b)


def _dot(a, b):
    return jnp.dot(a, b, preferred_element_type=F32)


def _pack_pairs(x):
    bits = lax.bitcast_convert_type(x, jnp.uint32)
    half = bits.shape[1] // 2
    w = lax.shift_right_logical(bits[:, :half], jnp.uint32(16)) | (bits[:, half:] & jnp.uint32(0xFFFF0000))
    return lax.bitcast_convert_type(w, jnp.int32)


def _unpack_pairs(w):
    w = lax.bitcast_convert_type(w, jnp.uint32)
    lo = lax.bitcast_convert_type(lax.shift_left(w, jnp.uint32(16)), F32)
    hi = lax.bitcast_convert_type(w & jnp.uint32(0xFFFF0000), F32)
    return jnp.concatenate([lo, hi], axis=1)


def _sigmoid(x):
    return 0.5 * jnp.tanh(0.5 * x) + 0.5


def _ada_kernel(c_ref, w_ref, b_ref, o_ref):
    c = c_ref[...]
    s = (c * jax.nn.sigmoid(c)).astype(BF16)
    o_ref[0] = _dot(s, w_ref[0].astype(BF16)) + b_ref[0]


def _ada_call(cond, w_ada, b_ada):
    depth, d, d6 = w_ada.shape
    nr = cond.shape[0]
    bn = d6 // 4
    return pl.pallas_call(
        _ada_kernel,
        grid=(depth, d6 // bn),
        in_specs=[
            pl.BlockSpec((nr, d), lambda l, n: (0, 0)),
            pl.BlockSpec((1, d, bn), lambda l, n: (l, 0, n)),
            pl.BlockSpec((1, 1, bn), lambda l, n: (l, 0, n)),
        ],
        out_specs=pl.BlockSpec((1, nr, bn), lambda l, n: (l, 0, n)),
        out_shape=jax.ShapeDtypeStruct((depth, nr, d6), F32),
        compiler_params=_cparams(("parallel", "parallel")),
        name="ada_mod",
    )(cond, w_ada, b_ada.reshape(depth, 1, d6))


def _to_time_major(x):
    s, t, d = x.shape
    return jnp.swapaxes(x, 0, 1).reshape(s * t, d)


def _from_time_major(x):
    r, d = x.shape
    return jnp.swapaxes(x.reshape(r // SUB, SUB, d), 0, 1)


def _inproj_kernel(n_ctx_tiles, *refs):
    if n_ctx_tiles is not None:
        xp_ref, xs_ref, row_ref, col_ref, mod_ref, g_ref, w_ref, o_ref, x0_ref = refs
        f = row_ref.shape[0]
        is_ctx = pl.program_id(0) * f < n_ctx_tiles

        @pl.when(is_ctx)
        def _():
            x0_ref[...] = _to_time_major(xp_ref[0])

        @pl.when(jnp.logical_not(is_ctx))
        def _():
            x = _to_time_major(xs_ref[...])
            half = x.shape[1] // 2
            for k in range(f):
                rows = slice(k * TM, (k + 1) * TM)
                x0_ref[rows, :] = jnp.concatenate([x[rows, :half] + row_ref[k], x[rows, half:] + col_ref[...]],
                                                  axis=1)

        x = x0_ref[...]
    else:
        x_ref, mod_ref, g_ref, w_ref, o_ref = refs
        x = x_ref[...]
    h = _modulate(_rms(x, g_ref[...]), mod_ref[0, 1], mod_ref[0, 0])
    o_ref[...] = _dot(h.astype(BF16), w_ref[...])


def _natural_specs(cfg, t0=0, f=1):
    n_ctx_tiles, tiles_per_ctx, _, d = cfg

    def ctx_idx(j):
        jc = jnp.minimum(j * f + t0, n_ctx_tiles - f)
        return (jc // tiles_per_ctx, 0, lax.rem(jc, tiles_per_ctx) // f, 0)

    return (pl.BlockSpec((1, SUB, f * GRID_W, d), ctx_idx),
            pl.BlockSpec((SUB, f * GRID_W, d), lambda j: (0, jnp.maximum(j * f + t0 - n_ctx_tiles, 0) // f, 0)))


def _wide(cfg):
    n_ctx_tiles, _, n_tiles, _ = cfg
    return 2 if n_ctx_tiles % 2 == 0 and n_tiles % 2 == 0 else 1


def _moe_parts(cfg):
    n_tiles = cfg[2]
    half = n_tiles // 2
    if _wide(cfg) == 2 and half % 2 == 0:
        return [(0, half), (half, n_tiles - half)]
    return [(0, n_tiles)]


def _mod_spec(cfg, f=1, t0=0):
    n_ctx_steps = cfg[0] // f
    d = cfg[3]
    s0 = t0 // f
    return pl.BlockSpec((1, 6, SUB, d), lambda j: (jnp.where(j + s0 >= n_ctx_steps, 1, 0), 0, 0, 0))


def _inproj_call(cfg, x, mod_l, g, w_in, first=None):
    n_ctx_tiles, _, n_tiles, d = cfg
    d_in = w_in.shape[1]
    f = _seq_tiles(cfg) if first is not None else _wide(cfg)
    tm = TM * f
    row_spec = pl.BlockSpec((tm, d), lambda j: (j, 0))
    out_specs = [pl.BlockSpec((tm, d_in), lambda j: (j, 0))]
    out_shape = [jax.ShapeDtypeStruct((n_tiles * TM, d_in), F32)]
    if first is not None:
        xp_spec, xs_spec = _natural_specs(cfg, 0, f)
        in_specs = [xp_spec, xs_spec,
                    pl.BlockSpec((f, 1, d // 2), lambda j: (jnp.maximum(j - n_ctx_tiles // f, 0), 0, 0)),
                    _const_spec((TM, d // 2))]
        args = list(first)
        out_specs.append(row_spec)
        out_shape.append(jax.ShapeDtypeStruct((n_tiles * TM, d), F32))
    else:
        in_specs = [row_spec]
        args = [x]
    in_specs += [_mod_spec(cfg, f), _const_spec((1, d)), _const_spec((d, d_in))]
    args += [mod_l, g.reshape(1, d), w_in]
    return pl.pallas_call(
        functools.partial(_inproj_kernel, n_ctx_tiles if first is not None else None),
        grid=(n_tiles // f,),
        in_specs=in_specs,
        out_specs=out_specs,
        out_shape=out_shape,
        compiler_params=_cparams(("parallel",)),
        name="in_proj",
    )(*args)


def _seq_flags(cfg, j):
    n_ctx_tiles, tiles_per_ctx, n_tiles, _ = cfg
    is_ctx = j < n_ctx_tiles
    pos = lax.rem(j, tiles_per_ctx)
    first = jnp.where(is_ctx, pos == 0, j == n_ctx_tiles)
    last = jnp.where(is_ctx, pos == tiles_per_ctx - 1, j == n_tiles - 1)
    return is_ctx, pos, first, last


def _conv(pad_ref, xa, prev, nxt, first, last, cw, cb):
    lo = CONV_LEFT * SUB
    pad_ref[0:lo, :] = jnp.where(first, 0.0, prev)
    pad_ref[lo:lo + TM, :] = xa
    pad_ref[lo + TM:lo + TM + CONV_RIGHT * SUB, :] = jnp.where(last, 0.0, nxt)
    y = cb
    for k in range(CONV_W):
        y = y + pad_ref[k * SUB:k * SUB + TM, :] * cw[k:k + 1, :]
    return y


def _gates(xc, wg_ref, ba, bx, lam):
    half = xc.shape[1] // 2
    xb = xc.astype(BF16)
    g0 = _dot(xb[:, :half], wg_ref[0])
    g1 = _dot(xb[:, half:], wg_ref[1])
    t_r = jnp.tanh(jnp.concatenate([g0[:, :half], g1[:, :half]], axis=1) + 0.5 * ba)
    t_i = jnp.tanh(jnp.concatenate([g0[:, half:], g1[:, half:]], axis=1) + 0.5 * bx)
    z = -lam
    half_decay = (0.5 * LRU_C) * (jnp.maximum(z, 0.0) + jnp.log1p(jnp.exp(-jnp.abs(z))))
    neg_log_a = t_r * half_decay + half_decay
    a = jnp.exp(-neg_log_a)
    z = jnp.tanh(neg_log_a) * (a * a + 1.0)
    root = jnp.where(z > 0.0, z * lax.rsqrt(z), 0.0)
    half_xc = 0.5 * xc
    u = root * (t_i * half_xc + half_xc)
    return a, u


def _scan(a_ref, u_ref, h_ref, row0, h, reverse):
    steps = TM // SUB

    def body(k, h):
        t = steps - 1 - k if reverse else k
        r0 = pl.multiple_of(t * SUB, SUB)
        h = a_ref[pl.ds(r0, SUB), :] * h + u_ref[pl.ds(r0, SUB), :]
        h_ref[pl.ds(row0 + r0, SUB), :] = h
        return h

    return lax.fori_loop(0, steps, body, h, unroll=8)


def _seq_tiles(cfg):
    n_ctx_tiles, tiles_per_ctx, n_tiles, _ = cfg
    return 2 if n_ctx_tiles % 2 == 0 and tiles_per_ctx % 2 == 0 and n_tiles % 2 == 0 else 1


def _inner_halo(ref, sub, f, col0, c, rows, outer_prev, outer_next):
    prev = outer_prev[...] if sub == 0 else ref[sub * TM - rows[0]:sub * TM, col0:col0 + c]
    nxt = outer_next[...] if sub == f - 1 else ref[(sub + 1) * TM:(sub + 1) * TM + rows[1], col0:col0 + c]
    return prev, nxt


def _scan_fwd_kernel(cfg, f, xa_ref, xp_ref, xn_ref, cw_ref, cb_ref, wg_ref, ba_ref, bx_ref,
                     lam_ref, h0_ref, hf_ref, st_ref, pad_ref, a_ref, u_ref, carry_ref):
    c = xa_ref.shape[1]
    for sub in range(f):
        _, _, first, last = _seq_flags(cfg, pl.program_id(0) * f + sub)
        prev, nxt = _inner_halo(xa_ref, sub, f, 0, c, (CONV_LEFT * SUB, CONV_RIGHT * SUB), xp_ref, xn_ref)
        xc = _conv(pad_ref, xa_ref[sub * TM:(sub + 1) * TM, :], prev, nxt, first, last, cw_ref[...], cb_ref[...])
        a, u = _gates(xc, wg_ref, ba_ref[...], bx_ref[...], lam_ref[...])
        a_ref[...] = a
        u_ref[...] = u

        @pl.when(first)
        def _():
            carry_ref[...] = h0_ref[0]

        h = _scan(a_ref, u_ref, hf_ref, sub * TM, carry_ref[...], reverse=False)
        carry_ref[...] = h
        st_ref[0] = h


def _halo_specs(cfg, col, rows_prev, rows_next, blk_of, f):
    n_tiles, d = cfg[2], cfg[3]
    c = d // 2
    nb_prev = TM // rows_prev
    nb_next = TM // rows_next
    last_next = n_tiles * nb_next - 1
    prev = pl.BlockSpec((rows_prev, c), lambda j: (jnp.maximum(blk_of(j) * f * nb_prev - 1, 0), col))
    nxt = pl.BlockSpec((rows_next, c), lambda j: (jnp.minimum((blk_of(j) + 1) * f * nb_next, last_next), col))
    return prev, nxt


def _state_spec(cfg, blk_of, f):
    n_ctx_tiles, tiles_per_ctx, _, d = cfg
    n_groups = n_ctx_tiles // tiles_per_ctx
    return pl.BlockSpec((1, SUB, d // 2),
                        lambda j: (jnp.minimum(blk_of(j) * f // tiles_per_ctx, n_groups), 0, 0))


def _h0_spec(cfg, blk_of, f):
    n_ctx_tiles, d = cfg[0], cfg[3]
    return pl.BlockSpec((1, SUB, d // 2), lambda j: (jnp.where(blk_of(j) * f >= n_ctx_tiles, 1, 0), 0, 0))


def _scan_fwd_call(cfg, proj, conv_w, conv_b, wg, ba, bx, lam, h0):
    n_ctx_tiles, tiles_per_ctx, n_tiles, _ = cfg
    n_groups = n_ctx_tiles // tiles_per_ctx
    c = conv_w.shape[1]
    f = _seq_tiles(cfg)
    ident = lambda j: j
    xp_spec, xn_spec = _halo_specs(cfg, 0, CONV_LEFT * SUB, CONV_RIGHT * SUB, ident, f)
    return pl.pallas_call(
        functools.partial(_scan_fwd_kernel, cfg, f),
        grid=(n_tiles // f,),
        in_specs=[
            pl.BlockSpec((f * TM, c), lambda j: (j, 0)), xp_spec, xn_spec,
            _const_spec((CONV_W, c)), _const_spec((1, c)), _const_spec((2, c // 2, c)),
            _const_spec((1, c)), _const_spec((1, c)), _const_spec((1, c)),
            _h0_spec(cfg, ident, f),
        ],
        out_specs=[pl.BlockSpec((f * TM, c), lambda j: (j, 0)), _state_spec(cfg, ident, f)],
        out_shape=[jax.ShapeDtypeStruct((n_tiles * TM, c), F32),
                   jax.ShapeDtypeStruct((n_groups + 1, SUB, c), F32)],
        scratch_shapes=[
            pltpu.VMEM((TM + (CONV_W - 1) * SUB, c), F32),
            pltpu.VMEM((TM, c), F32), pltpu.VMEM((TM, c), F32), pltpu.VMEM((SUB, c), F32),
        ],
        compiler_params=_cparams(("arbitrary",)),
        name="lru_fwd",
    )(proj, proj, proj, conv_w, conv_b.reshape(1, c), wg, ba.reshape(1, c), bx.reshape(1, c),
      lam.reshape(1, c), h0)


def _pool_inv_counts(ctx_len, c):
    gw = c // len(POOL_WINDOWS)

    def table(t0, t_len):
        t = t0 + jnp.arange(GRID_W)
        cols = []
        for k in POOL_WINDOWS:
            left = k // 2
            right = k - 1 - left
            cnt = jnp.minimum(t + right + 1, t_len) - jnp.maximum(t - left, 0)
            cols.append(jnp.broadcast_to((1.0 / cnt.astype(F32))[:, None], (GRID_W, gw)))
        return jnp.repeat(jnp.concatenate(cols, axis=1), SUB, axis=0)

    return jnp.stack([table(p * GRID_W, ctx_len) for p in range(ctx_len // GRID_W)] + [table(0, GRID_W)])


def _gelu_tanh(x):
    k0 = 0.7978845608028654
    hx = 0.5 * x
    return hx + hx * jnp.tanh(x * (k0 + (k0 * 0.044715) * (x * x)))


def _pool_mix(pad_ref, inv_ref, pw_ref, pb, ps):
    c = pad_ref.shape[1]
    gw = c // len(POOL_WINDOWS)
    outs = []
    for g, k in enumerate(POOL_WINDOWS):
        left = k // 2
        right = k - 1 - left
        lanes = slice(g * gw, (g + 1) * gw)
        s = None
        for o in range(-left, right + 1):
            v = pad_ref[POOL_HALO + o * SUB:POOL_HALO + o * SUB + TM, lanes]
            s = v if s is None else s + v
        outs.append(s * inv_ref[0, :, lanes] - pad_ref[POOL_HALO:POOL_HALO + TM, lanes])
    d = jnp.concatenate(outs, axis=1).astype(BF16)
    half = c // 2
    y = jnp.concatenate([_dot(d[:, :half], pw_ref[0]), _dot(d[:, half:], pw_ref[1])], axis=1)
    return (y + pb) * ps


def _scan_bwd_kernel(cfg, f, proj_ref, xp_ref, xn_ref, bp_ref, bn_ref, hf_ref, *refs):
    inv_refs, refs = refs[:f], refs[f:]
    (cw_ref, cb_ref, wg_ref, ba_ref, bx_ref, lam_ref, h0_ref, pw_ref, pb_ref, ps_ref,
     y_ref, st_ref, pad_ref, ppad_ref, a_ref, u_ref, hb_ref, carry_ref) = refs
    n_blocks = cfg[2] // f
    blk = n_blocks - 1 - pl.program_id(0)
    c = hf_ref.shape[1]
    for sub in reversed(range(f)):
        rows = slice(sub * TM, (sub + 1) * TM)
        is_ctx, _, first, last = _seq_flags(cfg, blk * f + sub)
        prev, nxt = _inner_halo(proj_ref, sub, f, 0, c, (CONV_LEFT * SUB, CONV_RIGHT * SUB), xp_ref, xn_ref)
        xc = _conv(pad_ref, proj_ref[rows, 0:c], prev, nxt, first, last, cw_ref[...], cb_ref[...])
        a, u = _gates(xc, wg_ref, ba_ref[...], bx_ref[...], lam_ref[...])
        a_ref[...] = a
        u_ref[...] = u

        @pl.when(last)
        def _():
            carry_ref[...] = h0_ref[0]

        h = _scan(a_ref, u_ref, hb_ref, 0, carry_ref[...], reverse=True)
        carry_ref[...] = h
        st_ref[0] = h

        ga = proj_ref[rows, c:2 * c]
        y_a = (hf_ref[rows, :] + hb_ref[...]) * _gelu_tanh(ga)

        use_prev = jnp.logical_and(is_ctx, jnp.logical_not(first))
        use_next = jnp.logical_and(is_ctx, jnp.logical_not(last))
        bprev, bnext = _inner_halo(proj_ref, sub, f, 2 * c, c, (POOL_HALO, POOL_HALO), bp_ref, bn_ref)
        ppad_ref[0:POOL_HALO, :] = jnp.where(use_prev, bprev, 0.0)
        ppad_ref[POOL_HALO:POOL_HALO + TM, :] = proj_ref[rows, 2 * c:3 * c]
        ppad_ref[POOL_HALO + TM:, :] = jnp.where(use_next, bnext, 0.0)
        y_b = _pool_mix(ppad_ref, inv_refs[sub], pw_ref, pb_ref[...], ps_ref[...])
        y_ref[rows, :] = jnp.concatenate([y_a, y_b], axis=1).astype(BF16)


def _scan_bwd_call(cfg, inv_cnt, proj, hf, conv_w, conv_b, wg, ba, bx, lam, h0, pw, pb, ps):
    n_ctx_tiles, tiles_per_ctx, n_tiles, d = cfg
    n_groups = n_ctx_tiles // tiles_per_ctx
    c = conv_w.shape[1]
    f = _seq_tiles(cfg)
    n_blocks = n_tiles // f
    rev = lambda j: n_blocks - 1 - j
    xp_spec, xn_spec = _halo_specs(cfg, 0, CONV_LEFT * SUB, CONV_RIGHT * SUB, rev, f)
    bp_spec, bn_spec = _halo_specs(cfg, 2, POOL_HALO, POOL_HALO, rev, f)

    def inv_spec(sub):
        def idx(j):
            tile = rev(j) * f + sub
            return (jnp.where(tile < n_ctx_tiles, lax.rem(tile, tiles_per_ctx), tiles_per_ctx), 0, 0)
        return pl.BlockSpec((1, TM, c), idx)

    return pl.pallas_call(
        functools.partial(_scan_bwd_kernel, cfg, f),
        grid=(n_blocks,),
        in_specs=[
            pl.BlockSpec((f * TM, 3 * c), lambda j: (rev(j), 0)), xp_spec, xn_spec, bp_spec, bn_spec,
            pl.BlockSpec((f * TM, c), lambda j: (rev(j), 0)),
            *[inv_spec(sub) for sub in range(f)],
            _const_spec((CONV_W, c)), _const_spec((1, c)), _const_spec((2, c // 2, c)),
            _const_spec((1, c)), _const_spec((1, c)), _const_spec((1, c)),
            _h0_spec(cfg, rev, f),
            _const_spec((2, c // 2, c // 2)), _const_spec((1, c)), _const_spec((1, c)),
        ],
        out_specs=[pl.BlockSpec((f * TM, d), lambda j: (rev(j), 0)), _state_spec(cfg, rev, f)],
        out_shape=[jax.ShapeDtypeStruct((n_tiles * TM, d), BF16),
                   jax.ShapeDtypeStruct((n_groups + 1, SUB, c), F32)],
        scratch_shapes=[
            pltpu.VMEM((TM + (CONV_W - 1) * SUB, c), F32),
            pltpu.VMEM((TM + 2 * POOL_HALO, c), F32),
            pltpu.VMEM((TM, c), F32), pltpu.VMEM((TM, c), F32), pltpu.VMEM((TM, c), F32),
            pltpu.VMEM((SUB, c), F32),
        ],
        compiler_params=_cparams(("arbitrary",)),
        name="lru_bwd_mix",
    )(proj, proj, proj, proj, proj, hf, *([inv_cnt] * f), conv_w, conv_b.reshape(1, c), wg, ba.reshape(1, c),
      bx.reshape(1, c), lam.reshape(1, c), h0, pw, pb.reshape(1, c), ps.reshape(1, c))


def _out_proj(x_ref, ym_ref, mod_ref, wo_ref, g2_ref):
    xm = _gated_add(x_ref[...], mod_ref[0, 2], _dot(ym_ref[...], wo_ref[...]))
    h2 = _modulate(_rms(xm, g2_ref[...]), mod_ref[0, 4], mod_ref[0, 3])
    return xm, h2


def _swiglu_act(h2b, wg_ref, wu_ref, act_ref):
    n_chunks = act_ref.shape[1] // FF_CHUNK

    def body(f, carry):
        c0 = pl.multiple_of(f * FF_CHUNK, FF_CHUNK)
        g = _dot(h2b, wg_ref[:, pl.ds(c0, FF_CHUNK)])
        u = _dot(h2b, wu_ref[:, pl.ds(c0, FF_CHUNK)])
        act_ref[:, pl.ds(c0, FF_CHUNK)] = (g * _sigmoid(g) * u).astype(BF16)
        return carry

    lax.fori_loop(0, n_chunks, body, 0, unroll=True)


def _dense_ffn_kernel(x_ref, ym_ref, mod_ref, wo_ref, g2_ref, wg_ref, wu_ref, wd_ref, o_ref, act_ref):
    xm, h2 = _out_proj(x_ref, ym_ref, mod_ref, wo_ref, g2_ref)
    _swiglu_act(h2.astype(BF16), wg_ref, wu_ref, act_ref)
    o_ref[...] = _gated_add(xm, mod_ref[0, 5], _dot(act_ref[...], wd_ref[...]))


def _dense_ffn_call(cfg, x, ymix, mod_l, w_out, g2, wg, wu, wd):
    _, _, n_tiles, d = cfg
    d_ff = wg.shape[1]
    f = _wide(cfg)
    row_spec = pl.BlockSpec((TM * f, d), lambda j: (j, 0))
    return pl.pallas_call(
        _dense_ffn_kernel,
        grid=(n_tiles // f,),
        in_specs=[row_spec, row_spec, _mod_spec(cfg, f), _const_spec((d, d)), _const_spec((1, d)),
                  _const_spec((d, d_ff)), _const_spec((d, d_ff)), _const_spec((d_ff, d))],
        out_specs=row_spec,
        out_shape=jax.ShapeDtypeStruct((n_tiles * TM, d), F32),
        scratch_shapes=[pltpu.VMEM((TM * f, d_ff), BF16)],
        compiler_params=_cparams(("parallel",)),
        name="out_proj_dense_ffn",
    )(x, ymix, mod_l, w_out, g2.reshape(1, d), wg, wu, wd)


M_E1, M_E2, M_W1, M_W2, M_R1, M_R2 = range(6)


def _route_kernel(cast_weights, x_ref, ym_ref, mod_ref, wo_ref, g2_ref, rw_ref, rb_ref, tri_ref, *refs):
    if cast_weights:
        wg_ref, wu_ref, wd_ref, *refs = refs
        wgu_ref, wdb_ref = refs[5:7]
        wgu_ref[...] = jnp.concatenate([wg_ref[0], wu_ref[0]], axis=1).astype(BF16)
        wdb_ref[...] = wd_ref[0].astype(BF16)
        refs = refs[:5] + refs[7:]
    xm_ref, hp_ref, meta_ref, mt_ref, cnt_ref, run_ref = refs

    @pl.when(pl.program_id(0) == 0)
    def _():
        run_ref[...] = jnp.zeros_like(run_ref)

    run = run_ref[...]
    for r0 in range(0, x_ref.shape[0], TM):
        rows = pl.ds(r0, TM)
        xm = _gated_add(x_ref[rows, :], mod_ref[0, 2], _dot(ym_ref[rows, :], wo_ref[...]))
        h2 = _modulate(_rms(xm, g2_ref[...]), mod_ref[0, 4], mod_ref[0, 3])
        xm_ref[rows, :] = xm
        h_hi = h2.astype(BF16)
        h_hi32 = h_hi.astype(F32)
        hp_ref[rows, :] = _pack_pairs(h_hi32)

        h_lo = (h2 - h_hi32).astype(BF16)
        p = _dot(h_hi, rw_ref[...])
        logits = p[:, :LANES] + p[:, LANES:] + _dot(h_lo, rw_ref[:, :LANES])
        lane = lax.broadcasted_iota(jnp.int32, logits.shape, 1)
        neg = jnp.float32(-jnp.inf)
        lg = jnp.where(lane < N_EXPERTS, logits + rb_ref[...], neg)
        m1 = jnp.max(lg, axis=1, keepdims=True)
        i1 = jnp.min(jnp.where(lg == m1, lane, LANES), axis=1, keepdims=True)
        lg2 = jnp.where(lane == i1, neg, lg)
        m2 = jnp.max(lg2, axis=1, keepdims=True)
        i2 = jnp.min(jnp.where(lg2 == m2, lane, LANES), axis=1, keepdims=True)
        e2 = jnp.exp(m2 - m1)
        den = 1.0 + e2

        sel1 = lane == i1
        sel2 = lane == i2
        onehot = jnp.where(jnp.logical_or(sel1, sel2), 1.0, 0.0)
        rank = _dot(tri_ref[...], onehot.astype(BF16)) + run
        r1 = jnp.sum(jnp.where(sel1, rank, 0.0), axis=1, keepdims=True)
        r2 = jnp.sum(jnp.where(sel2, rank, 0.0), axis=1, keepdims=True)
        run = run + jnp.sum(onehot, axis=0, keepdims=True)

        meta = jnp.zeros(logits.shape, F32)
        for k, v in ((M_E1, i1.astype(F32)), (M_E2, i2.astype(F32)), (M_W1, 1.0 / den), (M_W2, e2 / den),
                     (M_R1, r1), (M_R2, r2)):
            meta = jnp.where(lane == k, v, meta)
        meta_ref[rows, :] = meta
        mt_ref[:, rows] = jnp.transpose(meta)[:SUB]

    run_ref[...] = run
    cnt_ref[...] = run


def _cast_blocks(steps, rows):
    nblk = 1
    while nblk * 2 <= steps and rows % (nblk * 2 * 2 * SUB) == 0:
        nblk *= 2
    return nblk


def _route_call(cfg, part, x, ymix, mod_l, w_out, g2, router_w, router_b, expert_w=None):
    d = cfg[3]
    t0, nt = part
    n = nt * TM
    f = _wide(cfg)
    tm = TM * f
    s0 = t0 // f
    steps = nt // f
    in_row_spec = pl.BlockSpec((tm, d), lambda j: (j + s0, 0))
    row_spec = pl.BlockSpec((tm, d), lambda j: (j, 0))
    rw = jnp.zeros((d, LANES), F32).at[:, :N_EXPERTS].set(router_w)
    rw_hi = rw.astype(BF16)
    rw = jnp.concatenate([rw_hi, (rw - rw_hi.astype(F32)).astype(BF16)], axis=1)
    rb = jnp.zeros((1, LANES), F32).at[0, :N_EXPERTS].set(router_b)
    tri = jnp.tril(jnp.ones((TM, TM), BF16), -1)
    in_specs = [in_row_spec, in_row_spec, _mod_spec(cfg, f, t0), _const_spec((d, d)), _const_spec((1, d)),
                _const_spec((d, 2 * LANES)), _const_spec((1, LANES)), _const_spec((TM, TM))]
    args = [x, ymix, mod_l, w_out, g2.reshape(1, d), rw, rb, tri]
    out_specs = [row_spec, pl.BlockSpec((tm, d // 2), lambda j: (j, 0)),
                 pl.BlockSpec((tm, LANES), lambda j: (j, 0)), pl.BlockSpec((SUB, tm), lambda j: (0, j)),
                 pl.BlockSpec((1, LANES), lambda j: (0, 0))]
    out_shape = [jax.ShapeDtypeStruct((n, d), F32), jax.ShapeDtypeStruct((n, d // 2), jnp.int32),
                 jax.ShapeDtypeStruct((n, LANES), F32), jax.ShapeDtypeStruct((SUB, n), F32),
                 jax.ShapeDtypeStruct((1, LANES), F32)]
    if expert_w is not None:
        wg, wu, wd, layer = expert_w
        n_l, n_e, _, dfe = wg.shape
        nblk = min(_cast_blocks(steps, n_e * d), _cast_blocks(steps, n_e * dfe))
        blk = lambda j: (jnp.minimum(j, nblk - 1), 0)
        blk_in = lambda j: (layer, jnp.minimum(j, nblk - 1), 0)
        ru, rd = n_e * d // nblk, n_e * dfe // nblk
        in_specs += [pl.BlockSpec((1, ru, dfe), blk_in), pl.BlockSpec((1, ru, dfe), blk_in),
                     pl.BlockSpec((1, rd, d), blk_in)]
        args += [wg.reshape(n_l, n_e * d, dfe), wu.reshape(n_l, n_e * d, dfe), wd.reshape(n_l, n_e * dfe, d)]
        out_specs += [pl.BlockSpec((ru, 2 * dfe), blk), pl.BlockSpec((rd, d), blk)]
        out_shape += [jax.ShapeDtypeStruct((n_e * d, 2 * dfe), BF16), jax.ShapeDtypeStruct((n_e * dfe, d), BF16)]
    outs = pl.pallas_call(
        functools.partial(_route_kernel, expert_w is not None),
        grid=(steps,),
        in_specs=in_specs,
        out_specs=out_specs,
        out_shape=out_shape,
        scratch_shapes=[pltpu.VMEM((1, LANES), F32)],
        compiler_params=_cparams(("arbitrary",)),
        name="out_proj_route",
    )(*args)
    if expert_w is None:
        return outs, None
    return outs[:5], (outs[5].reshape(n_e, d, 2 * dfe), outs[6].reshape(n_e, dfe, d))


def _routing_tables(n_tiles, mt, cnt):
    counts = cnt[0, :N_EXPERTS].astype(jnp.int32)
    padded = ((counts + TM - 1) // TM) * TM
    ends = jnp.cumsum(padded)
    offs = ends - padded
    row = lambda k: mt[k].astype(jnp.int32)
    pos1 = offs[row(M_E1)] + row(M_R1)
    pos2 = offs[row(M_E2)] + row(M_R2)
    n_sorted_tiles = 2 * n_tiles + N_EXPERTS
    starts = jnp.arange(n_sorted_tiles, dtype=jnp.int32) * TM
    tile_e = jnp.minimum(jnp.sum((starts[:, None] >= ends[None, :]).astype(jnp.int32), axis=1), N_EXPERTS - 1)
    n_active = (ends[-1] // TM).reshape(1)
    return jnp.concatenate([pos1, pos2]), tile_e, n_active


SC_CORES = 2
SC_SUBCORES = 16
SC_CHUNK = 64


def _sc_mesh():
    return plsc.VectorSubcoreMesh(core_axis_name="c", subcore_axis_name="s",
                                  num_cores=SC_CORES, num_subcores=SC_SUBCORES)


def _sc_scatter_rows(rows, idx, n_out):
    n_src, width = rows.shape
    n_idx = idx.shape[0]
    n_workers = SC_CORES * SC_SUBCORES
    per_w = n_idx // n_workers
    chunks = per_w // SC_CHUNK
    assert n_idx % (n_workers * SC_CHUNK) == 0 and n_src % per_w == 0
    idx3 = idx.reshape(n_workers, chunks, SC_CHUNK)

    def body(rows_hbm, idx_hbm, out_hbm, idx_v, rows_v, sem):
        wid = lax.axis_index("s") * SC_CORES + lax.axis_index("c")
        src_base = lax.rem(wid * per_w, n_src)
        pltpu.sync_copy(idx_hbm.at[wid], idx_v)

        @pl.loop(0, chunks)
        def _(i):
            off = pl.multiple_of(i * SC_CHUNK, SC_CHUNK)
            pltpu.sync_copy(rows_hbm.at[pl.ds(src_base + off, SC_CHUNK)], rows_v)
            pltpu.async_copy(rows_v, out_hbm.at[idx_v.at[i]], sem).wait()

    return pl.kernel(
        body,
        out_type=jax.ShapeDtypeStruct((n_out, width), rows.dtype),
        mesh=_sc_mesh(),
        scratch_types=[pltpu.VMEM((chunks, SC_CHUNK), jnp.int32), pltpu.VMEM((SC_CHUNK, width), rows.dtype),
                       pltpu.SemaphoreType.DMA],
        name="sc_scatter_rows",
    )(rows, idx3)


def _expert_kernel(te_ref, na_ref, s_ref, wgu_ref, wd_ref, o_ref):
    del te_ref

    @pl.when(pl.program_id(0) < na_ref[0])
    def _():
        h = _unpack_pairs(s_ref[...]).astype(BF16)
        gu = _dot(h, wgu_ref[0])
        dfe = gu.shape[1] // 2
        g = gu[:, :dfe]
        act = (g * _sigmoid(g) * gu[:, dfe:]).astype(BF16)
        o_ref[...] = _pack_pairs(_dot(act, wd_ref[0]).astype(BF16).astype(F32))

    @pl.when(pl.program_id(0) >= na_ref[0])
    def _():
        o_ref[...] = jnp.zeros_like(o_ref)


def _expert_call(cfg, tile_e, n_active, buf, wgu, wd):
    d = cfg[3]
    dfe = wd.shape[1]
    n_sorted_tiles = buf.shape[0] // TM
    last = lambda t, na: jnp.maximum(jnp.minimum(t, na[0] - 1), 0)
    tile = lambda t, te, na: (last(t, na), 0)
    w_spec = lambda shape: pl.BlockSpec(shape, lambda t, te, na: (te[last(t, na)], 0, 0))
    return pl.pallas_call(
        _expert_kernel,
        grid_spec=pltpu.PrefetchScalarGridSpec(
            num_scalar_prefetch=2,
            grid=(n_sorted_tiles,),
            in_specs=[pl.BlockSpec((TM, d // 2), tile), w_spec((1, d, 2 * dfe)), w_spec((1, dfe, d))],
            out_specs=pl.BlockSpec((TM, d // 2), lambda t, te, na: (t, 0)),
        ),
        out_shape=jax.ShapeDtypeStruct((n_sorted_tiles * TM, d // 2), jnp.int32),
        compiler_params=_cparams(("arbitrary",)),
        name="moe_experts",
    )(tile_e, n_active, buf, wgu, wd)


def _store_natural(n_ctx_tiles, part, y, o_refs):
    t0, nt = part
    f = y.shape[0] // TM
    has_ctx = t0 < n_ctx_tiles
    has_lat = t0 + nt > n_ctx_tiles
    is_ctx = pl.program_id(0) * f + t0 < n_ctx_tiles
    if has_ctx:
        @pl.when(is_ctx)
        def _():
            o_refs[0][0] = _from_time_major(y)

    if has_lat:
        @pl.when(jnp.logical_not(is_ctx))
        def _():
            o_refs[-1][...] = _from_time_major(y)


def _natural_out(cfg, part, f=1):
    n_ctx_tiles, tiles_per_ctx, n_tiles, d = cfg
    n_groups = n_ctx_tiles // tiles_per_ctx
    t0, nt = part
    specs = _natural_specs(cfg, t0, f)
    shapes = (jax.ShapeDtypeStruct((n_groups, SUB, tiles_per_ctx * GRID_W, d), F32),
              jax.ShapeDtypeStruct((SUB, (n_tiles - n_ctx_tiles) * GRID_W, d), F32))
    keep = [k for k, used in enumerate((t0 < n_ctx_tiles, t0 + nt > n_ctx_tiles)) if used]
    return [specs[k] for k in keep], [shapes[k] for k in keep], keep


def _sc_gather_rows(table, idx):
    n_rows = idx.shape[0]
    width = table.shape[1]
    n_workers = SC_CORES * SC_SUBCORES
    assert n_rows % (n_workers * SC_CHUNK) == 0
    per_w = n_rows // n_workers
    mesh = _sc_mesh()

    def body(table_hbm, idx_hbm, out_hbm, idx_v, rows_v, sem):
        wid = lax.axis_index("s") * SC_CORES + lax.axis_index("c")
        base = wid * per_w
        pltpu.sync_copy(idx_hbm.at[pl.ds(base, per_w)], idx_v)

        @pl.loop(0, per_w // SC_CHUNK)
        def _(i):
            off = pl.multiple_of(i * SC_CHUNK, SC_CHUNK)
            pltpu.async_copy(table_hbm.at[idx_v.at[pl.ds(off, SC_CHUNK)]], rows_v, sem).wait()
            pltpu.sync_copy(rows_v, out_hbm.at[pl.ds(base + off, SC_CHUNK)])

    return pl.kernel(
        body,
        out_type=jax.ShapeDtypeStruct((n_rows, width), table.dtype),
        mesh=mesh,
        scratch_types=[pltpu.VMEM((per_w,), jnp.int32), pltpu.VMEM((SC_CHUNK, width), table.dtype),
                       pltpu.SemaphoreType.DMA],
        name="sc_gather_rows",
    )(table, idx)


def _combine_kernel(final_ctx_tiles, part, n_filled, xm_ref, y1_ref, y2_ref, meta_ref, mod_ref, gfin_ref,
                    *refs):
    o_refs = refs[n_filled:]
    meta = meta_ref[...]
    y = (meta[:, M_W1:M_W1 + 1] * _unpack_pairs(y1_ref[...])
         + meta[:, M_W2:M_W2 + 1] * _unpack_pairs(y2_ref[...]))
    x = _gated_add(xm_ref[...], mod_ref[0, 5], y)
    if final_ctx_tiles is None:
        o_refs[0][...] = x
    else:
        _store_natural(final_ctx_tiles, part, _rms(x, gfin_ref[...]), o_refs)


def _combine_call(cfg, part, xm, yg, meta, mod_l, g_final, final, filled):
    n_ctx_tiles, _, n_tiles, d = cfg
    t0, nt = part
    f = _seq_tiles(cfg) if t0 % 2 == 0 and nt % 2 == 0 else 1
    tm = f * TM
    steps = nt // f
    row_spec = pl.BlockSpec((tm, d), lambda j: (j, 0))
    if final:
        out_specs, out_shape, keys = _natural_out(cfg, part, f)
    else:
        out_specs = [pl.BlockSpec((tm, d), lambda j: (j + t0 // f, 0))]
        out_shape, keys = [jax.ShapeDtypeStruct((n_tiles * TM, d), F32)], [0]
    reuse = [k for k in keys if k in filled]
    base = [xm, yg, yg, meta, mod_l, g_final.reshape(1, d)]
    outs = pl.pallas_call(
        functools.partial(_combine_kernel, n_ctx_tiles if final else None, part, len(reuse)),
        grid=(steps,),
        in_specs=[row_spec, pl.BlockSpec((tm, d // 2), lambda j: (j, 0)),
                  pl.BlockSpec((tm, d // 2), lambda j: (steps + j, 0)),
                  pl.BlockSpec((tm, LANES), lambda j: (j, 0)), _mod_spec(cfg, f, t0), _const_spec((1, d))]
                 + [pl.BlockSpec(memory_space=pl.ANY)] * len(reuse),
        out_specs=out_specs,
        out_shape=out_shape,
        input_output_aliases={len(base) + i: keys.index(k) for i, k in enumerate(reuse)},
        compiler_params=_cparams(("arbitrary",)),
        name="moe_combine",
    )(*base, *[filled[k] for k in reuse])
    return {**filled, **dict(zip(keys, outs))}


def _final_norm_kernel(n_ctx_tiles, part, x_ref, g_ref, *o_refs):
    _store_natural(n_ctx_tiles, part, _rms(x_ref[...], g_ref[...]), o_refs)


def _final_norm_call(cfg, x, g):
    n_ctx_tiles, _, n_tiles, d = cfg
    out_specs, out_shape, _ = _natural_out(cfg, (0, n_tiles))
    return pl.pallas_call(
        functools.partial(_final_norm_kernel, n_ctx_tiles, (0, n_tiles)),
        grid=(n_tiles,),
        in_specs=[pl.BlockSpec((TM, d), lambda j: (j, 0)), _const_spec((1, d))],
        out_specs=out_specs,
        out_shape=out_shape,
        compiler_params=_cparams(("arbitrary",)),
        name="final_norm",
    )(x, g.reshape(1, d))


def _block_diag(w, per_block):
    *lead, n, k, _ = w.shape
    nb = n // per_block
    w = w.reshape(*lead, nb, per_block, k, k)
    eye = jnp.eye(per_block, dtype=w.dtype)
    out = w[..., :, :, None, :] * eye[:, None, :, None]
    return out.reshape(*lead, nb, per_block * k, per_block * k)


def _pos_tables(n_ctx_tiles, n_lat_tiles, d):
    quarter = d // 4
    omega = 1.0 / (POS_BASE ** (jnp.arange(quarter, dtype=F32) / quarter))
    er = jnp.arange(n_lat_tiles, dtype=F32)[:, None] * omega
    ec = jnp.arange(GRID_W, dtype=F32)[:, None] * omega
    row_emb = jnp.concatenate([jnp.sin(er), jnp.cos(er)], axis=-1)
    col_emb = jnp.concatenate([jnp.sin(ec), jnp.cos(ec)], axis=-1)
    del n_ctx_tiles
    return row_emb[:, None, :], jnp.repeat(col_emb, SUB, axis=0)


def kernel(x_prompt, x_sample, state_lru, c, c_ctx, norm_mix_g, w_ada, b_ada, w_in, conv_w, conv_b, lru_wa, lru_ba, lru_wx, lru_bx, lru_lam, pool_w, pool_b, pool_scale, w_out, norm_ffn_g, ffn_wg, ffn_wu, ffn_wd, moe_router_w, moe_router_b, moe_wg, moe_wu, moe_wd, norm_final_g):
    bc, tc, d = x_prompt.shape
    bl, tl, _ = x_sample.shape
    depth = w_ada.shape[0]
    c_lru = conv_w.shape[-1]
    assert bl == SUB and bc % SUB == 0 and tc % GRID_W == 0 and tl % GRID_W == 0
    assert c_lru == 4 * LANES and d == 2 * c_lru
    n_groups = bc // SUB
    tiles_per_ctx = tc // GRID_W
    n_ctx_tiles = n_groups * tiles_per_ctx
    n_lat_tiles = tl // GRID_W
    n_tiles = n_ctx_tiles + n_lat_tiles
    n_ctx = n_ctx_tiles * TM
    cfg = (n_ctx_tiles, tiles_per_ctx, n_tiles, d)

    cond = jnp.concatenate([c, c_ctx[None], jnp.zeros((SUB - 1, d), F32)], axis=0)
    mod = _ada_call(cond, w_ada, b_ada)
    mod_lat = mod[:, :SUB].reshape(depth, SUB, 6, d).transpose(0, 2, 1, 3)
    mod_ctx = jnp.broadcast_to(mod[:, SUB].reshape(depth, 6, 1, d), (depth, 6, SUB, d))
    mod = jnp.stack([mod_ctx, mod_lat], axis=1)

    row_tab, col_rep = _pos_tables(n_ctx_tiles, n_lat_tiles, d)
    first = (x_prompt.reshape(n_groups, SUB, tc, d), x_sample, row_tab, col_rep)
    x = None
    wg_lru = (0.5 * jnp.concatenate([_block_diag(lru_wa, 4), _block_diag(lru_wx, 4)], axis=-1)).astype(BF16)
    pw = _block_diag(pool_w, 2).astype(BF16)
    inv_cnt = _pool_inv_counts(tc, c_lru)
    h0 = jnp.concatenate([jnp.zeros_like(state_lru[None]), state_lru[None]], axis=0)

    states = []
    for l in range(depth):
        jdx = l // 2
        if l == 0:
            proj, x = _inproj_call(cfg, x, mod[l], norm_mix_g[l], w_in[l].astype(BF16), first)
        else:
            (proj,) = _inproj_call(cfg, x, mod[l], norm_mix_g[l], w_in[l].astype(BF16))
        hf, st_f = _scan_fwd_call(cfg, proj, conv_w[l], conv_b[l], wg_lru[l, 0], lru_ba[l, 0],
                                  lru_bx[l, 0], lru_lam[l, 0], h0[:, :, l, 0])
        ymix, st_b = _scan_bwd_call(cfg, inv_cnt, proj, hf, conv_w[l], conv_b[l], wg_lru[l, 1], lru_ba[l, 1],
                                    lru_bx[l, 1], lru_lam[l, 1], h0[:, :, l, 1], pw[l], pool_b[l],
                                    pool_scale[l])
        states.append(jnp.stack([st_f[:n_groups].reshape(bc, c_lru), st_b[:n_groups].reshape(bc, c_lru)], axis=1))
        if l % 2 == 0:
            x = _dense_ffn_call(cfg, x, ymix, mod[l], w_out[l].astype(BF16), norm_ffn_g[l],
                                ffn_wg[jdx].astype(BF16), ffn_wu[jdx].astype(BF16), ffn_wd[jdx].astype(BF16))
        else:
            final = l == depth - 1
            w_out_b = w_out[l].astype(BF16)
            filled = {}
            expert_w = None
            for part in _moe_parts(cfg):
                to_cast = (moe_wg, moe_wu, moe_wd, jdx) if expert_w is None else None
                (xm, hp, meta, mt, cnt), cast = _route_call(cfg, part, x, ymix, mod[l], w_out_b, norm_ffn_g[l],
                                                            moe_router_w[jdx], moe_router_b[jdx], to_cast)
                expert_w = expert_w or cast
                pos_flat, tile_e, n_active = _routing_tables(part[1], mt, cnt)
                buf = _sc_scatter_rows(hp, pos_flat, (2 * part[1] + N_EXPERTS) * TM)
                ys = _expert_call(cfg, tile_e, n_active, buf, *expert_w)
                yg = _sc_gather_rows(ys, pos_flat)
                filled = _combine_call(cfg, part, xm, yg, meta, mod[l], norm_final_g, final, filled)
            x = (filled[0], filled[1]) if final else filled[0]

    y_prompt, y_sample = x if depth % 2 == 0 else _final_norm_call(cfg, x, norm_final_g)
    new_state = jnp.stack(states, axis=1)
    return (y_prompt.reshape(bc, tc, d), y_sample, new_state)
```

```python
import functools

import jax
import jax.numpy as jnp
from jax import lax
from jax.experimental import pallas as pl
from jax.experimental.pallas import tpu as pltpu
from jax.experimental.pallas import tpu_sc as plsc

F32 = jnp.float32
BF16 = jnp.bfloat16

SUB = 8
LANES = 128
GRID_W = 64
TM = GRID_W * SUB
POS_BASE = 10000.0
N_LRU_HEADS = 8
CONV_W = 4
CONV_LEFT = CONV_W // 2
CONV_RIGHT = CONV_W - 1 - CONV_LEFT
LRU_C = 8.0
POOL_WINDOWS = (2, 4, 8, 16)
POOL_HALO = 8 * SUB
N_EXPERTS = 8
EPS = 1e-6
FF_CHUNK = 256
VMEM_LIMIT = 56 * 1024 * 1024


def _cparams(sem):
    return pltpu.CompilerParams(dimension_semantics=sem, vmem_limit_bytes=VMEM_LIMIT)


def _const_spec(shape):
    nd = len(shape)
    return pl.BlockSpec(shape, lambda *_: (0,) * nd, pipeline_mode=pl.Buffered(1))


def _rms(x, g):
    ms = jnp.mean(x * x, axis=-1, keepdims=True)
    return x * lax.rsqrt(ms + EPS) * g


def _per_seq(x, v, op):
    r, c = x.shape
    x3 = x.reshape(r // SUB, SUB, c)
    return op(x3, v[None]).reshape(r, c)


def _modulate(h, scale, shift):
    r, c = h.shape
    h3 = h.reshape(r // SUB, SUB, c)
    return (h3 * (1.0 + scale)[None] + shift[None]).reshape(r, c)


def _gated_add(x, gate, y):
    return x + _per_seq(y, gate, lambda a, b: a * b)


def _dot(a, b):
    return jnp.dot(a, b, preferred_element_type=F32)


def _pack_pairs(x):
    bits = lax.bitcast_convert_type(x, jnp.uint32)
    half = bits.shape[1] // 2
    w = lax.shift_right_logical(bits[:, :half], jnp.uint32(16)) | (bits[:, half:] & jnp.uint32(0xFFFF0000))
    return lax.bitcast_convert_type(w, jnp.int32)


def _unpack_pairs(w):
    w = lax.bitcast_convert_type(w, jnp.uint32)
    lo = lax.bitcast_convert_type(lax.shift_left(w, jnp.uint32(16)), F32)
    hi = lax.bitcast_convert_type(w & jnp.uint32(0xFFFF0000), F32)
    return jnp.concatenate([lo, hi], axis=1)


def _sigmoid(x):
    return 0.5 * jnp.tanh(0.5 * x) + 0.5


def _ada_kernel(c_ref, w_ref, b_ref, o_ref):
    c = c_ref[...]
    s = (c * jax.nn.sigmoid(c)).astype(BF16)
    o_ref[0] = _dot(s, w_ref[0].astype(BF16)) + b_ref[0]


def _ada_call(cond, w_ada, b_ada):
    depth, d, d6 = w_ada.shape
    nr = cond.shape[0]
    bn = d6 // 4
    return pl.pallas_call(
        _ada_kernel,
        grid=(depth, d6 // bn),
        in_specs=[
            pl.BlockSpec((nr, d), lambda l, n: (0, 0)),
            pl.BlockSpec((1, d, bn), lambda l, n: (l, 0, n)),
            pl.BlockSpec((1, 1, bn), lambda l, n: (l, 0, n)),
        ],
        out_specs=pl.BlockSpec((1, nr, bn), lambda l, n: (l, 0, n)),
        out_shape=jax.ShapeDtypeStruct((depth, nr, d6), F32),
        compiler_params=_cparams(("parallel", "parallel")),
        name="ada_mod",
    )(cond, w_ada, b_ada.reshape(depth, 1, d6))


def _to_time_major(x):
    s, t, d = x.shape
    return jnp.swapaxes(x, 0, 1).reshape(s * t, d)


def _from_time_major(x):
    r, d = x.shape
    return jnp.swapaxes(x.reshape(r // SUB, SUB, d), 0, 1)


def _inproj_kernel(n_ctx_tiles, *refs):
    if n_ctx_tiles is not None:
        xp_ref, xs_ref, row_ref, col_ref, mod_ref, g_ref, w_ref, o_ref, x0_ref = refs
        f = row_ref.shape[0]
        is_ctx = pl.program_id(0) * f < n_ctx_tiles

        @pl.when(is_ctx)
        def _():
            x0_ref[...] = _to_time_major(xp_ref[0])

        @pl.when(jnp.logical_not(is_ctx))
        def _():
            x = _to_time_major(xs_ref[...])
            half = x.shape[1] // 2
            for k in range(f):
                rows = slice(k * TM, (k + 1) * TM)
                x0_ref[rows, :] = jnp.concatenate([x[rows, :half] + row_ref[k], x[rows, half:] + col_ref[...]],
                                                  axis=1)

        x = x0_ref[...]
    else:
        x_ref, mod_ref, g_ref, w_ref, o_ref = refs
        x = x_ref[...]
    h = _modulate(_rms(x, g_ref[...]), mod_ref[0, 1], mod_ref[0, 0])
    o_ref[...] = _dot(h.astype(BF16), w_ref[...])


def _natural_specs(cfg, t0=0, f=1):
    n_ctx_tiles, tiles_per_ctx, _, d = cfg

    def ctx_idx(j):
        jc = jnp.minimum(j * f + t0, n_ctx_tiles - f)
        return (jc // tiles_per_ctx, 0, lax.rem(jc, tiles_per_ctx) // f, 0)

    return (pl.BlockSpec((1, SUB, f * GRID_W, d), ctx_idx),
            pl.BlockSpec((SUB, f * GRID_W, d), lambda j: (0, jnp.maximum(j * f + t0 - n_ctx_tiles, 0) // f, 0)))


def _wide(cfg):
    n_ctx_tiles, _, n_tiles, _ = cfg
    return 2 if n_ctx_tiles % 2 == 0 and n_tiles % 2 == 0 else 1


def _moe_parts(cfg):
    n_tiles = cfg[2]
    half = n_tiles // 2
    if _wide(cfg) == 2 and half % 2 == 0:
        return [(0, half), (half, n_tiles - half)]
    return [(0, n_tiles)]


def _mod_spec(cfg, f=1, t0=0):
    n_ctx_steps = cfg[0] // f
    d = cfg[3]
    s0 = t0 // f
    return pl.BlockSpec((1, 6, SUB, d), lambda j: (jnp.where(j + s0 >= n_ctx_steps, 1, 0), 0, 0, 0))


def _inproj_call(cfg, x, mod_l, g, w_in, first=None):
    n_ctx_tiles, _, n_tiles, d = cfg
    d_in = w_in.shape[1]
    f = _seq_tiles(cfg) if first is not None else _wide(cfg)
    tm = TM * f
    row_spec = pl.BlockSpec((tm, d), lambda j: (j, 0))
    out_specs = [pl.BlockSpec((tm, d_in), lambda j: (j, 0))]
    out_shape = [jax.ShapeDtypeStruct((n_tiles * TM, d_in), F32)]
    if first is not None:
        xp_spec, xs_spec = _natural_specs(cfg, 0, f)
        in_specs = [xp_spec, xs_spec,
                    pl.BlockSpec((f, 1, d // 2), lambda j: (jnp.maximum(j - n_ctx_tiles // f, 0), 0, 0)),
                    _const_spec((TM, d // 2))]
        args = list(first)
        out_specs.append(row_spec)
        out_shape.append(jax.ShapeDtypeStruct((n_tiles * TM, d), F32))
    else:
        in_specs = [row_spec]
        args = [x]
    in_specs += [_mod_spec(cfg, f), _const_spec((1, d)), _const_spec((d, d_in))]
    args += [mod_l, g.reshape(1, d), w_in]
    return pl.pallas_call(
        functools.partial(_inproj_kernel, n_ctx_tiles if first is not None else None),
        grid=(n_tiles // f,),
        in_specs=in_specs,
        out_specs=out_specs,
        out_shape=out_shape,
        compiler_params=_cparams(("parallel",)),
        name="in_proj",
    )(*args)


def _seq_flags(cfg, j):
    n_ctx_tiles, tiles_per_ctx, n_tiles, _ = cfg
    is_ctx = j < n_ctx_tiles
    pos = lax.rem(j, tiles_per_ctx)
    first = jnp.where(is_ctx, pos == 0, j == n_ctx_tiles)
    last = jnp.where(is_ctx, pos == tiles_per_ctx - 1, j == n_tiles - 1)
    return is_ctx, pos, first, last


def _conv(pad_ref, xa, prev, nxt, first, last, cw, cb):
    lo = CONV_LEFT * SUB
    pad_ref[0:lo, :] = jnp.where(first, 0.0, prev)
    pad_ref[lo:lo + TM, :] = xa
    pad_ref[lo + TM:lo + TM + CONV_RIGHT * SUB, :] = jnp.where(last, 0.0, nxt)
    y = cb
    for k in range(CONV_W):
        y = y + pad_ref[k * SUB:k * SUB + TM, :] * cw[k:k + 1, :]
    return y


def _gates(xc, wg_ref, ba, bx, lam):
    half = xc.shape[1] // 2
    xb = xc.astype(BF16)
    g0 = _dot(xb[:, :half], wg_ref[0])
    g1 = _dot(xb[:, half:], wg_ref[1])
    t_r = jnp.tanh(jnp.concatenate([g0[:, :half], g1[:, :half]], axis=1) + 0.5 * ba)
    t_i = jnp.tanh(jnp.concatenate([g0[:, half:], g1[:, half:]], axis=1) + 0.5 * bx)
    z = -lam
    half_decay = (0.5 * LRU_C) * (jnp.maximum(z, 0.0) + jnp.log1p(jnp.exp(-jnp.abs(z))))
    neg_log_a = t_r * half_decay + half_decay
    a = jnp.exp(-neg_log_a)
    z = jnp.tanh(neg_log_a) * (a * a + 1.0)
    root = jnp.where(z > 0.0, z * lax.rsqrt(z), 0.0)
    half_xc = 0.5 * xc
    u = root * (t_i * half_xc + half_xc)
    return a, u


def _scan(a_ref, u_ref, h_ref, row0, h, reverse):
    steps = TM // SUB

    def body(k, h):
        t = steps - 1 - k if reverse else k
        r0 = pl.multiple_of(t * SUB, SUB)
        h = a_ref[pl.ds(r0, SUB), :] * h + u_ref[pl.ds(r0, SUB), :]
        h_ref[pl.ds(row0 + r0, SUB), :] = h
        return h

    return lax.fori_loop(0, steps, body, h, unroll=8)


def _seq_tiles(cfg):
    n_ctx_tiles, tiles_per_ctx, n_tiles, _ = cfg
    return 2 if n_ctx_tiles % 2 == 0 and tiles_per_ctx % 2 == 0 and n_tiles % 2 == 0 else 1


def _inner_halo(ref, sub, f, col0, c, rows, outer_prev, outer_next):
    prev = outer_prev[...] if sub == 0 else ref[sub * TM - rows[0]:sub * TM, col0:col0 + c]
    nxt = outer_next[...] if sub == f - 1 else ref[(sub + 1) * TM:(sub + 1) * TM + rows[1], col0:col0 + c]
    return prev, nxt


def _scan_fwd_kernel(cfg, f, n_cast, xa_ref, xp_ref, xn_ref, cw_ref, cb_ref, wg_ref, ba_ref, bx_ref,
                     lam_ref, h0_ref, *refs):
    cast_in, (hf_ref, st_ref), cast_out = refs[:n_cast], refs[n_cast:n_cast + 2], refs[n_cast + 2:2 * n_cast + 2]
    pad_ref, a_ref, u_ref, carry_ref = refs[2 * n_cast + 2:]
    for w_ref, wb_ref in zip(cast_in, cast_out):
        wb_ref[...] = w_ref[0].astype(BF16)
    c = xa_ref.shape[1]
    for sub in range(f):
        _, _, first, last = _seq_flags(cfg, pl.program_id(0) * f + sub)
        prev, nxt = _inner_halo(xa_ref, sub, f, 0, c, (CONV_LEFT * SUB, CONV_RIGHT * SUB), xp_ref, xn_ref)
        xc = _conv(pad_ref, xa_ref[sub * TM:(sub + 1) * TM, :], prev, nxt, first, last, cw_ref[...], cb_ref[...])
        a, u = _gates(xc, wg_ref, ba_ref[...], bx_ref[...], lam_ref[...])
        a_ref[...] = a
        u_ref[...] = u

        @pl.when(first)
        def _():
            carry_ref[...] = h0_ref[0]

        h = _scan(a_ref, u_ref, hf_ref, sub * TM, carry_ref[...], reverse=False)
        carry_ref[...] = h
        st_ref[0] = h


def _halo_specs(cfg, col, rows_prev, rows_next, blk_of, f):
    n_tiles, d = cfg[2], cfg[3]
    c = d // 2
    nb_prev = TM // rows_prev
    nb_next = TM // rows_next
    last_next = n_tiles * nb_next - 1
    prev = pl.BlockSpec((rows_prev, c), lambda j: (jnp.maximum(blk_of(j) * f * nb_prev - 1, 0), col))
    nxt = pl.BlockSpec((rows_next, c), lambda j: (jnp.minimum((blk_of(j) + 1) * f * nb_next, last_next), col))
    return prev, nxt


def _state_spec(cfg, blk_of, f):
    n_ctx_tiles, tiles_per_ctx, _, d = cfg
    n_groups = n_ctx_tiles // tiles_per_ctx
    return pl.BlockSpec((1, SUB, d // 2),
                        lambda j: (jnp.minimum(blk_of(j) * f // tiles_per_ctx, n_groups), 0, 0))


def _h0_spec(cfg, blk_of, f):
    n_ctx_tiles, d = cfg[0], cfg[3]
    return pl.BlockSpec((1, SUB, d // 2), lambda j: (jnp.where(blk_of(j) * f >= n_ctx_tiles, 1, 0), 0, 0))


def _scan_fwd_call(cfg, proj, conv_w, conv_b, wg, ba, bx, lam, h0, cast=()):
    n_ctx_tiles, tiles_per_ctx, n_tiles, _ = cfg
    n_groups = n_ctx_tiles // tiles_per_ctx
    c = conv_w.shape[1]
    f = _seq_tiles(cfg)
    steps = n_tiles // f
    ident = lambda j: j
    xp_spec, xn_spec = _halo_specs(cfg, 0, CONV_LEFT * SUB, CONV_RIGHT * SUB, ident, f)
    cast_in, cast_out, cast_shape = [], [], []
    for w, layer in cast:
        _, rows, cols = w.shape
        nblk = _cast_blocks(steps, rows)
        cast_in.append(pl.BlockSpec((1, rows // nblk, cols),
                                    lambda j, layer=layer, nblk=nblk: (layer, jnp.minimum(j, nblk - 1), 0)))
        cast_out.append(pl.BlockSpec((rows // nblk, cols), lambda j, nblk=nblk: (jnp.minimum(j, nblk - 1), 0)))
        cast_shape.append(jax.ShapeDtypeStruct((rows, cols), BF16))
    return pl.pallas_call(
        functools.partial(_scan_fwd_kernel, cfg, f, len(cast)),
        grid=(steps,),
        in_specs=[
            pl.BlockSpec((f * TM, c), lambda j: (j, 0)), xp_spec, xn_spec,
            _const_spec((CONV_W, c)), _const_spec((1, c)), _const_spec((2, c // 2, c)),
            _const_spec((1, c)), _const_spec((1, c)), _const_spec((1, c)),
            _h0_spec(cfg, ident, f),
        ] + cast_in,
        out_specs=[pl.BlockSpec((f * TM, c), lambda j: (j, 0)), _state_spec(cfg, ident, f)] + cast_out,
        out_shape=[jax.ShapeDtypeStruct((n_tiles * TM, c), F32),
                   jax.ShapeDtypeStruct((n_groups + 1, SUB, c), F32)] + cast_shape,
        scratch_shapes=[
            pltpu.VMEM((TM + (CONV_W - 1) * SUB, c), F32),
            pltpu.VMEM((TM, c), F32), pltpu.VMEM((TM, c), F32), pltpu.VMEM((SUB, c), F32),
        ],
        compiler_params=_cparams(("arbitrary",)),
        name="lru_fwd",
    )(proj, proj, proj, conv_w, conv_b.reshape(1, c), wg, ba.reshape(1, c), bx.reshape(1, c),
      lam.reshape(1, c), h0, *[w for w, _ in cast])


def _pool_inv_counts(ctx_len, c):
    gw = c // len(POOL_WINDOWS)

    def table(t0, t_len):
        t = t0 + jnp.arange(GRID_W)
        cols = []
        for k in POOL_WINDOWS:
            left = k // 2
            right = k - 1 - left
            cnt = jnp.minimum(t + right + 1, t_len) - jnp.maximum(t - left, 0)
            cols.append(jnp.broadcast_to((1.0 / cnt.astype(F32))[:, None], (GRID_W, gw)))
        return jnp.repeat(jnp.concatenate(cols, axis=1), SUB, axis=0)

    return jnp.stack([table(p * GRID_W, ctx_len) for p in range(ctx_len // GRID_W)] + [table(0, GRID_W)])


def _gelu_tanh(x):
    k0 = 0.7978845608028654
    hx = 0.5 * x
    return hx + hx * jnp.tanh(x * (k0 + (k0 * 0.044715) * (x * x)))


def _pool_mix(pad_ref, inv_ref, pw_ref, pb, ps):
    c = pad_ref.shape[1]
    gw = c // len(POOL_WINDOWS)
    outs = []
    for g, k in enumerate(POOL_WINDOWS):
        left = k // 2
        right = k - 1 - left
        lanes = slice(g * gw, (g + 1) * gw)
        s = None
        for o in range(-left, right + 1):
            v = pad_ref[POOL_HALO + o * SUB:POOL_HALO + o * SUB + TM, lanes]
            s = v if s is None else s + v
        outs.append(s * inv_ref[0, :, lanes] - pad_ref[POOL_HALO:POOL_HALO + TM, lanes])
    d = jnp.concatenate(outs, axis=1).astype(BF16)
    half = c // 2
    y = jnp.concatenate([_dot(d[:, :half], pw_ref[0]), _dot(d[:, half:], pw_ref[1])], axis=1)
    return (y + pb) * ps


def _scan_bwd_kernel(cfg, f, proj_ref, xp_ref, xn_ref, bp_ref, bn_ref, hf_ref, *refs):
    inv_refs, refs = refs[:f], refs[f:]
    (cw_ref, cb_ref, wg_ref, ba_ref, bx_ref, lam_ref, h0_ref, pw_ref, pb_ref, ps_ref,
     y_ref, st_ref, pad_ref, ppad_ref, a_ref, u_ref, hb_ref, carry_ref) = refs
    n_blocks = cfg[2] // f
    blk = n_blocks - 1 - pl.program_id(0)
    c = hf_ref.shape[1]
    for sub in reversed(range(f)):
        rows = slice(sub * TM, (sub + 1) * TM)
        is_ctx, _, first, last = _seq_flags(cfg, blk * f + sub)
        prev, nxt = _inner_halo(proj_ref, sub, f, 0, c, (CONV_LEFT * SUB, CONV_RIGHT * SUB), xp_ref, xn_ref)
        xc = _conv(pad_ref, proj_ref[rows, 0:c], prev, nxt, first, last, cw_ref[...], cb_ref[...])
        a, u = _gates(xc, wg_ref, ba_ref[...], bx_ref[...], lam_ref[...])
        a_ref[...] = a
        u_ref[...] = u

        @pl.when(last)
        def _():
            carry_ref[...] = h0_ref[0]

        h = _scan(a_ref, u_ref, hb_ref, 0, carry_ref[...], reverse=True)
        carry_ref[...] = h
        st_ref[0] = h

        ga = proj_ref[rows, c:2 * c]
        y_a = (hf_ref[rows, :] + hb_ref[...]) * _gelu_tanh(ga)

        use_prev = jnp.logical_and(is_ctx, jnp.logical_not(first))
        use_next = jnp.logical_and(is_ctx, jnp.logical_not(last))
        bprev, bnext = _inner_halo(proj_ref, sub, f, 2 * c, c, (POOL_HALO, POOL_HALO), bp_ref, bn_ref)
        ppad_ref[0:POOL_HALO, :] = jnp.where(use_prev, bprev, 0.0)
        ppad_ref[POOL_HALO:POOL_HALO + TM, :] = proj_ref[rows, 2 * c:3 * c]
        ppad_ref[POOL_HALO + TM:, :] = jnp.where(use_next, bnext, 0.0)
        y_b = _pool_mix(ppad_ref, inv_refs[sub], pw_ref, pb_ref[...], ps_ref[...])
        y_ref[rows, :] = jnp.concatenate([y_a, y_b], axis=1).astype(BF16)


def _scan_bwd_call(cfg, inv_cnt, proj, hf, conv_w, conv_b, wg, ba, bx, lam, h0, pw, pb, ps):
    n_ctx_tiles, tiles_per_ctx, n_tiles, d = cfg
    n_groups = n_ctx_tiles // tiles_per_ctx
    c = conv_w.shape[1]
    f = _seq_tiles(cfg)
    n_blocks = n_tiles // f
    rev = lambda j: n_blocks - 1 - j
    xp_spec, xn_spec = _halo_specs(cfg, 0, CONV_LEFT * SUB, CONV_RIGHT * SUB, rev, f)
    bp_spec, bn_spec = _halo_specs(cfg, 2, POOL_HALO, POOL_HALO, rev, f)

    def inv_spec(sub):
        def idx(j):
            tile = rev(j) * f + sub
            return (jnp.where(tile < n_ctx_tiles, lax.rem(tile, tiles_per_ctx), tiles_per_ctx), 0, 0)
        return pl.BlockSpec((1, TM, c), idx)

    return pl.pallas_call(
        functools.partial(_scan_bwd_kernel, cfg, f),
        grid=(n_blocks,),
        in_specs=[
            pl.BlockSpec((f * TM, 3 * c), lambda j: (rev(j), 0)), xp_spec, xn_spec, bp_spec, bn_spec,
            pl.BlockSpec((f * TM, c), lambda j: (rev(j), 0)),
            *[inv_spec(sub) for sub in range(f)],
            _const_spec((CONV_W, c)), _const_spec((1, c)), _const_spec((2, c // 2, c)),
            _const_spec((1, c)), _const_spec((1, c)), _const_spec((1, c)),
            _h0_spec(cfg, rev, f),
            _const_spec((2, c // 2, c // 2)), _const_spec((1, c)), _const_spec((1, c)),
        ],
        out_specs=[pl.BlockSpec((f * TM, d), lambda j: (rev(j), 0)), _state_spec(cfg, rev, f)],
        out_shape=[jax.ShapeDtypeStruct((n_tiles * TM, d), BF16),
                   jax.ShapeDtypeStruct((n_groups + 1, SUB, c), F32)],
        scratch_shapes=[
            pltpu.VMEM((TM + (CONV_W - 1) * SUB, c), F32),
            pltpu.VMEM((TM + 2 * POOL_HALO, c), F32),
            pltpu.VMEM((TM, c), F32), pltpu.VMEM((TM, c), F32), pltpu.VMEM((TM, c), F32),
            pltpu.VMEM((SUB, c), F32),
        ],
        compiler_params=_cparams(("arbitrary",)),
        name="lru_bwd_mix",
    )(proj, proj, proj, proj, proj, hf, *([inv_cnt] * f), conv_w, conv_b.reshape(1, c), wg, ba.reshape(1, c),
      bx.reshape(1, c), lam.reshape(1, c), h0, pw, pb.reshape(1, c), ps.reshape(1, c))


def _out_proj(x_ref, ym_ref, mod_ref, wo_ref, g2_ref):
    xm = _gated_add(x_ref[...], mod_ref[0, 2], _dot(ym_ref[...], wo_ref[...]))
    h2 = _modulate(_rms(xm, g2_ref[...]), mod_ref[0, 4], mod_ref[0, 3])
    return xm, h2


def _swiglu_act(h2b, wg_ref, wu_ref, act_ref):
    n_chunks = act_ref.shape[1] // FF_CHUNK

    def body(f, carry):
        c0 = pl.multiple_of(f * FF_CHUNK, FF_CHUNK)
        g = _dot(h2b, wg_ref[:, pl.ds(c0, FF_CHUNK)])
        u = _dot(h2b, wu_ref[:, pl.ds(c0, FF_CHUNK)])
        act_ref[:, pl.ds(c0, FF_CHUNK)] = (g * _sigmoid(g) * u).astype(BF16)
        return carry

    lax.fori_loop(0, n_chunks, body, 0, unroll=True)


def _dense_ffn_kernel(x_ref, ym_ref, mod_ref, wo_ref, g2_ref, wg_ref, wu_ref, wd_ref, o_ref, act_ref):
    xm, h2 = _out_proj(x_ref, ym_ref, mod_ref, wo_ref, g2_ref)
    _swiglu_act(h2.astype(BF16), wg_ref, wu_ref, act_ref)
    o_ref[...] = _gated_add(xm, mod_ref[0, 5], _dot(act_ref[...], wd_ref[...]))


def _dense_ffn_call(cfg, x, ymix, mod_l, w_out, g2, wg, wu, wd):
    _, _, n_tiles, d = cfg
    d_ff = wg.shape[1]
    f = _wide(cfg)
    row_spec = pl.BlockSpec((TM * f, d), lambda j: (j, 0))
    return pl.pallas_call(
        _dense_ffn_kernel,
        grid=(n_tiles // f,),
        in_specs=[row_spec, row_spec, _mod_spec(cfg, f), _const_spec((d, d)), _const_spec((1, d)),
                  _const_spec((d, d_ff)), _const_spec((d, d_ff)), _const_spec((d_ff, d))],
        out_specs=row_spec,
        out_shape=jax.ShapeDtypeStruct((n_tiles * TM, d), F32),
        scratch_shapes=[pltpu.VMEM((TM * f, d_ff), BF16)],
        compiler_params=_cparams(("parallel",)),
        name="out_proj_dense_ffn",
    )(x, ymix, mod_l, w_out, g2.reshape(1, d), wg, wu, wd)


M_E1, M_E2, M_W1, M_W2, M_R1, M_R2 = range(6)


def _route_kernel(cast_weights, x_ref, ym_ref, mod_ref, wo_ref, g2_ref, rw_ref, rb_ref, tri_ref, *refs):
    if cast_weights:
        wg_ref, wu_ref, wd_ref, *refs = refs
        wgu_ref, wdb_ref = refs[5:7]
        wgu_ref[...] = jnp.concatenate([wg_ref[0], wu_ref[0]], axis=1).astype(BF16)
        wdb_ref[...] = wd_ref[0].astype(BF16)
        refs = refs[:5] + refs[7:]
    xm_ref, hp_ref, meta_ref, mt_ref, cnt_ref, run_ref = refs

    @pl.when(pl.program_id(0) == 0)
    def _():
        run_ref[...] = jnp.zeros_like(run_ref)

    run = run_ref[...]
    for r0 in range(0, x_ref.shape[0], TM):
        rows = pl.ds(r0, TM)
        xm = _gated_add(x_ref[rows, :], mod_ref[0, 2], _dot(ym_ref[rows, :], wo_ref[...]))
        h2 = _modulate(_rms(xm, g2_ref[...]), mod_ref[0, 4], mod_ref[0, 3])
        xm_ref[rows, :] = xm
        h_hi = h2.astype(BF16)
        h_hi32 = h_hi.astype(F32)
        hp_ref[rows, :] = _pack_pairs(h_hi32)

        h_lo = (h2 - h_hi32).astype(BF16)
        p = _dot(h_hi, rw_ref[...])
        logits = p[:, :LANES] + p[:, LANES:] + _dot(h_lo, rw_ref[:, :LANES])
        lane = lax.broadcasted_iota(jnp.int32, logits.shape, 1)
        neg = jnp.float32(-jnp.inf)
        lg = jnp.where(lane < N_EXPERTS, logits + rb_ref[...], neg)
        m1 = jnp.max(lg, axis=1, keepdims=True)
        i1 = jnp.min(jnp.where(lg == m1, lane, LANES), axis=1, keepdims=True)
        lg2 = jnp.where(lane == i1, neg, lg)
        m2 = jnp.max(lg2, axis=1, keepdims=True)
        i2 = jnp.min(jnp.where(lg2 == m2, lane, LANES), axis=1, keepdims=True)
        e2 = jnp.exp(m2 - m1)
        den = 1.0 + e2

        sel1 = lane == i1
        sel2 = lane == i2
        onehot = jnp.where(jnp.logical_or(sel1, sel2), 1.0, 0.0)
        rank = _dot(tri_ref[...], onehot.astype(BF16)) + run
        r1 = jnp.sum(jnp.where(sel1, rank, 0.0), axis=1, keepdims=True)
        r2 = jnp.sum(jnp.where(sel2, rank, 0.0), axis=1, keepdims=True)
        run = run + jnp.sum(onehot, axis=0, keepdims=True)

        meta = jnp.zeros(logits.shape, F32)
        for k, v in ((M_E1, i1.astype(F32)), (M_E2, i2.astype(F32)), (M_W1, 1.0 / den), (M_W2, e2 / den),
                     (M_R1, r1), (M_R2, r2)):
            meta = jnp.where(lane == k, v, meta)
        meta_ref[rows, :] = meta
        mt_ref[:, rows] = jnp.transpose(meta)[:SUB]

    run_ref[...] = run
    cnt_ref[...] = run


def _cast_blocks(steps, rows):
    nblk = 1
    while nblk * 2 <= steps and rows % (nblk * 2 * 2 * SUB) == 0:
        nblk *= 2
    return nblk


def _route_call(cfg, part, x, ymix, mod_l, w_out, g2, router_w, router_b, expert_w=None):
    d = cfg[3]
    t0, nt = part
    n = nt * TM
    f = _wide(cfg)
    tm = TM * f
    s0 = t0 // f
    steps = nt // f
    in_row_spec = pl.BlockSpec((tm, d), lambda j: (j + s0, 0))
    row_spec = pl.BlockSpec((tm, d), lambda j: (j, 0))
    rw = jnp.zeros((d, LANES), F32).at[:, :N_EXPERTS].set(router_w)
    rw_hi = rw.astype(BF16)
    rw = jnp.concatenate([rw_hi, (rw - rw_hi.astype(F32)).astype(BF16)], axis=1)
    rb = jnp.zeros((1, LANES), F32).at[0, :N_EXPERTS].set(router_b)
    tri = jnp.tril(jnp.ones((TM, TM), BF16), -1)
    in_specs = [in_row_spec, in_row_spec, _mod_spec(cfg, f, t0), _const_spec((d, d)), _const_spec((1, d)),
                _const_spec((d, 2 * LANES)), _const_spec((1, LANES)), _const_spec((TM, TM))]
    args = [x, ymix, mod_l, w_out, g2.reshape(1, d), rw, rb, tri]
    out_specs = [row_spec, pl.BlockSpec((tm, d // 2), lambda j: (j, 0)),
                 pl.BlockSpec((tm, LANES), lambda j: (j, 0)), pl.BlockSpec((SUB, tm), lambda j: (0, j)),
                 pl.BlockSpec((1, LANES), lambda j: (0, 0))]
    out_shape = [jax.ShapeDtypeStruct((n, d), F32), jax.ShapeDtypeStruct((n, d // 2), jnp.int32),
                 jax.ShapeDtypeStruct((n, LANES), F32), jax.ShapeDtypeStruct((SUB, n), F32),
                 jax.ShapeDtypeStruct((1, LANES), F32)]
    if expert_w is not None:
        wg, wu, wd, layer = expert_w
        n_l, n_e, _, dfe = wg.shape
        nblk = min(_cast_blocks(steps, n_e * d), _cast_blocks(steps, n_e * dfe))
        blk = lambda j: (jnp.minimum(j, nblk - 1), 0)
        blk_in = lambda j: (layer, jnp.minimum(j, nblk - 1), 0)
        ru, rd = n_e * d // nblk, n_e * dfe // nblk
        in_specs += [pl.BlockSpec((1, ru, dfe), blk_in), pl.BlockSpec((1, ru, dfe), blk_in),
                     pl.BlockSpec((1, rd, d), blk_in)]
        args += [wg.reshape(n_l, n_e * d, dfe), wu.reshape(n_l, n_e * d, dfe), wd.reshape(n_l, n_e * dfe, d)]
        out_specs += [pl.BlockSpec((ru, 2 * dfe), blk), pl.BlockSpec((rd, d), blk)]
        out_shape += [jax.ShapeDtypeStruct((n_e * d, 2 * dfe), BF16), jax.ShapeDtypeStruct((n_e * dfe, d), BF16)]
    outs = pl.pallas_call(
        functools.partial(_route_kernel, expert_w is not None),
        grid=(steps,),
        in_specs=in_specs,
        out_specs=out_specs,
        out_shape=out_shape,
        scratch_shapes=[pltpu.VMEM((1, LANES), F32)],
        compiler_params=_cparams(("arbitrary",)),
        name="out_proj_route",
    )(*args)
    if expert_w is None:
        return outs, None
    return outs[:5], (outs[5].reshape(n_e, d, 2 * dfe), outs[6].reshape(n_e, dfe, d))


def _routing_tables(n_tiles, mt, cnt):
    counts = cnt[0, :N_EXPERTS].astype(jnp.int32)
    padded = ((counts + TM - 1) // TM) * TM
    ends = jnp.cumsum(padded)
    offs = ends - padded
    row = lambda k: mt[k].astype(jnp.int32)
    pos1 = offs[row(M_E1)] + row(M_R1)
    pos2 = offs[row(M_E2)] + row(M_R2)
    n_sorted_tiles = 2 * n_tiles + N_EXPERTS
    starts = jnp.arange(n_sorted_tiles, dtype=jnp.int32) * TM
    tile_e = jnp.minimum(jnp.sum((starts[:, None] >= ends[None, :]).astype(jnp.int32), axis=1), N_EXPERTS - 1)
    n_active = (ends[-1] // TM).reshape(1)
    return jnp.concatenate([pos1, pos2]), tile_e, n_active


SC_CORES = 2
SC_SUBCORES = 16
SC_CHUNK = 64


def _sc_mesh():
    return plsc.VectorSubcoreMesh(core_axis_name="c", subcore_axis_name="s",
                                  num_cores=SC_CORES, num_subcores=SC_SUBCORES)


def _sc_scatter_rows(rows, idx, n_out):
    n_src, width = rows.shape
    n_idx = idx.shape[0]
    n_workers = SC_CORES * SC_SUBCORES
    per_w = n_idx // n_workers
    chunks = per_w // SC_CHUNK
    assert n_idx % (n_workers * SC_CHUNK) == 0 and n_src % per_w == 0
    idx3 = idx.reshape(n_workers, chunks, SC_CHUNK)

    def body(rows_hbm, idx_hbm, out_hbm, idx_v, rows_v, sem):
        wid = lax.axis_index("s") * SC_CORES + lax.axis_index("c")
        src_base = lax.rem(wid * per_w, n_src)
        pltpu.sync_copy(idx_hbm.at[wid], idx_v)

        @pl.loop(0, chunks)
        def _(i):
            off = pl.multiple_of(i * SC_CHUNK, SC_CHUNK)
            pltpu.sync_copy(rows_hbm.at[pl.ds(src_base + off, SC_CHUNK)], rows_v)
            pltpu.async_copy(rows_v, out_hbm.at[idx_v.at[i]], sem).wait()

    return pl.kernel(
        body,
        out_type=jax.ShapeDtypeStruct((n_out, width), rows.dtype),
        mesh=_sc_mesh(),
        scratch_types=[pltpu.VMEM((chunks, SC_CHUNK), jnp.int32), pltpu.VMEM((SC_CHUNK, width), rows.dtype),
                       pltpu.SemaphoreType.DMA],
        name="sc_scatter_rows",
    )(rows, idx3)


def _expert_kernel(te_ref, na_ref, s_ref, wgu_ref, wd_ref, o_ref):
    del te_ref

    @pl.when(pl.program_id(0) < na_ref[0])
    def _():
        h = _unpack_pairs(s_ref[...]).astype(BF16)
        gu = _dot(h, wgu_ref[0])
        dfe = gu.shape[1] // 2
        g = gu[:, :dfe]
        act = (g * _sigmoid(g) * gu[:, dfe:]).astype(BF16)
        o_ref[...] = _pack_pairs(_dot(act, wd_ref[0]).astype(BF16).astype(F32))

    @pl.when(pl.program_id(0) >= na_ref[0])
    def _():
        o_ref[...] = jnp.zeros_like(o_ref)


def _expert_call(cfg, tile_e, n_active, buf, wgu, wd):
    d = cfg[3]
    dfe = wd.shape[1]
    n_sorted_tiles = buf.shape[0] // TM
    last = lambda t, na: jnp.maximum(jnp.minimum(t, na[0] - 1), 0)
    tile = lambda t, te, na: (last(t, na), 0)
    w_spec = lambda shape: pl.BlockSpec(shape, lambda t, te, na: (te[last(t, na)], 0, 0))
    return pl.pallas_call(
        _expert_kernel,
        grid_spec=pltpu.PrefetchScalarGridSpec(
            num_scalar_prefetch=2,
            grid=(n_sorted_tiles,),
            in_specs=[pl.BlockSpec((TM, d // 2), tile), w_spec((1, d, 2 * dfe)), w_spec((1, dfe, d))],
            out_specs=pl.BlockSpec((TM, d // 2), lambda t, te, na: (t, 0)),
        ),
        out_shape=jax.ShapeDtypeStruct((n_sorted_tiles * TM, d // 2), jnp.int32),
        compiler_params=_cparams(("arbitrary",)),
        name="moe_experts",
    )(tile_e, n_active, buf, wgu, wd)


def _store_natural(n_ctx_tiles, part, y, o_refs):
    t0, nt = part
    f = y.shape[0] // TM
    has_ctx = t0 < n_ctx_tiles
    has_lat = t0 + nt > n_ctx_tiles
    is_ctx = pl.program_id(0) * f + t0 < n_ctx_tiles
    if has_ctx:
        @pl.when(is_ctx)
        def _():
            o_refs[0][0] = _from_time_major(y)

    if has_lat:
        @pl.when(jnp.logical_not(is_ctx))
        def _():
            o_refs[-1][...] = _from_time_major(y)


def _natural_out(cfg, part, f=1):
    n_ctx_tiles, tiles_per_ctx, n_tiles, d = cfg
    n_groups = n_ctx_tiles // tiles_per_ctx
    t0, nt = part
    specs = _natural_specs(cfg, t0, f)
    shapes = (jax.ShapeDtypeStruct((n_groups, SUB, tiles_per_ctx * GRID_W, d), F32),
              jax.ShapeDtypeStruct((SUB, (n_tiles - n_ctx_tiles) * GRID_W, d), F32))
    keep = [k for k, used in enumerate((t0 < n_ctx_tiles, t0 + nt > n_ctx_tiles)) if used]
    return [specs[k] for k in keep], [shapes[k] for k in keep], keep


def _sc_gather_rows(table, idx):
    n_rows = idx.shape[0]
    width = table.shape[1]
    n_workers = SC_CORES * SC_SUBCORES
    assert n_rows % (n_workers * SC_CHUNK) == 0
    per_w = n_rows // n_workers
    mesh = _sc_mesh()

    def body(table_hbm, idx_hbm, out_hbm, idx_v, rows_v, sem):
        wid = lax.axis_index("s") * SC_CORES + lax.axis_index("c")
        base = wid * per_w
        pltpu.sync_copy(idx_hbm.at[pl.ds(base, per_w)], idx_v)

        @pl.loop(0, per_w // SC_CHUNK)
        def _(i):
            off = pl.multiple_of(i * SC_CHUNK, SC_CHUNK)
            pltpu.async_copy(table_hbm.at[idx_v.at[pl.ds(off, SC_CHUNK)]], rows_v, sem).wait()
            pltpu.sync_copy(rows_v, out_hbm.at[pl.ds(base + off, SC_CHUNK)])

    return pl.kernel(
        body,
        out_type=jax.ShapeDtypeStruct((n_rows, width), table.dtype),
        mesh=mesh,
        scratch_types=[pltpu.VMEM((per_w,), jnp.int32), pltpu.VMEM((SC_CHUNK, width), table.dtype),
                       pltpu.SemaphoreType.DMA],
        name="sc_gather_rows",
    )(table, idx)


def _combine_kernel(final_ctx_tiles, part, n_filled, xm_ref, y1_ref, y2_ref, meta_ref, mod_ref, gfin_ref,
                    *refs):
    o_refs = refs[n_filled:]
    meta = meta_ref[...]
    y = (meta[:, M_W1:M_W1 + 1] * _unpack_pairs(y1_ref[...])
         + meta[:, M_W2:M_W2 + 1] * _unpack_pairs(y2_ref[...]))
    x = _gated_add(xm_ref[...], mod_ref[0, 5], y)
    if final_ctx_tiles is None:
        o_refs[0][...] = x
    else:
        _store_natural(final_ctx_tiles, part, _rms(x, gfin_ref[...]), o_refs)


def _combine_call(cfg, part, xm, yg, meta, mod_l, g_final, final, filled):
    n_ctx_tiles, _, n_tiles, d = cfg
    t0, nt = part
    f = _seq_tiles(cfg) if t0 % 2 == 0 and nt % 2 == 0 else 1
    tm = f * TM
    steps = nt // f
    row_spec = pl.BlockSpec((tm, d), lambda j: (j, 0))
    if final:
        out_specs, out_shape, keys = _natural_out(cfg, part, f)
    else:
        out_specs = [pl.BlockSpec((tm, d), lambda j: (j + t0 // f, 0))]
        out_shape, keys = [jax.ShapeDtypeStruct((n_tiles * TM, d), F32)], [0]
    reuse = [k for k in keys if k in filled]
    base = [xm, yg, yg, meta, mod_l, g_final.reshape(1, d)]
    outs = pl.pallas_call(
        functools.partial(_combine_kernel, n_ctx_tiles if final else None, part, len(reuse)),
        grid=(steps,),
        in_specs=[row_spec, pl.BlockSpec((tm, d // 2), lambda j: (j, 0)),
                  pl.BlockSpec((tm, d // 2), lambda j: (steps + j, 0)),
                  pl.BlockSpec((tm, LANES), lambda j: (j, 0)), _mod_spec(cfg, f, t0), _const_spec((1, d))]
                 + [pl.BlockSpec(memory_space=pl.ANY)] * len(reuse),
        out_specs=out_specs,
        out_shape=out_shape,
        input_output_aliases={len(base) + i: keys.index(k) for i, k in enumerate(reuse)},
        compiler_params=_cparams(("arbitrary",)),
        name="moe_combine",
    )(*base, *[filled[k] for k in reuse])
    return {**filled, **dict(zip(keys, outs))}


def _final_norm_kernel(n_ctx_tiles, part, x_ref, g_ref, *o_refs):
    _store_natural(n_ctx_tiles, part, _rms(x_ref[...], g_ref[...]), o_refs)


def _final_norm_call(cfg, x, g):
    n_ctx_tiles, _, n_tiles, d = cfg
    out_specs, out_shape, _ = _natural_out(cfg, (0, n_tiles))
    return pl.pallas_call(
        functools.partial(_final_norm_kernel, n_ctx_tiles, (0, n_tiles)),
        grid=(n_tiles,),
        in_specs=[pl.BlockSpec((TM, d), lambda j: (j, 0)), _const_spec((1, d))],
        out_specs=out_specs,
        out_shape=out_shape,
        compiler_params=_cparams(("arbitrary",)),
        name="final_norm",
    )(x, g.reshape(1, d))


def _block_diag(w, per_block):
    *lead, n, k, _ = w.shape
    nb = n // per_block
    w = w.reshape(*lead, nb, per_block, k, k)
    eye = jnp.eye(per_block, dtype=w.dtype)
    out = w[..., :, :, None, :] * eye[:, None, :, None]
    return out.reshape(*lead, nb, per_block * k, per_block * k)


def _pos_tables(n_ctx_tiles, n_lat_tiles, d):
    quarter = d // 4
    omega = 1.0 / (POS_BASE ** (jnp.arange(quarter, dtype=F32) / quarter))
    er = jnp.arange(n_lat_tiles, dtype=F32)[:, None] * omega
    ec = jnp.arange(GRID_W, dtype=F32)[:, None] * omega
    row_emb = jnp.concatenate([jnp.sin(er), jnp.cos(er)], axis=-1)
    col_emb = jnp.concatenate([jnp.sin(ec), jnp.cos(ec)], axis=-1)
    del n_ctx_tiles
    return row_emb[:, None, :], jnp.repeat(col_emb, SUB, axis=0)


def kernel(x_prompt, x_sample, state_lru, c, c_ctx, norm_mix_g, w_ada, b_ada, w_in, conv_w, conv_b, lru_wa, lru_ba, lru_wx, lru_bx, lru_lam, pool_w, pool_b, pool_scale, w_out, norm_ffn_g, ffn_wg, ffn_wu, ffn_wd, moe_router_w, moe_router_b, moe_wg, moe_wu, moe_wd, norm_final_g):
    bc, tc, d = x_prompt.shape
    bl, tl, _ = x_sample.shape
    depth = w_ada.shape[0]
    c_lru = conv_w.shape[-1]
    assert bl == SUB and bc % SUB == 0 and tc % GRID_W == 0 and tl % GRID_W == 0
    assert c_lru == 4 * LANES and d == 2 * c_lru
    n_groups = bc // SUB
    tiles_per_ctx = tc // GRID_W
    n_ctx_tiles = n_groups * tiles_per_ctx
    n_lat_tiles = tl // GRID_W
    n_tiles = n_ctx_tiles + n_lat_tiles
    n_ctx = n_ctx_tiles * TM
    cfg = (n_ctx_tiles, tiles_per_ctx, n_tiles, d)

    cond = jnp.concatenate([c, c_ctx[None], jnp.zeros((SUB - 1, d), F32)], axis=0)
    mod = _ada_call(cond, w_ada, b_ada)
    mod_lat = mod[:, :SUB].reshape(depth, SUB, 6, d).transpose(0, 2, 1, 3)
    mod_ctx = jnp.broadcast_to(mod[:, SUB].reshape(depth, 6, 1, d), (depth, 6, SUB, d))
    mod = jnp.stack([mod_ctx, mod_lat], axis=1)

    row_tab, col_rep = _pos_tables(n_ctx_tiles, n_lat_tiles, d)
    first = (x_prompt.reshape(n_groups, SUB, tc, d), x_sample, row_tab, col_rep)
    x = None
    wg_lru = (0.5 * jnp.concatenate([_block_diag(lru_wa, 4), _block_diag(lru_wx, 4)], axis=-1)).astype(BF16)
    pw = _block_diag(pool_w, 2).astype(BF16)
    inv_cnt = _pool_inv_counts(tc, c_lru)
    h0 = jnp.concatenate([jnp.zeros_like(state_lru[None]), state_lru[None]], axis=0)

    states = []
    for l in range(depth):
        jdx = l // 2
        if l == 0:
            proj, x = _inproj_call(cfg, x, mod[l], norm_mix_g[l], w_in[l].astype(BF16), first)
        else:
            (proj,) = _inproj_call(cfg, x, mod[l], norm_mix_g[l], w_in[l].astype(BF16))
        to_cast = [(w_out, l)] + ([(ffn_wg, jdx), (ffn_wu, jdx), (ffn_wd, jdx)] if l % 2 == 0 else [])
        hf, st_f, w_out_b, *ffn_b = _scan_fwd_call(cfg, proj, conv_w[l], conv_b[l], wg_lru[l, 0], lru_ba[l, 0],
                                                   lru_bx[l, 0], lru_lam[l, 0], h0[:, :, l, 0], to_cast)
        ymix, st_b = _scan_bwd_call(cfg, inv_cnt, proj, hf, conv_w[l], conv_b[l], wg_lru[l, 1], lru_ba[l, 1],
                                    lru_bx[l, 1], lru_lam[l, 1], h0[:, :, l, 1], pw[l], pool_b[l],
                                    pool_scale[l])
        states.append(jnp.stack([st_f[:n_groups].reshape(bc, c_lru), st_b[:n_groups].reshape(bc, c_lru)], axis=1))
        if l % 2 == 0:
            x = _dense_ffn_call(cfg, x, ymix, mod[l], w_out_b, norm_ffn_g[l], *ffn_b)
        else:
            final = l == depth - 1
            filled = {}
            expert_w = None
            for part in _moe_parts(cfg):
                to_cast = (moe_wg, moe_wu, moe_wd, jdx) if expert_w is None else None
                (xm, hp, meta, mt, cnt), cast = _route_call(cfg, part, x, ymix, mod[l], w_out_b, norm_ffn_g[l],
                                                            moe_router_w[jdx], moe_router_b[jdx], to_cast)
                expert_w = expert_w or cast
                pos_flat, tile_e, n_active = _routing_tables(part[1], mt, cnt)
                buf = _sc_scatter_rows(hp, pos_flat, (2 * part[1] + N_EXPERTS) * TM)
                ys = _expert_call(cfg, tile_e, n_active, buf, *expert_w)
                yg = _sc_gather_rows(ys, pos_flat)
                filled = _combine_call(cfg, part, xm, yg, meta, mod[l], norm_final_g, final, filled)
            x = (filled[0], filled[1]) if final else filled[0]

    y_prompt, y_sample = x if depth % 2 == 0 else _final_norm_call(cfg, x, norm_final_g)
    new_state = jnp.stack(states, axis=1)
    return (y_prompt.reshape(bc, tc, d), y_sample, new_state)
```

```python
import functools

import jax
import jax.numpy as jnp
from jax import lax
from jax.experimental import pallas as pl
from jax.experimental.pallas import tpu as pltpu
from jax.experimental.pallas import tpu_sc as plsc

F32 = jnp.float32
BF16 = jnp.bfloat16

SUB = 8
LANES = 128
GRID_W = 64
TM = GRID_W * SUB
POS_BASE = 10000.0
N_LRU_HEADS = 8
CONV_W = 4
CONV_LEFT = CONV_W // 2
CONV_RIGHT = CONV_W - 1 - CONV_LEFT
LRU_C = 8.0
POOL_WINDOWS = (2, 4, 8, 16)
POOL_HALO = 8 * SUB
N_EXPERTS = 8
EPS = 1e-6
FF_CHUNK = 256
FWD_TILES = 4
VMEM_LIMIT = 56 * 1024 * 1024


def _cparams(sem):
    return pltpu.CompilerParams(dimension_semantics=sem, vmem_limit_bytes=VMEM_LIMIT)


def _const_spec(shape):
    nd = len(shape)
    return pl.BlockSpec(shape, lambda *_: (0,) * nd, pipeline_mode=pl.Buffered(1))


def _rms(x, g):
    ms = jnp.mean(x * x, axis=-1, keepdims=True)
    return x * lax.rsqrt(ms + EPS) * g


def _per_seq(x, v, op):
    r, c = x.shape
    x3 = x.reshape(r // SUB, SUB, c)
    return op(x3, v[None]).reshape(r, c)


def _modulate(h, scale, shift):
    r, c = h.shape
    h3 = h.reshape(r // SUB, SUB, c)
    return (h3 * (1.0 + scale)[None] + shift[None]).reshape(r, c)


def _gated_add(x, gate, y):
    return x + _per_seq(y, gate, lambda a, b: a * b)


def _dot(a, b):
    return jnp.dot(a, b, preferred_element_type=F32)


def _pack_pairs(x):
    bits = lax.bitcast_convert_type(x, jnp.uint32)
    half = bits.shape[1] // 2
    w = lax.shift_right_logical(bits[:, :half], jnp.uint32(16)) | (bits[:, half:] & jnp.uint32(0xFFFF0000))
    return lax.bitcast_convert_type(w, jnp.int32)


def _unpack_pairs(w):
    w = lax.bitcast_convert_type(w, jnp.uint32)
    lo = lax.bitcast_convert_type(lax.shift_left(w, jnp.uint32(16)), F32)
    hi = lax.bitcast_convert_type(w & jnp.uint32(0xFFFF0000), F32)
    return jnp.concatenate([lo, hi], axis=1)


def _sigmoid(x):
    return 0.5 * jnp.tanh(0.5 * x) + 0.5


def _ada_kernel(c_ref, w_ref, b_ref, o_ref):
    c = c_ref[...]
    s = (c * jax.nn.sigmoid(c)).astype(BF16)
    o_ref[0] = _dot(s, w_ref[0].astype(BF16)) + b_ref[0]


def _ada_call(cond, w_ada, b_ada):
    depth, d, d6 = w_ada.shape
    nr = cond.shape[0]
    bn = d6 // 4
    return pl.pallas_call(
        _ada_kernel,
        grid=(depth, d6 // bn),
        in_specs=[
            pl.BlockSpec((nr, d), lambda l, n: (0, 0)),
            pl.BlockSpec((1, d, bn), lambda l, n: (l, 0, n)),
            pl.BlockSpec((1, 1, bn), lambda l, n: (l, 0, n)),
        ],
        out_specs=pl.BlockSpec((1, nr, bn), lambda l, n: (l, 0, n)),
        out_shape=jax.ShapeDtypeStruct((depth, nr, d6), F32),
        compiler_params=_cparams(("parallel", "parallel")),
        name="ada_mod",
    )(cond, w_ada, b_ada.reshape(depth, 1, d6))


def _to_time_major(x):
    s, t, d = x.shape
    return jnp.swapaxes(x, 0, 1).reshape(s * t, d)


def _from_time_major(x):
    r, d = x.shape
    return jnp.swapaxes(x.reshape(r // SUB, SUB, d), 0, 1)


def _inproj_kernel(n_ctx_tiles, *refs):
    if n_ctx_tiles is not None:
        xp_ref, xs_ref, row_ref, col_ref, mod_ref, g_ref, w_ref, o_ref, x0_ref = refs
        f = row_ref.shape[0]
        is_ctx = pl.program_id(0) * f < n_ctx_tiles

        @pl.when(is_ctx)
        def _():
            x0_ref[...] = _to_time_major(xp_ref[0])

        @pl.when(jnp.logical_not(is_ctx))
        def _():
            x = _to_time_major(xs_ref[...])
            half = x.shape[1] // 2
            for k in range(f):
                rows = slice(k * TM, (k + 1) * TM)
                x0_ref[rows, :] = jnp.concatenate([x[rows, :half] + row_ref[k], x[rows, half:] + col_ref[...]],
                                                  axis=1)

        x = x0_ref[...]
    else:
        x_ref, mod_ref, g_ref, w_ref, o_ref = refs
        x = x_ref[...]
    h = _modulate(_rms(x, g_ref[...]), mod_ref[0, 1], mod_ref[0, 0])
    o_ref[...] = _dot(h.astype(BF16), w_ref[...])


def _natural_specs(cfg, t0=0, f=1):
    n_ctx_tiles, tiles_per_ctx, _, d = cfg

    def ctx_idx(j):
        jc = jnp.minimum(j * f + t0, n_ctx_tiles - f)
        return (jc // tiles_per_ctx, 0, lax.rem(jc, tiles_per_ctx) // f, 0)

    return (pl.BlockSpec((1, SUB, f * GRID_W, d), ctx_idx),
            pl.BlockSpec((SUB, f * GRID_W, d), lambda j: (0, jnp.maximum(j * f + t0 - n_ctx_tiles, 0) // f, 0)))


def _wide(cfg):
    n_ctx_tiles, _, n_tiles, _ = cfg
    return 2 if n_ctx_tiles % 2 == 0 and n_tiles % 2 == 0 else 1


def _moe_parts(cfg):
    n_tiles = cfg[2]
    half = n_tiles // 2
    if _wide(cfg) == 2 and half % 2 == 0:
        return [(0, half), (half, n_tiles - half)]
    return [(0, n_tiles)]


def _mod_spec(cfg, f=1, t0=0):
    n_ctx_steps = cfg[0] // f
    d = cfg[3]
    s0 = t0 // f
    return pl.BlockSpec((1, 6, SUB, d), lambda j: (jnp.where(j + s0 >= n_ctx_steps, 1, 0), 0, 0, 0))


def _inproj_call(cfg, x, mod_l, g, w_in, first=None):
    n_ctx_tiles, _, n_tiles, d = cfg
    d_in = w_in.shape[1]
    f = _seq_tiles(cfg) if first is not None else _wide(cfg)
    tm = TM * f
    row_spec = pl.BlockSpec((tm, d), lambda j: (j, 0))
    out_specs = [pl.BlockSpec((tm, d_in), lambda j: (j, 0))]
    out_shape = [jax.ShapeDtypeStruct((n_tiles * TM, d_in), F32)]
    if first is not None:
        xp_spec, xs_spec = _natural_specs(cfg, 0, f)
        in_specs = [xp_spec, xs_spec,
                    pl.BlockSpec((f, 1, d // 2), lambda j: (jnp.maximum(j - n_ctx_tiles // f, 0), 0, 0)),
                    _const_spec((TM, d // 2))]
        args = list(first)
        out_specs.append(row_spec)
        out_shape.append(jax.ShapeDtypeStruct((n_tiles * TM, d), F32))
    else:
        in_specs = [row_spec]
        args = [x]
    in_specs += [_mod_spec(cfg, f), _const_spec((1, d)), _const_spec((d, d_in))]
    args += [mod_l, g.reshape(1, d), w_in]
    return pl.pallas_call(
        functools.partial(_inproj_kernel, n_ctx_tiles if first is not None else None),
        grid=(n_tiles // f,),
        in_specs=in_specs,
        out_specs=out_specs,
        out_shape=out_shape,
        compiler_params=_cparams(("parallel",)),
        name="in_proj",
    )(*args)


def _seq_flags(cfg, j):
    n_ctx_tiles, tiles_per_ctx, n_tiles, _ = cfg
    is_ctx = j < n_ctx_tiles
    pos = lax.rem(j, tiles_per_ctx)
    first = jnp.where(is_ctx, pos == 0, j == n_ctx_tiles)
    last = jnp.where(is_ctx, pos == tiles_per_ctx - 1, j == n_tiles - 1)
    return is_ctx, pos, first, last


def _conv(pad_ref, xa, prev, nxt, first, last, cw, cb):
    lo = CONV_LEFT * SUB
    pad_ref[0:lo, :] = jnp.where(first, 0.0, prev)
    pad_ref[lo:lo + TM, :] = xa
    pad_ref[lo + TM:lo + TM + CONV_RIGHT * SUB, :] = jnp.where(last, 0.0, nxt)
    y = cb
    for k in range(CONV_W):
        y = y + pad_ref[k * SUB:k * SUB + TM, :] * cw[k:k + 1, :]
    return y


def _gates(xc, wg_ref, ba, bx, lam):
    half = xc.shape[1] // 2
    xb = xc.astype(BF16)
    g0 = _dot(xb[:, :half], wg_ref[0])
    g1 = _dot(xb[:, half:], wg_ref[1])
    t_r = jnp.tanh(jnp.concatenate([g0[:, :half], g1[:, :half]], axis=1) + 0.5 * ba)
    t_i = jnp.tanh(jnp.concatenate([g0[:, half:], g1[:, half:]], axis=1) + 0.5 * bx)
    z = -lam
    half_decay = (0.5 * LRU_C) * (jnp.maximum(z, 0.0) + jnp.log1p(jnp.exp(-jnp.abs(z))))
    neg_log_a = t_r * half_decay + half_decay
    a = jnp.exp(-neg_log_a)
    z = jnp.tanh(neg_log_a) * (a * a + 1.0)
    root = jnp.where(z > 0.0, z * lax.rsqrt(z), 0.0)
    half_xc = 0.5 * xc
    u = root * (t_i * half_xc + half_xc)
    return a, u


def _scan(a_ref, u_ref, h_ref, row0, h, reverse):
    steps = TM // SUB

    def body(k, h):
        t = steps - 1 - k if reverse else k
        r0 = pl.multiple_of(t * SUB, SUB)
        h = a_ref[pl.ds(r0, SUB), :] * h + u_ref[pl.ds(r0, SUB), :]
        h_ref[pl.ds(row0 + r0, SUB), :] = h
        return h

    return lax.fori_loop(0, steps, body, h, unroll=8)


def _seq_tiles(cfg, want=2):
    n_ctx_tiles, tiles_per_ctx, n_tiles, _ = cfg
    f = 1
    while f * 2 <= want and n_ctx_tiles % (f * 2) == 0 and tiles_per_ctx % (f * 2) == 0 and n_tiles % (f * 2) == 0:
        f *= 2
    return f


def _inner_halo(ref, sub, f, col0, c, rows, outer_prev, outer_next):
    prev = outer_prev[...] if sub == 0 else ref[sub * TM - rows[0]:sub * TM, col0:col0 + c]
    nxt = outer_next[...] if sub == f - 1 else ref[(sub + 1) * TM:(sub + 1) * TM + rows[1], col0:col0 + c]
    return prev, nxt


def _scan_fwd_kernel(cfg, f, n_cast, xa_ref, xp_ref, xn_ref, cw_ref, cb_ref, wg_ref, ba_ref, bx_ref,
                     lam_ref, h0_ref, *refs):
    cast_in, (hf_ref, st_ref), cast_out = refs[:n_cast], refs[n_cast:n_cast + 2], refs[n_cast + 2:2 * n_cast + 2]
    pad_ref, a_ref, u_ref, carry_ref = refs[2 * n_cast + 2:]
    for w_ref, wb_ref in zip(cast_in, cast_out):
        wb_ref[...] = w_ref[0].astype(BF16)
    c = xa_ref.shape[1]
    for sub in range(f):
        _, _, first, last = _seq_flags(cfg, pl.program_id(0) * f + sub)
        prev, nxt = _inner_halo(xa_ref, sub, f, 0, c, (CONV_LEFT * SUB, CONV_RIGHT * SUB), xp_ref, xn_ref)
        xc = _conv(pad_ref, xa_ref[sub * TM:(sub + 1) * TM, :], prev, nxt, first, last, cw_ref[...], cb_ref[...])
        a, u = _gates(xc, wg_ref, ba_ref[...], bx_ref[...], lam_ref[...])
        a_ref[...] = a
        u_ref[...] = u

        @pl.when(first)
        def _():
            carry_ref[...] = h0_ref[0]

        h = _scan(a_ref, u_ref, hf_ref, sub * TM, carry_ref[...], reverse=False)
        carry_ref[...] = h
        st_ref[0] = h


def _halo_specs(cfg, col, rows_prev, rows_next, blk_of, f):
    n_tiles, d = cfg[2], cfg[3]
    c = d // 2
    nb_prev = TM // rows_prev
    nb_next = TM // rows_next
    last_next = n_tiles * nb_next - 1
    prev = pl.BlockSpec((rows_prev, c), lambda j: (jnp.maximum(blk_of(j) * f * nb_prev - 1, 0), col))
    nxt = pl.BlockSpec((rows_next, c), lambda j: (jnp.minimum((blk_of(j) + 1) * f * nb_next, last_next), col))
    return prev, nxt


def _state_spec(cfg, blk_of, f):
    n_ctx_tiles, tiles_per_ctx, _, d = cfg
    n_groups = n_ctx_tiles // tiles_per_ctx
    return pl.BlockSpec((1, SUB, d // 2),
                        lambda j: (jnp.minimum(blk_of(j) * f // tiles_per_ctx, n_groups), 0, 0))


def _h0_spec(cfg, blk_of, f):
    n_ctx_tiles, d = cfg[0], cfg[3]
    return pl.BlockSpec((1, SUB, d // 2), lambda j: (jnp.where(blk_of(j) * f >= n_ctx_tiles, 1, 0), 0, 0))


def _scan_fwd_call(cfg, proj, conv_w, conv_b, wg, ba, bx, lam, h0, cast=()):
    n_ctx_tiles, tiles_per_ctx, n_tiles, _ = cfg
    n_groups = n_ctx_tiles // tiles_per_ctx
    c = conv_w.shape[1]
    f = _seq_tiles(cfg, FWD_TILES)
    steps = n_tiles // f
    ident = lambda j: j
    xp_spec, xn_spec = _halo_specs(cfg, 0, CONV_LEFT * SUB, CONV_RIGHT * SUB, ident, f)
    cast_in, cast_out, cast_shape = [], [], []
    for w, layer in cast:
        _, rows, cols = w.shape
        nblk = _cast_blocks(steps, rows)
        cast_in.append(pl.BlockSpec((1, rows // nblk, cols),
                                    lambda j, layer=layer, nblk=nblk: (layer, jnp.minimum(j, nblk - 1), 0)))
        cast_out.append(pl.BlockSpec((rows // nblk, cols), lambda j, nblk=nblk: (jnp.minimum(j, nblk - 1), 0)))
        cast_shape.append(jax.ShapeDtypeStruct((rows, cols), BF16))
    return pl.pallas_call(
        functools.partial(_scan_fwd_kernel, cfg, f, len(cast)),
        grid=(steps,),
        in_specs=[
            pl.BlockSpec((f * TM, c), lambda j: (j, 0)), xp_spec, xn_spec,
            _const_spec((CONV_W, c)), _const_spec((1, c)), _const_spec((2, c // 2, c)),
            _const_spec((1, c)), _const_spec((1, c)), _const_spec((1, c)),
            _h0_spec(cfg, ident, f),
        ] + cast_in,
        out_specs=[pl.BlockSpec((f * TM, c), lambda j: (j, 0)), _state_spec(cfg, ident, f)] + cast_out,
        out_shape=[jax.ShapeDtypeStruct((n_tiles * TM, c), F32),
                   jax.ShapeDtypeStruct((n_groups + 1, SUB, c), F32)] + cast_shape,
        scratch_shapes=[
            pltpu.VMEM((TM + (CONV_W - 1) * SUB, c), F32),
            pltpu.VMEM((TM, c), F32), pltpu.VMEM((TM, c), F32), pltpu.VMEM((SUB, c), F32),
        ],
        compiler_params=_cparams(("arbitrary",)),
        name="lru_fwd",
    )(proj, proj, proj, conv_w, conv_b.reshape(1, c), wg, ba.reshape(1, c), bx.reshape(1, c),
      lam.reshape(1, c), h0, *[w for w, _ in cast])


def _pool_inv_counts(ctx_len, c):
    gw = c // len(POOL_WINDOWS)

    def table(t0, t_len):
        t = t0 + jnp.arange(GRID_W)
        cols = []
        for k in POOL_WINDOWS:
            left = k // 2
            right = k - 1 - left
            cnt = jnp.minimum(t + right + 1, t_len) - jnp.maximum(t - left, 0)
            cols.append(jnp.broadcast_to((1.0 / cnt.astype(F32))[:, None], (GRID_W, gw)))
        return jnp.repeat(jnp.concatenate(cols, axis=1), SUB, axis=0)

    return jnp.stack([table(p * GRID_W, ctx_len) for p in range(ctx_len // GRID_W)] + [table(0, GRID_W)])


def _gelu_tanh(x):
    k0 = 0.7978845608028654
    hx = 0.5 * x
    return hx + hx * jnp.tanh(x * (k0 + (k0 * 0.044715) * (x * x)))


def _pool_mix(pad_ref, inv_ref, pw_ref, pb, ps):
    c = pad_ref.shape[1]
    gw = c // len(POOL_WINDOWS)
    outs = []
    for g, k in enumerate(POOL_WINDOWS):
        left = k // 2
        right = k - 1 - left
        lanes = slice(g * gw, (g + 1) * gw)
        s = None
        for o in range(-left, right + 1):
            v = pad_ref[POOL_HALO + o * SUB:POOL_HALO + o * SUB + TM, lanes]
            s = v if s is None else s + v
        outs.append(s * inv_ref[0, :, lanes] - pad_ref[POOL_HALO:POOL_HALO + TM, lanes])
    d = jnp.concatenate(outs, axis=1).astype(BF16)
    half = c // 2
    y = jnp.concatenate([_dot(d[:, :half], pw_ref[0]), _dot(d[:, half:], pw_ref[1])], axis=1)
    return (y + pb) * ps


def _scan_bwd_kernel(cfg, f, proj_ref, xp_ref, xn_ref, bp_ref, bn_ref, hf_ref, *refs):
    inv_refs, refs = refs[:f], refs[f:]
    (cw_ref, cb_ref, wg_ref, ba_ref, bx_ref, lam_ref, h0_ref, pw_ref, pb_ref, ps_ref,
     y_ref, st_ref, pad_ref, ppad_ref, a_ref, u_ref, hb_ref, carry_ref) = refs
    n_blocks = cfg[2] // f
    blk = n_blocks - 1 - pl.program_id(0)
    c = hf_ref.shape[1]
    for sub in reversed(range(f)):
        rows = slice(sub * TM, (sub + 1) * TM)
        is_ctx, _, first, last = _seq_flags(cfg, blk * f + sub)
        prev, nxt = _inner_halo(proj_ref, sub, f, 0, c, (CONV_LEFT * SUB, CONV_RIGHT * SUB), xp_ref, xn_ref)
        xc = _conv(pad_ref, proj_ref[rows, 0:c], prev, nxt, first, last, cw_ref[...], cb_ref[...])
        a, u = _gates(xc, wg_ref, ba_ref[...], bx_ref[...], lam_ref[...])
        a_ref[...] = a
        u_ref[...] = u

        @pl.when(last)
        def _():
            carry_ref[...] = h0_ref[0]

        h = _scan(a_ref, u_ref, hb_ref, 0, carry_ref[...], reverse=True)
        carry_ref[...] = h
        st_ref[0] = h

        ga = proj_ref[rows, c:2 * c]
        y_a = (hf_ref[rows, :] + hb_ref[...]) * _gelu_tanh(ga)

        use_prev = jnp.logical_and(is_ctx, jnp.logical_not(first))
        use_next = jnp.logical_and(is_ctx, jnp.logical_not(last))
        bprev, bnext = _inner_halo(proj_ref, sub, f, 2 * c, c, (POOL_HALO, POOL_HALO), bp_ref, bn_ref)
        ppad_ref[0:POOL_HALO, :] = jnp.where(use_prev, bprev, 0.0)
        ppad_ref[POOL_HALO:POOL_HALO + TM, :] = proj_ref[rows, 2 * c:3 * c]
        ppad_ref[POOL_HALO + TM:, :] = jnp.where(use_next, bnext, 0.0)
        y_b = _pool_mix(ppad_ref, inv_refs[sub], pw_ref, pb_ref[...], ps_ref[...])
        y_ref[rows, :] = jnp.concatenate([y_a, y_b], axis=1).astype(BF16)


def _scan_bwd_call(cfg, inv_cnt, proj, hf, conv_w, conv_b, wg, ba, bx, lam, h0, pw, pb, ps):
    n_ctx_tiles, tiles_per_ctx, n_tiles, d = cfg
    n_groups = n_ctx_tiles // tiles_per_ctx
    c = conv_w.shape[1]
    f = _seq_tiles(cfg)
    n_blocks = n_tiles // f
    rev = lambda j: n_blocks - 1 - j
    xp_spec, xn_spec = _halo_specs(cfg, 0, CONV_LEFT * SUB, CONV_RIGHT * SUB, rev, f)
    bp_spec, bn_spec = _halo_specs(cfg, 2, POOL_HALO, POOL_HALO, rev, f)

    def inv_spec(sub):
        def idx(j):
            tile = rev(j) * f + sub
            return (jnp.where(tile < n_ctx_tiles, lax.rem(tile, tiles_per_ctx), tiles_per_ctx), 0, 0)
        return pl.BlockSpec((1, TM, c), idx)

    return pl.pallas_call(
        functools.partial(_scan_bwd_kernel, cfg, f),
        grid=(n_blocks,),
        in_specs=[
            pl.BlockSpec((f * TM, 3 * c), lambda j: (rev(j), 0)), xp_spec, xn_spec, bp_spec, bn_spec,
            pl.BlockSpec((f * TM, c), lambda j: (rev(j), 0)),
            *[inv_spec(sub) for sub in range(f)],
            _const_spec((CONV_W, c)), _const_spec((1, c)), _const_spec((2, c // 2, c)),
            _const_spec((1, c)), _const_spec((1, c)), _const_spec((1, c)),
            _h0_spec(cfg, rev, f),
            _const_spec((2, c // 2, c // 2)), _const_spec((1, c)), _const_spec((1, c)),
        ],
        out_specs=[pl.BlockSpec((f * TM, d), lambda j: (rev(j), 0)), _state_spec(cfg, rev, f)],
        out_shape=[jax.ShapeDtypeStruct((n_tiles * TM, d), BF16),
                   jax.ShapeDtypeStruct((n_groups + 1, SUB, c), F32)],
        scratch_shapes=[
            pltpu.VMEM((TM + (CONV_W - 1) * SUB, c), F32),
            pltpu.VMEM((TM + 2 * POOL_HALO, c), F32),
            pltpu.VMEM((TM, c), F32), pltpu.VMEM((TM, c), F32), pltpu.VMEM((TM, c), F32),
            pltpu.VMEM((SUB, c), F32),
        ],
        compiler_params=_cparams(("arbitrary",)),
        name="lru_bwd_mix",
    )(proj, proj, proj, proj, proj, hf, *([inv_cnt] * f), conv_w, conv_b.reshape(1, c), wg, ba.reshape(1, c),
      bx.reshape(1, c), lam.reshape(1, c), h0, pw, pb.reshape(1, c), ps.reshape(1, c))


def _out_proj(x_ref, ym_ref, mod_ref, wo_ref, g2_ref):
    xm = _gated_add(x_ref[...], mod_ref[0, 2], _dot(ym_ref[...], wo_ref[...]))
    h2 = _modulate(_rms(xm, g2_ref[...]), mod_ref[0, 4], mod_ref[0, 3])
    return xm, h2


def _swiglu_act(h2b, wg_ref, wu_ref, act_ref):
    n_chunks = act_ref.shape[1] // FF_CHUNK

    def body(f, carry):
        c0 = pl.multiple_of(f * FF_CHUNK, FF_CHUNK)
        g = _dot(h2b, wg_ref[:, pl.ds(c0, FF_CHUNK)])
        u = _dot(h2b, wu_ref[:, pl.ds(c0, FF_CHUNK)])
        act_ref[:, pl.ds(c0, FF_CHUNK)] = (g * _sigmoid(g) * u).astype(BF16)
        return carry

    lax.fori_loop(0, n_chunks, body, 0, unroll=True)


def _dense_ffn_kernel(x_ref, ym_ref, mod_ref, wo_ref, g2_ref, wg_ref, wu_ref, wd_ref, o_ref, act_ref):
    xm, h2 = _out_proj(x_ref, ym_ref, mod_ref, wo_ref, g2_ref)
    _swiglu_act(h2.astype(BF16), wg_ref, wu_ref, act_ref)
    o_ref[...] = _gated_add(xm, mod_ref[0, 5], _dot(act_ref[...], wd_ref[...]))


def _dense_ffn_call(cfg, x, ymix, mod_l, w_out, g2, wg, wu, wd):
    _, _, n_tiles, d = cfg
    d_ff = wg.shape[1]
    f = _wide(cfg)
    row_spec = pl.BlockSpec((TM * f, d), lambda j: (j, 0))
    return pl.pallas_call(
        _dense_ffn_kernel,
        grid=(n_tiles // f,),
        in_specs=[row_spec, row_spec, _mod_spec(cfg, f), _const_spec((d, d)), _const_spec((1, d)),
                  _const_spec((d, d_ff)), _const_spec((d, d_ff)), _const_spec((d_ff, d))],
        out_specs=row_spec,
        out_shape=jax.ShapeDtypeStruct((n_tiles * TM, d), F32),
        scratch_shapes=[pltpu.VMEM((TM * f, d_ff), BF16)],
        compiler_params=_cparams(("parallel",)),
        name="out_proj_dense_ffn",
    )(x, ymix, mod_l, w_out, g2.reshape(1, d), wg, wu, wd)


M_E1, M_E2, M_W1, M_W2, M_R1, M_R2 = range(6)


def _route_kernel(cast_weights, x_ref, ym_ref, mod_ref, wo_ref, g2_ref, rw_ref, rb_ref, tri_ref, *refs):
    if cast_weights:
        wg_ref, wu_ref, wd_ref, *refs = refs
        wgu_ref, wdb_ref = refs[5:7]
        wgu_ref[...] = jnp.concatenate([wg_ref[0], wu_ref[0]], axis=1).astype(BF16)
        wdb_ref[...] = wd_ref[0].astype(BF16)
        refs = refs[:5] + refs[7:]
    xm_ref, hp_ref, meta_ref, mt_ref, cnt_ref, run_ref = refs

    @pl.when(pl.program_id(0) == 0)
    def _():
        run_ref[...] = jnp.zeros_like(run_ref)

    run = run_ref[...]
    for r0 in range(0, x_ref.shape[0], TM):
        rows = pl.ds(r0, TM)
        xm = _gated_add(x_ref[rows, :], mod_ref[0, 2], _dot(ym_ref[rows, :], wo_ref[...]))
        h2 = _modulate(_rms(xm, g2_ref[...]), mod_ref[0, 4], mod_ref[0, 3])
        xm_ref[rows, :] = xm
        h_hi = h2.astype(BF16)
        h_hi32 = h_hi.astype(F32)
        hp_ref[rows, :] = _pack_pairs(h_hi32)

        h_lo = (h2 - h_hi32).astype(BF16)
        p = _dot(h_hi, rw_ref[...])
        logits = p[:, :LANES] + p[:, LANES:] + _dot(h_lo, rw_ref[:, :LANES])
        lane = lax.broadcasted_iota(jnp.int32, logits.shape, 1)
        neg = jnp.float32(-jnp.inf)
        lg = jnp.where(lane < N_EXPERTS, logits + rb_ref[...], neg)
        m1 = jnp.max(lg, axis=1, keepdims=True)
        i1 = jnp.min(jnp.where(lg == m1, lane, LANES), axis=1, keepdims=True)
        lg2 = jnp.where(lane == i1, neg, lg)
        m2 = jnp.max(lg2, axis=1, keepdims=True)
        i2 = jnp.min(jnp.where(lg2 == m2, lane, LANES), axis=1, keepdims=True)
        e2 = jnp.exp(m2 - m1)
        den = 1.0 + e2

        sel1 = lane == i1
        sel2 = lane == i2
        onehot = jnp.where(jnp.logical_or(sel1, sel2), 1.0, 0.0)
        rank = _dot(tri_ref[...], onehot.astype(BF16)) + run
        r1 = jnp.sum(jnp.where(sel1, rank, 0.0), axis=1, keepdims=True)
        r2 = jnp.sum(jnp.where(sel2, rank, 0.0), axis=1, keepdims=True)
        run = run + jnp.sum(onehot, axis=0, keepdims=True)

        meta = jnp.zeros(logits.shape, F32)
        for k, v in ((M_E1, i1.astype(F32)), (M_E2, i2.astype(F32)), (M_W1, 1.0 / den), (M_W2, e2 / den),
                     (M_R1, r1), (M_R2, r2)):
            meta = jnp.where(lane == k, v, meta)
        meta_ref[rows, :] = meta
        mt_ref[:, rows] = jnp.transpose(meta)[:SUB]

    run_ref[...] = run
    cnt_ref[...] = run


def _cast_blocks(steps, rows):
    nblk = 1
    while nblk * 2 <= steps and rows % (nblk * 2 * 2 * SUB) == 0:
        nblk *= 2
    return nblk


def _route_call(cfg, part, x, ymix, mod_l, w_out, g2, router_w, router_b, expert_w=None):
    d = cfg[3]
    t0, nt = part
    n = nt * TM
    f = _wide(cfg)
    tm = TM * f
    s0 = t0 // f
    steps = nt // f
    in_row_spec = pl.BlockSpec((tm, d), lambda j: (j + s0, 0))
    row_spec = pl.BlockSpec((tm, d), lambda j: (j, 0))
    rw = jnp.zeros((d, LANES), F32).at[:, :N_EXPERTS].set(router_w)
    rw_hi = rw.astype(BF16)
    rw = jnp.concatenate([rw_hi, (rw - rw_hi.astype(F32)).astype(BF16)], axis=1)
    rb = jnp.zeros((1, LANES), F32).at[0, :N_EXPERTS].set(router_b)
    tri = jnp.tril(jnp.ones((TM, TM), BF16), -1)
    in_specs = [in_row_spec, in_row_spec, _mod_spec(cfg, f, t0), _const_spec((d, d)), _const_spec((1, d)),
                _const_spec((d, 2 * LANES)), _const_spec((1, LANES)), _const_spec((TM, TM))]
    args = [x, ymix, mod_l, w_out, g2.reshape(1, d), rw, rb, tri]
    out_specs = [row_spec, pl.BlockSpec((tm, d // 2), lambda j: (j, 0)),
                 pl.BlockSpec((tm, LANES), lambda j: (j, 0)), pl.BlockSpec((SUB, tm), lambda j: (0, j)),
                 pl.BlockSpec((1, LANES), lambda j: (0, 0))]
    out_shape = [jax.ShapeDtypeStruct((n, d), F32), jax.ShapeDtypeStruct((n, d // 2), jnp.int32),
                 jax.ShapeDtypeStruct((n, LANES), F32), jax.ShapeDtypeStruct((SUB, n), F32),
                 jax.ShapeDtypeStruct((1, LANES), F32)]
    if expert_w is not None:
        wg, wu, wd, layer = expert_w
        n_l, n_e, _, dfe = wg.shape
        nblk = min(_cast_blocks(steps, n_e * d), _cast_blocks(steps, n_e * dfe))
        blk = lambda j: (jnp.minimum(j, nblk - 1), 0)
        blk_in = lambda j: (layer, jnp.minimum(j, nblk - 1), 0)
        ru, rd = n_e * d // nblk, n_e * dfe // nblk
        in_specs += [pl.BlockSpec((1, ru, dfe), blk_in), pl.BlockSpec((1, ru, dfe), blk_in),
                     pl.BlockSpec((1, rd, d), blk_in)]
        args += [wg.reshape(n_l, n_e * d, dfe), wu.reshape(n_l, n_e * d, dfe), wd.reshape(n_l, n_e * dfe, d)]
        out_specs += [pl.BlockSpec((ru, 2 * dfe), blk), pl.BlockSpec((rd, d), blk)]
        out_shape += [jax.ShapeDtypeStruct((n_e * d, 2 * dfe), BF16), jax.ShapeDtypeStruct((n_e * dfe, d), BF16)]
    outs = pl.pallas_call(
        functools.partial(_route_kernel, expert_w is not None),
        grid=(steps,),
        in_specs=in_specs,
        out_specs=out_specs,
        out_shape=out_shape,
        scratch_shapes=[pltpu.VMEM((1, LANES), F32)],
        compiler_params=_cparams(("arbitrary",)),
        name="out_proj_route",
    )(*args)
    if expert_w is None:
        return outs, None
    return outs[:5], (outs[5].reshape(n_e, d, 2 * dfe), outs[6].reshape(n_e, dfe, d))


def _routing_tables(n_tiles, mt, cnt):
    counts = cnt[0, :N_EXPERTS].astype(jnp.int32)
    padded = ((counts + TM - 1) // TM) * TM
    ends = jnp.cumsum(padded)
    offs = ends - padded
    row = lambda k: mt[k].astype(jnp.int32)
    pos1 = offs[row(M_E1)] + row(M_R1)
    pos2 = offs[row(M_E2)] + row(M_R2)
    n_sorted_tiles = 2 * n_tiles + N_EXPERTS
    starts = jnp.arange(n_sorted_tiles, dtype=jnp.int32) * TM
    tile_e = jnp.minimum(jnp.sum((starts[:, None] >= ends[None, :]).astype(jnp.int32), axis=1), N_EXPERTS - 1)
    n_active = (ends[-1] // TM).reshape(1)
    return jnp.concatenate([pos1, pos2]), tile_e, n_active


SC_CORES = 2
SC_SUBCORES = 16
SC_CHUNK = 64


def _sc_mesh():
    return plsc.VectorSubcoreMesh(core_axis_name="c", subcore_axis_name="s",
                                  num_cores=SC_CORES, num_subcores=SC_SUBCORES)


def _sc_scatter_rows(rows, idx, n_out):
    n_src, width = rows.shape
    n_idx = idx.shape[0]
    n_workers = SC_CORES * SC_SUBCORES
    per_w = n_idx // n_workers
    chunks = per_w // SC_CHUNK
    assert n_idx % (n_workers * SC_CHUNK) == 0 and n_src % per_w == 0
    idx3 = idx.reshape(n_workers, chunks, SC_CHUNK)

    def body(rows_hbm, idx_hbm, out_hbm, idx_v, rows_v, sem):
        wid = lax.axis_index("s") * SC_CORES + lax.axis_index("c")
        src_base = lax.rem(wid * per_w, n_src)
        pltpu.sync_copy(idx_hbm.at[wid], idx_v)

        @pl.loop(0, chunks)
        def _(i):
            off = pl.multiple_of(i * SC_CHUNK, SC_CHUNK)
            pltpu.sync_copy(rows_hbm.at[pl.ds(src_base + off, SC_CHUNK)], rows_v)
            pltpu.async_copy(rows_v, out_hbm.at[idx_v.at[i]], sem).wait()

    return pl.kernel(
        body,
        out_type=jax.ShapeDtypeStruct((n_out, width), rows.dtype),
        mesh=_sc_mesh(),
        scratch_types=[pltpu.VMEM((chunks, SC_CHUNK), jnp.int32), pltpu.VMEM((SC_CHUNK, width), rows.dtype),
                       pltpu.SemaphoreType.DMA],
        name="sc_scatter_rows",
    )(rows, idx3)


def _expert_kernel(te_ref, na_ref, s_ref, wgu_ref, wd_ref, o_ref):
    del te_ref

    @pl.when(pl.program_id(0) < na_ref[0])
    def _():
        h = _unpack_pairs(s_ref[...]).astype(BF16)
        gu = _dot(h, wgu_ref[0])
        dfe = gu.shape[1] // 2
        g = gu[:, :dfe]
        act = (g * _sigmoid(g) * gu[:, dfe:]).astype(BF16)
        o_ref[...] = _pack_pairs(_dot(act, wd_ref[0]).astype(BF16).astype(F32))

    @pl.when(pl.program_id(0) >= na_ref[0])
    def _():
        o_ref[...] = jnp.zeros_like(o_ref)


def _expert_call(cfg, tile_e, n_active, buf, wgu, wd):
    d = cfg[3]
    dfe = wd.shape[1]
    n_sorted_tiles = buf.shape[0] // TM
    last = lambda t, na: jnp.maximum(jnp.minimum(t, na[0] - 1), 0)
    tile = lambda t, te, na: (last(t, na), 0)
    w_spec = lambda shape: pl.BlockSpec(shape, lambda t, te, na: (te[last(t, na)], 0, 0))
    return pl.pallas_call(
        _expert_kernel,
        grid_spec=pltpu.PrefetchScalarGridSpec(
            num_scalar_prefetch=2,
            grid=(n_sorted_tiles,),
            in_specs=[pl.BlockSpec((TM, d // 2), tile), w_spec((1, d, 2 * dfe)), w_spec((1, dfe, d))],
            out_specs=pl.BlockSpec((TM, d // 2), lambda t, te, na: (t, 0)),
        ),
        out_shape=jax.ShapeDtypeStruct((n_sorted_tiles * TM, d // 2), jnp.int32),
        compiler_params=_cparams(("arbitrary",)),
        name="moe_experts",
    )(tile_e, n_active, buf, wgu, wd)


def _store_natural(n_ctx_tiles, part, y, o_refs):
    t0, nt = part
    f = y.shape[0] // TM
    has_ctx = t0 < n_ctx_tiles
    has_lat = t0 + nt > n_ctx_tiles
    is_ctx = pl.program_id(0) * f + t0 < n_ctx_tiles
    if has_ctx:
        @pl.when(is_ctx)
        def _():
            o_refs[0][0] = _from_time_major(y)

    if has_lat:
        @pl.when(jnp.logical_not(is_ctx))
        def _():
            o_refs[-1][...] = _from_time_major(y)


def _natural_out(cfg, part, f=1):
    n_ctx_tiles, tiles_per_ctx, n_tiles, d = cfg
    n_groups = n_ctx_tiles // tiles_per_ctx
    t0, nt = part
    specs = _natural_specs(cfg, t0, f)
    shapes = (jax.ShapeDtypeStruct((n_groups, SUB, tiles_per_ctx * GRID_W, d), F32),
              jax.ShapeDtypeStruct((SUB, (n_tiles - n_ctx_tiles) * GRID_W, d), F32))
    keep = [k for k, used in enumerate((t0 < n_ctx_tiles, t0 + nt > n_ctx_tiles)) if used]
    return [specs[k] for k in keep], [shapes[k] for k in keep], keep


def _sc_gather_rows(table, idx):
    n_rows = idx.shape[0]
    width = table.shape[1]
    n_workers = SC_CORES * SC_SUBCORES
    assert n_rows % (n_workers * SC_CHUNK) == 0
    per_w = n_rows // n_workers
    mesh = _sc_mesh()

    def body(table_hbm, idx_hbm, out_hbm, idx_v, rows_v, sem):
        wid = lax.axis_index("s") * SC_CORES + lax.axis_index("c")
        base = wid * per_w
        pltpu.sync_copy(idx_hbm.at[pl.ds(base, per_w)], idx_v)

        @pl.loop(0, per_w // SC_CHUNK)
        def _(i):
            off = pl.multiple_of(i * SC_CHUNK, SC_CHUNK)
            pltpu.async_copy(table_hbm.at[idx_v.at[pl.ds(off, SC_CHUNK)]], rows_v, sem).wait()
            pltpu.sync_copy(rows_v, out_hbm.at[pl.ds(base + off, SC_CHUNK)])

    return pl.kernel(
        body,
        out_type=jax.ShapeDtypeStruct((n_rows, width), table.dtype),
        mesh=mesh,
        scratch_types=[pltpu.VMEM((per_w,), jnp.int32), pltpu.VMEM((SC_CHUNK, width), table.dtype),
                       pltpu.SemaphoreType.DMA],
        name="sc_gather_rows",
    )(table, idx)


def _combine_kernel(final_ctx_tiles, part, n_filled, xm_ref, y1_ref, y2_ref, meta_ref, mod_ref, gfin_ref,
                    *refs):
    o_refs = refs[n_filled:]
    meta = meta_ref[...]
    y = (meta[:, M_W1:M_W1 + 1] * _unpack_pairs(y1_ref[...])
         + meta[:, M_W2:M_W2 + 1] * _unpack_pairs(y2_ref[...]))
    x = _gated_add(xm_ref[...], mod_ref[0, 5], y)
    if final_ctx_tiles is None:
        o_refs[0][...] = x
    else:
        _store_natural(final_ctx_tiles, part, _rms(x, gfin_ref[...]), o_refs)


def _combine_call(cfg, part, xm, yg, meta, mod_l, g_final, final, filled):
    n_ctx_tiles, _, n_tiles, d = cfg
    t0, nt = part
    f = _seq_tiles(cfg) if t0 % 2 == 0 and nt % 2 == 0 else 1
    tm = f * TM
    steps = nt // f
    row_spec = pl.BlockSpec((tm, d), lambda j: (j, 0))
    if final:
        out_specs, out_shape, keys = _natural_out(cfg, part, f)
    else:
        out_specs = [pl.BlockSpec((tm, d), lambda j: (j + t0 // f, 0))]
        out_shape, keys = [jax.ShapeDtypeStruct((n_tiles * TM, d), F32)], [0]
    reuse = [k for k in keys if k in filled]
    base = [xm, yg, yg, meta, mod_l, g_final.reshape(1, d)]
    outs = pl.pallas_call(
        functools.partial(_combine_kernel, n_ctx_tiles if final else None, part, len(reuse)),
        grid=(steps,),
        in_specs=[row_spec, pl.BlockSpec((tm, d // 2), lambda j: (j, 0)),
                  pl.BlockSpec((tm, d // 2), lambda j: (steps + j, 0)),
                  pl.BlockSpec((tm, LANES), lambda j: (j, 0)), _mod_spec(cfg, f, t0), _const_spec((1, d))]
                 + [pl.BlockSpec(memory_space=pl.ANY)] * len(reuse),
        out_specs=out_specs,
        out_shape=out_shape,
        input_output_aliases={len(base) + i: keys.index(k) for i, k in enumerate(reuse)},
        compiler_params=_cparams(("arbitrary",)),
        name="moe_combine",
    )(*base, *[filled[k] for k in reuse])
    return {**filled, **dict(zip(keys, outs))}


def _final_norm_kernel(n_ctx_tiles, part, x_ref, g_ref, *o_refs):
    _store_natural(n_ctx_tiles, part, _rms(x_ref[...], g_ref[...]), o_refs)


def _final_norm_call(cfg, x, g):
    n_ctx_tiles, _, n_tiles, d = cfg
    out_specs, out_shape, _ = _natural_out(cfg, (0, n_tiles))
    return pl.pallas_call(
        functools.partial(_final_norm_kernel, n_ctx_tiles, (0, n_tiles)),
        grid=(n_tiles,),
        in_specs=[pl.BlockSpec((TM, d), lambda j: (j, 0)), _const_spec((1, d))],
        out_specs=out_specs,
        out_shape=out_shape,
        compiler_params=_cparams(("arbitrary",)),
        name="final_norm",
    )(x, g.reshape(1, d))


def _block_diag(w, per_block):
    *lead, n, k, _ = w.shape
    nb = n // per_block
    w = w.reshape(*lead, nb, per_block, k, k)
    eye = jnp.eye(per_block, dtype=w.dtype)
    out = w[..., :, :, None, :] * eye[:, None, :, None]
    return out.reshape(*lead, nb, per_block * k, per_block * k)


def _pos_tables(n_ctx_tiles, n_lat_tiles, d):
    quarter = d // 4
    omega = 1.0 / (POS_BASE ** (jnp.arange(quarter, dtype=F32) / quarter))
    er = jnp.arange(n_lat_tiles, dtype=F32)[:, None] * omega
    ec = jnp.arange(GRID_W, dtype=F32)[:, None] * omega
    row_emb = jnp.concatenate([jnp.sin(er), jnp.cos(er)], axis=-1)
    col_emb = jnp.concatenate([jnp.sin(ec), jnp.cos(ec)], axis=-1)
    del n_ctx_tiles
    return row_emb[:, None, :], jnp.repeat(col_emb, SUB, axis=0)


def kernel(x_prompt, x_sample, state_lru, c, c_ctx, norm_mix_g, w_ada, b_ada, w_in, conv_w, conv_b, lru_wa, lru_ba, lru_wx, lru_bx, lru_lam, pool_w, pool_b, pool_scale, w_out, norm_ffn_g, ffn_wg, ffn_wu, ffn_wd, moe_router_w, moe_router_b, moe_wg, moe_wu, moe_wd, norm_final_g):
    bc, tc, d = x_prompt.shape
    bl, tl, _ = x_sample.shape
    depth = w_ada.shape[0]
    c_lru = conv_w.shape[-1]
    assert bl == SUB and bc % SUB == 0 and tc % GRID_W == 0 and tl % GRID_W == 0
    assert c_lru == 4 * LANES and d == 2 * c_lru
    n_groups = bc // SUB
    tiles_per_ctx = tc // GRID_W
    n_ctx_tiles = n_groups * tiles_per_ctx
    n_lat_tiles = tl // GRID_W
    n_tiles = n_ctx_tiles + n_lat_tiles
    n_ctx = n_ctx_tiles * TM
    cfg = (n_ctx_tiles, tiles_per_ctx, n_tiles, d)

    cond = jnp.concatenate([c, c_ctx[None], jnp.zeros((SUB - 1, d), F32)], axis=0)
    mod = _ada_call(cond, w_ada, b_ada)
    mod_lat = mod[:, :SUB].reshape(depth, SUB, 6, d).transpose(0, 2, 1, 3)
    mod_ctx = jnp.broadcast_to(mod[:, SUB].reshape(depth, 6, 1, d), (depth, 6, SUB, d))
    mod = jnp.stack([mod_ctx, mod_lat], axis=1)

    row_tab, col_rep = _pos_tables(n_ctx_tiles, n_lat_tiles, d)
    first = (x_prompt.reshape(n_groups, SUB, tc, d), x_sample, row_tab, col_rep)
    x = None
    wg_lru = (0.5 * jnp.concatenate([_block_diag(lru_wa, 4), _block_diag(lru_wx, 4)], axis=-1)).astype(BF16)
    pw = _block_diag(pool_w, 2).astype(BF16)
    inv_cnt = _pool_inv_counts(tc, c_lru)
    h0 = jnp.concatenate([jnp.zeros_like(state_lru[None]), state_lru[None]], axis=0)

    states = []
    for l in range(depth):
        jdx = l // 2
        if l == 0:
            proj, x = _inproj_call(cfg, x, mod[l], norm_mix_g[l], w_in[l].astype(BF16), first)
        else:
            (proj,) = _inproj_call(cfg, x, mod[l], norm_mix_g[l], w_in[l].astype(BF16))
        to_cast = [(w_out, l)] + ([(ffn_wg, jdx), (ffn_wu, jdx), (ffn_wd, jdx)] if l % 2 == 0 else [])
        hf, st_f, w_out_b, *ffn_b = _scan_fwd_call(cfg, proj, conv_w[l], conv_b[l], wg_lru[l, 0], lru_ba[l, 0],
                                                   lru_bx[l, 0], lru_lam[l, 0], h0[:, :, l, 0], to_cast)
        ymix, st_b = _scan_bwd_call(cfg, inv_cnt, proj, hf, conv_w[l], conv_b[l], wg_lru[l, 1], lru_ba[l, 1],
                                    lru_bx[l, 1], lru_lam[l, 1], h0[:, :, l, 1], pw[l], pool_b[l],
                                    pool_scale[l])
        states.append(jnp.stack([st_f[:n_groups].reshape(bc, c_lru), st_b[:n_groups].reshape(bc, c_lru)], axis=1))
        if l % 2 == 0:
            x = _dense_ffn_call(cfg, x, ymix, mod[l], w_out_b, norm_ffn_g[l], *ffn_b)
        else:
            final = l == depth - 1
            filled = {}
            expert_w = None
            for part in _moe_parts(cfg):
                to_cast = (moe_wg, moe_wu, moe_wd, jdx) if expert_w is None else None
                (xm, hp, meta, mt, cnt), cast = _route_call(cfg, part, x, ymix, mod[l], w_out_b, norm_ffn_g[l],
                                                            moe_router_w[jdx], moe_router_b[jdx], to_cast)
                expert_w = expert_w or cast
                pos_flat, tile_e, n_active = _routing_tables(part[1], mt, cnt)
                buf = _sc_scatter_rows(hp, pos_flat, (2 * part[1] + N_EXPERTS) * TM)
                ys = _expert_call(cfg, tile_e, n_active, buf, *expert_w)
                yg = _sc_gather_rows(ys, pos_flat)
                filled = _combine_call(cfg, part, xm, yg, meta, mod[l], norm_final_g, final, filled)
            x = (filled[0], filled[1]) if final else filled[0]

    y_prompt, y_sample = x if depth % 2 == 0 else _final_norm_call(cfg, x, norm_final_g)
    new_state = jnp.stack(states, axis=1)
    return (y_prompt.reshape(bc, tc, d), y_sample, new_state)
```

```python
import functools

import jax
import jax.numpy as jnp
from jax import lax
from jax.experimental import pallas as pl
from jax.experimental.pallas import tpu as pltpu
from jax.experimental.pallas import tpu_sc as plsc

F32 = jnp.float32
BF16 = jnp.bfloat16

SUB = 8
LANES = 128
GRID_W = 64
TM = GRID_W * SUB
POS_BASE = 10000.0
N_LRU_HEADS = 8
CONV_W = 4
CONV_LEFT = CONV_W // 2
CONV_RIGHT = CONV_W - 1 - CONV_LEFT
LRU_C = 8.0
POOL_WINDOWS = (2, 4, 8, 16)
POOL_HALO = 8 * SUB
N_EXPERTS = 8
EPS = 1e-6
FF_CHUNK = 256
FWD_TILES = 4
INPROJ_TILES = 4
VMEM_LIMIT = 56 * 1024 * 1024


def _cparams(sem):
    return pltpu.CompilerParams(dimension_semantics=sem, vmem_limit_bytes=VMEM_LIMIT)


def _const_spec(shape):
    nd = len(shape)
    return pl.BlockSpec(shape, lambda *_: (0,) * nd, pipeline_mode=pl.Buffered(1))


def _rms(x, g):
    ms = jnp.mean(x * x, axis=-1, keepdims=True)
    return x * lax.rsqrt(ms + EPS) * g


def _per_seq(x, v, op):
    r, c = x.shape
    x3 = x.reshape(r // SUB, SUB, c)
    return op(x3, v[None]).reshape(r, c)


def _modulate(h, scale, shift):
    r, c = h.shape
    h3 = h.reshape(r // SUB, SUB, c)
    return (h3 * (1.0 + scale)[None] + shift[None]).reshape(r, c)


def _gated_add(x, gate, y):
    return x + _per_seq(y, gate, lambda a, b: a * b)


def _dot(a, b):
    return jnp.dot(a, b, preferred_element_type=F32)


def _pack_pairs(x):
    bits = lax.bitcast_convert_type(x, jnp.uint32)
    half = bits.shape[1] // 2
    w = lax.shift_right_logical(bits[:, :half], jnp.uint32(16)) | (bits[:, half:] & jnp.uint32(0xFFFF0000))
    return lax.bitcast_convert_type(w, jnp.int32)


def _unpack_pairs(w):
    w = lax.bitcast_convert_type(w, jnp.uint32)
    lo = lax.bitcast_convert_type(lax.shift_left(w, jnp.uint32(16)), F32)
    hi = lax.bitcast_convert_type(w & jnp.uint32(0xFFFF0000), F32)
    return jnp.concatenate([lo, hi], axis=1)


def _sigmoid(x):
    return 0.5 * jnp.tanh(0.5 * x) + 0.5


def _ada_kernel(c_ref, w_ref, b_ref, o_ref):
    c = c_ref[...]
    s = (c * jax.nn.sigmoid(c)).astype(BF16)
    o_ref[0] = _dot(s, w_ref[0].astype(BF16)) + b_ref[0]


def _ada_call(cond, w_ada, b_ada):
    depth, d, d6 = w_ada.shape
    nr = cond.shape[0]
    bn = d6 // 4
    return pl.pallas_call(
        _ada_kernel,
        grid=(depth, d6 // bn),
        in_specs=[
            pl.BlockSpec((nr, d), lambda l, n: (0, 0)),
            pl.BlockSpec((1, d, bn), lambda l, n: (l, 0, n)),
            pl.BlockSpec((1, 1, bn), lambda l, n: (l, 0, n)),
        ],
        out_specs=pl.BlockSpec((1, nr, bn), lambda l, n: (l, 0, n)),
        out_shape=jax.ShapeDtypeStruct((depth, nr, d6), F32),
        compiler_params=_cparams(("parallel", "parallel")),
        name="ada_mod",
    )(cond, w_ada, b_ada.reshape(depth, 1, d6))


def _to_time_major(x):
    s, t, d = x.shape
    return jnp.swapaxes(x, 0, 1).reshape(s * t, d)


def _from_time_major(x):
    r, d = x.shape
    return jnp.swapaxes(x.reshape(r // SUB, SUB, d), 0, 1)


def _inproj_kernel(n_ctx_tiles, *refs):
    if n_ctx_tiles is not None:
        xp_ref, xs_ref, row_ref, col_ref, mod_ref, g_ref, w_ref, o_ref, x0_ref = refs
        f = row_ref.shape[0]
        is_ctx = pl.program_id(0) * f < n_ctx_tiles

        @pl.when(is_ctx)
        def _():
            x0_ref[...] = _to_time_major(xp_ref[0])

        @pl.when(jnp.logical_not(is_ctx))
        def _():
            x = _to_time_major(xs_ref[...])
            half = x.shape[1] // 2
            for k in range(f):
                rows = slice(k * TM, (k + 1) * TM)
                x0_ref[rows, :] = jnp.concatenate([x[rows, :half] + row_ref[k], x[rows, half:] + col_ref[...]],
                                                  axis=1)

        x = x0_ref[...]
    else:
        x_ref, mod_ref, g_ref, w_ref, o_ref = refs
        x = x_ref[...]
    h = _modulate(_rms(x, g_ref[...]), mod_ref[0, 1], mod_ref[0, 0])
    o_ref[...] = _dot(h.astype(BF16), w_ref[...])


def _natural_specs(cfg, t0=0, f=1):
    n_ctx_tiles, tiles_per_ctx, _, d = cfg

    def ctx_idx(j):
        jc = jnp.minimum(j * f + t0, n_ctx_tiles - f)
        return (jc // tiles_per_ctx, 0, lax.rem(jc, tiles_per_ctx) // f, 0)

    return (pl.BlockSpec((1, SUB, f * GRID_W, d), ctx_idx),
            pl.BlockSpec((SUB, f * GRID_W, d), lambda j: (0, jnp.maximum(j * f + t0 - n_ctx_tiles, 0) // f, 0)))


def _wide(cfg):
    n_ctx_tiles, _, n_tiles, _ = cfg
    return 2 if n_ctx_tiles % 2 == 0 and n_tiles % 2 == 0 else 1


def _moe_parts(cfg):
    n_tiles = cfg[2]
    half = n_tiles // 2
    if _wide(cfg) == 2 and half % 2 == 0:
        return [(0, half), (half, n_tiles - half)]
    return [(0, n_tiles)]


def _mod_spec(cfg, f=1, t0=0):
    n_ctx_steps = cfg[0] // f
    d = cfg[3]
    s0 = t0 // f
    return pl.BlockSpec((1, 6, SUB, d), lambda j: (jnp.where(j + s0 >= n_ctx_steps, 1, 0), 0, 0, 0))


def _inproj_call(cfg, x, mod_l, g, w_in, first=None):
    n_ctx_tiles, _, n_tiles, d = cfg
    d_in = w_in.shape[1]
    f = _seq_tiles(cfg) if first is not None else _wide(cfg)
    if first is None and n_ctx_tiles % INPROJ_TILES == 0 and n_tiles % INPROJ_TILES == 0:
        f = INPROJ_TILES
    tm = TM * f
    row_spec = pl.BlockSpec((tm, d), lambda j: (j, 0))
    out_specs = [pl.BlockSpec((tm, d_in), lambda j: (j, 0))]
    out_shape = [jax.ShapeDtypeStruct((n_tiles * TM, d_in), F32)]
    if first is not None:
        xp_spec, xs_spec = _natural_specs(cfg, 0, f)
        in_specs = [xp_spec, xs_spec,
                    pl.BlockSpec((f, 1, d // 2), lambda j: (jnp.maximum(j - n_ctx_tiles // f, 0), 0, 0)),
                    _const_spec((TM, d // 2))]
        args = list(first)
        out_specs.append(row_spec)
        out_shape.append(jax.ShapeDtypeStruct((n_tiles * TM, d), F32))
    else:
        in_specs = [row_spec]
        args = [x]
    in_specs += [_mod_spec(cfg, f), _const_spec((1, d)), _const_spec((d, d_in))]
    args += [mod_l, g.reshape(1, d), w_in]
    return pl.pallas_call(
        functools.partial(_inproj_kernel, n_ctx_tiles if first is not None else None),
        grid=(n_tiles // f,),
        in_specs=in_specs,
        out_specs=out_specs,
        out_shape=out_shape,
        compiler_params=_cparams(("parallel",)),
        name="in_proj",
    )(*args)


def _seq_flags(cfg, j):
    n_ctx_tiles, tiles_per_ctx, n_tiles, _ = cfg
    is_ctx = j < n_ctx_tiles
    pos = lax.rem(j, tiles_per_ctx)
    first = jnp.where(is_ctx, pos == 0, j == n_ctx_tiles)
    last = jnp.where(is_ctx, pos == tiles_per_ctx - 1, j == n_tiles - 1)
    return is_ctx, pos, first, last


def _conv(pad_ref, xa, prev, nxt, first, last, cw, cb):
    lo = CONV_LEFT * SUB
    pad_ref[0:lo, :] = jnp.where(first, 0.0, prev)
    pad_ref[lo:lo + TM, :] = xa
    pad_ref[lo + TM:lo + TM + CONV_RIGHT * SUB, :] = jnp.where(last, 0.0, nxt)
    y = cb
    for k in range(CONV_W):
        y = y + pad_ref[k * SUB:k * SUB + TM, :] * cw[k:k + 1, :]
    return y


def _gates(xc, wg_ref, ba, bx, lam):
    half = xc.shape[1] // 2
    xb = xc.astype(BF16)
    g0 = _dot(xb[:, :half], wg_ref[0])
    g1 = _dot(xb[:, half:], wg_ref[1])
    t_r = jnp.tanh(jnp.concatenate([g0[:, :half], g1[:, :half]], axis=1) + 0.5 * ba)
    t_i = jnp.tanh(jnp.concatenate([g0[:, half:], g1[:, half:]], axis=1) + 0.5 * bx)
    z = -lam
    half_decay = (0.5 * LRU_C) * (jnp.maximum(z, 0.0) + jnp.log1p(jnp.exp(-jnp.abs(z))))
    neg_log_a = t_r * half_decay + half_decay
    a = jnp.exp(-neg_log_a)
    z = jnp.tanh(neg_log_a) * (a * a + 1.0)
    root = jnp.where(z > 0.0, z * lax.rsqrt(z), 0.0)
    half_xc = 0.5 * xc
    u = root * (t_i * half_xc + half_xc)
    return a, u


def _scan(a_ref, u_ref, h_ref, row0, h, reverse):
    steps = TM // SUB

    def body(k, h):
        t = steps - 1 - k if reverse else k
        r0 = pl.multiple_of(t * SUB, SUB)
        h = a_ref[pl.ds(r0, SUB), :] * h + u_ref[pl.ds(r0, SUB), :]
        h_ref[pl.ds(row0 + r0, SUB), :] = h
        return h

    return lax.fori_loop(0, steps, body, h, unroll=8)


def _seq_tiles(cfg, want=2):
    n_ctx_tiles, tiles_per_ctx, n_tiles, _ = cfg
    f = 1
    while f * 2 <= want and n_ctx_tiles % (f * 2) == 0 and tiles_per_ctx % (f * 2) == 0 and n_tiles % (f * 2) == 0:
        f *= 2
    return f


def _inner_halo(ref, sub, f, col0, c, rows, outer_prev, outer_next):
    prev = outer_prev[...] if sub == 0 else ref[sub * TM - rows[0]:sub * TM, col0:col0 + c]
    nxt = outer_next[...] if sub == f - 1 else ref[(sub + 1) * TM:(sub + 1) * TM + rows[1], col0:col0 + c]
    return prev, nxt


def _scan_fwd_kernel(cfg, f, n_cast, xa_ref, xp_ref, xn_ref, cw_ref, cb_ref, wg_ref, ba_ref, bx_ref,
                     lam_ref, h0_ref, *refs):
    cast_in, (hf_ref, st_ref), cast_out = refs[:n_cast], refs[n_cast:n_cast + 2], refs[n_cast + 2:2 * n_cast + 2]
    pad_ref, a_ref, u_ref, carry_ref = refs[2 * n_cast + 2:]
    for w_ref, wb_ref in zip(cast_in, cast_out):
        wb_ref[...] = w_ref[0].astype(BF16)
    c = xa_ref.shape[1]
    for sub in range(f):
        _, _, first, last = _seq_flags(cfg, pl.program_id(0) * f + sub)
        prev, nxt = _inner_halo(xa_ref, sub, f, 0, c, (CONV_LEFT * SUB, CONV_RIGHT * SUB), xp_ref, xn_ref)
        xc = _conv(pad_ref, xa_ref[sub * TM:(sub + 1) * TM, :], prev, nxt, first, last, cw_ref[...], cb_ref[...])
        a, u = _gates(xc, wg_ref, ba_ref[...], bx_ref[...], lam_ref[...])
        a_ref[...] = a
        u_ref[...] = u

        @pl.when(first)
        def _():
            carry_ref[...] = h0_ref[0]

        h = _scan(a_ref, u_ref, hf_ref, sub * TM, carry_ref[...], reverse=False)
        carry_ref[...] = h
        st_ref[0] = h


def _halo_specs(cfg, col, rows_prev, rows_next, blk_of, f):
    n_tiles, d = cfg[2], cfg[3]
    c = d // 2
    nb_prev = TM // rows_prev
    nb_next = TM // rows_next
    last_next = n_tiles * nb_next - 1
    prev = pl.BlockSpec((rows_prev, c), lambda j: (jnp.maximum(blk_of(j) * f * nb_prev - 1, 0), col))
    nxt = pl.BlockSpec((rows_next, c), lambda j: (jnp.minimum((blk_of(j) + 1) * f * nb_next, last_next), col))
    return prev, nxt


def _state_spec(cfg, blk_of, f):
    n_ctx_tiles, tiles_per_ctx, _, d = cfg
    n_groups = n_ctx_tiles // tiles_per_ctx
    return pl.BlockSpec((1, SUB, d // 2),
                        lambda j: (jnp.minimum(blk_of(j) * f // tiles_per_ctx, n_groups), 0, 0))


def _h0_spec(cfg, blk_of, f):
    n_ctx_tiles, d = cfg[0], cfg[3]
    return pl.BlockSpec((1, SUB, d // 2), lambda j: (jnp.where(blk_of(j) * f >= n_ctx_tiles, 1, 0), 0, 0))


def _scan_fwd_call(cfg, proj, conv_w, conv_b, wg, ba, bx, lam, h0, cast=()):
    n_ctx_tiles, tiles_per_ctx, n_tiles, _ = cfg
    n_groups = n_ctx_tiles // tiles_per_ctx
    c = conv_w.shape[1]
    f = _seq_tiles(cfg, FWD_TILES)
    steps = n_tiles // f
    ident = lambda j: j
    xp_spec, xn_spec = _halo_specs(cfg, 0, CONV_LEFT * SUB, CONV_RIGHT * SUB, ident, f)
    cast_in, cast_out, cast_shape = [], [], []
    for w, layer in cast:
        _, rows, cols = w.shape
        nblk = _cast_blocks(steps, rows)
        cast_in.append(pl.BlockSpec((1, rows // nblk, cols),
                                    lambda j, layer=layer, nblk=nblk: (layer, jnp.minimum(j, nblk - 1), 0)))
        cast_out.append(pl.BlockSpec((rows // nblk, cols), lambda j, nblk=nblk: (jnp.minimum(j, nblk - 1), 0)))
        cast_shape.append(jax.ShapeDtypeStruct((rows, cols), BF16))
    return pl.pallas_call(
        functools.partial(_scan_fwd_kernel, cfg, f, len(cast)),
        grid=(steps,),
        in_specs=[
            pl.BlockSpec((f * TM, c), lambda j: (j, 0)), xp_spec, xn_spec,
            _const_spec((CONV_W, c)), _const_spec((1, c)), _const_spec((2, c // 2, c)),
            _const_spec((1, c)), _const_spec((1, c)), _const_spec((1, c)),
            _h0_spec(cfg, ident, f),
        ] + cast_in,
        out_specs=[pl.BlockSpec((f * TM, c), lambda j: (j, 0)), _state_spec(cfg, ident, f)] + cast_out,
        out_shape=[jax.ShapeDtypeStruct((n_tiles * TM, c), F32),
                   jax.ShapeDtypeStruct((n_groups + 1, SUB, c), F32)] + cast_shape,
        scratch_shapes=[
            pltpu.VMEM((TM + (CONV_W - 1) * SUB, c), F32),
            pltpu.VMEM((TM, c), F32), pltpu.VMEM((TM, c), F32), pltpu.VMEM((SUB, c), F32),
        ],
        compiler_params=_cparams(("arbitrary",)),
        name="lru_fwd",
    )(proj, proj, proj, conv_w, conv_b.reshape(1, c), wg, ba.reshape(1, c), bx.reshape(1, c),
      lam.reshape(1, c), h0, *[w for w, _ in cast])


def _pool_inv_counts(ctx_len, c):
    gw = c // len(POOL_WINDOWS)

    def table(t0, t_len):
        t = t0 + jnp.arange(GRID_W)
        cols = []
        for k in POOL_WINDOWS:
            left = k // 2
            right = k - 1 - left
            cnt = jnp.minimum(t + right + 1, t_len) - jnp.maximum(t - left, 0)
            cols.append(jnp.broadcast_to((1.0 / cnt.astype(F32))[:, None], (GRID_W, gw)))
        return jnp.repeat(jnp.concatenate(cols, axis=1), SUB, axis=0)

    return jnp.stack([table(p * GRID_W, ctx_len) for p in range(ctx_len // GRID_W)] + [table(0, GRID_W)])


def _gelu_tanh(x):
    k0 = 0.7978845608028654
    hx = 0.5 * x
    return hx + hx * jnp.tanh(x * (k0 + (k0 * 0.044715) * (x * x)))


def _pool_mix(pad_ref, inv_ref, pw_ref, pb, ps):
    c = pad_ref.shape[1]
    gw = c // len(POOL_WINDOWS)
    outs = []
    for g, k in enumerate(POOL_WINDOWS):
        left = k // 2
        right = k - 1 - left
        lanes = slice(g * gw, (g + 1) * gw)
        s = None
        for o in range(-left, right + 1):
            v = pad_ref[POOL_HALO + o * SUB:POOL_HALO + o * SUB + TM, lanes]
            s = v if s is None else s + v
        outs.append(s * inv_ref[0, :, lanes] - pad_ref[POOL_HALO:POOL_HALO + TM, lanes])
    d = jnp.concatenate(outs, axis=1).astype(BF16)
    half = c // 2
    y = jnp.concatenate([_dot(d[:, :half], pw_ref[0]), _dot(d[:, half:], pw_ref[1])], axis=1)
    return (y + pb) * ps


def _scan_bwd_kernel(cfg, f, proj_ref, xp_ref, xn_ref, bp_ref, bn_ref, hf_ref, *refs):
    inv_refs, refs = refs[:f], refs[f:]
    (cw_ref, cb_ref, wg_ref, ba_ref, bx_ref, lam_ref, h0_ref, pw_ref, pb_ref, ps_ref,
     y_ref, st_ref, pad_ref, ppad_ref, a_ref, u_ref, hb_ref, carry_ref) = refs
    n_blocks = cfg[2] // f
    blk = n_blocks - 1 - pl.program_id(0)
    c = hf_ref.shape[1]
    for sub in reversed(range(f)):
        rows = slice(sub * TM, (sub + 1) * TM)
        is_ctx, _, first, last = _seq_flags(cfg, blk * f + sub)
        prev, nxt = _inner_halo(proj_ref, sub, f, 0, c, (CONV_LEFT * SUB, CONV_RIGHT * SUB), xp_ref, xn_ref)
        xc = _conv(pad_ref, proj_ref[rows, 0:c], prev, nxt, first, last, cw_ref[...], cb_ref[...])
        a, u = _gates(xc, wg_ref, ba_ref[...], bx_ref[...], lam_ref[...])
        a_ref[...] = a
        u_ref[...] = u

        @pl.when(last)
        def _():
            carry_ref[...] = h0_ref[0]

        h = _scan(a_ref, u_ref, hb_ref, 0, carry_ref[...], reverse=True)
        carry_ref[...] = h
        st_ref[0] = h

        ga = proj_ref[rows, c:2 * c]
        y_a = (hf_ref[rows, :] + hb_ref[...]) * _gelu_tanh(ga)

        use_prev = jnp.logical_and(is_ctx, jnp.logical_not(first))
        use_next = jnp.logical_and(is_ctx, jnp.logical_not(last))
        bprev, bnext = _inner_halo(proj_ref, sub, f, 2 * c, c, (POOL_HALO, POOL_HALO), bp_ref, bn_ref)
        ppad_ref[0:POOL_HALO, :] = jnp.where(use_prev, bprev, 0.0)
        ppad_ref[POOL_HALO:POOL_HALO + TM, :] = proj_ref[rows, 2 * c:3 * c]
        ppad_ref[POOL_HALO + TM:, :] = jnp.where(use_next, bnext, 0.0)
        y_b = _pool_mix(ppad_ref, inv_refs[sub], pw_ref, pb_ref[...], ps_ref[...])
        y_ref[rows, :] = jnp.concatenate([y_a, y_b], axis=1).astype(BF16)


def _scan_bwd_call(cfg, inv_cnt, proj, hf, conv_w, conv_b, wg, ba, bx, lam, h0, pw, pb, ps):
    n_ctx_tiles, tiles_per_ctx, n_tiles, d = cfg
    n_groups = n_ctx_tiles // tiles_per_ctx
    c = conv_w.shape[1]
    f = _seq_tiles(cfg)
    n_blocks = n_tiles // f
    rev = lambda j: n_blocks - 1 - j
    xp_spec, xn_spec = _halo_specs(cfg, 0, CONV_LEFT * SUB, CONV_RIGHT * SUB, rev, f)
    bp_spec, bn_spec = _halo_specs(cfg, 2, POOL_HALO, POOL_HALO, rev, f)

    def inv_spec(sub):
        def idx(j):
            tile = rev(j) * f + sub
            return (jnp.where(tile < n_ctx_tiles, lax.rem(tile, tiles_per_ctx), tiles_per_ctx), 0, 0)
        return pl.BlockSpec((1, TM, c), idx)

    return pl.pallas_call(
        functools.partial(_scan_bwd_kernel, cfg, f),
        grid=(n_blocks,),
        in_specs=[
            pl.BlockSpec((f * TM, 3 * c), lambda j: (rev(j), 0)), xp_spec, xn_spec, bp_spec, bn_spec,
            pl.BlockSpec((f * TM, c), lambda j: (rev(j), 0)),
            *[inv_spec(sub) for sub in range(f)],
            _const_spec((CONV_W, c)), _const_spec((1, c)), _const_spec((2, c // 2, c)),
            _const_spec((1, c)), _const_spec((1, c)), _const_spec((1, c)),
            _h0_spec(cfg, rev, f),
            _const_spec((2, c // 2, c // 2)), _const_spec((1, c)), _const_spec((1, c)),
        ],
        out_specs=[pl.BlockSpec((f * TM, d), lambda j: (rev(j), 0)), _state_spec(cfg, rev, f)],
        out_shape=[jax.ShapeDtypeStruct((n_tiles * TM, d), BF16),
                   jax.ShapeDtypeStruct((n_groups + 1, SUB, c), F32)],
        scratch_shapes=[
            pltpu.VMEM((TM + (CONV_W - 1) * SUB, c), F32),
            pltpu.VMEM((TM + 2 * POOL_HALO, c), F32),
            pltpu.VMEM((TM, c), F32), pltpu.VMEM((TM, c), F32), pltpu.VMEM((TM, c), F32),
            pltpu.VMEM((SUB, c), F32),
        ],
        compiler_params=_cparams(("arbitrary",)),
        name="lru_bwd_mix",
    )(proj, proj, proj, proj, proj, hf, *([inv_cnt] * f), conv_w, conv_b.reshape(1, c), wg, ba.reshape(1, c),
      bx.reshape(1, c), lam.reshape(1, c), h0, pw, pb.reshape(1, c), ps.reshape(1, c))


def _out_proj(x_ref, ym_ref, mod_ref, wo_ref, g2_ref):
    xm = _gated_add(x_ref[...], mod_ref[0, 2], _dot(ym_ref[...], wo_ref[...]))
    h2 = _modulate(_rms(xm, g2_ref[...]), mod_ref[0, 4], mod_ref[0, 3])
    return xm, h2


def _swiglu_act(h2b, wg_ref, wu_ref, act_ref):
    n_chunks = act_ref.shape[1] // FF_CHUNK

    def body(f, carry):
        c0 = pl.multiple_of(f * FF_CHUNK, FF_CHUNK)
        g = _dot(h2b, wg_ref[:, pl.ds(c0, FF_CHUNK)])
        u = _dot(h2b, wu_ref[:, pl.ds(c0, FF_CHUNK)])
        act_ref[:, pl.ds(c0, FF_CHUNK)] = (g * _sigmoid(g) * u).astype(BF16)
        return carry

    lax.fori_loop(0, n_chunks, body, 0, unroll=True)


def _dense_ffn_kernel(x_ref, ym_ref, mod_ref, wo_ref, g2_ref, wg_ref, wu_ref, wd_ref, o_ref, act_ref):
    xm, h2 = _out_proj(x_ref, ym_ref, mod_ref, wo_ref, g2_ref)
    _swiglu_act(h2.astype(BF16), wg_ref, wu_ref, act_ref)
    o_ref[...] = _gated_add(xm, mod_ref[0, 5], _dot(act_ref[...], wd_ref[...]))


def _dense_ffn_call(cfg, x, ymix, mod_l, w_out, g2, wg, wu, wd):
    _, _, n_tiles, d = cfg
    d_ff = wg.shape[1]
    f = _wide(cfg)
    row_spec = pl.BlockSpec((TM * f, d), lambda j: (j, 0))
    return pl.pallas_call(
        _dense_ffn_kernel,
        grid=(n_tiles // f,),
        in_specs=[row_spec, row_spec, _mod_spec(cfg, f), _const_spec((d, d)), _const_spec((1, d)),
                  _const_spec((d, d_ff)), _const_spec((d, d_ff)), _const_spec((d_ff, d))],
        out_specs=row_spec,
        out_shape=jax.ShapeDtypeStruct((n_tiles * TM, d), F32),
        scratch_shapes=[pltpu.VMEM((TM * f, d_ff), BF16)],
        compiler_params=_cparams(("parallel",)),
        name="out_proj_dense_ffn",
    )(x, ymix, mod_l, w_out, g2.reshape(1, d), wg, wu, wd)


M_E1, M_E2, M_W1, M_W2, M_R1, M_R2 = range(6)


def _route_kernel(cast_weights, x_ref, ym_ref, mod_ref, wo_ref, g2_ref, rw_ref, rb_ref, tri_ref, *refs):
    if cast_weights:
        wg_ref, wu_ref, wd_ref, *refs = refs
        wgu_ref, wdb_ref = refs[5:7]
        wgu_ref[...] = jnp.concatenate([wg_ref[0], wu_ref[0]], axis=1).astype(BF16)
        wdb_ref[...] = wd_ref[0].astype(BF16)
        refs = refs[:5] + refs[7:]
    xm_ref, hp_ref, meta_ref, mt_ref, cnt_ref, run_ref = refs

    @pl.when(pl.program_id(0) == 0)
    def _():
        run_ref[...] = jnp.zeros_like(run_ref)

    run = run_ref[...]
    for r0 in range(0, x_ref.shape[0], TM):
        rows = pl.ds(r0, TM)
        xm = _gated_add(x_ref[rows, :], mod_ref[0, 2], _dot(ym_ref[rows, :], wo_ref[...]))
        h2 = _modulate(_rms(xm, g2_ref[...]), mod_ref[0, 4], mod_ref[0, 3])
        xm_ref[rows, :] = xm
        h_hi = h2.astype(BF16)
        h_hi32 = h_hi.astype(F32)
        hp_ref[rows, :] = _pack_pairs(h_hi32)

        h_lo = (h2 - h_hi32).astype(BF16)
        p = _dot(h_hi, rw_ref[...])
        logits = p[:, :LANES] + p[:, LANES:] + _dot(h_lo, rw_ref[:, :LANES])
        lane = lax.broadcasted_iota(jnp.int32, logits.shape, 1)
        neg = jnp.float32(-jnp.inf)
        lg = jnp.where(lane < N_EXPERTS, logits + rb_ref[...], neg)
        m1 = jnp.max(lg, axis=1, keepdims=True)
        i1 = jnp.min(jnp.where(lg == m1, lane, LANES), axis=1, keepdims=True)
        lg2 = jnp.where(lane == i1, neg, lg)
        m2 = jnp.max(lg2, axis=1, keepdims=True)
        i2 = jnp.min(jnp.where(lg2 == m2, lane, LANES), axis=1, keepdims=True)
        e2 = jnp.exp(m2 - m1)
        den = 1.0 + e2

        sel1 = lane == i1
        sel2 = lane == i2
        onehot = jnp.where(jnp.logical_or(sel1, sel2), 1.0, 0.0)
        rank = _dot(tri_ref[...], onehot.astype(BF16)) + run
        r1 = jnp.sum(jnp.where(sel1, rank, 0.0), axis=1, keepdims=True)
        r2 = jnp.sum(jnp.where(sel2, rank, 0.0), axis=1, keepdims=True)
        run = run + jnp.sum(onehot, axis=0, keepdims=True)

        meta = jnp.zeros(logits.shape, F32)
        for k, v in ((M_E1, i1.astype(F32)), (M_E2, i2.astype(F32)), (M_W1, 1.0 / den), (M_W2, e2 / den),
                     (M_R1, r1), (M_R2, r2)):
            meta = jnp.where(lane == k, v, meta)
        meta_ref[rows, :] = meta
        mt_ref[:, rows] = jnp.transpose(meta)[:SUB]

    run_ref[...] = run
    cnt_ref[...] = run


def _cast_blocks(steps, rows):
    nblk = 1
    while nblk * 2 <= steps and rows % (nblk * 2 * 2 * SUB) == 0:
        nblk *= 2
    return nblk


def _route_call(cfg, part, x, ymix, mod_l, w_out, g2, router_w, router_b, expert_w=None):
    d = cfg[3]
    t0, nt = part
    n = nt * TM
    f = _wide(cfg)
    tm = TM * f
    s0 = t0 // f
    steps = nt // f
    in_row_spec = pl.BlockSpec((tm, d), lambda j: (j + s0, 0))
    row_spec = pl.BlockSpec((tm, d), lambda j: (j, 0))
    rw = jnp.zeros((d, LANES), F32).at[:, :N_EXPERTS].set(router_w)
    rw_hi = rw.astype(BF16)
    rw = jnp.concatenate([rw_hi, (rw - rw_hi.astype(F32)).astype(BF16)], axis=1)
    rb = jnp.zeros((1, LANES), F32).at[0, :N_EXPERTS].set(router_b)
    tri = jnp.tril(jnp.ones((TM, TM), BF16), -1)
    in_specs = [in_row_spec, in_row_spec, _mod_spec(cfg, f, t0), _const_spec((d, d)), _const_spec((1, d)),
                _const_spec((d, 2 * LANES)), _const_spec((1, LANES)), _const_spec((TM, TM))]
    args = [x, ymix, mod_l, w_out, g2.reshape(1, d), rw, rb, tri]
    out_specs = [row_spec, pl.BlockSpec((tm, d // 2), lambda j: (j, 0)),
                 pl.BlockSpec((tm, LANES), lambda j: (j, 0)), pl.BlockSpec((SUB, tm), lambda j: (0, j)),
                 pl.BlockSpec((1, LANES), lambda j: (0, 0))]
    out_shape = [jax.ShapeDtypeStruct((n, d), F32), jax.ShapeDtypeStruct((n, d // 2), jnp.int32),
                 jax.ShapeDtypeStruct((n, LANES), F32), jax.ShapeDtypeStruct((SUB, n), F32),
                 jax.ShapeDtypeStruct((1, LANES), F32)]
    if expert_w is not None:
        wg, wu, wd, layer = expert_w
        n_l, n_e, _, dfe = wg.shape
        nblk = min(_cast_blocks(steps, n_e * d), _cast_blocks(steps, n_e * dfe))
        blk = lambda j: (jnp.minimum(j, nblk - 1), 0)
        blk_in = lambda j: (layer, jnp.minimum(j, nblk - 1), 0)
        ru, rd = n_e * d // nblk, n_e * dfe // nblk
        in_specs += [pl.BlockSpec((1, ru, dfe), blk_in), pl.BlockSpec((1, ru, dfe), blk_in),
                     pl.BlockSpec((1, rd, d), blk_in)]
        args += [wg.reshape(n_l, n_e * d, dfe), wu.reshape(n_l, n_e * d, dfe), wd.reshape(n_l, n_e * dfe, d)]
        out_specs += [pl.BlockSpec((ru, 2 * dfe), blk), pl.BlockSpec((rd, d), blk)]
        out_shape += [jax.ShapeDtypeStruct((n_e * d, 2 * dfe), BF16), jax.ShapeDtypeStruct((n_e * dfe, d), BF16)]
    outs = pl.pallas_call(
        functools.partial(_route_kernel, expert_w is not None),
        grid=(steps,),
        in_specs=in_specs,
        out_specs=out_specs,
        out_shape=out_shape,
        scratch_shapes=[pltpu.VMEM((1, LANES), F32)],
        compiler_params=_cparams(("arbitrary",)),
        name="out_proj_route",
    )(*args)
    if expert_w is None:
        return outs, None
    return outs[:5], (outs[5].reshape(n_e, d, 2 * dfe), outs[6].reshape(n_e, dfe, d))


def _routing_tables(n_tiles, mt, cnt):
    counts = cnt[0, :N_EXPERTS].astype(jnp.int32)
    padded = ((counts + TM - 1) // TM) * TM
    ends = jnp.cumsum(padded)
    offs = ends - padded
    row = lambda k: mt[k].astype(jnp.int32)
    pos1 = offs[row(M_E1)] + row(M_R1)
    pos2 = offs[row(M_E2)] + row(M_R2)
    n_sorted_tiles = 2 * n_tiles + N_EXPERTS
    starts = jnp.arange(n_sorted_tiles, dtype=jnp.int32) * TM
    tile_e = jnp.minimum(jnp.sum((starts[:, None] >= ends[None, :]).astype(jnp.int32), axis=1), N_EXPERTS - 1)
    n_active = (ends[-1] // TM).reshape(1)
    return jnp.concatenate([pos1, pos2]), tile_e, n_active


SC_CORES = 2
SC_SUBCORES = 16
SC_CHUNK = 64


def _sc_mesh():
    return plsc.VectorSubcoreMesh(core_axis_name="c", subcore_axis_name="s",
                                  num_cores=SC_CORES, num_subcores=SC_SUBCORES)


def _sc_scatter_rows(rows, idx, n_out):
    n_src, width = rows.shape
    n_idx = idx.shape[0]
    n_workers = SC_CORES * SC_SUBCORES
    per_w = n_idx // n_workers
    chunks = per_w // SC_CHUNK
    assert n_idx % (n_workers * SC_CHUNK) == 0 and n_src % per_w == 0
    idx3 = idx.reshape(n_workers, chunks, SC_CHUNK)

    def body(rows_hbm, idx_hbm, out_hbm, idx_v, rows_v, sem):
        wid = lax.axis_index("s") * SC_CORES + lax.axis_index("c")
        src_base = lax.rem(wid * per_w, n_src)
        pltpu.sync_copy(idx_hbm.at[wid], idx_v)

        @pl.loop(0, chunks)
        def _(i):
            off = pl.multiple_of(i * SC_CHUNK, SC_CHUNK)
            pltpu.sync_copy(rows_hbm.at[pl.ds(src_base + off, SC_CHUNK)], rows_v)
            pltpu.async_copy(rows_v, out_hbm.at[idx_v.at[i]], sem).wait()

    return pl.kernel(
        body,
        out_type=jax.ShapeDtypeStruct((n_out, width), rows.dtype),
        mesh=_sc_mesh(),
        scratch_types=[pltpu.VMEM((chunks, SC_CHUNK), jnp.int32), pltpu.VMEM((SC_CHUNK, width), rows.dtype),
                       pltpu.SemaphoreType.DMA],
        name="sc_scatter_rows",
    )(rows, idx3)


def _expert_kernel(te_ref, na_ref, s_ref, wgu_ref, wd_ref, o_ref):
    del te_ref

    @pl.when(pl.program_id(0) < na_ref[0])
    def _():
        h = _unpack_pairs(s_ref[...]).astype(BF16)
        gu = _dot(h, wgu_ref[0])
        dfe = gu.shape[1] // 2
        g = gu[:, :dfe]
        act = (g * _sigmoid(g) * gu[:, dfe:]).astype(BF16)
        o_ref[...] = _pack_pairs(_dot(act, wd_ref[0]).astype(BF16).astype(F32))

    @pl.when(pl.program_id(0) >= na_ref[0])
    def _():
        o_ref[...] = jnp.zeros_like(o_ref)


def _expert_call(cfg, tile_e, n_active, buf, wgu, wd):
    d = cfg[3]
    dfe = wd.shape[1]
    n_sorted_tiles = buf.shape[0] // TM
    last = lambda t, na: jnp.maximum(jnp.minimum(t, na[0] - 1), 0)
    tile = lambda t, te, na: (last(t, na), 0)
    w_spec = lambda shape: pl.BlockSpec(shape, lambda t, te, na: (te[last(t, na)], 0, 0))
    return pl.pallas_call(
        _expert_kernel,
        grid_spec=pltpu.PrefetchScalarGridSpec(
            num_scalar_prefetch=2,
            grid=(n_sorted_tiles,),
            in_specs=[pl.BlockSpec((TM, d // 2), tile), w_spec((1, d, 2 * dfe)), w_spec((1, dfe, d))],
            out_specs=pl.BlockSpec((TM, d // 2), lambda t, te, na: (t, 0)),
        ),
        out_shape=jax.ShapeDtypeStruct((n_sorted_tiles * TM, d // 2), jnp.int32),
        compiler_params=_cparams(("arbitrary",)),
        name="moe_experts",
    )(tile_e, n_active, buf, wgu, wd)


def _store_natural(n_ctx_tiles, part, y, o_refs):
    t0, nt = part
    f = y.shape[0] // TM
    has_ctx = t0 < n_ctx_tiles
    has_lat = t0 + nt > n_ctx_tiles
    is_ctx = pl.program_id(0) * f + t0 < n_ctx_tiles
    if has_ctx:
        @pl.when(is_ctx)
        def _():
            o_refs[0][0] = _from_time_major(y)

    if has_lat:
        @pl.when(jnp.logical_not(is_ctx))
        def _():
            o_refs[-1][...] = _from_time_major(y)


def _natural_out(cfg, part, f=1):
    n_ctx_tiles, tiles_per_ctx, n_tiles, d = cfg
    n_groups = n_ctx_tiles // tiles_per_ctx
    t0, nt = part
    specs = _natural_specs(cfg, t0, f)
    shapes = (jax.ShapeDtypeStruct((n_groups, SUB, tiles_per_ctx * GRID_W, d), F32),
              jax.ShapeDtypeStruct((SUB, (n_tiles - n_ctx_tiles) * GRID_W, d), F32))
    keep = [k for k, used in enumerate((t0 < n_ctx_tiles, t0 + nt > n_ctx_tiles)) if used]
    return [specs[k] for k in keep], [shapes[k] for k in keep], keep


def _sc_gather_rows(table, idx):
    n_rows = idx.shape[0]
    width = table.shape[1]
    n_workers = SC_CORES * SC_SUBCORES
    assert n_rows % (n_workers * SC_CHUNK) == 0
    per_w = n_rows // n_workers
    mesh = _sc_mesh()

    def body(table_hbm, idx_hbm, out_hbm, idx_v, rows_v, sem):
        wid = lax.axis_index("s") * SC_CORES + lax.axis_index("c")
        base = wid * per_w
        pltpu.sync_copy(idx_hbm.at[pl.ds(base, per_w)], idx_v)

        @pl.loop(0, per_w // SC_CHUNK)
        def _(i):
            off = pl.multiple_of(i * SC_CHUNK, SC_CHUNK)
            pltpu.async_copy(table_hbm.at[idx_v.at[pl.ds(off, SC_CHUNK)]], rows_v, sem).wait()
            pltpu.sync_copy(rows_v, out_hbm.at[pl.ds(base + off, SC_CHUNK)])

    return pl.kernel(
        body,
        out_type=jax.ShapeDtypeStruct((n_rows, width), table.dtype),
        mesh=mesh,
        scratch_types=[pltpu.VMEM((per_w,), jnp.int32), pltpu.VMEM((SC_CHUNK, width), table.dtype),
                       pltpu.SemaphoreType.DMA],
        name="sc_gather_rows",
    )(table, idx)


def _combine_kernel(final_ctx_tiles, part, n_filled, xm_ref, y1_ref, y2_ref, meta_ref, mod_ref, gfin_ref,
                    *refs):
    o_refs = refs[n_filled:]
    meta = meta_ref[...]
    y = (meta[:, M_W1:M_W1 + 1] * _unpack_pairs(y1_ref[...])
         + meta[:, M_W2:M_W2 + 1] * _unpack_pairs(y2_ref[...]))
    x = _gated_add(xm_ref[...], mod_ref[0, 5], y)
    if final_ctx_tiles is None:
        o_refs[0][...] = x
    else:
        _store_natural(final_ctx_tiles, part, _rms(x, gfin_ref[...]), o_refs)


def _combine_call(cfg, part, xm, yg, meta, mod_l, g_final, final, filled):
    n_ctx_tiles, _, n_tiles, d = cfg
    t0, nt = part
    f = _seq_tiles(cfg) if t0 % 2 == 0 and nt % 2 == 0 else 1
    tm = f * TM
    steps = nt // f
    row_spec = pl.BlockSpec((tm, d), lambda j: (j, 0))
    if final:
        out_specs, out_shape, keys = _natural_out(cfg, part, f)
    else:
        out_specs = [pl.BlockSpec((tm, d), lambda j: (j + t0 // f, 0))]
        out_shape, keys = [jax.ShapeDtypeStruct((n_tiles * TM, d), F32)], [0]
    reuse = [k for k in keys if k in filled]
    base = [xm, yg, yg, meta, mod_l, g_final.reshape(1, d)]
    outs = pl.pallas_call(
        functools.partial(_combine_kernel, n_ctx_tiles if final else None, part, len(reuse)),
        grid=(steps,),
        in_specs=[row_spec, pl.BlockSpec((tm, d // 2), lambda j: (j, 0)),
                  pl.BlockSpec((tm, d // 2), lambda j: (steps + j, 0)),
                  pl.BlockSpec((tm, LANES), lambda j: (j, 0)), _mod_spec(cfg, f, t0), _const_spec((1, d))]
                 + [pl.BlockSpec(memory_space=pl.ANY)] * len(reuse),
        out_specs=out_specs,
        out_shape=out_shape,
        input_output_aliases={len(base) + i: keys.index(k) for i, k in enumerate(reuse)},
        compiler_params=_cparams(("arbitrary",)),
        name="moe_combine",
    )(*base, *[filled[k] for k in reuse])
    return {**filled, **dict(zip(keys, outs))}


def _final_norm_kernel(n_ctx_tiles, part, x_ref, g_ref, *o_refs):
    _store_natural(n_ctx_tiles, part, _rms(x_ref[...], g_ref[...]), o_refs)


def _final_norm_call(cfg, x, g):
    n_ctx_tiles, _, n_tiles, d = cfg
    out_specs, out_shape, _ = _natural_out(cfg, (0, n_tiles))
    return pl.pallas_call(
        functools.partial(_final_norm_kernel, n_ctx_tiles, (0, n_tiles)),
        grid=(n_tiles,),
        in_specs=[pl.BlockSpec((TM, d), lambda j: (j, 0)), _const_spec((1, d))],
        out_specs=out_specs,
        out_shape=out_shape,
        compiler_params=_cparams(("arbitrary",)),
        name="final_norm",
    )(x, g.reshape(1, d))


def _block_diag(w, per_block):
    *lead, n, k, _ = w.shape
    nb = n // per_block
    w = w.reshape(*lead, nb, per_block, k, k)
    eye = jnp.eye(per_block, dtype=w.dtype)
    out = w[..., :, :, None, :] * eye[:, None, :, None]
    return out.reshape(*lead, nb, per_block * k, per_block * k)


def _pos_tables(n_ctx_tiles, n_lat_tiles, d):
    quarter = d // 4
    omega = 1.0 / (POS_BASE ** (jnp.arange(quarter, dtype=F32) / quarter))
    er = jnp.arange(n_lat_tiles, dtype=F32)[:, None] * omega
    ec = jnp.arange(GRID_W, dtype=F32)[:, None] * omega
    row_emb = jnp.concatenate([jnp.sin(er), jnp.cos(er)], axis=-1)
    col_emb = jnp.concatenate([jnp.sin(ec), jnp.cos(ec)], axis=-1)
    del n_ctx_tiles
    return row_emb[:, None, :], jnp.repeat(col_emb, SUB, axis=0)


def kernel(x_prompt, x_sample, state_lru, c, c_ctx, norm_mix_g, w_ada, b_ada, w_in, conv_w, conv_b, lru_wa, lru_ba, lru_wx, lru_bx, lru_lam, pool_w, pool_b, pool_scale, w_out, norm_ffn_g, ffn_wg, ffn_wu, ffn_wd, moe_router_w, moe_router_b, moe_wg, moe_wu, moe_wd, norm_final_g):
    bc, tc, d = x_prompt.shape
    bl, tl, _ = x_sample.shape
    depth = w_ada.shape[0]
    c_lru = conv_w.shape[-1]
    assert bl == SUB and bc % SUB == 0 and tc % GRID_W == 0 and tl % GRID_W == 0
    assert c_lru == 4 * LANES and d == 2 * c_lru
    n_groups = bc // SUB
    tiles_per_ctx = tc // GRID_W
    n_ctx_tiles = n_groups * tiles_per_ctx
    n_lat_tiles = tl // GRID_W
    n_tiles = n_ctx_tiles + n_lat_tiles
    n_ctx = n_ctx_tiles * TM
    cfg = (n_ctx_tiles, tiles_per_ctx, n_tiles, d)

    cond = jnp.concatenate([c, c_ctx[None], jnp.zeros((SUB - 1, d), F32)], axis=0)
    mod = _ada_call(cond, w_ada, b_ada)
    mod_lat = mod[:, :SUB].reshape(depth, SUB, 6, d).transpose(0, 2, 1, 3)
    mod_ctx = jnp.broadcast_to(mod[:, SUB].reshape(depth, 6, 1, d), (depth, 6, SUB, d))
    mod = jnp.stack([mod_ctx, mod_lat], axis=1)

    row_tab, col_rep = _pos_tables(n_ctx_tiles, n_lat_tiles, d)
    first = (x_prompt.reshape(n_groups, SUB, tc, d), x_sample, row_tab, col_rep)
    x = None
    wg_lru = (0.5 * jnp.concatenate([_block_diag(lru_wa, 4), _block_diag(lru_wx, 4)], axis=-1)).astype(BF16)
    pw = _block_diag(pool_w, 2).astype(BF16)
    inv_cnt = _pool_inv_counts(tc, c_lru)
    h0 = jnp.concatenate([jnp.zeros_like(state_lru[None]), state_lru[None]], axis=0)

    states = []
    for l in range(depth):
        jdx = l // 2
        if l == 0:
            proj, x = _inproj_call(cfg, x, mod[l], norm_mix_g[l], w_in[l].astype(BF16), first)
        else:
            (proj,) = _inproj_call(cfg, x, mod[l], norm_mix_g[l], w_in[l].astype(BF16))
        to_cast = [(w_out, l)] + ([(ffn_wg, jdx), (ffn_wu, jdx), (ffn_wd, jdx)] if l % 2 == 0 else [])
        hf, st_f, w_out_b, *ffn_b = _scan_fwd_call(cfg, proj, conv_w[l], conv_b[l], wg_lru[l, 0], lru_ba[l, 0],
                                                   lru_bx[l, 0], lru_lam[l, 0], h0[:, :, l, 0], to_cast)
        ymix, st_b = _scan_bwd_call(cfg, inv_cnt, proj, hf, conv_w[l], conv_b[l], wg_lru[l, 1], lru_ba[l, 1],
                                    lru_bx[l, 1], lru_lam[l, 1], h0[:, :, l, 1], pw[l], pool_b[l],
                                    pool_scale[l])
        states.append(jnp.stack([st_f[:n_groups].reshape(bc, c_lru), st_b[:n_groups].reshape(bc, c_lru)], axis=1))
        if l % 2 == 0:
            x = _dense_ffn_call(cfg, x, ymix, mod[l], w_out_b, norm_ffn_g[l], *ffn_b)
        else:
            final = l == depth - 1
            filled = {}
            expert_w = None
            for part in _moe_parts(cfg):
                to_cast = (moe_wg, moe_wu, moe_wd, jdx) if expert_w is None else None
                (xm, hp, meta, mt, cnt), cast = _route_call(cfg, part, x, ymix, mod[l], w_out_b, norm_ffn_g[l],
                                                            moe_router_w[jdx], moe_router_b[jdx], to_cast)
                expert_w = expert_w or cast
                pos_flat, tile_e, n_active = _routing_tables(part[1], mt, cnt)
                buf = _sc_scatter_rows(hp, pos_flat, (2 * part[1] + N_EXPERTS) * TM)
                ys = _expert_call(cfg, tile_e, n_active, buf, *expert_w)
                yg = _sc_gather_rows(ys, pos_flat)
                filled = _combine_call(cfg, part, xm, yg, meta, mod[l], norm_final_g, final, filled)
            x = (filled[0], filled[1]) if final else filled[0]

    y_prompt, y_sample = x if depth % 2 == 0 else _final_norm_call(cfg, x, norm_final_g)
    new_state = jnp.stack(states, axis=1)
    return (y_prompt.reshape(bc, tc, d), y_sample, new_state)
```

```python
import functools

import jax
import jax.numpy as jnp
from jax import lax
from jax.experimental import pallas as pl
from jax.experimental.pallas import tpu as pltpu
from jax.experimental.pallas import tpu_sc as plsc

F32 = jnp.float32
BF16 = jnp.bfloat16

SUB = 8
LANES = 128
GRID_W = 64
TM = GRID_W * SUB
POS_BASE = 10000.0
CONV_W = 4
CONV_LEFT = CONV_W // 2
CONV_RIGHT = CONV_W - 1 - CONV_LEFT
LRU_C = 8.0
POOL_WINDOWS = (2, 4, 8, 16)
POOL_HALO = 8 * SUB
N_EXPERTS = 8
EPS = 1e-6
FF_CHUNK = 256
FWD_TILES = 4
INPROJ_TILES = 4
VMEM_LIMIT = 56 * 1024 * 1024


def _cparams(sem):
    return pltpu.CompilerParams(dimension_semantics=sem, vmem_limit_bytes=VMEM_LIMIT)


def _const_spec(shape):
    nd = len(shape)
    return pl.BlockSpec(shape, lambda *_: (0,) * nd, pipeline_mode=pl.Buffered(1))


def _rms(x, g):
    ms = jnp.mean(x * x, axis=-1, keepdims=True)
    return x * lax.rsqrt(ms + EPS) * g


def _per_seq(x, v, op):
    r, c = x.shape
    x3 = x.reshape(r // SUB, SUB, c)
    return op(x3, v[None]).reshape(r, c)


def _modulate(h, scale, shift):
    r, c = h.shape
    h3 = h.reshape(r // SUB, SUB, c)
    return (h3 * (1.0 + scale)[None] + shift[None]).reshape(r, c)


def _gated_add(x, gate, y):
    return x + _per_seq(y, gate, lambda a, b: a * b)


def _dot(a, b):
    return jnp.dot(a, b, preferred_element_type=F32)


def _pack_pairs(x):
    bits = lax.bitcast_convert_type(x, jnp.uint32)
    half = bits.shape[1] // 2
    w = lax.shift_right_logical(bits[:, :half], jnp.uint32(16)) | (bits[:, half:] & jnp.uint32(0xFFFF0000))
    return lax.bitcast_convert_type(w, jnp.int32)


def _unpack_pairs(w):
    w = lax.bitcast_convert_type(w, jnp.uint32)
    lo = lax.bitcast_convert_type(lax.shift_left(w, jnp.uint32(16)), F32)
    hi = lax.bitcast_convert_type(w & jnp.uint32(0xFFFF0000), F32)
    return jnp.concatenate([lo, hi], axis=1)


def _sigmoid(x):
    return 0.5 * jnp.tanh(0.5 * x) + 0.5


def _ada_kernel(c_ref, w_ref, b_ref, o_ref):
    c = c_ref[...]
    s = (c * jax.nn.sigmoid(c)).astype(BF16)
    o_ref[0] = _dot(s, w_ref[0].astype(BF16)) + b_ref[0]


def _ada_call(cond, w_ada, b_ada):
    depth, d, d6 = w_ada.shape
    nr = cond.shape[0]
    bn = d6 // 4
    return pl.pallas_call(
        _ada_kernel,
        grid=(depth, d6 // bn),
        in_specs=[
            pl.BlockSpec((nr, d), lambda l, n: (0, 0)),
            pl.BlockSpec((1, d, bn), lambda l, n: (l, 0, n)),
            pl.BlockSpec((1, 1, bn), lambda l, n: (l, 0, n)),
        ],
        out_specs=pl.BlockSpec((1, nr, bn), lambda l, n: (l, 0, n)),
        out_shape=jax.ShapeDtypeStruct((depth, nr, d6), F32),
        compiler_params=_cparams(("parallel", "parallel")),
        name="ada_mod",
    )(cond, w_ada, b_ada.reshape(depth, 1, d6))


def _to_time_major(x):
    s, t, d = x.shape
    return jnp.swapaxes(x, 0, 1).reshape(s * t, d)


def _from_time_major(x):
    r, d = x.shape
    return jnp.swapaxes(x.reshape(r // SUB, SUB, d), 0, 1)


def _inproj_kernel(n_ctx_tiles, *refs):
    if n_ctx_tiles is not None:
        xp_ref, xs_ref, row_ref, col_ref, mod_ref, g_ref, w_ref, o_ref, x0_ref = refs
        f = row_ref.shape[0]
        is_ctx = pl.program_id(0) * f < n_ctx_tiles

        @pl.when(is_ctx)
        def _():
            x0_ref[...] = _to_time_major(xp_ref[0])

        @pl.when(jnp.logical_not(is_ctx))
        def _():
            x = _to_time_major(xs_ref[...])
            half = x.shape[1] // 2
            for k in range(f):
                rows = slice(k * TM, (k + 1) * TM)
                x0_ref[rows, :] = jnp.concatenate([x[rows, :half] + row_ref[k], x[rows, half:] + col_ref[...]],
                                                  axis=1)

        x = x0_ref[...]
    else:
        x_ref, mod_ref, g_ref, w_ref, o_ref = refs
        x = x_ref[...]
    h = _modulate(_rms(x, g_ref[...]), mod_ref[0, 1], mod_ref[0, 0])
    o_ref[...] = _dot(h.astype(BF16), w_ref[...])


def _natural_specs(cfg, t0=0, f=1):
    n_ctx_tiles, tiles_per_ctx, _, d = cfg

    def ctx_idx(j):
        jc = jnp.minimum(j * f + t0, n_ctx_tiles - f)
        return (jc // tiles_per_ctx, 0, lax.rem(jc, tiles_per_ctx) // f, 0)

    return (pl.BlockSpec((1, SUB, f * GRID_W, d), ctx_idx),
            pl.BlockSpec((SUB, f * GRID_W, d), lambda j: (0, jnp.maximum(j * f + t0 - n_ctx_tiles, 0) // f, 0)))


def _wide(cfg):
    n_ctx_tiles, _, n_tiles, _ = cfg
    return 2 if n_ctx_tiles % 2 == 0 and n_tiles % 2 == 0 else 1


def _moe_parts(cfg):
    n_tiles = cfg[2]
    half = n_tiles // 2
    if _wide(cfg) == 2 and half % 2 == 0:
        return [(0, half), (half, n_tiles - half)]
    return [(0, n_tiles)]


def _mod_spec(cfg, f=1, t0=0):
    n_ctx_steps = cfg[0] // f
    d = cfg[3]
    s0 = t0 // f
    return pl.BlockSpec((1, 6, SUB, d), lambda j: (jnp.where(j + s0 >= n_ctx_steps, 1, 0), 0, 0, 0))


def _inproj_call(cfg, x, mod_l, g, w_in, first=None):
    n_ctx_tiles, _, n_tiles, d = cfg
    d_in = w_in.shape[1]
    f = _seq_tiles(cfg) if first is not None else _wide(cfg)
    if first is None and n_ctx_tiles % INPROJ_TILES == 0 and n_tiles % INPROJ_TILES == 0:
        f = INPROJ_TILES
    tm = TM * f
    row_spec = pl.BlockSpec((tm, d), lambda j: (j, 0))
    out_specs = [pl.BlockSpec((tm, d_in), lambda j: (j, 0))]
    out_shape = [jax.ShapeDtypeStruct((n_tiles * TM, d_in), F32)]
    if first is not None:
        xp_spec, xs_spec = _natural_specs(cfg, 0, f)
        in_specs = [xp_spec, xs_spec,
                    pl.BlockSpec((f, 1, d // 2), lambda j: (jnp.maximum(j - n_ctx_tiles // f, 0), 0, 0)),
                    _const_spec((TM, d // 2))]
        args = list(first)
        out_specs.append(row_spec)
        out_shape.append(jax.ShapeDtypeStruct((n_tiles * TM, d), F32))
    else:
        in_specs = [row_spec]
        args = [x]
    in_specs += [_mod_spec(cfg, f), _const_spec((1, d)), _const_spec((d, d_in))]
    args += [mod_l, g.reshape(1, d), w_in]
    return pl.pallas_call(
        functools.partial(_inproj_kernel, n_ctx_tiles if first is not None else None),
        grid=(n_tiles // f,),
        in_specs=in_specs,
        out_specs=out_specs,
        out_shape=out_shape,
        compiler_params=_cparams(("parallel",)),
        name="in_proj",
    )(*args)


def _seq_flags(cfg, j):
    n_ctx_tiles, tiles_per_ctx, n_tiles, _ = cfg
    is_ctx = j < n_ctx_tiles
    pos = lax.rem(j, tiles_per_ctx)
    first = jnp.where(is_ctx, pos == 0, j == n_ctx_tiles)
    last = jnp.where(is_ctx, pos == tiles_per_ctx - 1, j == n_tiles - 1)
    return is_ctx, pos, first, last


def _conv(pad_ref, xa, prev, nxt, first, last, cw, cb):
    lo = CONV_LEFT * SUB
    pad_ref[0:lo, :] = jnp.where(first, 0.0, prev)
    pad_ref[lo:lo + TM, :] = xa
    pad_ref[lo + TM:lo + TM + CONV_RIGHT * SUB, :] = jnp.where(last, 0.0, nxt)
    y = cb
    for k in range(CONV_W):
        y = y + pad_ref[k * SUB:k * SUB + TM, :] * cw[k:k + 1, :]
    return y


def _gates(xc, wg_ref, ba, bx, lam, a_ref, u_ref):
    half = xc.shape[1] // 2
    for hh in range(2):
        cols = slice(hh * half, (hh + 1) * half)
        xh = xc[:, cols]
        g = _dot(xh.astype(BF16), wg_ref[hh])
        t_r = jnp.tanh(g[:, :half] + 0.5 * ba[:, cols])
        t_i = jnp.tanh(g[:, half:] + 0.5 * bx[:, cols])
        z = -lam[:, cols]
        half_decay = (0.5 * LRU_C) * (jnp.maximum(z, 0.0) + jnp.log1p(jnp.exp(-jnp.abs(z))))
        neg_log_a = t_r * half_decay + half_decay
        a = jnp.exp(-neg_log_a)
        z = jnp.tanh(neg_log_a) * (a * a + 1.0)
        root = jnp.where(z > 0.0, z * lax.rsqrt(z), 0.0)
        half_xc = 0.5 * xh
        a_ref[:, cols] = a
        u_ref[:, cols] = root * (t_i * half_xc + half_xc)


def _scan(a_ref, u_ref, h_ref, row0, h, reverse):
    steps = TM // SUB

    def body(k, h):
        t = steps - 1 - k if reverse else k
        r0 = pl.multiple_of(t * SUB, SUB)
        h = a_ref[pl.ds(r0, SUB), :] * h + u_ref[pl.ds(r0, SUB), :]
        h_ref[pl.ds(row0 + r0, SUB), :] = h
        return h

    return lax.fori_loop(0, steps, body, h, unroll=8)


def _seq_tiles(cfg, want=2):
    n_ctx_tiles, tiles_per_ctx, n_tiles, _ = cfg
    f = 1
    while f * 2 <= want and n_ctx_tiles % (f * 2) == 0 and tiles_per_ctx % (f * 2) == 0 and n_tiles % (f * 2) == 0:
        f *= 2
    return f


def _inner_halo(ref, sub, f, col0, c, rows, outer_prev, outer_next):
    prev = outer_prev[...] if sub == 0 else ref[sub * TM - rows[0]:sub * TM, col0:col0 + c]
    nxt = outer_next[...] if sub == f - 1 else ref[(sub + 1) * TM:(sub + 1) * TM + rows[1], col0:col0 + c]
    return prev, nxt


def _scan_fwd_kernel(cfg, f, n_cast, xa_ref, xp_ref, xn_ref, cw_ref, cb_ref, wg_ref, ba_ref, bx_ref,
                     lam_ref, h0_ref, *refs):
    cast_in, (hf_ref, st_ref), cast_out = refs[:n_cast], refs[n_cast:n_cast + 2], refs[n_cast + 2:2 * n_cast + 2]
    pad_ref, a_ref, u_ref, carry_ref = refs[2 * n_cast + 2:]
    for w_ref, wb_ref in zip(cast_in, cast_out):
        wb_ref[...] = w_ref[0].astype(BF16)
    c = xa_ref.shape[1]
    for sub in range(f):
        _, _, first, last = _seq_flags(cfg, pl.program_id(0) * f + sub)
        prev, nxt = _inner_halo(xa_ref, sub, f, 0, c, (CONV_LEFT * SUB, CONV_RIGHT * SUB), xp_ref, xn_ref)
        xc = _conv(pad_ref, xa_ref[sub * TM:(sub + 1) * TM, :], prev, nxt, first, last, cw_ref[...], cb_ref[...])
        _gates(xc, wg_ref, ba_ref[...], bx_ref[...], lam_ref[...], a_ref, u_ref)

        @pl.when(first)
        def _():
            carry_ref[...] = h0_ref[0]

        h = _scan(a_ref, u_ref, hf_ref, sub * TM, carry_ref[...], reverse=False)
        carry_ref[...] = h
        st_ref[0] = h


def _halo_specs(cfg, col, rows_prev, rows_next, blk_of, f):
    n_tiles, d = cfg[2], cfg[3]
    c = d // 2
    nb_prev = TM // rows_prev
    nb_next = TM // rows_next
    last_next = n_tiles * nb_next - 1
    prev = pl.BlockSpec((rows_prev, c), lambda j: (jnp.maximum(blk_of(j) * f * nb_prev - 1, 0), col))
    nxt = pl.BlockSpec((rows_next, c), lambda j: (jnp.minimum((blk_of(j) + 1) * f * nb_next, last_next), col))
    return prev, nxt


def _state_spec(cfg, blk_of, f):
    n_ctx_tiles, tiles_per_ctx, _, d = cfg
    n_groups = n_ctx_tiles // tiles_per_ctx
    return pl.BlockSpec((1, SUB, d // 2),
                        lambda j: (jnp.minimum(blk_of(j) * f // tiles_per_ctx, n_groups), 0, 0))


def _h0_spec(cfg, blk_of, f):
    n_ctx_tiles, d = cfg[0], cfg[3]
    return pl.BlockSpec((1, SUB, d // 2), lambda j: (jnp.where(blk_of(j) * f >= n_ctx_tiles, 1, 0), 0, 0))


def _scan_fwd_call(cfg, proj, conv_w, conv_b, wg, ba, bx, lam, h0, cast=()):
    n_ctx_tiles, tiles_per_ctx, n_tiles, _ = cfg
    n_groups = n_ctx_tiles // tiles_per_ctx
    c = conv_w.shape[1]
    f = _seq_tiles(cfg, FWD_TILES)
    steps = n_tiles // f
    ident = lambda j: j
    xp_spec, xn_spec = _halo_specs(cfg, 0, CONV_LEFT * SUB, CONV_RIGHT * SUB, ident, f)
    cast_in, cast_out, cast_shape = [], [], []
    for w, layer in cast:
        _, rows, cols = w.shape
        nblk = _cast_blocks(steps, rows)
        cast_in.append(pl.BlockSpec((1, rows // nblk, cols),
                                    lambda j, layer=layer, nblk=nblk: (layer, jnp.minimum(j, nblk - 1), 0)))
        cast_out.append(pl.BlockSpec((rows // nblk, cols), lambda j, nblk=nblk: (jnp.minimum(j, nblk - 1), 0)))
        cast_shape.append(jax.ShapeDtypeStruct((rows, cols), BF16))
    return pl.pallas_call(
        functools.partial(_scan_fwd_kernel, cfg, f, len(cast)),
        grid=(steps,),
        in_specs=[
            pl.BlockSpec((f * TM, c), lambda j: (j, 0)), xp_spec, xn_spec,
            _const_spec((CONV_W, c)), _const_spec((1, c)), _const_spec((2, c // 2, c)),
            _const_spec((1, c)), _const_spec((1, c)), _const_spec((1, c)),
            _h0_spec(cfg, ident, f),
        ] + cast_in,
        out_specs=[pl.BlockSpec((f * TM, c), lambda j: (j, 0)), _state_spec(cfg, ident, f)] + cast_out,
        out_shape=[jax.ShapeDtypeStruct((n_tiles * TM, c), F32),
                   jax.ShapeDtypeStruct((n_groups + 1, SUB, c), F32)] + cast_shape,
        scratch_shapes=[
            pltpu.VMEM((TM + (CONV_W - 1) * SUB, c), F32),
            pltpu.VMEM((TM, c), F32), pltpu.VMEM((TM, c), F32), pltpu.VMEM((SUB, c), F32),
        ],
        compiler_params=_cparams(("arbitrary",)),
        name="lru_fwd",
    )(proj, proj, proj, conv_w, conv_b.reshape(1, c), wg, ba.reshape(1, c), bx.reshape(1, c),
      lam.reshape(1, c), h0, *[w for w, _ in cast])


def _pool_inv_counts(ctx_len, c):
    gw = c // len(POOL_WINDOWS)

    def table(t0, t_len):
        t = t0 + jnp.arange(GRID_W)
        cols = []
        for k in POOL_WINDOWS:
            left = k // 2
            right = k - 1 - left
            cnt = jnp.minimum(t + right + 1, t_len) - jnp.maximum(t - left, 0)
            cols.append(jnp.broadcast_to((1.0 / cnt.astype(F32))[:, None], (GRID_W, gw)))
        return jnp.repeat(jnp.concatenate(cols, axis=1), SUB, axis=0)

    return jnp.stack([table(p * GRID_W, ctx_len) for p in range(ctx_len // GRID_W)] + [table(0, GRID_W)])


def _gelu_tanh(x):
    k0 = 0.7978845608028654
    hx = 0.5 * x
    return hx + hx * jnp.tanh(x * (k0 + (k0 * 0.044715) * (x * x)))


def _pool_mix_half(pad_ref, inv_ref, pw_ref, pb, ps, hh):
    c = pad_ref.shape[1]
    gw = c // len(POOL_WINDOWS)
    outs = []
    for g in (2 * hh, 2 * hh + 1):
        k = POOL_WINDOWS[g]
        left = k // 2
        right = k - 1 - left
        lanes = slice(g * gw, (g + 1) * gw)
        s = None
        for o in range(-left, right + 1):
            v = pad_ref[POOL_HALO + o * SUB:POOL_HALO + o * SUB + TM, lanes]
            s = v if s is None else s + v
        outs.append(s * inv_ref[0, :, lanes] - pad_ref[POOL_HALO:POOL_HALO + TM, lanes])
    d = jnp.concatenate(outs, axis=1).astype(BF16)
    cols = slice(hh * 2 * gw, (hh + 1) * 2 * gw)
    return (_dot(d, pw_ref[hh]) + pb[:, cols]) * ps[:, cols]


def _scan_bwd_kernel(cfg, f, proj_ref, xp_ref, xn_ref, bp_ref, bn_ref, hf_ref, *refs):
    inv_refs, refs = refs[:f], refs[f:]
    (cw_ref, cb_ref, wg_ref, ba_ref, bx_ref, lam_ref, h0_ref, pw_ref, pb_ref, ps_ref,
     y_ref, st_ref, pad_ref, ppad_ref, a_ref, u_ref, hb_ref, carry_ref) = refs
    n_blocks = cfg[2] // f
    blk = n_blocks - 1 - pl.program_id(0)
    c = hf_ref.shape[1]
    for sub in reversed(range(f)):
        rows = slice(sub * TM, (sub + 1) * TM)
        is_ctx, _, first, last = _seq_flags(cfg, blk * f + sub)
        prev, nxt = _inner_halo(proj_ref, sub, f, 0, c, (CONV_LEFT * SUB, CONV_RIGHT * SUB), xp_ref, xn_ref)
        xc = _conv(pad_ref, proj_ref[rows, 0:c], prev, nxt, first, last, cw_ref[...], cb_ref[...])
        _gates(xc, wg_ref, ba_ref[...], bx_ref[...], lam_ref[...], a_ref, u_ref)

        @pl.when(last)
        def _():
            carry_ref[...] = h0_ref[0]

        h = _scan(a_ref, u_ref, hb_ref, 0, carry_ref[...], reverse=True)
        carry_ref[...] = h
        st_ref[0] = h

        ga = proj_ref[rows, c:2 * c]
        y_a = (hf_ref[rows, :] + hb_ref[...]) * _gelu_tanh(ga)

        use_prev = jnp.logical_and(is_ctx, jnp.logical_not(first))
        use_next = jnp.logical_and(is_ctx, jnp.logical_not(last))
        bprev, bnext = _inner_halo(proj_ref, sub, f, 2 * c, c, (POOL_HALO, POOL_HALO), bp_ref, bn_ref)
        ppad_ref[0:POOL_HALO, :] = jnp.where(use_prev, bprev, 0.0)
        ppad_ref[POOL_HALO:POOL_HALO + TM, :] = proj_ref[rows, 2 * c:3 * c]
        ppad_ref[POOL_HALO + TM:, :] = jnp.where(use_next, bnext, 0.0)
        y_ref[rows, 0:c] = y_a.astype(BF16)
        for hh in range(2):
            y_b = _pool_mix_half(ppad_ref, inv_refs[sub], pw_ref, pb_ref[...], ps_ref[...], hh)
            y_ref[rows, c + hh * (c // 2):c + (hh + 1) * (c // 2)] = y_b.astype(BF16)


def _scan_bwd_call(cfg, inv_cnt, proj, hf, conv_w, conv_b, wg, ba, bx, lam, h0, pw, pb, ps):
    n_ctx_tiles, tiles_per_ctx, n_tiles, d = cfg
    n_groups = n_ctx_tiles // tiles_per_ctx
    c = conv_w.shape[1]
    f = _seq_tiles(cfg)
    n_blocks = n_tiles // f
    rev = lambda j: n_blocks - 1 - j
    xp_spec, xn_spec = _halo_specs(cfg, 0, CONV_LEFT * SUB, CONV_RIGHT * SUB, rev, f)
    bp_spec, bn_spec = _halo_specs(cfg, 2, POOL_HALO, POOL_HALO, rev, f)

    def inv_spec(sub):
        def idx(j):
            tile = rev(j) * f + sub
            return (jnp.where(tile < n_ctx_tiles, lax.rem(tile, tiles_per_ctx), tiles_per_ctx), 0, 0)
        return pl.BlockSpec((1, TM, c), idx)

    return pl.pallas_call(
        functools.partial(_scan_bwd_kernel, cfg, f),
        grid=(n_blocks,),
        in_specs=[
            pl.BlockSpec((f * TM, 3 * c), lambda j: (rev(j), 0)), xp_spec, xn_spec, bp_spec, bn_spec,
            pl.BlockSpec((f * TM, c), lambda j: (rev(j), 0)),
            *[inv_spec(sub) for sub in range(f)],
            _const_spec((CONV_W, c)), _const_spec((1, c)), _const_spec((2, c // 2, c)),
            _const_spec((1, c)), _const_spec((1, c)), _const_spec((1, c)),
            _h0_spec(cfg, rev, f),
            _const_spec((2, c // 2, c // 2)), _const_spec((1, c)), _const_spec((1, c)),
        ],
        out_specs=[pl.BlockSpec((f * TM, d), lambda j: (rev(j), 0)), _state_spec(cfg, rev, f)],
        out_shape=[jax.ShapeDtypeStruct((n_tiles * TM, d), BF16),
                   jax.ShapeDtypeStruct((n_groups + 1, SUB, c), F32)],
        scratch_shapes=[
            pltpu.VMEM((TM + (CONV_W - 1) * SUB, c), F32),
            pltpu.VMEM((TM + 2 * POOL_HALO, c), F32),
            pltpu.VMEM((TM, c), F32), pltpu.VMEM((TM, c), F32), pltpu.VMEM((TM, c), F32),
            pltpu.VMEM((SUB, c), F32),
        ],
        compiler_params=_cparams(("arbitrary",)),
        name="lru_bwd_mix",
    )(proj, proj, proj, proj, proj, hf, *([inv_cnt] * f), conv_w, conv_b.reshape(1, c), wg, ba.reshape(1, c),
      bx.reshape(1, c), lam.reshape(1, c), h0, pw, pb.reshape(1, c), ps.reshape(1, c))


def _out_proj(x_ref, ym_ref, mod_ref, wo_ref, g2_ref):
    xm = _gated_add(x_ref[...], mod_ref[0, 2], _dot(ym_ref[...], wo_ref[...]))
    h2 = _modulate(_rms(xm, g2_ref[...]), mod_ref[0, 4], mod_ref[0, 3])
    return xm, h2


def _swiglu_act(h2b, wg_ref, wu_ref, act_ref):
    n_chunks = act_ref.shape[1] // FF_CHUNK

    def body(f, carry):
        c0 = pl.multiple_of(f * FF_CHUNK, FF_CHUNK)
        g = _dot(h2b, wg_ref[:, pl.ds(c0, FF_CHUNK)])
        u = _dot(h2b, wu_ref[:, pl.ds(c0, FF_CHUNK)])
        act_ref[:, pl.ds(c0, FF_CHUNK)] = (g * _sigmoid(g) * u).astype(BF16)
        return carry

    lax.fori_loop(0, n_chunks, body, 0, unroll=True)


def _dense_ffn_kernel(x_ref, ym_ref, mod_ref, wo_ref, g2_ref, wg_ref, wu_ref, wd_ref, o_ref, act_ref):
    xm, h2 = _out_proj(x_ref, ym_ref, mod_ref, wo_ref, g2_ref)
    _swiglu_act(h2.astype(BF16), wg_ref, wu_ref, act_ref)
    o_ref[...] = _gated_add(xm, mod_ref[0, 5], _dot(act_ref[...], wd_ref[...]))


def _dense_ffn_call(cfg, x, ymix, mod_l, w_out, g2, wg, wu, wd):
    _, _, n_tiles, d = cfg
    d_ff = wg.shape[1]
    f = _wide(cfg)
    row_spec = pl.BlockSpec((TM * f, d), lambda j: (j, 0))
    return pl.pallas_call(
        _dense_ffn_kernel,
        grid=(n_tiles // f,),
        in_specs=[row_spec, row_spec, _mod_spec(cfg, f), _const_spec((d, d)), _const_spec((1, d)),
                  _const_spec((d, d_ff)), _const_spec((d, d_ff)), _const_spec((d_ff, d))],
        out_specs=row_spec,
        out_shape=jax.ShapeDtypeStruct((n_tiles * TM, d), F32),
        scratch_shapes=[pltpu.VMEM((TM * f, d_ff), BF16)],
        compiler_params=_cparams(("parallel",)),
        name="out_proj_dense_ffn",
    )(x, ymix, mod_l, w_out, g2.reshape(1, d), wg, wu, wd)


M_E1, M_E2, M_W1, M_W2, M_R1, M_R2 = range(6)


def _route_kernel(cast_weights, x_ref, ym_ref, mod_ref, wo_ref, g2_ref, rw_ref, rb_ref, tri_ref, *refs):
    if cast_weights:
        wg_ref, wu_ref, wd_ref, *refs = refs
        wgu_ref, wdb_ref = refs[5:7]
        wgu_ref[...] = jnp.concatenate([wg_ref[0], wu_ref[0]], axis=1).astype(BF16)
        wdb_ref[...] = wd_ref[0].astype(BF16)
        refs = refs[:5] + refs[7:]
    xm_ref, hp_ref, meta_ref, mt_ref, cnt_ref, run_ref = refs

    @pl.when(pl.program_id(0) == 0)
    def _():
        run_ref[...] = jnp.zeros_like(run_ref)

    run = run_ref[...]
    for r0 in range(0, x_ref.shape[0], TM):
        rows = pl.ds(r0, TM)
        xm = _gated_add(x_ref[rows, :], mod_ref[0, 2], _dot(ym_ref[rows, :], wo_ref[...]))
        h2 = _modulate(_rms(xm, g2_ref[...]), mod_ref[0, 4], mod_ref[0, 3])
        xm_ref[rows, :] = xm
        h_hi = h2.astype(BF16)
        h_hi32 = h_hi.astype(F32)
        hp_ref[rows, :] = _pack_pairs(h_hi32)

        h_lo = (h2 - h_hi32).astype(BF16)
        p = _dot(h_hi, rw_ref[...])
        logits = p[:, :LANES] + p[:, LANES:] + _dot(h_lo, rw_ref[:, :LANES])
        lane = lax.broadcasted_iota(jnp.int32, logits.shape, 1)
        neg = jnp.float32(-jnp.inf)
        lg = jnp.where(lane < N_EXPERTS, logits + rb_ref[...], neg)
        m1 = jnp.max(lg, axis=1, keepdims=True)
        i1 = jnp.min(jnp.where(lg == m1, lane, LANES), axis=1, keepdims=True)
        lg2 = jnp.where(lane == i1, neg, lg)
        m2 = jnp.max(lg2, axis=1, keepdims=True)
        i2 = jnp.min(jnp.where(lg2 == m2, lane, LANES), axis=1, keepdims=True)
        e2 = jnp.exp(m2 - m1)
        den = 1.0 + e2

        sel1 = lane == i1
        sel2 = lane == i2
        onehot = jnp.where(jnp.logical_or(sel1, sel2), 1.0, 0.0)
        rank = _dot(tri_ref[...], onehot.astype(BF16)) + run
        r1 = jnp.sum(jnp.where(sel1, rank, 0.0), axis=1, keepdims=True)
        r2 = jnp.sum(jnp.where(sel2, rank, 0.0), axis=1, keepdims=True)
        run = run + jnp.sum(onehot, axis=0, keepdims=True)

        meta = jnp.zeros(logits.shape, F32)
        for k, v in ((M_E1, i1.astype(F32)), (M_E2, i2.astype(F32)), (M_W1, 1.0 / den), (M_W2, e2 / den),
                     (M_R1, r1), (M_R2, r2)):
            meta = jnp.where(lane == k, v, meta)
        meta_ref[rows, :] = meta
        mt_ref[:, rows] = jnp.transpose(meta)[:SUB]

    run_ref[...] = run
    cnt_ref[...] = run


def _cast_blocks(steps, rows):
    nblk = 1
    while nblk * 2 <= steps and rows % (nblk * 2 * 2 * SUB) == 0:
        nblk *= 2
    return nblk


def _route_call(cfg, part, x, ymix, mod_l, w_out, g2, router_w, router_b, expert_w=None):
    d = cfg[3]
    t0, nt = part
    n = nt * TM
    f = _wide(cfg)
    tm = TM * f
    s0 = t0 // f
    steps = nt // f
    in_row_spec = pl.BlockSpec((tm, d), lambda j: (j + s0, 0))
    row_spec = pl.BlockSpec((tm, d), lambda j: (j, 0))
    rw = jnp.zeros((d, LANES), F32).at[:, :N_EXPERTS].set(router_w)
    rw_hi = rw.astype(BF16)
    rw = jnp.concatenate([rw_hi, (rw - rw_hi.astype(F32)).astype(BF16)], axis=1)
    rb = jnp.zeros((1, LANES), F32).at[0, :N_EXPERTS].set(router_b)
    tri = jnp.tril(jnp.ones((TM, TM), BF16), -1)
    in_specs = [in_row_spec, in_row_spec, _mod_spec(cfg, f, t0), _const_spec((d, d)), _const_spec((1, d)),
                _const_spec((d, 2 * LANES)), _const_spec((1, LANES)), _const_spec((TM, TM))]
    args = [x, ymix, mod_l, w_out, g2.reshape(1, d), rw, rb, tri]
    out_specs = [row_spec, pl.BlockSpec((tm, d // 2), lambda j: (j, 0)),
                 pl.BlockSpec((tm, LANES), lambda j: (j, 0)), pl.BlockSpec((SUB, tm), lambda j: (0, j)),
                 pl.BlockSpec((1, LANES), lambda j: (0, 0))]
    out_shape = [jax.ShapeDtypeStruct((n, d), F32), jax.ShapeDtypeStruct((n, d // 2), jnp.int32),
                 jax.ShapeDtypeStruct((n, LANES), F32), jax.ShapeDtypeStruct((SUB, n), F32),
                 jax.ShapeDtypeStruct((1, LANES), F32)]
    if expert_w is not None:
        wg, wu, wd, layer = expert_w
        n_l, n_e, _, dfe = wg.shape
        nblk = min(_cast_blocks(steps, n_e * d), _cast_blocks(steps, n_e * dfe))
        blk = lambda j: (jnp.minimum(j, nblk - 1), 0)
        blk_in = lambda j: (layer, jnp.minimum(j, nblk - 1), 0)
        ru, rd = n_e * d // nblk, n_e * dfe // nblk
        in_specs += [pl.BlockSpec((1, ru, dfe), blk_in), pl.BlockSpec((1, ru, dfe), blk_in),
                     pl.BlockSpec((1, rd, d), blk_in)]
        args += [wg.reshape(n_l, n_e * d, dfe), wu.reshape(n_l, n_e * d, dfe), wd.reshape(n_l, n_e * dfe, d)]
        out_specs += [pl.BlockSpec((ru, 2 * dfe), blk), pl.BlockSpec((rd, d), blk)]
        out_shape += [jax.ShapeDtypeStruct((n_e * d, 2 * dfe), BF16), jax.ShapeDtypeStruct((n_e * dfe, d), BF16)]
    outs = pl.pallas_call(
        functools.partial(_route_kernel, expert_w is not None),
        grid=(steps,),
        in_specs=in_specs,
        out_specs=out_specs,
        out_shape=out_shape,
        scratch_shapes=[pltpu.VMEM((1, LANES), F32)],
        compiler_params=_cparams(("arbitrary",)),
        name="out_proj_route",
    )(*args)
    if expert_w is None:
        return outs, None
    return outs[:5], (outs[5].reshape(n_e, d, 2 * dfe), outs[6].reshape(n_e, dfe, d))


def _routing_tables(n_tiles, mt, cnt):
    counts = cnt[0, :N_EXPERTS].astype(jnp.int32)
    padded = ((counts + TM - 1) // TM) * TM
    ends = jnp.cumsum(padded)
    offs = ends - padded
    row = lambda k: mt[k].astype(jnp.int32)
    pos1 = offs[row(M_E1)] + row(M_R1)
    pos2 = offs[row(M_E2)] + row(M_R2)
    n_sorted_tiles = 2 * n_tiles + N_EXPERTS
    starts = jnp.arange(n_sorted_tiles, dtype=jnp.int32) * TM
    tile_e = jnp.minimum(jnp.sum((starts[:, None] >= ends[None, :]).astype(jnp.int32), axis=1), N_EXPERTS - 1)
    n_active = (ends[-1] // TM).reshape(1)
    return jnp.concatenate([pos1, pos2]), tile_e, n_active


SC_CORES = 2
SC_SUBCORES = 16
SC_CHUNK = 64


def _sc_mesh():
    return plsc.VectorSubcoreMesh(core_axis_name="c", subcore_axis_name="s",
                                  num_cores=SC_CORES, num_subcores=SC_SUBCORES)


def _sc_scatter_rows(rows, idx, n_out):
    n_src, width = rows.shape
    n_idx = idx.shape[0]
    n_workers = SC_CORES * SC_SUBCORES
    per_w = n_idx // n_workers
    chunks = per_w // SC_CHUNK
    assert n_idx % (n_workers * SC_CHUNK) == 0 and n_src % per_w == 0
    idx3 = idx.reshape(n_workers, chunks, SC_CHUNK)

    def body(rows_hbm, idx_hbm, out_hbm, idx_v, rows_v, sem):
        wid = lax.axis_index("s") * SC_CORES + lax.axis_index("c")
        src_base = lax.rem(wid * per_w, n_src)
        pltpu.sync_copy(idx_hbm.at[wid], idx_v)

        @pl.loop(0, chunks)
        def _(i):
            off = pl.multiple_of(i * SC_CHUNK, SC_CHUNK)
            pltpu.sync_copy(rows_hbm.at[pl.ds(src_base + off, SC_CHUNK)], rows_v)
            pltpu.async_copy(rows_v, out_hbm.at[idx_v.at[i]], sem).wait()

    return pl.kernel(
        body,
        out_type=jax.ShapeDtypeStruct((n_out, width), rows.dtype),
        mesh=_sc_mesh(),
        scratch_types=[pltpu.VMEM((chunks, SC_CHUNK), jnp.int32), pltpu.VMEM((SC_CHUNK, width), rows.dtype),
                       pltpu.SemaphoreType.DMA],
        name="sc_scatter_rows",
    )(rows, idx3)


def _expert_kernel(te_ref, na_ref, s_ref, wgu_ref, wd_ref, o_ref):
    del te_ref

    @pl.when(pl.program_id(0) < na_ref[0])
    def _():
        h = _unpack_pairs(s_ref[...]).astype(BF16)
        gu = _dot(h, wgu_ref[0])
        dfe = gu.shape[1] // 2
        g = gu[:, :dfe]
        act = (g * _sigmoid(g) * gu[:, dfe:]).astype(BF16)
        o_ref[...] = _pack_pairs(_dot(act, wd_ref[0]).astype(BF16).astype(F32))

    @pl.when(pl.program_id(0) >= na_ref[0])
    def _():
        o_ref[...] = jnp.zeros_like(o_ref)


def _expert_call(cfg, tile_e, n_active, buf, wgu, wd):
    d = cfg[3]
    dfe = wd.shape[1]
    n_sorted_tiles = buf.shape[0] // TM
    last = lambda t, na: jnp.maximum(jnp.minimum(t, na[0] - 1), 0)
    tile = lambda t, te, na: (last(t, na), 0)
    w_spec = lambda shape: pl.BlockSpec(shape, lambda t, te, na: (te[last(t, na)], 0, 0))
    return pl.pallas_call(
        _expert_kernel,
        grid_spec=pltpu.PrefetchScalarGridSpec(
            num_scalar_prefetch=2,
            grid=(n_sorted_tiles,),
            in_specs=[pl.BlockSpec((TM, d // 2), tile), w_spec((1, d, 2 * dfe)), w_spec((1, dfe, d))],
            out_specs=pl.BlockSpec((TM, d // 2), lambda t, te, na: (t, 0)),
        ),
        out_shape=jax.ShapeDtypeStruct((n_sorted_tiles * TM, d // 2), jnp.int32),
        compiler_params=_cparams(("arbitrary",)),
        name="moe_experts",
    )(tile_e, n_active, buf, wgu, wd)


def _store_natural(n_ctx_tiles, part, y, o_refs):
    t0, nt = part
    f = y.shape[0] // TM
    has_ctx = t0 < n_ctx_tiles
    has_lat = t0 + nt > n_ctx_tiles
    is_ctx = pl.program_id(0) * f + t0 < n_ctx_tiles
    if has_ctx:
        @pl.when(is_ctx)
        def _():
            o_refs[0][0] = _from_time_major(y)

    if has_lat:
        @pl.when(jnp.logical_not(is_ctx))
        def _():
            o_refs[-1][...] = _from_time_major(y)


def _natural_out(cfg, part, f=1):
    n_ctx_tiles, tiles_per_ctx, n_tiles, d = cfg
    n_groups = n_ctx_tiles // tiles_per_ctx
    t0, nt = part
    specs = _natural_specs(cfg, t0, f)
    shapes = (jax.ShapeDtypeStruct((n_groups, SUB, tiles_per_ctx * GRID_W, d), F32),
              jax.ShapeDtypeStruct((SUB, (n_tiles - n_ctx_tiles) * GRID_W, d), F32))
    keep = [k for k, used in enumerate((t0 < n_ctx_tiles, t0 + nt > n_ctx_tiles)) if used]
    return [specs[k] for k in keep], [shapes[k] for k in keep], keep


def _sc_gather_rows(table, idx):
    n_rows = idx.shape[0]
    width = table.shape[1]
    n_workers = SC_CORES * SC_SUBCORES
    assert n_rows % (n_workers * SC_CHUNK) == 0
    per_w = n_rows // n_workers
    mesh = _sc_mesh()

    def body(table_hbm, idx_hbm, out_hbm, idx_v, rows_v, sem):
        wid = lax.axis_index("s") * SC_CORES + lax.axis_index("c")
        base = wid * per_w
        pltpu.sync_copy(idx_hbm.at[pl.ds(base, per_w)], idx_v)

        @pl.loop(0, per_w // SC_CHUNK)
        def _(i):
            off = pl.multiple_of(i * SC_CHUNK, SC_CHUNK)
            pltpu.async_copy(table_hbm.at[idx_v.at[pl.ds(off, SC_CHUNK)]], rows_v, sem).wait()
            pltpu.sync_copy(rows_v, out_hbm.at[pl.ds(base + off, SC_CHUNK)])

    return pl.kernel(
        body,
        out_type=jax.ShapeDtypeStruct((n_rows, width), table.dtype),
        mesh=mesh,
        scratch_types=[pltpu.VMEM((per_w,), jnp.int32), pltpu.VMEM((SC_CHUNK, width), table.dtype),
                       pltpu.SemaphoreType.DMA],
        name="sc_gather_rows",
    )(table, idx)


def _combine_kernel(final_ctx_tiles, part, n_filled, xm_ref, y1_ref, y2_ref, meta_ref, mod_ref, gfin_ref,
                    *refs):
    o_refs = refs[n_filled:]
    meta = meta_ref[...]
    y = (meta[:, M_W1:M_W1 + 1] * _unpack_pairs(y1_ref[...])
         + meta[:, M_W2:M_W2 + 1] * _unpack_pairs(y2_ref[...]))
    x = _gated_add(xm_ref[...], mod_ref[0, 5], y)
    if final_ctx_tiles is None:
        o_refs[0][...] = x
    else:
        _store_natural(final_ctx_tiles, part, _rms(x, gfin_ref[...]), o_refs)


def _combine_call(cfg, part, xm, yg, meta, mod_l, g_final, final, filled):
    n_ctx_tiles, _, n_tiles, d = cfg
    t0, nt = part
    f = _seq_tiles(cfg) if t0 % 2 == 0 and nt % 2 == 0 else 1
    tm = f * TM
    steps = nt // f
    row_spec = pl.BlockSpec((tm, d), lambda j: (j, 0))
    if final:
        out_specs, out_shape, keys = _natural_out(cfg, part, f)
    else:
        out_specs = [pl.BlockSpec((tm, d), lambda j: (j + t0 // f, 0))]
        out_shape, keys = [jax.ShapeDtypeStruct((n_tiles * TM, d), F32)], [0]
    reuse = [k for k in keys if k in filled]
    base = [xm, yg, yg, meta, mod_l, g_final.reshape(1, d)]
    outs = pl.pallas_call(
        functools.partial(_combine_kernel, n_ctx_tiles if final else None, part, len(reuse)),
        grid=(steps,),
        in_specs=[row_spec, pl.BlockSpec((tm, d // 2), lambda j: (j, 0)),
                  pl.BlockSpec((tm, d // 2), lambda j: (steps + j, 0)),
                  pl.BlockSpec((tm, LANES), lambda j: (j, 0)), _mod_spec(cfg, f, t0), _const_spec((1, d))]
                 + [pl.BlockSpec(memory_space=pl.ANY)] * len(reuse),
        out_specs=out_specs,
        out_shape=out_shape,
        input_output_aliases={len(base) + i: keys.index(k) for i, k in enumerate(reuse)},
        compiler_params=_cparams(("arbitrary",)),
        name="moe_combine",
    )(*base, *[filled[k] for k in reuse])
    return {**filled, **dict(zip(keys, outs))}


def _final_norm_kernel(n_ctx_tiles, part, x_ref, g_ref, *o_refs):
    _store_natural(n_ctx_tiles, part, _rms(x_ref[...], g_ref[...]), o_refs)


def _final_norm_call(cfg, x, g):
    n_ctx_tiles, _, n_tiles, d = cfg
    out_specs, out_shape, _ = _natural_out(cfg, (0, n_tiles))
    return pl.pallas_call(
        functools.partial(_final_norm_kernel, n_ctx_tiles, (0, n_tiles)),
        grid=(n_tiles,),
        in_specs=[pl.BlockSpec((TM, d), lambda j: (j, 0)), _const_spec((1, d))],
        out_specs=out_specs,
        out_shape=out_shape,
        compiler_params=_cparams(("arbitrary",)),
        name="final_norm",
    )(x, g.reshape(1, d))


def _block_diag(w, per_block):
    *lead, n, k, _ = w.shape
    nb = n // per_block
    w = w.reshape(*lead, nb, per_block, k, k)
    eye = jnp.eye(per_block, dtype=w.dtype)
    out = w[..., :, :, None, :] * eye[:, None, :, None]
    return out.reshape(*lead, nb, per_block * k, per_block * k)


def _pos_tables(n_lat_tiles, d):
    quarter = d // 4
    omega = 1.0 / (POS_BASE ** (jnp.arange(quarter, dtype=F32) / quarter))
    er = jnp.arange(n_lat_tiles, dtype=F32)[:, None] * omega
    ec = jnp.arange(GRID_W, dtype=F32)[:, None] * omega
    row_emb = jnp.concatenate([jnp.sin(er), jnp.cos(er)], axis=-1)
    col_emb = jnp.concatenate([jnp.sin(ec), jnp.cos(ec)], axis=-1)
    return row_emb[:, None, :], jnp.repeat(col_emb, SUB, axis=0)


def kernel(x_prompt, x_sample, state_lru, c, c_ctx, norm_mix_g, w_ada, b_ada, w_in, conv_w, conv_b, lru_wa, lru_ba, lru_wx, lru_bx, lru_lam, pool_w, pool_b, pool_scale, w_out, norm_ffn_g, ffn_wg, ffn_wu, ffn_wd, moe_router_w, moe_router_b, moe_wg, moe_wu, moe_wd, norm_final_g):
    bc, tc, d = x_prompt.shape
    bl, tl, _ = x_sample.shape
    depth = w_ada.shape[0]
    c_lru = conv_w.shape[-1]
    assert bl == SUB and bc % SUB == 0 and tc % GRID_W == 0 and tl % GRID_W == 0
    assert c_lru == 4 * LANES and d == 2 * c_lru
    n_groups = bc // SUB
    tiles_per_ctx = tc // GRID_W
    n_ctx_tiles = n_groups * tiles_per_ctx
    n_lat_tiles = tl // GRID_W
    n_tiles = n_ctx_tiles + n_lat_tiles
    cfg = (n_ctx_tiles, tiles_per_ctx, n_tiles, d)

    cond = jnp.concatenate([c, c_ctx[None], jnp.zeros((SUB - 1, d), F32)], axis=0)
    mod = _ada_call(cond, w_ada, b_ada)
    mod_lat = mod[:, :SUB].reshape(depth, SUB, 6, d).transpose(0, 2, 1, 3)
    mod_ctx = jnp.broadcast_to(mod[:, SUB].reshape(depth, 6, 1, d), (depth, 6, SUB, d))
    mod = jnp.stack([mod_ctx, mod_lat], axis=1)

    row_tab, col_rep = _pos_tables(n_lat_tiles, d)
    first = (x_prompt.reshape(n_groups, SUB, tc, d), x_sample, row_tab, col_rep)
    x = None
    wg_lru = (0.5 * jnp.concatenate([_block_diag(lru_wa, 4), _block_diag(lru_wx, 4)], axis=-1)).astype(BF16)
    pw = _block_diag(pool_w, 2).astype(BF16)
    inv_cnt = _pool_inv_counts(tc, c_lru)
    h0 = jnp.concatenate([jnp.zeros_like(state_lru[None]), state_lru[None]], axis=0)

    states = []
    w_in_b = w_in[0].astype(BF16)
    for l in range(depth):
        jdx = l // 2
        if l == 0:
            proj, x = _inproj_call(cfg, x, mod[l], norm_mix_g[l], w_in_b, first)
        else:
            (proj,) = _inproj_call(cfg, x, mod[l], norm_mix_g[l], w_in_b)
        to_cast = [(w_out, l)] + ([(w_in, l + 1)] if l + 1 < depth else [])
        to_cast += [(ffn_wg, jdx), (ffn_wu, jdx), (ffn_wd, jdx)] if l % 2 == 0 else []
        hf, st_f, w_out_b, *cast_b = _scan_fwd_call(cfg, proj, conv_w[l], conv_b[l], wg_lru[l, 0], lru_ba[l, 0],
                                                    lru_bx[l, 0], lru_lam[l, 0], h0[:, :, l, 0], to_cast)
        if l + 1 < depth:
            w_in_b = cast_b.pop(0)
        ffn_b = cast_b
        ymix, st_b = _scan_bwd_call(cfg, inv_cnt, proj, hf, conv_w[l], conv_b[l], wg_lru[l, 1], lru_ba[l, 1],
                                    lru_bx[l, 1], lru_lam[l, 1], h0[:, :, l, 1], pw[l], pool_b[l],
                                    pool_scale[l])
        states.append(jnp.stack([st_f[:n_groups].reshape(bc, c_lru), st_b[:n_groups].reshape(bc, c_lru)], axis=1))
        if l % 2 == 0:
            x = _dense_ffn_call(cfg, x, ymix, mod[l], w_out_b, norm_ffn_g[l], *ffn_b)
        else:
            final = l == depth - 1
            filled = {}
            expert_w = None
            for part in _moe_parts(cfg):
                to_cast = (moe_wg, moe_wu, moe_wd, jdx) if expert_w is None else None
                (xm, hp, meta, mt, cnt), cast = _route_call(cfg, part, x, ymix, mod[l], w_out_b, norm_ffn_g[l],
                                                            moe_router_w[jdx], moe_router_b[jdx], to_cast)
                expert_w = expert_w or cast
                pos_flat, tile_e, n_active = _routing_tables(part[1], mt, cnt)
                buf = _sc_scatter_rows(hp, pos_flat, (2 * part[1] + N_EXPERTS) * TM)
                ys = _expert_call(cfg, tile_e, n_active, buf, *expert_w)
                yg = _sc_gather_rows(ys, pos_flat)
                filled = _combine_call(cfg, part, xm, yg, meta, mod[l], norm_final_g, final, filled)
            x = (filled[0], filled[1]) if final else filled[0]

    y_prompt, y_sample = x if depth % 2 == 0 else _final_norm_call(cfg, x, norm_final_g)
    new_state = jnp.stack(states, axis=1)
    return (y_prompt.reshape(bc, tc, d), y_sample, new_state)
```
